```python
import math
import jax, jax.numpy as jnp
from jax import lax
import numpy as np

D_MODEL = 2048
BATCH = 32
SEQ = 256
DEPTH = 2
DEC_BATCH = 8
DEC_SEQ = 2048
PAST_LEN = 256

GRID_W = 64
S5_WIDTH = D_MODEL // 4
S5_CH = 16
S5_GROUPS = S5_WIDTH // S5_CH
S5_STATE = 64
RWKV_WIDTH = D_MODEL // 4
RWKV_HEAD_DIM = 64
RWKV_HEADS = RWKV_WIDTH // RWKV_HEAD_DIM
DECAY_LORA = 64
ICLR_LORA = 64
GATE_LORA = 128
RWKV_IN = 3 * RWKV_WIDTH + DECAY_LORA + ICLR_LORA + GATE_LORA
MLA_WIDTH = D_MODEL - S5_WIDTH - RWKV_WIDTH
MLA_V_DIM = 128
MLA_HEADS = MLA_WIDTH // MLA_V_DIM
MLA_NOPE_DIM = 128
MLA_ROPE_DIM = 64
MLA_Q_RANK = D_MODEL // 4
MLA_KV_RANK = D_MODEL // 8
MLA_IN = MLA_Q_RANK + MLA_KV_RANK + MLA_ROPE_DIM
IN_COLS = S5_WIDTH + RWKV_IN + MLA_IN
MIX_WIDTH = S5_WIDTH + RWKV_WIDTH + MLA_WIDTH
D_FF = 4 * D_MODEL
N_MOD = 6
Q_BLOCK = 128
ROPE_THETA = 10000.0
NORM_EPS = 1e-6
GN_EPS = 64e-5

kernel_name = 'hybrid_s5_rwkv7_mla_prefix_dit_step'


def rms_norm(x, g):
    xf = x.astype(jnp.float32)
    y = xf * lax.rsqrt(jnp.mean(xf * xf, axis=-1, keepdims=True) + NORM_EPS)
    return (y * g.astype(jnp.float32)).astype(x.dtype)


def modulation(cond, w_ada, b_ada):
    m = jax.nn.silu(cond) @ w_ada + b_ada
    return jnp.split(m[..., None, :], N_MOD, axis=-1)


def centred_shift(p):
    prev = jnp.pad(p[:, :-1], ((0, 0), (1, 0), (0, 0)))
    nxt = jnp.pad(p[:, 1:], ((0, 0), (0, 1), (0, 0)))
    return 0.5 * (prev + nxt)


def rotate_half(x):
    x1, x2 = jnp.split(x, 2, axis=-1)
    return jnp.concatenate([-x2, x1], axis=-1)


def axial_rope_tables(length):
    rows = length // GRID_W
    row_pos = jnp.repeat(jnp.arange(rows, dtype=jnp.float32), GRID_W)
    col_pos = jnp.tile(jnp.arange(GRID_W, dtype=jnp.float32), rows)
    axis_dim = MLA_ROPE_DIM // 2
    inv_freq = 1.0 / (ROPE_THETA ** (jnp.arange(0, axis_dim, 2, dtype=jnp.float32) / axis_dim))
    ang_r = row_pos[:, None] * inv_freq[None, :]
    ang_c = col_pos[:, None] * inv_freq[None, :]
    ang = jnp.concatenate([ang_r, ang_r, ang_c, ang_c], axis=-1)
    return jnp.cos(ang), jnp.sin(ang)


def apply_axial_rope(x, cos, sin):
    xr, xc = jnp.split(x, 2, axis=-1)
    rot = jnp.concatenate([rotate_half(xr), rotate_half(xc)], axis=-1)
    return (x * cos + rot * sin).astype(x.dtype)


def _ssm_combine(e1, e2):
    a1, b1 = e1
    a2, b2 = e2
    return a2 * a1, a2 * b1 + b2


def s5_mixer(u, a_re, a_im, log_dt, b_re, b_im, c_re, c_im, d_skip, w_glu, state0):
    bsz, seq, _ = u.shape
    f32 = jnp.float32
    uf = u.astype(f32).reshape(bsz, seq, S5_GROUPS, S5_CH)
    h0 = lax.complex(state0[..., 0].astype(f32), state0[..., 1].astype(f32))
    lam = lax.complex(a_re.astype(f32), a_im.astype(f32))
    dt = jnp.exp(log_dt.astype(f32))[..., None]
    lam_bar = jnp.exp(lam * dt)
    b_mat = lax.complex(b_re.astype(f32), b_im.astype(f32))
    b_bar = ((lam_bar - 1.0) / lam)[..., None] * b_mat
    c_mat = lax.complex(c_re.astype(f32), c_im.astype(f32))
    uc = uf.astype(jnp.complex64)
    outs = []
    finals = []
    for d, rev in enumerate((False, True)):
        bu = jnp.einsum('blgc,gpc->blgp', uc, b_bar[d])
        edge = seq - 1 if rev else 0
        bu = bu.at[:, edge].add(lam_bar[d] * h0[:, d])
        lam_seq = jnp.broadcast_to(lam_bar[d], bu.shape)
        _, h = lax.associative_scan(_ssm_combine, (lam_seq, bu), reverse=rev, axis=1)
        finals.append(h[:, 0] if rev else h[:, -1])
        outs.append(jnp.einsum('blgp,gcp->blgc', h, c_mat[d]).real)
    y = outs[0] + outs[1] + uf * d_skip.astype(f32).reshape(S5_GROUPS, S5_CH)
    y = jax.nn.gelu(y.reshape(bsz, seq, S5_WIDTH))
    val, gate = jnp.split(y @ w_glu.astype(f32), 2, axis=-1)
    h_fin = jnp.stack(finals, axis=1)
    return (val * jax.nn.sigmoid(gate)).astype(u.dtype), jnp.stack([h_fin.real, h_fin.imag], axis=-1)


def rwkv_scan(r, w, k, v, a, b, s0, reverse):
    xs = tuple(jnp.moveaxis(t, 1, 0) for t in (r, w, k, v, a, b))

    def step(s, inp):
        r_t, w_t, k_t, v_t, a_t, b_t = inp
        sa = jnp.einsum('bhij,bhj->bhi', s, a_t)
        s = s * w_t[:, :, None, :] + sa[..., None] * b_t[:, :, None, :] + v_t[..., None] * k_t[:, :, None, :]
        return s, jnp.einsum('bhij,bhj->bhi', s, r_t)

    s_fin, ys = lax.scan(step, s0.astype(jnp.float32), xs, reverse=reverse)
    return jnp.moveaxis(ys, 0, 1), s_fin


def rwkv_mixer(p, mu, w0, w2, a0, a2, g2, k_k, k_a, r_k, ln_w, ln_b, s0):
    bsz, seq, _ = p.shape
    f32 = jnp.float32
    hd = (bsz, seq, RWKV_HEADS, RWKV_HEAD_DIM)
    p = p + mu * (centred_shift(p) - p)
    cw = RWKV_WIDTH
    r, k, v, wl, al, gl = jnp.split(p, [cw, 2 * cw, 3 * cw, 3 * cw + DECAY_LORA, 3 * cw + DECAY_LORA + ICLR_LORA], axis=-1)
    rf = r.astype(f32).reshape(hd)
    vf = v.astype(f32).reshape(hd)
    kk = (k * k_k).astype(f32).reshape(hd)
    kk = kk * lax.rsqrt(jnp.sum(kk * kk, axis=-1, keepdims=True) + 1e-12)
    g = (jax.nn.sigmoid(gl) @ g2).astype(f32)
    tw = jnp.tanh(wl)
    ys, ks, finals = [], [], []
    for d, rev in enumerate((False, True)):
        w = -jax.nn.softplus(-(w0[d] + tw @ w2[d]).astype(f32)) - 0.5
        decay = jnp.exp(-jnp.exp(w)).reshape(hd)
        a = jax.nn.sigmoid((a0[d] + al @ a2[d]).astype(f32))
        kd = (k.astype(f32) * (1.0 + (a - 1.0) * k_a.astype(f32))).reshape(hd)
        yd, sf = rwkv_scan(rf, decay, kd, vf, -kk, kk * a.reshape(hd), s0[:, d], rev)
        ys.append(yd)
        ks.append(kd)
        finals.append(sf)
    y = ys[0] + ys[1]
    mean = jnp.mean(y, axis=-1, keepdims=True)
    var = jnp.mean(jnp.square(y - mean), axis=-1, keepdims=True)
    yn = (y - mean) * lax.rsqrt(var + GN_EPS)
    yn = yn * ln_w.astype(f32).reshape(RWKV_HEADS, RWKV_HEAD_DIM) + ln_b.astype(f32).reshape(RWKV_HEADS, RWKV_HEAD_DIM)
    bonus = jnp.sum(rf * (ks[0] + ks[1]) * r_k.astype(f32), axis=-1, keepdims=True) * vf
    out = (yn + bonus).reshape(bsz, seq, cw) * g
    return out.astype(p.dtype), jnp.stack(finals, axis=1)


def block_attention(q, k, v):
    bsz, lq, nh, dq = q.shape
    nb = lq // Q_BLOCK
    qb = q.reshape(bsz, nb, Q_BLOCK, nh, dq).transpose(1, 0, 2, 3, 4)
    scale = dq ** -0.5

    def one_block(q_blk):
        s = jnp.einsum('bqhd,bkhd->bhqk', q_blk, k, preferred_element_type=jnp.float32) * scale
        pr = jax.nn.softmax(s, axis=-1)
        return jnp.einsum('bhqk,bkhd->bqhd', pr.astype(v.dtype), v)

    o = lax.map(one_block, qb)
    return o.transpose(1, 0, 2, 3, 4).reshape(bsz, lq, nh, v.shape[-1])


def mla_mixer(p, q_norm, w_uq, kv_norm, w_ukv, ctx_ckv, ctx_krope, rope):
    bsz, seq, _ = p.shape
    c_q, c_kv, k_rope = jnp.split(p, [MLA_Q_RANK, MLA_Q_RANK + MLA_KV_RANK], axis=-1)
    q = (rms_norm(c_q, q_norm) @ w_uq).reshape(bsz, seq, MLA_HEADS, MLA_NOPE_DIM + MLA_ROPE_DIM)
    q_nope, q_rope = jnp.split(q, [MLA_NOPE_DIM], axis=-1)
    ckv_n = rms_norm(c_kv, kv_norm)
    if rope is None:
        keys_ckv, keys_rope = ckv_n, k_rope
    else:
        cos, sin = rope
        q_rope = apply_axial_rope(q_rope, cos[:, None, :], sin[:, None, :])
        keys_ckv = jnp.concatenate([ckv_n, ctx_ckv.astype(ckv_n.dtype)], axis=1)
        keys_rope = jnp.concatenate([apply_axial_rope(k_rope, cos, sin), ctx_krope.astype(k_rope.dtype)], axis=1)
    n_keys = keys_ckv.shape[1]
    kv = (keys_ckv @ w_ukv).reshape(bsz, n_keys, MLA_HEADS, MLA_NOPE_DIM + MLA_V_DIM)
    k_nope, v = jnp.split(kv, [MLA_NOPE_DIM], axis=-1)
    k_pe = jnp.broadcast_to(keys_rope[:, :, None, :], (bsz, n_keys, MLA_HEADS, MLA_ROPE_DIM))
    k = jnp.concatenate([k_nope, k_pe], axis=-1)
    qf = jnp.concatenate([q_nope, q_rope], axis=-1)
    o = block_attention(qf, k, v)
    return o.reshape(bsz, seq, MLA_WIDTH), ckv_n, k_rope


def trunk_layer(x, cond, lw, s5_state0, rwkv_state0, ctx_ckv, ctx_krope, rope):
    sh_m, sc_m, g_m, sh_f, sc_f, g_f = modulation(cond, lw['w_ada'], lw['b_ada'])
    h = rms_norm(x, lw['norm_mix']) * (1.0 + sc_m) + sh_m
    proj = h @ lw['w_in']
    u_s5, p_rwkv, p_mla = jnp.split(proj, [S5_WIDTH, S5_WIDTH + RWKV_IN], axis=-1)
    y_s5, s5_fin = s5_mixer(u_s5, lw['s5_a_re'], lw['s5_a_im'], lw['s5_log_dt'], lw['s5_b_re'], lw['s5_b_im'],
                            lw['s5_c_re'], lw['s5_c_im'], lw['s5_d'], lw['s5_w_glu'], s5_state0)
    y_rwkv, rwkv_fin = rwkv_mixer(p_rwkv, lw['rwkv_mu'], lw['rwkv_w0'], lw['rwkv_w2'], lw['rwkv_a0'], lw['rwkv_a2'],
                                  lw['rwkv_g2'], lw['rwkv_k_k'], lw['rwkv_k_a'], lw['rwkv_r_k'], lw['rwkv_ln_w'],
                                  lw['rwkv_ln_b'], rwkv_state0)
    y_mla, ckv_n, k_rope = mla_mixer(p_mla, lw['mla_q_norm'], lw['mla_w_uq'], lw['mla_kv_norm'], lw['mla_w_ukv'],
                                     ctx_ckv, ctx_krope, rope)
    merged = jnp.concatenate([rms_norm(y_s5, lw['s5_out_norm']), y_rwkv, rms_norm(y_mla, lw['mla_out_norm'])], axis=-1)
    x = x + g_m * (merged @ lw['w_out'])
    h = rms_norm(x, lw['norm_mlp']) * (1.0 + sc_f) + sh_f
    x = x + g_f * (jnp.square(jax.nn.relu(h @ lw['mlp_w1'])) @ lw['mlp_w2'])
    return x, (ckv_n, k_rope, s5_fin, rwkv_fin)


def setup_inputs(seed: int = 0) -> dict:
    key = jax.random.key(seed)
    keys = iter(jax.random.split(key, 64))
    f32 = jnp.float32

    def nrm(shape, scale):
        return scale * jax.random.normal(next(keys), shape, f32)

    def gain(shape):
        return 1.0 + nrm(shape, 0.02)

    L, G, P, H, N = DEPTH, S5_GROUPS, S5_STATE, RWKV_HEADS, RWKV_HEAD_DIM
    ratio = jnp.arange(RWKV_WIDTH, dtype=f32) / (RWKV_WIDTH - 1)
    return {
        'x_prompt': nrm((BATCH, SEQ, D_MODEL), 1.0),
        'x_sample': nrm((DEC_BATCH, DEC_SEQ, D_MODEL), 1.0),
        'cache_mla_ckv': nrm((DEC_BATCH, L, PAST_LEN, MLA_KV_RANK), 1.0),
        'cache_mla_krope': nrm((DEC_BATCH, L, PAST_LEN, MLA_ROPE_DIM), 1.0),
        'state_s5': nrm((DEC_BATCH, L, 2, G, P, 2), 0.1),
        'state_rwkv': nrm((DEC_BATCH, L, 2, H, N, N), 0.1),
        'c': nrm((DEC_BATCH, D_MODEL), 1.0),
        'c_ctx': nrm((D_MODEL,), 1.0),
        'norm_mix': gain((L, D_MODEL)),
        'norm_mlp': gain((L, D_MODEL)),
        'norm_final': gain((D_MODEL,)),
        'w_ada': nrm((L, D_MODEL, N_MOD * D_MODEL), 0.3 * D_MODEL ** -0.5),
        'b_ada': nrm((L, N_MOD * D_MODEL), 0.02),
        'w_in': nrm((L, D_MODEL, IN_COLS), D_MODEL ** -0.5),
        'w_out': nrm((L, MIX_WIDTH, D_MODEL), MIX_WIDTH ** -0.5),
        's5_a_re': -0.5 + nrm((L, 2, G, P), 0.01),
        's5_a_im': math.pi * jnp.arange(P, dtype=f32) + nrm((L, 2, G, P), 0.01),
        's5_log_dt': jax.random.uniform(next(keys), (L, 2, G), dtype=f32, minval=math.log(1e-3), maxval=math.log(1e-1)),
        's5_b_re': nrm((L, 2, G, P, S5_CH), (2 * S5_CH) ** -0.5),
        's5_b_im': nrm((L, 2, G, P, S5_CH), (2 * S5_CH) ** -0.5),
        's5_c_re': nrm((L, 2, G, S5_CH, P), (2 * P) ** -0.5),
        's5_c_im': nrm((L, 2, G, S5_CH, P), (2 * P) ** -0.5),
        's5_d': nrm((L, S5_WIDTH), 1.0),
        's5_w_glu': nrm((L, S5_WIDTH, 2 * S5_WIDTH), S5_WIDTH ** -0.5),
        's5_out_norm': gain((L, S5_WIDTH)),
        'rwkv_mu': jax.random.uniform(next(keys), (L, RWKV_IN), dtype=f32),
        'rwkv_w0': -6.0 + 5.0 * ratio ** 0.9 + nrm((L, 2, RWKV_WIDTH), 0.01),
        'rwkv_w2': nrm((L, 2, DECAY_LORA, RWKV_WIDTH), 0.1 * DECAY_LORA ** -0.5),
        'rwkv_a0': nrm((L, 2, RWKV_WIDTH), 0.1),
        'rwkv_a2': nrm((L, 2, ICLR_LORA, RWKV_WIDTH), 0.1 * ICLR_LORA ** -0.5),
        'rwkv_g2': nrm((L, GATE_LORA, RWKV_WIDTH), GATE_LORA ** -0.5),
        'rwkv_k_k': 0.85 + nrm((L, RWKV_WIDTH), 0.02),
        'rwkv_k_a': 1.0 + nrm((L, RWKV_WIDTH), 0.02),
        'rwkv_r_k': nrm((L, H, N), 0.1),
        'rwkv_ln_w': gain((L, RWKV_WIDTH)),
        'rwkv_ln_b': nrm((L, RWKV_WIDTH), 0.01),
        'mla_q_norm': gain((L, MLA_Q_RANK)),
        'mla_w_uq': nrm((L, MLA_Q_RANK, MLA_HEADS * (MLA_NOPE_DIM + MLA_ROPE_DIM)), MLA_Q_RANK ** -0.5),
        'mla_kv_norm': gain((L, MLA_KV_RANK)),
        'mla_w_ukv': nrm((L, MLA_KV_RANK, MLA_HEADS * (MLA_NOPE_DIM + MLA_V_DIM)), MLA_KV_RANK ** -0.5),
        'mla_out_norm': gain((L, MLA_WIDTH)),
        'mlp_w1': nrm((L, D_MODEL, D_FF), D_MODEL ** -0.5),
        'mlp_w2': nrm((L, D_FF, D_MODEL), D_FF ** -0.5),
    }


def reference(x_prompt, x_sample, cache_mla_ckv, cache_mla_krope, state_s5, state_rwkv, c, c_ctx,
              norm_mix, norm_mlp, norm_final, w_ada, b_ada, w_in, w_out,
              s5_a_re, s5_a_im, s5_log_dt, s5_b_re, s5_b_im, s5_c_re, s5_c_im, s5_d, s5_w_glu, s5_out_norm,
              rwkv_mu, rwkv_w0, rwkv_w2, rwkv_a0, rwkv_a2, rwkv_g2, rwkv_k_k, rwkv_k_a, rwkv_r_k,
              rwkv_ln_w, rwkv_ln_b,
              mla_q_norm, mla_w_uq, mla_kv_norm, mla_w_ukv, mla_out_norm,
              mlp_w1, mlp_w2):
    rope = axial_rope_tables(x_sample.shape[1])
    bsz_ctx = x_prompt.shape[0]
    zero_s5 = jnp.zeros((bsz_ctx, 2, S5_GROUPS, S5_STATE, 2), jnp.float32)
    zero_rwkv = jnp.zeros((bsz_ctx, 2, RWKV_HEADS, RWKV_HEAD_DIM, RWKV_HEAD_DIM), jnp.float32)
    xp, xs = x_prompt, x_sample
    new_ckv, new_krope, new_s5, new_rwkv = [], [], [], []
    for l in range(DEPTH):
        lw = {
            'norm_mix': norm_mix[l], 'norm_mlp': norm_mlp[l], 'w_ada': w_ada[l], 'b_ada': b_ada[l],
            'w_in': w_in[l], 'w_out': w_out[l],
            's5_a_re': s5_a_re[l], 's5_a_im': s5_a_im[l], 's5_log_dt': s5_log_dt[l],
            's5_b_re': s5_b_re[l], 's5_b_im': s5_b_im[l], 's5_c_re': s5_c_re[l], 's5_c_im': s5_c_im[l],
            's5_d': s5_d[l], 's5_w_glu': s5_w_glu[l], 's5_out_norm': s5_out_norm[l],
            'rwkv_mu': rwkv_mu[l], 'rwkv_w0': rwkv_w0[l], 'rwkv_w2': rwkv_w2[l], 'rwkv_a0': rwkv_a0[l],
            'rwkv_a2': rwkv_a2[l], 'rwkv_g2': rwkv_g2[l], 'rwkv_k_k': rwkv_k_k[l], 'rwkv_k_a': rwkv_k_a[l],
            'rwkv_r_k': rwkv_r_k[l], 'rwkv_ln_w': rwkv_ln_w[l], 'rwkv_ln_b': rwkv_ln_b[l],
            'mla_q_norm': mla_q_norm[l], 'mla_w_uq': mla_w_uq[l], 'mla_kv_norm': mla_kv_norm[l],
            'mla_w_ukv': mla_w_ukv[l], 'mla_out_norm': mla_out_norm[l],
            'mlp_w1': mlp_w1[l], 'mlp_w2': mlp_w2[l],
        }
        xp, (ckv_l, krope_l, s5_l, rwkv_l) = trunk_layer(xp, c_ctx, lw, zero_s5, zero_rwkv, None, None, None)
        new_ckv.append(ckv_l)
        new_krope.append(krope_l)
        new_s5.append(s5_l)
        new_rwkv.append(rwkv_l)
        xs, _ = trunk_layer(xs, c, lw, state_s5[:, l], state_rwkv[:, l], cache_mla_ckv[:, l], cache_mla_krope[:, l], rope)
    y_prompt = rms_norm(xp, norm_final)
    y_sample = rms_norm(xs, norm_final)
    return (y_prompt, y_sample, jnp.stack(new_ckv, axis=1), jnp.stack(new_krope, axis=1),
            jnp.stack(new_s5, axis=1), jnp.stack(new_rwkv, axis=1))
```

```python
import functools
import math

import jax
import jax.numpy as jnp
from jax import lax
from jax.experimental import pallas as pl
from jax.experimental.pallas import tpu as pltpu

F32 = jnp.float32
BF16 = jnp.bfloat16

D_MODEL = 2048
N_MOD = 6
GRID_W = 64
S5_WIDTH = 512
S5_CH = 16
S5_GROUPS = 32
S5_STATE = 64
S5_CHUNK = 16
S5_TC = S5_CHUNK * S5_CH
RWKV_WIDTH = 512
RWKV_HEAD_DIM = 64
RWKV_HEADS = 8
LORA_PAD = 128
RWKV_COLS = 3 * RWKV_WIDTH + 3 * LORA_PAD
MLA_HEADS = 8
MLA_V_DIM = 128
MLA_NOPE_DIM = 128
MLA_ROPE_DIM = 64
MLA_QK_PAD = 256
MLA_Q_RANK = 512
MLA_KV_RANK = 256
MLA_WIDTH = 1024
MLA_COLS = MLA_Q_RANK + MLA_KV_RANK + 2 * 128
D_FF = 8192
ROPE_THETA = 10000.0
NORM_EPS = 1e-6
GN_EPS = 64e-5

VMEM_LIMIT_BYTES = 56 * 1024 * 1024


def _cp(*sem):
    return pltpu.CompilerParams(dimension_semantics=sem, vmem_limit_bytes=VMEM_LIMIT_BYTES)


def _dot(a, b):
    return jnp.dot(a, b, preferred_element_type=F32)


def _rms(x, g):
    ms = jnp.mean(x * x, axis=-1, keepdims=True)
    return x * lax.rsqrt(ms + NORM_EPS) * g


def _split_bf16(x):
    hi = x.astype(BF16)
    lo = (x - hi.astype(F32)).astype(BF16)
    return hi, lo


def _segsum(x, bd):
    hi, lo = _split_bf16(x)
    left = _dot(hi[:, :256], bd) + _dot(lo[:, :256], bd)
    right = _dot(hi[:, 256:], bd) + _dot(lo[:, 256:], bd)
    return jnp.concatenate([left, right], axis=1)


def _segsum_bf16(xb, bd):
    return jnp.concatenate([_dot(xb[:, :256], bd), _dot(xb[:, 256:], bd)], axis=1)


def _mod_kernel(c_ref, w_ref, b_ref, o_ref):
    c = c_ref[...]
    s = (c * jax.nn.sigmoid(c)).astype(BF16)
    o_ref[...] = _dot(s, w_ref[...].astype(BF16)) + b_ref[...]


def _modulation(cond, w_ada, b_ada):
    depth, d, n = w_ada.shape
    rows = cond.shape[0]
    tn = 1024
    return pl.pallas_call(
        _mod_kernel,
        grid=(depth, n // tn),
        in_specs=[
            pl.BlockSpec((rows, d), lambda l, j: (0, 0)),
            pl.BlockSpec((None, d, tn), lambda l, j: (l, 0, j)),
            pl.BlockSpec((None, 1, tn), lambda l, j: (l, 0, j)),
        ],
        out_specs=pl.BlockSpec((None, rows, tn), lambda l, j: (l, 0, j)),
        out_shape=jax.ShapeDtypeStruct((depth, rows, n), F32),
        compiler_params=_cp("parallel", "arbitrary"),
        name="adaln_modulation",
    )(cond, w_ada, b_ada.reshape(depth, 1, n))


def _mod_spec(layer, row_of_tile):
    return pl.BlockSpec((None, None, N_MOD, D_MODEL), lambda i, *_: (layer, row_of_tile(i), 0, 0))


def _inproj_kernel(x_ref, nw_ref, mod_ref, ws_ref, wr_ref, wm_ref, os_ref, or_ref, om_ref):
    h = _rms(x_ref[...], nw_ref[...]) * (1.0 + mod_ref[1:2, :]) + mod_ref[0:1, :]
    hb = h.astype(BF16)
    os_ref[...] = _dot(hb, ws_ref[...])
    or_ref[...] = _dot(hb, wr_ref[...])
    om_ref[...] = _dot(hb, wm_ref[...])


def _inproj(x2, nw, mod, layer, row_of_tile, tm, ws, wr, wm):
    n = x2.shape[0]
    full = lambda a: pl.BlockSpec(a.shape, lambda i: (0,) * a.ndim)
    return pl.pallas_call(
        _inproj_kernel,
        grid=(n // tm,),
        in_specs=[
            pl.BlockSpec((tm, D_MODEL), lambda i: (i, 0)),
            full(nw),
            _mod_spec(layer, row_of_tile),
            full(ws), full(wr), full(wm),
        ],
        out_specs=[
            pl.BlockSpec((tm, S5_WIDTH), lambda i: (i, 0)),
            pl.BlockSpec((tm, RWKV_COLS), lambda i: (i, 0)),
            pl.BlockSpec((tm, MLA_COLS), lambda i: (i, 0)),
        ],
        out_shape=[
            jax.ShapeDtypeStruct((n, S5_WIDTH), F32),
            jax.ShapeDtypeStruct((n, RWKV_COLS), F32),
            jax.ShapeDtypeStruct((n, MLA_COLS), F32),
        ],
        compiler_params=_cp("parallel"),
        name="in_projection",
    )(x2, nw, mod, ws, wr, wm)


def _s5_prep_kernel(are_ref, aim_ref, ldt_ref, bre_ref, bim_ref, cre_ref, cim_ref,
                    k_ref, pin_ref, poutt_ref, lam_ref):
    T = S5_CHUNK
    for d in range(2):
        are = are_ref[d:d + 1, :]
        aim = aim_ref[d:d + 1, :]
        dt = jnp.exp(ldt_ref[d:d + 1, :])
        lre = jnp.exp(are * dt) * jnp.cos(aim * dt)
        lim = jnp.exp(are * dt) * jnp.sin(aim * dt)
        den = are * are + aim * aim
        xr = lre - 1.0
        zre = (xr * are + lim * aim) / den
        zim = (lim * are - xr * aim) / den
        bre = bre_ref[d]
        bim = bim_ref[d]
        bbre = zre * bre - zim * bim
        bbim = zre * bim + zim * bre
        cre = cre_ref[d]
        cim = cim_ref[d]

        def powers(tau):
            mag = jnp.exp(tau * (are * dt))
            ang = tau * (aim * dt)
            return mag * jnp.cos(ang), mag * jnp.sin(ang)

        tau0 = lax.broadcasted_iota(jnp.int32, (T, 1), 0).astype(F32)
        ere, eim = powers(tau0)
        xre = (ere[:, None, :] * cre[None] - eim[:, None, :] * cim[None]).reshape(T * S5_CH, S5_STATE)
        xim = (ere[:, None, :] * cim[None] + eim[:, None, :] * cre[None]).reshape(T * S5_CH, S5_STATE)
        nt = (((1,), (1,)), ((), ()))
        k_ref[d] = (lax.dot_general(xre, bbre, nt, precision=lax.Precision.HIGHEST, preferred_element_type=F32)
                    - lax.dot_general(xim, bbim, nt, precision=lax.Precision.HIGHEST, preferred_element_type=F32))
        tau_out = tau0 + 1.0 if d == 0 else float(T) - tau0
        ore, oim = powers(tau_out)
        poutt_ref[2 * d] = (ore[:, None, :] * cre[None] - oim[:, None, :] * cim[None]).reshape(T * S5_CH, S5_STATE)
        poutt_ref[2 * d + 1] = -(ore[:, None, :] * cim[None] + oim[:, None, :] * cre[None]).reshape(T * S5_CH, S5_STATE)
        tau_in = float(T - 1) - tau0 if d == 0 else tau0
        ire, iim = powers(tau_in)
        pin_ref[2 * d] = (ire[:, None, :] * bbre[None] - iim[:, None, :] * bbim[None]).reshape(T * S5_CH, S5_STATE)
        pin_ref[2 * d + 1] = (ire[:, None, :] * bbim[None] + iim[:, None, :] * bbre[None]).reshape(T * S5_CH, S5_STATE)
        tre, tim = powers(jnp.full((1, 1), float(T), F32))
        lam_ref[2 * d:2 * d + 1, :] = tre
        lam_ref[2 * d + 1:2 * d + 2, :] = tim


def _s5_prep(a_re, a_im, log_dt, b_re, b_im, c_re, c_im):
    G, P, CH, T = S5_GROUPS, S5_STATE, S5_CH, S5_CHUNK
    g_first = lambda a: jnp.swapaxes(a, 0, 1)
    are = g_first(a_re)
    aim = g_first(a_im)
    ldt = g_first(log_dt)[..., None]
    bre = jnp.swapaxes(g_first(b_re), -1, -2)
    bim = jnp.swapaxes(g_first(b_im), -1, -2)
    cre = g_first(c_re)
    cim = g_first(c_im)
    spec3 = lambda s: pl.BlockSpec((None,) + s, lambda g: (g,) + (0,) * len(s))
    k, pin, poutt, lam = pl.pallas_call(
        _s5_prep_kernel,
        grid=(G,),
        in_specs=[spec3((2, P)), spec3((2, P)), spec3((2, 1)),
                  spec3((2, CH, P)), spec3((2, CH, P)), spec3((2, CH, P)), spec3((2, CH, P))],
        out_specs=[spec3((2, T * CH, CH)), spec3((4, T * CH, P)), spec3((4, T * CH, P)), spec3((4, P))],
        out_shape=[jax.ShapeDtypeStruct((G, 2, T * CH, CH), F32),
                   jax.ShapeDtypeStruct((G, 4, T * CH, P), F32),
                   jax.ShapeDtypeStruct((G, 4, T * CH, P), F32),
                   jax.ShapeDtypeStruct((G, 4, P), F32)],
        compiler_params=_cp("parallel"),
        name="s5_weight_prep",
    )(are, aim, ldt, bre, bim, cre, cim)
    k = k.reshape(G, 2, T, CH, CH)
    s_idx = jnp.arange(T)[:, None]
    t_idx = jnp.arange(T)[None, :]
    kf = jnp.where((t_idx >= s_idx)[None, :, :, None, None], k[:, 0][:, jnp.clip(t_idx - s_idx, 0, T - 1)], 0.0)
    kb = jnp.where((t_idx <= s_idx)[None, :, :, None, None], k[:, 1][:, jnp.clip(s_idx - t_idx, 0, T - 1)], 0.0)
    m = (kf + kb).transpose(0, 1, 4, 2, 3).reshape(G, T * CH, T * CH)
    pin_m = pin.transpose(0, 2, 1, 3).reshape(G, T * CH, 4 * P)
    pout_m = poutt.transpose(0, 1, 3, 2).reshape(G, 4 * P, T * CH)
    lre = jnp.concatenate([lam[:, 0], lam[:, 0], lam[:, 2], lam[:, 2]], axis=-1)
    lim = jnp.concatenate([-lam[:, 1], lam[:, 1], -lam[:, 3], lam[:, 3]], axis=-1)
    lam_rows = jnp.stack([lre, lim], axis=1)
    return m.astype(BF16), pin_m.astype(BF16), pout_m.astype(BF16), lam_rows


def _s5_scan_kernel(u_ref, m_ref, pin_ref, pout_ref, lam_ref, h0_ref, y_ref, hfin_ref, g_scr, hin_scr, *, n_chunks, bsz):
    ub = u_ref[...].astype(BF16)
    g_scr[...] = _dot(ub, pin_ref[...])
    lre = lam_ref[0:1, :]
    lim = lam_ref[1:2, :]

    def body(c, carry):
        hf, hb = carry
        rf = pl.multiple_of(c * bsz, bsz)
        rb = pl.multiple_of((n_chunks - 1 - c) * bsz, bsz)
        hin_scr[pl.ds(rf, bsz), 0:128] = hf
        hin_scr[pl.ds(rb, bsz), 128:256] = hb
        hf = lre[:, 0:128] * hf + lim[:, 0:128] * pltpu.roll(hf, 64, 1) + g_scr[pl.ds(rf, bsz), 0:128]
        hb = lre[:, 128:256] * hb + lim[:, 128:256] * pltpu.roll(hb, 64, 1) + g_scr[pl.ds(rb, bsz), 128:256]
        return hf, hb

    h0 = h0_ref[...]
    hf, hb = lax.fori_loop(0, n_chunks, body, (h0[:, 0:128], h0[:, 128:256]))
    hfin_ref[:, 0:128] = hf
    hfin_ref[:, 128:256] = hb
    hi, lo = _split_bf16(hin_scr[...])
    pout = pout_ref[...]
    y_ref[...] = _dot(ub, m_ref[...]) + _dot(hi, pout) + _dot(lo, pout)


def _s5_scan(u, h0, m, pin, pout, lam_rows):
    bsz, seq, _ = u.shape
    G, P, T = S5_GROUPS, S5_STATE, S5_CHUNK
    nc = seq // T
    rows = nc * bsz
    ug = u.reshape(bsz, nc, T, G, S5_CH).transpose(3, 1, 0, 2, 4).reshape(G, rows, S5_TC)
    h0g = h0.transpose(2, 0, 1, 4, 3).reshape(G, bsz, 4 * P)
    spec = lambda s: pl.BlockSpec((None,) + s, lambda g: (g,) + (0,) * len(s))
    y, hfin = pl.pallas_call(
        functools.partial(_s5_scan_kernel, n_chunks=nc, bsz=bsz),
        grid=(G,),
        in_specs=[spec((rows, S5_TC)), spec((S5_TC, S5_TC)), spec((S5_TC, 4 * P)), spec((4 * P, S5_TC)),
                  spec((2, 4 * P)), spec((bsz, 4 * P))],
        out_specs=[spec((rows, S5_TC)), spec((bsz, 4 * P))],
        out_shape=[jax.ShapeDtypeStruct((G, rows, S5_TC), F32), jax.ShapeDtypeStruct((G, bsz, 4 * P), F32)],
        scratch_shapes=[pltpu.VMEM((rows, 4 * P), F32), pltpu.VMEM((rows, 4 * P), F32)],
        compiler_params=_cp("parallel"),
        name="s5_chunk_scan",
    )(ug, m, pin, pout, lam_rows, h0g)
    y = y.reshape(G, nc, bsz, T, S5_CH).transpose(2, 1, 3, 0, 4).reshape(bsz, seq, S5_WIDTH)
    hfin = hfin.reshape(G, bsz, 2, 2, P).transpose(1, 2, 0, 4, 3)
    return y, hfin


def _s5_out_kernel(y_ref, u_ref, d_ref, w_ref, nw_ref, o_ref):
    y = y_ref[...] + u_ref[...] * d_ref[...]
    c = math.sqrt(2.0 / math.pi)
    y = y * (0.5 * (1.0 + jnp.tanh(c * (y + 0.044715 * (y * y * y)))))
    z = _dot(y.astype(BF16), w_ref[...])
    o = z[:, :S5_WIDTH] * jax.nn.sigmoid(z[:, S5_WIDTH:])
    o_ref[...] = _rms(o, nw_ref[...]).astype(BF16)


def _s5_out(y2, u2, d_skip, w_glu, nw, tm):
    n = y2.shape[0]
    full = lambda a: pl.BlockSpec(a.shape, lambda i: (0,) * a.ndim)
    row = pl.BlockSpec((tm, S5_WIDTH), lambda i: (i, 0))
    return pl.pallas_call(
        _s5_out_kernel,
        grid=(n // tm,),
        in_specs=[row, row, full(d_skip), full(w_glu), full(nw)],
        out_specs=row,
        out_shape=jax.ShapeDtypeStruct((n, S5_WIDTH), BF16),
        compiler_params=_cp("parallel"),
        name="s5_gelu_glu",
    )(y2, u2, d_skip, w_glu, nw)


def _rwkv_pre_kernel(p_ref, hp_ref, hn_ref, mu_ref, w0_ref, a0_ref, w2_ref, a2_ref, g2_ref,
                     kk_ref, ka_ref, rk_ref, bd_ref,
                     r_ref, v_ref, nk_ref, dec_ref, kd_ref, bb_ref, g_ref, bonus_ref, *, tiles_per_seq, tm):
    i = pl.program_id(0)
    j = i % tiles_per_seq
    p = p_ref[...]
    rows = lax.broadcasted_iota(jnp.int32, (tm, 1), 0)
    prev_edge = jnp.where(j == 0, 0.0, hp_ref[7:8, :])
    next_edge = jnp.where(j == tiles_per_seq - 1, 0.0, hn_ref[0:1, :])
    prev = jnp.where(rows == 0, prev_edge, pltpu.roll(p, 1, 0))
    nxt = jnp.where(rows == tm - 1, next_edge, pltpu.roll(p, tm - 1, 0))
    p = p + mu_ref[...] * (0.5 * (prev + nxt) - p)
    W = RWKV_WIDTH
    r = p[:, 0:W]
    k = p[:, W:2 * W]
    v = p[:, 2 * W:3 * W]
    wl = p[:, 3 * W:3 * W + LORA_PAD]
    al = p[:, 3 * W + LORA_PAD:3 * W + 2 * LORA_PAD]
    gl = p[:, 3 * W + 2 * LORA_PAD:3 * W + 3 * LORA_PAD]
    bd = bd_ref[...]
    kk = k * kk_ref[...]
    kk = kk * lax.rsqrt(_segsum(kk * kk, bd) + 1e-12)
    g_ref[...] = _dot(jax.nn.sigmoid(gl).astype(BF16), g2_ref[...])
    tw = jnp.tanh(wl).astype(BF16)
    alb = al.astype(BF16)
    ksum = None
    for d in range(2):
        z = -(w0_ref[d:d + 1, :] + _dot(tw, w2_ref[d]))
        w = -(jnp.maximum(z, 0.0) + jnp.log(1.0 + jnp.exp(-jnp.abs(z)))) - 0.5
        dec_ref[d] = jnp.exp(-jnp.exp(w))
        a = jax.nn.sigmoid(a0_ref[d:d + 1, :] + _dot(alb, a2_ref[d]))
        kd = k * (1.0 + (a - 1.0) * ka_ref[...])
        kd_ref[d] = kd
        bb_ref[d] = kk * a
        ksum = kd if ksum is None else ksum + kd
    r_ref[...] = r
    v_ref[...] = v
    nk_ref[...] = -kk
    bonus_ref[...] = _segsum(r * ksum * rk_ref[...], bd) * v


def _rwkv_pre(p2, bsz, seq, wts, bd):
    n = p2.shape[0]
    tm = min(256, seq)
    tps = seq // tm
    r8 = tm // 8
    nblk8 = n // 8
    full = lambda a: pl.BlockSpec(a.shape, lambda i: (0,) * a.ndim)
    o3 = pl.BlockSpec((None, tm, RWKV_WIDTH), lambda i: (i // tps, i % tps, 0))
    o4 = pl.BlockSpec((2, None, tm, RWKV_WIDTH), lambda i: (0, i // tps, i % tps, 0))
    s3 = jax.ShapeDtypeStruct((bsz, seq, RWKV_WIDTH), F32)
    s4 = jax.ShapeDtypeStruct((2, bsz, seq, RWKV_WIDTH), F32)
    names = ("mu", "w0", "a0", "w2", "a2", "g2", "k_k", "k_a", "r_k")
    return pl.pallas_call(
        functools.partial(_rwkv_pre_kernel, tiles_per_seq=tps, tm=tm),
        grid=(n // tm,),
        in_specs=[pl.BlockSpec((tm, RWKV_COLS), lambda i: (i, 0)),
                  pl.BlockSpec((8, RWKV_COLS), lambda i: (jnp.maximum(i * r8 - 1, 0), 0)),
                  pl.BlockSpec((8, RWKV_COLS), lambda i: (jnp.minimum((i + 1) * r8, nblk8 - 1), 0))]
                 + [full(wts[k]) for k in names] + [full(bd)],
        out_specs=[o3, o3, o3, o4, o4, o4, o3, o3],
        out_shape=[s3, s3, s3, s4, s4, s4, s3, s3],
        compiler_params=_cp("parallel"),
        name="rwkv_prepare",
    )(p2, p2, p2, *[wts[k] for k in names], bd)


def _rwkv_scan_kernel(r_ref, v_ref, nk_ref, dec_ref, kd_ref, bb_ref, s0_ref, bd_ref, y_ref, sfin_ref, s_scr,
                      *, nb, tlen, n_chunks):
    d = pl.program_id(0)
    c = pl.program_id(2)

    @pl.when(c == 0)
    def _():
        s_scr[...] = s0_ref[...]

    bd = bd_ref[...]
    N = RWKV_HEAD_DIM
    W = RWKV_WIDTH
    lane = lax.broadcasted_iota(jnp.int32, (N, W), 1)
    sub = lax.broadcasted_iota(jnp.int32, (N, W), 0)
    eye = (lane % N == sub).astype(F32)[None]

    def seg(x3):
        return _segsum_bf16(x3.reshape(nb * N, W), bd).reshape(nb, N, W)

    def step(t, carry):
        tt = t + d * (tlen - 1 - 2 * t)
        row = lambda ref: ref[:, pl.ds(tt, 1), :]
        s = s_scr[...]
        sa = seg((s * row(nk_ref)).astype(BF16))
        vb = seg((eye * row(v_ref)).astype(BF16))
        s = s * row(dec_ref) + sa * row(bb_ref) + vb * row(kd_ref)
        s_scr[...] = s
        yb = seg((s * row(r_ref)).astype(BF16))
        y_ref[:, pl.ds(tt, 1), :] = jnp.sum(eye * yb, axis=1, keepdims=True)
        return carry

    lax.fori_loop(0, tlen, step, 0)

    @pl.when(c == n_chunks - 1)
    def _():
        sfin_ref[...] = s_scr[...]


def _rwkv_scan(r, v, nk, dec, kd, bb, s0, bd):
    bsz, seq, W = r.shape
    N = RWKV_HEAD_DIM
    nb = 4 if bsz % 4 == 0 else 2
    tlen = min(256, seq)
    nc = seq // tlen
    cc = lambda d, c: c + d * (nc - 1 - 2 * c)
    shared = pl.BlockSpec((nb, tlen, W), lambda d, b, c: (b, cc(d, c), 0))
    per_dir = pl.BlockSpec((None, nb, tlen, W), lambda d, b, c: (d, b, cc(d, c), 0))
    st = pl.BlockSpec((None, nb, N, W), lambda d, b, c: (d, b, 0, 0))
    return pl.pallas_call(
        functools.partial(_rwkv_scan_kernel, nb=nb, tlen=tlen, n_chunks=nc),
        grid=(2, bsz // nb, nc),
        in_specs=[shared, shared, shared, per_dir, per_dir, per_dir, st,
                  pl.BlockSpec(bd.shape, lambda d, b, c: (0, 0))],
        out_specs=[per_dir, st],
        out_shape=[jax.ShapeDtypeStruct((2, bsz, seq, W), F32), jax.ShapeDtypeStruct((2, bsz, N, W), F32)],
        scratch_shapes=[pltpu.VMEM((nb, N, W), F32)],
        compiler_params=_cp("parallel", "parallel", "arbitrary"),
        name="rwkv_scan",
    )(r, v, nk, dec, kd, bb, s0, bd)


def _rwkv_post_kernel(y_ref, bonus_ref, g_ref, lw_ref, lb_ref, bd_ref, o_ref):
    bd = bd_ref[...]
    y = y_ref[0] + y_ref[1]
    inv_n = 1.0 / RWKV_HEAD_DIM
    mean = _segsum(y, bd) * inv_n
    yc = y - mean
    var = _segsum(yc * yc, bd) * inv_n
    yn = yc * lax.rsqrt(var + GN_EPS) * lw_ref[...] + lb_ref[...]
    o_ref[...] = ((yn + bonus_ref[...]) * g_ref[...]).astype(BF16)


def _rwkv_post(y, bonus, g, ln_w, ln_b, bd):
    _, bsz, seq, W = y.shape
    tm = min(256, seq)
    tps = seq // tm
    n = bsz * seq
    full = lambda a: pl.BlockSpec(a.shape, lambda i: (0,) * a.ndim)
    i3 = pl.BlockSpec((None, tm, W), lambda i: (i // tps, i % tps, 0))
    return pl.pallas_call(
        _rwkv_post_kernel,
        grid=(n // tm,),
        in_specs=[pl.BlockSpec((2, None, tm, W), lambda i: (0, i // tps, i % tps, 0)), i3, i3,
                  full(ln_w), full(ln_b), full(bd)],
        out_specs=pl.BlockSpec((tm, W), lambda i: (i, 0)),
        out_shape=jax.ShapeDtypeStruct((n, W), BF16),
        compiler_params=_cp("parallel"),
        name="rwkv_groupnorm_gate",
    )(y, bonus, g, ln_w, ln_b, bd)


def _store_k_heads(k_ref, kn, kpe):
    for h in range(MLA_HEADS):
        k_ref[:, h * MLA_QK_PAD:h * MLA_QK_PAD + 128] = kn[:, h * 128:(h + 1) * 128].astype(BF16)
        k_ref[:, h * MLA_QK_PAD + 128:(h + 1) * MLA_QK_PAD] = kpe


def _mla_prep_kernel(*refs, rope):
    if rope:
        (p_ref, qn_ref, kvn_ref, wq_ref, wqr_ref, wk_ref, wv_ref, cq_ref, sq_ref, ck_ref, sk_ref,
         q_ref, k_ref, v_ref) = refs
    else:
        (p_ref, qn_ref, kvn_ref, wq_ref, wk_ref, wv_ref, q_ref, k_ref, v_ref, ckv_ref, kr_ref) = refs
    p = p_ref[...]
    qn = _rms(p[:, 0:MLA_Q_RANK], qn_ref[...]).astype(BF16)
    q = _dot(qn, wq_ref[...])
    ckv = _rms(p[:, MLA_Q_RANK:MLA_Q_RANK + MLA_KV_RANK], kvn_ref[...])
    kr = p[:, MLA_Q_RANK + MLA_KV_RANK:MLA_Q_RANK + MLA_KV_RANK + 128]
    if rope:
        cq = jnp.concatenate([cq_ref[...]] * MLA_HEADS, axis=1)
        sq = jnp.concatenate([sq_ref[...]] * MLA_HEADS, axis=1)
        q = q * cq + _dot(qn, wqr_ref[...]) * sq
        krot = p[:, MLA_Q_RANK + MLA_KV_RANK + 128:MLA_Q_RANK + MLA_KV_RANK + 256]
        kpe = kr * ck_ref[...] + krot * sk_ref[...]
    else:
        kpe = kr
        ckv_ref[...] = ckv
        kr_ref[...] = kr[:, 0:MLA_ROPE_DIM]
    q_ref[...] = q.astype(BF16)
    cb = ckv.astype(BF16)
    _store_k_heads(k_ref, _dot(cb, wk_ref[...]), kpe.astype(BF16))
    v_ref[...] = _dot(cb, wv_ref[...]).astype(BF16)


def _mla_prep(p2, seq, wts, tables):
    n = p2.shape[0]
    rope = tables is not None
    tm = min(256, seq)
    tps = seq // tm
    full = lambda a: pl.BlockSpec(a.shape, lambda i: (0,) * a.ndim)
    row = lambda w: pl.BlockSpec((tm, w), lambda i: (i, 0))
    ins = [p2, wts["q_norm"], wts["kv_norm"], wts["wq"]]
    specs = [row(MLA_COLS), full(wts["q_norm"]), full(wts["kv_norm"]), full(wts["wq"])]
    if rope:
        ins.append(wts["wq_rot"])
        specs.append(full(wts["wq_rot"]))
    ins += [wts["wk"], wts["wv"]]
    specs += [full(wts["wk"]), full(wts["wv"])]
    outs = [row(MLA_HEADS * MLA_QK_PAD), row(MLA_HEADS * MLA_QK_PAD), row(MLA_HEADS * MLA_V_DIM)]
    shapes = [jax.ShapeDtypeStruct((n, MLA_HEADS * MLA_QK_PAD), BF16),
              jax.ShapeDtypeStruct((n, MLA_HEADS * MLA_QK_PAD), BF16),
              jax.ShapeDtypeStruct((n, MLA_HEADS * MLA_V_DIM), BF16)]
    if rope:
        ins += list(tables)
        specs += [pl.BlockSpec((tm, t.shape[1]), lambda i: (i % tps, 0)) for t in tables]
    else:
        outs += [row(MLA_KV_RANK), row(MLA_ROPE_DIM)]
        shapes += [jax.ShapeDtypeStruct((n, MLA_KV_RANK), F32), jax.ShapeDtypeStruct((n, MLA_ROPE_DIM), F32)]
    return pl.pallas_call(
        functools.partial(_mla_prep_kernel, rope=rope),
        grid=(n // tm,),
        in_specs=specs, out_specs=outs, out_shape=shapes,
        compiler_params=_cp("parallel"),
        name="mla_prepare_rope" if rope else "mla_prepare",
    )(*ins)


def _mla_cache_kernel(ckv_ref, kr_ref, wk_ref, wv_ref, k_ref, v_ref):
    cb = ckv_ref[...].astype(BF16)
    _store_k_heads(k_ref, _dot(cb, wk_ref[...]), kr_ref[...].astype(BF16))
    v_ref[...] = _dot(cb, wv_ref[...]).astype(BF16)


def _mla_cache(cache_ckv, cache_kr_pad, layer, wk, wv):
    bsz, _, past, _ = cache_ckv.shape
    full = lambda a: pl.BlockSpec(a.shape, lambda b: (0,) * a.ndim)
    return pl.pallas_call(
        _mla_cache_kernel,
        grid=(bsz,),
        in_specs=[pl.BlockSpec((None, None, past, MLA_KV_RANK), lambda b: (b, layer, 0, 0)),
                  pl.BlockSpec((None, None, past, 128), lambda b: (b, layer, 0, 0)),
                  full(wk), full(wv)],
        out_specs=[pl.BlockSpec((None, past, MLA_HEADS * MLA_QK_PAD), lambda b: (b, 0, 0)),
                   pl.BlockSpec((None, past, MLA_HEADS * MLA_V_DIM), lambda b: (b, 0, 0))],
        out_shape=[jax.ShapeDtypeStruct((bsz, past, MLA_HEADS * MLA_QK_PAD), BF16),
                   jax.ShapeDtypeStruct((bsz, past, MLA_HEADS * MLA_V_DIM), BF16)],
        compiler_params=_cp("parallel"),
        name="mla_cache_keys",
    )(cache_ckv, cache_kr_pad, wk, wv)


def _attn_kernel(*refs, cache, scale):
    if cache:
        q_ref, k_ref, v_ref, kc_ref, vc_ref, o_ref = refs
    else:
        q_ref, k_ref, v_ref, o_ref = refs
    nt = (((1,), (1,)), ((), ()))
    q = q_ref[...]
    s = lax.dot_general(q, k_ref[...], nt, preferred_element_type=F32) * scale
    m = jnp.max(s, axis=-1, keepdims=True)
    if cache:
        sc = lax.dot_general(q, kc_ref[...], nt, preferred_element_type=F32) * scale
        m = jnp.maximum(m, jnp.max(sc, axis=-1, keepdims=True))
    e = jnp.exp(s - m)
    den = jnp.sum(e, axis=-1, keepdims=True)
    o = _dot(e.astype(BF16), v_ref[...])
    if cache:
        ec = jnp.exp(sc - m)
        den = den + jnp.sum(ec, axis=-1, keepdims=True)
        o = o + _dot(ec.astype(BF16), vc_ref[...])
    o_ref[...] = o / den


def _attention(q, k, v, kc=None, vc=None):
    bsz, seq, _ = q.shape
    tq = min(256, seq)
    cache = kc is not None
    scale = float(MLA_NOPE_DIM + MLA_ROPE_DIM) ** -0.5
    ins = [q, k, v]
    specs = [pl.BlockSpec((None, tq, MLA_QK_PAD), lambda b, h, i: (b, i, h)),
             pl.BlockSpec((None, seq, MLA_QK_PAD), lambda b, h, i: (b, 0, h)),
             pl.BlockSpec((None, seq, MLA_V_DIM), lambda b, h, i: (b, 0, h))]
    if cache:
        past = kc.shape[1]
        ins += [kc, vc]
        specs += [pl.BlockSpec((None, past, MLA_QK_PAD), lambda b, h, i: (b, 0, h)),
                  pl.BlockSpec((None, past, MLA_V_DIM), lambda b, h, i: (b, 0, h))]
    return pl.pallas_call(
        functools.partial(_attn_kernel, cache=cache, scale=scale),
        grid=(bsz, MLA_HEADS, seq // tq),
        in_specs=specs,
        out_specs=pl.BlockSpec((None, tq, MLA_V_DIM), lambda b, h, i: (b, i, h)),
        out_shape=jax.ShapeDtypeStruct((bsz, seq, MLA_HEADS * MLA_V_DIM), F32),
        compiler_params=_cp("parallel", "parallel", "arbitrary"),
        name="mla_attention_cached" if cache else "mla_attention",
    )(*ins)


def _outproj_kernel(ys_ref, yr_ref, ym_ref, x_ref, mod_ref, nm_ref, w_ref, o_ref):
    ym = _rms(ym_ref[...], nm_ref[...]).astype(BF16)
    acc = _dot(ys_ref[...], w_ref[0:S5_WIDTH, :])
    acc += _dot(yr_ref[...], w_ref[S5_WIDTH:S5_WIDTH + RWKV_WIDTH, :])
    acc += _dot(ym, w_ref[S5_WIDTH + RWKV_WIDTH:, :])
    o_ref[...] = x_ref[...] + mod_ref[2:3, :] * acc


def _outproj(ys, yr, ym, x2, mod, layer, row_of_tile, tm, nm, w_out):
    n = x2.shape[0]
    full = lambda a: pl.BlockSpec(a.shape, lambda i: (0,) * a.ndim)
    row = lambda w: pl.BlockSpec((tm, w), lambda i: (i, 0))
    return pl.pallas_call(
        _outproj_kernel,
        grid=(n // tm,),
        in_specs=[row(S5_WIDTH), row(RWKV_WIDTH), row(MLA_WIDTH), row(D_MODEL), _mod_spec(layer, row_of_tile),
                  full(nm), full(w_out)],
        out_specs=row(D_MODEL),
        out_shape=jax.ShapeDtypeStruct((n, D_MODEL), F32),
        compiler_params=_cp("parallel"),
        name="out_projection",
    )(ys, yr, ym, x2, mod, nm, w_out)


def _mlp_kernel(x_ref, nw_ref, mod_ref, w1_ref, w2_ref, nf_ref, o_ref, h_scr, acc_scr, *, final_norm):
    j = pl.program_id(1)

    @pl.when(j == 0)
    def _():
        h = _rms(x_ref[...], nw_ref[...]) * (1.0 + mod_ref[4:5, :]) + mod_ref[3:4, :]
        h_scr[...] = h.astype(BF16)
        acc_scr[...] = jnp.zeros_like(acc_scr)

    a = _dot(h_scr[...], w1_ref[...])
    a = jnp.square(jnp.maximum(a, 0.0)).astype(BF16)
    acc_scr[...] += _dot(a, w2_ref[...])

    @pl.when(j == pl.num_programs(1) - 1)
    def _():
        y = x_ref[...] + mod_ref[5:6, :] * acc_scr[...]
        if final_norm:
            y = _rms(y, nf_ref[...])
        o_ref[...] = y


def _mlp(x2, nw, mod, layer, row_of_tile, tm, w1, w2, nf, final_norm):
    n = x2.shape[0]
    tf = 1024
    full = lambda a: pl.BlockSpec(a.shape, lambda i, j: (0,) * a.ndim)
    return pl.pallas_call(
        functools.partial(_mlp_kernel, final_norm=final_norm),
        grid=(n // tm, D_FF // tf),
        in_specs=[pl.BlockSpec((tm, D_MODEL), lambda i, j: (i, 0)), full(nw), _mod_spec(layer, row_of_tile),
                  pl.BlockSpec((D_MODEL, tf), lambda i, j: (0, j)),
                  pl.BlockSpec((tf, D_MODEL), lambda i, j: (j, 0)), full(nf)],
        out_specs=pl.BlockSpec((tm, D_MODEL), lambda i, j: (i, 0)),
        out_shape=jax.ShapeDtypeStruct((n, D_MODEL), F32),
        scratch_shapes=[pltpu.VMEM((tm, D_MODEL), BF16), pltpu.VMEM((tm, D_MODEL), F32)],
        compiler_params=_cp("parallel", "arbitrary"),
        name="mlp_final" if final_norm else "mlp",
    )(x2, nw, mod, w1, w2, nf)


def _rope_tables(length):
    rows = length // GRID_W
    row_pos = jnp.repeat(jnp.arange(rows, dtype=F32), GRID_W)
    col_pos = jnp.tile(jnp.arange(GRID_W, dtype=F32), rows)
    axis_dim = MLA_ROPE_DIM // 2
    inv_freq = 1.0 / (ROPE_THETA ** (jnp.arange(0, axis_dim, 2, dtype=F32) / axis_dim))
    ang_r = row_pos[:, None] * inv_freq[None, :]
    ang_c = col_pos[:, None] * inv_freq[None, :]
    ang = jnp.concatenate([ang_r, ang_r, ang_c, ang_c], axis=-1)
    cos, sin = jnp.cos(ang), jnp.sin(ang)
    z64 = jnp.zeros((length, 64), F32)
    cos_q = jnp.concatenate([jnp.ones((length, MLA_NOPE_DIM), F32), cos, z64], axis=1)
    sin_q = jnp.concatenate([jnp.zeros((length, MLA_NOPE_DIM), F32), sin, z64], axis=1)
    cos_k = jnp.concatenate([cos, z64], axis=1)
    sin_k = jnp.concatenate([sin, z64], axis=1)
    return cos_q, sin_q, cos_k, sin_k


def _rot_cols(w):
    a, b, c, d = w[..., 0:16], w[..., 16:32], w[..., 32:48], w[..., 48:64]
    return jnp.concatenate([-b, a, -d, c], axis=-1)


def _layer_weights(l, p):
    d = D_MODEL
    w_in = p["w_in"][l]
    z64 = jnp.zeros((d, 64), F32)
    o = S5_WIDTH
    rk = w_in[:, o:o + 3 * RWKV_WIDTH]
    o += 3 * RWKV_WIDTH
    wl, al, gl = w_in[:, o:o + 64], w_in[:, o + 64:o + 128], w_in[:, o + 128:o + 256]
    o += 256
    cq, ckv, kr = w_in[:, o:o + 512], w_in[:, o + 512:o + 768], w_in[:, o + 768:o + 832]
    mu = p["rwkv_mu"][l]
    z1 = jnp.zeros((64,), F32)
    mu_pad = jnp.concatenate([mu[:1536], mu[1536:1600], z1, mu[1600:1664], z1, mu[1664:1792]])[None]
    pad_rows = lambda w: jnp.concatenate([w, jnp.zeros_like(w)], axis=-2)
    w_uq = p["mla_w_uq"][l].reshape(MLA_Q_RANK, MLA_HEADS, MLA_NOPE_DIM + MLA_ROPE_DIM)
    zq = jnp.zeros((MLA_Q_RANK, MLA_HEADS, 64), F32)
    wq = jnp.concatenate([w_uq, zq], axis=-1).reshape(MLA_Q_RANK, MLA_HEADS * MLA_QK_PAD)
    wq_rot = jnp.concatenate([jnp.zeros((MLA_Q_RANK, MLA_HEADS, MLA_NOPE_DIM), F32),
                              _rot_cols(w_uq[..., MLA_NOPE_DIM:]), zq], axis=-1).reshape(MLA_Q_RANK, MLA_HEADS * MLA_QK_PAD)
    w_ukv = p["mla_w_ukv"][l].reshape(MLA_KV_RANK, MLA_HEADS, MLA_NOPE_DIM + MLA_V_DIM)
    row = lambda a: a.reshape(1, -1)
    return {
        "norm_mix": row(p["norm_mix"][l]), "norm_mlp": row(p["norm_mlp"][l]),
        "w_s5": w_in[:, 0:S5_WIDTH].astype(BF16),
        "w_rwkv": jnp.concatenate([rk, wl, z64, al, z64, gl], axis=1).astype(BF16),
        "w_mla": jnp.concatenate([cq, ckv, kr, z64, _rot_cols(kr), z64], axis=1).astype(BF16),
        "w_out": p["w_out"][l].astype(BF16),
        "s5_d": row(p["s5_d"][l]), "s5_w_glu": p["s5_w_glu"][l].astype(BF16), "s5_out_norm": row(p["s5_out_norm"][l]),
        "rwkv": {
            "mu": mu_pad, "w0": p["rwkv_w0"][l], "a0": p["rwkv_a0"][l],
            "w2": pad_rows(p["rwkv_w2"][l]).astype(BF16), "a2": pad_rows(p["rwkv_a2"][l]).astype(BF16),
            "g2": p["rwkv_g2"][l].astype(BF16), "k_k": row(p["rwkv_k_k"][l]), "k_a": row(p["rwkv_k_a"][l]),
            "r_k": row(p["rwkv_r_k"][l]),
        },
        "rwkv_ln_w": row(p["rwkv_ln_w"][l]), "rwkv_ln_b": row(p["rwkv_ln_b"][l]),
        "mla": {
            "q_norm": row(p["mla_q_norm"][l]), "kv_norm": row(p["mla_kv_norm"][l]),
            "wq": wq.astype(BF16), "wq_rot": wq_rot.astype(BF16),
            "wk": w_ukv[..., :MLA_NOPE_DIM].reshape(MLA_KV_RANK, -1).astype(BF16),
            "wv": w_ukv[..., MLA_NOPE_DIM:].reshape(MLA_KV_RANK, -1).astype(BF16),
        },
        "mla_out_norm": row(p["mla_out_norm"][l]),
        "mlp_w1": p["mlp_w1"][l].astype(BF16), "mlp_w2": p["mlp_w2"][l].astype(BF16),
    }


def _trunk_layer(x, mod, layer, lw, s5w, bd, row_of_tile_fn, s5_h0, rwkv_s0, cache, tables, norm_final, final_norm):
    bsz, seq, d = x.shape
    n = bsz * seq
    x2 = x.reshape(n, d)
    tm = min(512, seq) if cache is not None else min(512, n)
    row_of_tile = row_of_tile_fn(tm)
    u2, pr2, pm2 = _inproj(x2, lw["norm_mix"], mod, layer, row_of_tile, tm, lw["w_s5"], lw["w_rwkv"], lw["w_mla"])

    ys, s5_fin = _s5_scan(u2.reshape(bsz, seq, S5_WIDTH), s5_h0, *s5w)
    ys2 = _s5_out(ys.reshape(n, S5_WIDTH), u2, lw["s5_d"], lw["s5_w_glu"], lw["s5_out_norm"], min(512, n))

    r, v, nk, dec, kd, bb, g, bonus = _rwkv_pre(pr2, bsz, seq, lw["rwkv"], bd)
    s0 = rwkv_s0.transpose(1, 0, 3, 2, 4).reshape(2, bsz, RWKV_HEAD_DIM, RWKV_WIDTH)
    yr, s_fin = _rwkv_scan(r, v, nk, dec, kd, bb, s0, bd)
    yr2 = _rwkv_post(yr, bonus, g, lw["rwkv_ln_w"], lw["rwkv_ln_b"], bd)
    rwkv_fin = s_fin.reshape(2, bsz, RWKV_HEAD_DIM, RWKV_HEADS, RWKV_HEAD_DIM).transpose(1, 0, 3, 2, 4)

    shape3 = lambda a: a.reshape(bsz, seq, a.shape[-1])
    if cache is None:
        q, k, v_, ckv_n, k_rope = _mla_prep(pm2, seq, lw["mla"], None)
        ym = _attention(shape3(q), shape3(k), shape3(v_))
        extras = (shape3(ckv_n), shape3(k_rope), s5_fin, rwkv_fin)
    else:
        q, k, v_ = _mla_prep(pm2, seq, lw["mla"], tables)
        kc, vc = _mla_cache(cache[0], cache[1], layer, lw["mla"]["wk"], lw["mla"]["wv"])
        ym = _attention(shape3(q), shape3(k), shape3(v_), kc, vc)
        extras = None

    x2 = _outproj(ys2, yr2, ym.reshape(n, MLA_WIDTH), x2, mod, layer, row_of_tile, tm, lw["mla_out_norm"], lw["w_out"])
    x2 = _mlp(x2, lw["norm_mlp"], mod, layer, row_of_tile, tm, lw["mlp_w1"], lw["mlp_w2"], norm_final, final_norm)
    return x2.reshape(bsz, seq, d), extras


def kernel(x_prompt, x_sample, cache_mla_ckv, cache_mla_krope, state_s5, state_rwkv, c, c_ctx, norm_mix, norm_mlp, norm_final, w_ada, b_ada, w_in, w_out, s5_a_re, s5_a_im, s5_log_dt, s5_b_re, s5_b_im, s5_c_re, s5_c_im, s5_d, s5_w_glu, s5_out_norm, rwkv_mu, rwkv_w0, rwkv_w2, rwkv_a0, rwkv_a2, rwkv_g2, rwkv_k_k, rwkv_k_a, rwkv_r_k, rwkv_ln_w, rwkv_ln_b, mla_q_norm, mla_w_uq, mla_kv_norm, mla_w_ukv, mla_out_norm, mlp_w1, mlp_w2):
    p = dict(norm_mix=norm_mix, norm_mlp=norm_mlp, w_in=w_in, w_out=w_out, s5_d=s5_d, s5_w_glu=s5_w_glu,
             s5_out_norm=s5_out_norm, rwkv_mu=rwkv_mu, rwkv_w0=rwkv_w0, rwkv_w2=rwkv_w2, rwkv_a0=rwkv_a0,
             rwkv_a2=rwkv_a2, rwkv_g2=rwkv_g2, rwkv_k_k=rwkv_k_k, rwkv_k_a=rwkv_k_a, rwkv_r_k=rwkv_r_k,
             rwkv_ln_w=rwkv_ln_w, rwkv_ln_b=rwkv_ln_b, mla_q_norm=mla_q_norm, mla_w_uq=mla_w_uq,
             mla_kv_norm=mla_kv_norm, mla_w_ukv=mla_w_ukv, mla_out_norm=mla_out_norm, mlp_w1=mlp_w1, mlp_w2=mlp_w2)
    depth = w_in.shape[0]
    b_ctx, l_ctx, d = x_prompt.shape
    b_dec, l_dec, _ = x_sample.shape

    rows = -(-(1 + b_dec) // 8) * 8
    cond = jnp.zeros((rows, d), F32).at[0].set(c_ctx).at[1:1 + b_dec].set(c)
    mod = _modulation(cond, w_ada, b_ada).reshape(depth, rows, N_MOD, d)

    bd = jnp.kron(jnp.eye(4, dtype=F32), jnp.ones((64, 64), F32)).astype(BF16)
    tables = _rope_tables(l_dec)
    kr_pad = jnp.pad(cache_mla_krope, ((0, 0), (0, 0), (0, 0), (0, 128 - MLA_ROPE_DIM)))
    zero_s5 = jnp.zeros((b_ctx, 2, S5_GROUPS, S5_STATE, 2), F32)
    zero_rwkv = jnp.zeros((b_ctx, 2, RWKV_HEADS, RWKV_HEAD_DIM, RWKV_HEAD_DIM), F32)
    nf = norm_final.reshape(1, d)

    ctx_rows = lambda tm: (lambda i: 0)
    dec_rows = lambda tm: (lambda i: 1 + i // (l_dec // tm))

    xp, xs = x_prompt, x_sample
    new_ckv, new_krope, new_s5, new_rwkv = [], [], [], []
    for l in range(depth):
        lw = _layer_weights(l, p)
        s5w = _s5_prep(s5_a_re[l], s5_a_im[l], s5_log_dt[l], s5_b_re[l], s5_b_im[l], s5_c_re[l], s5_c_im[l])
        last = l == depth - 1
        xp, (ckv_l, krope_l, s5_l, rwkv_l) = _trunk_layer(
            xp, mod, l, lw, s5w, bd, ctx_rows, zero_s5, zero_rwkv, None, None, nf, last)
        new_ckv.append(ckv_l)
        new_krope.append(krope_l)
        new_s5.append(s5_l)
        new_rwkv.append(rwkv_l)
        xs, _ = _trunk_layer(
            xs, mod, l, lw, s5w, bd, dec_rows, state_s5[:, l], state_rwkv[:, l], (cache_mla_ckv, kr_pad), tables, nf, last)
    return (xp, xs, jnp.stack(new_ckv, axis=1), jnp.stack(new_krope, axis=1),
            jnp.stack(new_s5, axis=1), jnp.stack(new_rwkv, axis=1))
```

```python
import functools
import math

import jax
import jax.numpy as jnp
from jax import lax
from jax.experimental import pallas as pl
from jax.experimental.pallas import tpu as pltpu

F32 = jnp.float32
BF16 = jnp.bfloat16

D_MODEL = 2048
N_MOD = 6
GRID_W = 64
S5_WIDTH = 512
S5_CH = 16
S5_GROUPS = 32
S5_STATE = 64
S5_CHUNK = 16
S5_TC = S5_CHUNK * S5_CH
RWKV_WIDTH = 512
RWKV_HEAD_DIM = 64
RWKV_HEADS = 8
LORA_PAD = 128
RWKV_COLS = 3 * RWKV_WIDTH + 3 * LORA_PAD
MLA_HEADS = 8
MLA_V_DIM = 128
MLA_NOPE_DIM = 128
MLA_ROPE_DIM = 64
MLA_QK_PAD = 256
MLA_Q_RANK = 512
MLA_KV_RANK = 256
MLA_WIDTH = 1024
MLA_COLS = MLA_Q_RANK + MLA_KV_RANK + 2 * 128
D_FF = 8192
ROPE_THETA = 10000.0
NORM_EPS = 1e-6
GN_EPS = 64e-5

VMEM_LIMIT_BYTES = 56 * 1024 * 1024


def _cp(*sem):
    return pltpu.CompilerParams(dimension_semantics=sem, vmem_limit_bytes=VMEM_LIMIT_BYTES)


def _dot(a, b):
    return jnp.dot(a, b, preferred_element_type=F32)


def _rms(x, g):
    ms = jnp.mean(x * x, axis=-1, keepdims=True)
    return x * lax.rsqrt(ms + NORM_EPS) * g


def _split_bf16(x):
    hi = x.astype(BF16)
    lo = (x - hi.astype(F32)).astype(BF16)
    return hi, lo


def _segsum(x, bd):
    hi, lo = _split_bf16(x)
    left = _dot(hi[:, :256], bd) + _dot(lo[:, :256], bd)
    right = _dot(hi[:, 256:], bd) + _dot(lo[:, 256:], bd)
    return jnp.concatenate([left, right], axis=1)


def _segsum_bf16(xb, bd):
    return jnp.concatenate([_dot(xb[:, :256], bd), _dot(xb[:, 256:], bd)], axis=1)


def _mod_kernel(c_ref, w_ref, b_ref, o_ref):
    c = c_ref[...]
    s = (c * jax.nn.sigmoid(c)).astype(BF16)
    o_ref[...] = _dot(s, w_ref[...].astype(BF16)) + b_ref[...]


def _modulation(cond, w_ada, b_ada):
    depth, d, n = w_ada.shape
    rows = cond.shape[0]
    tn = 1024
    return pl.pallas_call(
        _mod_kernel,
        grid=(depth, n // tn),
        in_specs=[
            pl.BlockSpec((rows, d), lambda l, j: (0, 0)),
            pl.BlockSpec((None, d, tn), lambda l, j: (l, 0, j)),
            pl.BlockSpec((None, 1, tn), lambda l, j: (l, 0, j)),
        ],
        out_specs=pl.BlockSpec((None, rows, tn), lambda l, j: (l, 0, j)),
        out_shape=jax.ShapeDtypeStruct((depth, rows, n), F32),
        compiler_params=_cp("parallel", "arbitrary"),
        name="adaln_modulation",
    )(cond, w_ada, b_ada.reshape(depth, 1, n))


def _mod_spec(layer, row_of_tile):
    return pl.BlockSpec((None, None, N_MOD, D_MODEL), lambda i, *_: (layer, row_of_tile(i), 0, 0))


def _inproj_kernel(x_ref, nw_ref, mod_ref, ws_ref, wr_ref, wm_ref, os_ref, or_ref, om_ref):
    h = _rms(x_ref[...], nw_ref[...]) * (1.0 + mod_ref[1:2, :]) + mod_ref[0:1, :]
    hb = h.astype(BF16)
    os_ref[...] = _dot(hb, ws_ref[...])
    or_ref[...] = _dot(hb, wr_ref[...])
    om_ref[...] = _dot(hb, wm_ref[...])


def _inproj(x2, nw, mod, layer, row_of_tile, tm, ws, wr, wm):
    n = x2.shape[0]
    full = lambda a: pl.BlockSpec(a.shape, lambda i: (0,) * a.ndim)
    return pl.pallas_call(
        _inproj_kernel,
        grid=(n // tm,),
        in_specs=[
            pl.BlockSpec((tm, D_MODEL), lambda i: (i, 0)),
            full(nw),
            _mod_spec(layer, row_of_tile),
            full(ws), full(wr), full(wm),
        ],
        out_specs=[
            pl.BlockSpec((tm, S5_WIDTH), lambda i: (i, 0)),
            pl.BlockSpec((tm, RWKV_COLS), lambda i: (i, 0)),
            pl.BlockSpec((tm, MLA_COLS), lambda i: (i, 0)),
        ],
        out_shape=[
            jax.ShapeDtypeStruct((n, S5_WIDTH), F32),
            jax.ShapeDtypeStruct((n, RWKV_COLS), F32),
            jax.ShapeDtypeStruct((n, MLA_COLS), F32),
        ],
        compiler_params=_cp("parallel"),
        name="in_projection",
    )(x2, nw, mod, ws, wr, wm)


def _s5_prep_kernel(are_ref, aim_ref, ldt_ref, bre_ref, bim_ref, cre_ref, cim_ref,
                    k_ref, pin_ref, poutt_ref, lam_ref):
    T = S5_CHUNK
    for d in range(2):
        are = are_ref[d:d + 1, :]
        aim = aim_ref[d:d + 1, :]
        dt = jnp.exp(ldt_ref[d:d + 1, :])
        lre = jnp.exp(are * dt) * jnp.cos(aim * dt)
        lim = jnp.exp(are * dt) * jnp.sin(aim * dt)
        den = are * are + aim * aim
        xr = lre - 1.0
        zre = (xr * are + lim * aim) / den
        zim = (lim * are - xr * aim) / den
        bre = bre_ref[d]
        bim = bim_ref[d]
        bbre = zre * bre - zim * bim
        bbim = zre * bim + zim * bre
        cre = cre_ref[d]
        cim = cim_ref[d]

        def powers(tau):
            mag = jnp.exp(tau * (are * dt))
            ang = tau * (aim * dt)
            return mag * jnp.cos(ang), mag * jnp.sin(ang)

        tau0 = lax.broadcasted_iota(jnp.int32, (T, 1), 0).astype(F32)
        ere, eim = powers(tau0)
        xre = (ere[:, None, :] * cre[None] - eim[:, None, :] * cim[None]).reshape(T * S5_CH, S5_STATE)
        xim = (ere[:, None, :] * cim[None] + eim[:, None, :] * cre[None]).reshape(T * S5_CH, S5_STATE)
        nt = (((1,), (1,)), ((), ()))
        k_ref[d] = (lax.dot_general(xre, bbre, nt, precision=lax.Precision.HIGHEST, preferred_element_type=F32)
                    - lax.dot_general(xim, bbim, nt, precision=lax.Precision.HIGHEST, preferred_element_type=F32))
        tau_out = tau0 + 1.0 if d == 0 else float(T) - tau0
        ore, oim = powers(tau_out)
        poutt_ref[2 * d] = (ore[:, None, :] * cre[None] - oim[:, None, :] * cim[None]).reshape(T * S5_CH, S5_STATE)
        poutt_ref[2 * d + 1] = -(ore[:, None, :] * cim[None] + oim[:, None, :] * cre[None]).reshape(T * S5_CH, S5_STATE)
        tau_in = float(T - 1) - tau0 if d == 0 else tau0
        ire, iim = powers(tau_in)
        pin_ref[2 * d] = (ire[:, None, :] * bbre[None] - iim[:, None, :] * bbim[None]).reshape(T * S5_CH, S5_STATE)
        pin_ref[2 * d + 1] = (ire[:, None, :] * bbim[None] + iim[:, None, :] * bbre[None]).reshape(T * S5_CH, S5_STATE)
        tre, tim = powers(jnp.full((1, 1), float(T), F32))
        lam_ref[2 * d:2 * d + 1, :] = tre
        lam_ref[2 * d + 1:2 * d + 2, :] = tim


def _s5_prep(a_re, a_im, log_dt, b_re, b_im, c_re, c_im):
    G, P, CH, T = S5_GROUPS, S5_STATE, S5_CH, S5_CHUNK
    g_first = lambda a: jnp.swapaxes(a, 0, 1)
    are = g_first(a_re)
    aim = g_first(a_im)
    ldt = g_first(log_dt)[..., None]
    bre = jnp.swapaxes(g_first(b_re), -1, -2)
    bim = jnp.swapaxes(g_first(b_im), -1, -2)
    cre = g_first(c_re)
    cim = g_first(c_im)
    spec3 = lambda s: pl.BlockSpec((None,) + s, lambda g: (g,) + (0,) * len(s))
    k, pin, poutt, lam = pl.pallas_call(
        _s5_prep_kernel,
        grid=(G,),
        in_specs=[spec3((2, P)), spec3((2, P)), spec3((2, 1)),
                  spec3((2, CH, P)), spec3((2, CH, P)), spec3((2, CH, P)), spec3((2, CH, P))],
        out_specs=[spec3((2, T * CH, CH)), spec3((4, T * CH, P)), spec3((4, T * CH, P)), spec3((4, P))],
        out_shape=[jax.ShapeDtypeStruct((G, 2, T * CH, CH), F32),
                   jax.ShapeDtypeStruct((G, 4, T * CH, P), F32),
                   jax.ShapeDtypeStruct((G, 4, T * CH, P), F32),
                   jax.ShapeDtypeStruct((G, 4, P), F32)],
        compiler_params=_cp("parallel"),
        name="s5_weight_prep",
    )(are, aim, ldt, bre, bim, cre, cim)
    k = k.reshape(G, 2, T, CH, CH)
    s_idx = jnp.arange(T)[:, None]
    t_idx = jnp.arange(T)[None, :]
    kf = jnp.where((t_idx >= s_idx)[None, :, :, None, None], k[:, 0][:, jnp.clip(t_idx - s_idx, 0, T - 1)], 0.0)
    kb = jnp.where((t_idx <= s_idx)[None, :, :, None, None], k[:, 1][:, jnp.clip(s_idx - t_idx, 0, T - 1)], 0.0)
    m = (kf + kb).transpose(0, 1, 4, 2, 3).reshape(G, T * CH, T * CH)
    pin_m = pin.transpose(0, 2, 1, 3).reshape(G, T * CH, 4 * P)
    pout_m = poutt.transpose(0, 1, 3, 2).reshape(G, 4 * P, T * CH)
    lre = jnp.concatenate([lam[:, 0], lam[:, 0], lam[:, 2], lam[:, 2]], axis=-1)
    lim = jnp.concatenate([-lam[:, 1], lam[:, 1], -lam[:, 3], lam[:, 3]], axis=-1)
    lam_rows = jnp.stack([lre, lim], axis=1)
    return m.astype(BF16), pin_m.astype(BF16), pout_m.astype(BF16), lam_rows


def _s5_scan_kernel(u_ref, m_ref, pin_ref, pout_ref, lam_ref, h0_ref, y_ref, hfin_ref, g_scr, hin_scr, *, n_chunks, bsz):
    ub = u_ref[...].astype(BF16)
    g_scr[...] = _dot(ub, pin_ref[...])
    lre = lam_ref[0:1, :]
    lim = lam_ref[1:2, :]

    def body(c, carry):
        hf, hb = carry
        rf = pl.multiple_of(c * bsz, bsz)
        rb = pl.multiple_of((n_chunks - 1 - c) * bsz, bsz)
        hin_scr[pl.ds(rf, bsz), 0:128] = hf
        hin_scr[pl.ds(rb, bsz), 128:256] = hb
        hf = lre[:, 0:128] * hf + lim[:, 0:128] * pltpu.roll(hf, 64, 1) + g_scr[pl.ds(rf, bsz), 0:128]
        hb = lre[:, 128:256] * hb + lim[:, 128:256] * pltpu.roll(hb, 64, 1) + g_scr[pl.ds(rb, bsz), 128:256]
        return hf, hb

    h0 = h0_ref[...]
    hf, hb = lax.fori_loop(0, n_chunks, body, (h0[:, 0:128], h0[:, 128:256]))
    hfin_ref[:, 0:128] = hf
    hfin_ref[:, 128:256] = hb
    hi, lo = _split_bf16(hin_scr[...])
    pout = pout_ref[...]
    y_ref[...] = _dot(ub, m_ref[...]) + _dot(hi, pout) + _dot(lo, pout)


def _s5_scan(u, h0, m, pin, pout, lam_rows):
    bsz, seq, _ = u.shape
    G, P, T = S5_GROUPS, S5_STATE, S5_CHUNK
    nc = seq // T
    rows = nc * bsz
    ug = u.reshape(bsz, nc, T, G, S5_CH).transpose(3, 1, 0, 2, 4).reshape(G, rows, S5_TC)
    h0g = h0.transpose(2, 0, 1, 4, 3).reshape(G, bsz, 4 * P)
    spec = lambda s: pl.BlockSpec((None,) + s, lambda g: (g,) + (0,) * len(s))
    y, hfin = pl.pallas_call(
        functools.partial(_s5_scan_kernel, n_chunks=nc, bsz=bsz),
        grid=(G,),
        in_specs=[spec((rows, S5_TC)), spec((S5_TC, S5_TC)), spec((S5_TC, 4 * P)), spec((4 * P, S5_TC)),
                  spec((2, 4 * P)), spec((bsz, 4 * P))],
        out_specs=[spec((rows, S5_TC)), spec((bsz, 4 * P))],
        out_shape=[jax.ShapeDtypeStruct((G, rows, S5_TC), F32), jax.ShapeDtypeStruct((G, bsz, 4 * P), F32)],
        scratch_shapes=[pltpu.VMEM((rows, 4 * P), F32), pltpu.VMEM((rows, 4 * P), F32)],
        compiler_params=_cp("parallel"),
        name="s5_chunk_scan",
    )(ug, m, pin, pout, lam_rows, h0g)
    y = y.reshape(G, nc, bsz, T, S5_CH).transpose(2, 1, 3, 0, 4).reshape(bsz, seq, S5_WIDTH)
    hfin = hfin.reshape(G, bsz, 2, 2, P).transpose(1, 2, 0, 4, 3)
    return y, hfin


def _s5_out_kernel(y_ref, u_ref, d_ref, w_ref, nw_ref, o_ref):
    y = y_ref[...] + u_ref[...] * d_ref[...]
    c = math.sqrt(2.0 / math.pi)
    y = y * (0.5 * (1.0 + jnp.tanh(c * (y + 0.044715 * (y * y * y)))))
    z = _dot(y.astype(BF16), w_ref[...])
    o = z[:, :S5_WIDTH] * jax.nn.sigmoid(z[:, S5_WIDTH:])
    o_ref[...] = _rms(o, nw_ref[...]).astype(BF16)


def _s5_out(y2, u2, d_skip, w_glu, nw, tm):
    n = y2.shape[0]
    full = lambda a: pl.BlockSpec(a.shape, lambda i: (0,) * a.ndim)
    row = pl.BlockSpec((tm, S5_WIDTH), lambda i: (i, 0))
    return pl.pallas_call(
        _s5_out_kernel,
        grid=(n // tm,),
        in_specs=[row, row, full(d_skip), full(w_glu), full(nw)],
        out_specs=row,
        out_shape=jax.ShapeDtypeStruct((n, S5_WIDTH), BF16),
        compiler_params=_cp("parallel"),
        name="s5_gelu_glu",
    )(y2, u2, d_skip, w_glu, nw)


def _rwkv_pre_kernel(p_ref, hp_ref, hn_ref, mu_ref, w0_ref, a0_ref, w2_ref, a2_ref, g2_ref,
                     kk_ref, ka_ref, rk_ref, bd_ref,
                     v_ref, nk_ref, dec_ref, kd_ref, bb_ref, qq_ref, vkr_ref, g_ref, bonus_ref, *, tiles_per_seq, tm):
    i = pl.program_id(0)
    j = i % tiles_per_seq
    p = p_ref[...]
    rows = lax.broadcasted_iota(jnp.int32, (tm, 1), 0)
    prev_edge = jnp.where(j == 0, 0.0, hp_ref[7:8, :])
    next_edge = jnp.where(j == tiles_per_seq - 1, 0.0, hn_ref[0:1, :])
    prev = jnp.where(rows == 0, prev_edge, pltpu.roll(p, 1, 0))
    nxt = jnp.where(rows == tm - 1, next_edge, pltpu.roll(p, tm - 1, 0))
    p = p + mu_ref[...] * (0.5 * (prev + nxt) - p)
    W = RWKV_WIDTH
    r = p[:, 0:W]
    k = p[:, W:2 * W]
    v = p[:, 2 * W:3 * W]
    wl = p[:, 3 * W:3 * W + LORA_PAD]
    al = p[:, 3 * W + LORA_PAD:3 * W + 2 * LORA_PAD]
    gl = p[:, 3 * W + 2 * LORA_PAD:3 * W + 3 * LORA_PAD]
    bd = bd_ref[...]
    kk = k * kk_ref[...]
    kk = kk * lax.rsqrt(_segsum(kk * kk, bd) + 1e-12)
    g_ref[...] = _dot(jax.nn.sigmoid(gl).astype(BF16), g2_ref[...])
    tw = jnp.tanh(wl).astype(BF16)
    alb = al.astype(BF16)
    ksum = None
    for d in range(2):
        z = -(w0_ref[d:d + 1, :] + _dot(tw, w2_ref[d]))
        w = -(jnp.maximum(z, 0.0) + jnp.log(1.0 + jnp.exp(-jnp.abs(z)))) - 0.5
        dec = jnp.exp(-jnp.exp(w))
        dec_ref[d] = dec
        a = jax.nn.sigmoid(a0_ref[d:d + 1, :] + _dot(alb, a2_ref[d]))
        kd = k * (1.0 + (a - 1.0) * ka_ref[...])
        kd_ref[d] = kd
        bb = kk * a
        bb_ref[d] = bb
        qq_ref[d] = dec * r - kk * _segsum(bb * r, bd)
        vkr_ref[d] = v * _segsum(kd * r, bd)
        ksum = kd if ksum is None else ksum + kd
    v_ref[...] = v
    nk_ref[...] = -kk
    bonus_ref[...] = _segsum(r * ksum * rk_ref[...], bd) * v


def _rwkv_pre(p2, bsz, seq, wts, bd):
    n = p2.shape[0]
    tm = min(256, seq)
    tps = seq // tm
    r8 = tm // 8
    nblk8 = n // 8
    full = lambda a: pl.BlockSpec(a.shape, lambda i: (0,) * a.ndim)
    o3 = pl.BlockSpec((None, tm, RWKV_WIDTH), lambda i: (i // tps, i % tps, 0))
    o4 = pl.BlockSpec((2, None, tm, RWKV_WIDTH), lambda i: (0, i // tps, i % tps, 0))
    s3 = jax.ShapeDtypeStruct((bsz, seq, RWKV_WIDTH), F32)
    s4 = jax.ShapeDtypeStruct((2, bsz, seq, RWKV_WIDTH), F32)
    names = ("mu", "w0", "a0", "w2", "a2", "g2", "k_k", "k_a", "r_k")
    return pl.pallas_call(
        functools.partial(_rwkv_pre_kernel, tiles_per_seq=tps, tm=tm),
        grid=(n // tm,),
        in_specs=[pl.BlockSpec((tm, RWKV_COLS), lambda i: (i, 0)),
                  pl.BlockSpec((8, RWKV_COLS), lambda i: (jnp.maximum(i * r8 - 1, 0), 0)),
                  pl.BlockSpec((8, RWKV_COLS), lambda i: (jnp.minimum((i + 1) * r8, nblk8 - 1), 0))]
                 + [full(wts[k]) for k in names] + [full(bd)],
        out_specs=[o3, o3, o4, o4, o4, o4, o4, o3, o3],
        out_shape=[s3, s3, s4, s4, s4, s4, s4, s3, s3],
        compiler_params=_cp("parallel"),
        name="rwkv_prepare",
    )(p2, p2, p2, *[wts[k] for k in names], bd)


RWKV_SCAN_COLS = 32


def _rwkv_scan_kernel(nk_ref, v8_ref, dec_ref, kd_ref, bb_ref, qq_ref, v8k_ref, s0_ref, y_ref, sfin_ref, s_scr,
                      *, nb, tlen, n_chunks):
    d = pl.program_id(0)
    c = pl.program_id(2)

    @pl.when(c == 0)
    def _():
        s_scr[...] = s0_ref[...]

    N = RWKV_HEAD_DIM
    W = RWKV_WIDTH
    H = RWKV_HEADS
    nt = (((1,), (1,)), ((), ()))
    hmask = (lax.broadcasted_iota(jnp.int32, (H, W), 1) // N == lax.broadcasted_iota(jnp.int32, (H, W), 0)).astype(F32)
    eye_pad = (lax.broadcasted_iota(jnp.int32, (N, 128), 1) == lax.broadcasted_iota(jnp.int32, (N, 128), 0)).astype(BF16)
    sel = (lax.broadcasted_iota(jnp.int32, (H, RWKV_SCAN_COLS), 1)
           == lax.broadcasted_iota(jnp.int32, (H, RWKV_SCAN_COLS), 0) + 2 * H).astype(BF16)
    z_w = jnp.zeros((H, W), F32)
    z_e = jnp.zeros((H, 128), F32)
    z_row = jnp.zeros((H, W + 128), F32)

    def step(t, carry):
        tt = t + d * (tlen - 1 - 2 * t)
        row = lambda ref, n: ref[n, pl.ds(tt, 1), :] * hmask
        obs = []
        for n in range(nb):
            rhs = jnp.concatenate([
                jnp.concatenate([row(nk_ref, n), z_e], axis=1),
                jnp.concatenate([z_w, v8_ref[n, tt]], axis=1),
                jnp.concatenate([row(qq_ref, n), v8k_ref[n, tt]], axis=1),
                z_row], axis=0).astype(BF16)
            lhs = jnp.concatenate([s_scr[n].astype(BF16), eye_pad], axis=1)
            obs.append(lax.dot_general(lhs, rhs, nt, preferred_element_type=F32).astype(BF16))
        for n in range(nb):
            w2 = jnp.concatenate([row(bb_ref, n), row(kd_ref, n), z_w, z_w], axis=0).astype(BF16)
            s_scr[n] = s_scr[n] * dec_ref[n, pl.ds(tt, 1), :] + _dot(obs[n], w2)
        for n in range(nb):
            y_ref[n, tt] = lax.dot_general(sel, obs[n], nt, preferred_element_type=F32)
        return carry

    lax.fori_loop(0, tlen, step, 0)

    @pl.when(c == n_chunks - 1)
    def _():
        sfin_ref[...] = s_scr[...]


def _rwkv_scan(nk, v, dec, kd, bb, qq, vkr, s0):
    bsz, seq, W = nk.shape
    N, H = RWKV_HEAD_DIM, RWKV_HEADS
    nb = 8 if bsz % 8 == 0 else 4
    tlen = min(64, seq)
    nc = seq // tlen
    pad = lambda a: jnp.pad(a.reshape(a.shape[:-1] + (H, N)), [(0, 0)] * a.ndim + [(0, 128 - N)])
    v8, v8k = pad(v), pad(vkr)
    cc = lambda d, c: c + d * (nc - 1 - 2 * c)
    shared = pl.BlockSpec((nb, tlen, W), lambda d, b, c: (b, cc(d, c), 0))
    per_dir = pl.BlockSpec((None, nb, tlen, W), lambda d, b, c: (d, b, cc(d, c), 0))
    shared8 = pl.BlockSpec((nb, tlen, H, 128), lambda d, b, c: (b, cc(d, c), 0, 0))
    per_dir8 = pl.BlockSpec((None, nb, tlen, H, 128), lambda d, b, c: (d, b, cc(d, c), 0, 0))
    st = pl.BlockSpec((None, nb, N, W), lambda d, b, c: (d, b, 0, 0))
    y8, s_fin = pl.pallas_call(
        functools.partial(_rwkv_scan_kernel, nb=nb, tlen=tlen, n_chunks=nc),
        grid=(2, bsz // nb, nc),
        in_specs=[shared, shared8, per_dir, per_dir, per_dir, per_dir, per_dir8, st],
        out_specs=[pl.BlockSpec((None, nb, tlen, H, N), lambda d, b, c: (d, b, cc(d, c), 0, 0)), st],
        out_shape=[jax.ShapeDtypeStruct((2, bsz, seq, H, N), F32), jax.ShapeDtypeStruct((2, bsz, N, W), F32)],
        scratch_shapes=[pltpu.VMEM((nb, N, W), F32)],
        compiler_params=_cp("parallel", "parallel", "arbitrary"),
        name="rwkv_scan",
    )(nk, v8, dec, kd, bb, qq, v8k, s0)
    return y8.reshape(2, bsz, seq, W), s_fin


def _rwkv_post_kernel(y_ref, bonus_ref, g_ref, lw_ref, lb_ref, bd_ref, o_ref):
    bd = bd_ref[...]
    y = y_ref[0] + y_ref[1]
    inv_n = 1.0 / RWKV_HEAD_DIM
    mean = _segsum(y, bd) * inv_n
    yc = y - mean
    var = _segsum(yc * yc, bd) * inv_n
    yn = yc * lax.rsqrt(var + GN_EPS) * lw_ref[...] + lb_ref[...]
    o_ref[...] = ((yn + bonus_ref[...]) * g_ref[...]).astype(BF16)


def _rwkv_post(y, bonus, g, ln_w, ln_b, bd):
    _, bsz, seq, W = y.shape
    tm = min(256, seq)
    tps = seq // tm
    n = bsz * seq
    full = lambda a: pl.BlockSpec(a.shape, lambda i: (0,) * a.ndim)
    i3 = pl.BlockSpec((None, tm, W), lambda i: (i // tps, i % tps, 0))
    return pl.pallas_call(
        _rwkv_post_kernel,
        grid=(n // tm,),
        in_specs=[pl.BlockSpec((2, None, tm, W), lambda i: (0, i // tps, i % tps, 0)), i3, i3,
                  full(ln_w), full(ln_b), full(bd)],
        out_specs=pl.BlockSpec((tm, W), lambda i: (i, 0)),
        out_shape=jax.ShapeDtypeStruct((n, W), BF16),
        compiler_params=_cp("parallel"),
        name="rwkv_groupnorm_gate",
    )(y, bonus, g, ln_w, ln_b, bd)


def _store_k_heads(k_ref, kn, kpe):
    for h in range(MLA_HEADS):
        k_ref[:, h * MLA_QK_PAD:h * MLA_QK_PAD + 128] = kn[:, h * 128:(h + 1) * 128].astype(BF16)
        k_ref[:, h * MLA_QK_PAD + 128:(h + 1) * MLA_QK_PAD] = kpe


def _mla_prep_kernel(*refs, rope):
    if rope:
        (p_ref, qn_ref, kvn_ref, wq_ref, wqr_ref, wk_ref, wv_ref, cq_ref, sq_ref, ck_ref, sk_ref,
         q_ref, k_ref, v_ref) = refs
    else:
        (p_ref, qn_ref, kvn_ref, wq_ref, wk_ref, wv_ref, q_ref, k_ref, v_ref, ckv_ref, kr_ref) = refs
    p = p_ref[...]
    qn = _rms(p[:, 0:MLA_Q_RANK], qn_ref[...]).astype(BF16)
    q = _dot(qn, wq_ref[...])
    ckv = _rms(p[:, MLA_Q_RANK:MLA_Q_RANK + MLA_KV_RANK], kvn_ref[...])
    kr = p[:, MLA_Q_RANK + MLA_KV_RANK:MLA_Q_RANK + MLA_KV_RANK + 128]
    if rope:
        cq = jnp.concatenate([cq_ref[...]] * MLA_HEADS, axis=1)
        sq = jnp.concatenate([sq_ref[...]] * MLA_HEADS, axis=1)
        q = q * cq + _dot(qn, wqr_ref[...]) * sq
        krot = p[:, MLA_Q_RANK + MLA_KV_RANK + 128:MLA_Q_RANK + MLA_KV_RANK + 256]
        kpe = kr * ck_ref[...] + krot * sk_ref[...]
    else:
        kpe = kr
        ckv_ref[...] = ckv
        kr_ref[...] = kr[:, 0:MLA_ROPE_DIM]
    q_ref[...] = q.astype(BF16)
    cb = ckv.astype(BF16)
    _store_k_heads(k_ref, _dot(cb, wk_ref[...]), kpe.astype(BF16))
    v_ref[...] = _dot(cb, wv_ref[...]).astype(BF16)


def _mla_prep(p2, seq, wts, tables):
    n = p2.shape[0]
    rope = tables is not None
    tm = min(256, seq)
    tps = seq // tm
    full = lambda a: pl.BlockSpec(a.shape, lambda i: (0,) * a.ndim)
    row = lambda w: pl.BlockSpec((tm, w), lambda i: (i, 0))
    ins = [p2, wts["q_norm"], wts["kv_norm"], wts["wq"]]
    specs = [row(MLA_COLS), full(wts["q_norm"]), full(wts["kv_norm"]), full(wts["wq"])]
    if rope:
        ins.append(wts["wq_rot"])
        specs.append(full(wts["wq_rot"]))
    ins += [wts["wk"], wts["wv"]]
    specs += [full(wts["wk"]), full(wts["wv"])]
    outs = [row(MLA_HEADS * MLA_QK_PAD), row(MLA_HEADS * MLA_QK_PAD), row(MLA_HEADS * MLA_V_DIM)]
    shapes = [jax.ShapeDtypeStruct((n, MLA_HEADS * MLA_QK_PAD), BF16),
              jax.ShapeDtypeStruct((n, MLA_HEADS * MLA_QK_PAD), BF16),
              jax.ShapeDtypeStruct((n, MLA_HEADS * MLA_V_DIM), BF16)]
    if rope:
        ins += list(tables)
        specs += [pl.BlockSpec((tm, t.shape[1]), lambda i: (i % tps, 0)) for t in tables]
    else:
        outs += [row(MLA_KV_RANK), row(MLA_ROPE_DIM)]
        shapes += [jax.ShapeDtypeStruct((n, MLA_KV_RANK), F32), jax.ShapeDtypeStruct((n, MLA_ROPE_DIM), F32)]
    return pl.pallas_call(
        functools.partial(_mla_prep_kernel, rope=rope),
        grid=(n // tm,),
        in_specs=specs, out_specs=outs, out_shape=shapes,
        compiler_params=_cp("parallel"),
        name="mla_prepare_rope" if rope else "mla_prepare",
    )(*ins)


def _mla_cache_kernel(ckv_ref, kr_ref, wk_ref, wv_ref, k_ref, v_ref):
    cb = ckv_ref[...].astype(BF16)
    _store_k_heads(k_ref, _dot(cb, wk_ref[...]), kr_ref[...].astype(BF16))
    v_ref[...] = _dot(cb, wv_ref[...]).astype(BF16)


def _mla_cache(cache_ckv, cache_kr_pad, layer, wk, wv):
    bsz, _, past, _ = cache_ckv.shape
    full = lambda a: pl.BlockSpec(a.shape, lambda b: (0,) * a.ndim)
    return pl.pallas_call(
        _mla_cache_kernel,
        grid=(bsz,),
        in_specs=[pl.BlockSpec((None, None, past, MLA_KV_RANK), lambda b: (b, layer, 0, 0)),
                  pl.BlockSpec((None, None, past, 128), lambda b: (b, layer, 0, 0)),
                  full(wk), full(wv)],
        out_specs=[pl.BlockSpec((None, past, MLA_HEADS * MLA_QK_PAD), lambda b: (b, 0, 0)),
                   pl.BlockSpec((None, past, MLA_HEADS * MLA_V_DIM), lambda b: (b, 0, 0))],
        out_shape=[jax.ShapeDtypeStruct((bsz, past, MLA_HEADS * MLA_QK_PAD), BF16),
                   jax.ShapeDtypeStruct((bsz, past, MLA_HEADS * MLA_V_DIM), BF16)],
        compiler_params=_cp("parallel"),
        name="mla_cache_keys",
    )(cache_ckv, cache_kr_pad, wk, wv)


def _attn_kernel(*refs, cache, scale):
    if cache:
        q_ref, k_ref, v_ref, kc_ref, vc_ref, o_ref = refs
    else:
        q_ref, k_ref, v_ref, o_ref = refs
    nt = (((1,), (1,)), ((), ()))
    q = q_ref[...]
    s = lax.dot_general(q, k_ref[...], nt, preferred_element_type=F32) * scale
    m = jnp.max(s, axis=-1, keepdims=True)
    if cache:
        sc = lax.dot_general(q, kc_ref[...], nt, preferred_element_type=F32) * scale
        m = jnp.maximum(m, jnp.max(sc, axis=-1, keepdims=True))
    e = jnp.exp(s - m)
    den = jnp.sum(e, axis=-1, keepdims=True)
    o = _dot(e.astype(BF16), v_ref[...])
    if cache:
        ec = jnp.exp(sc - m)
        den = den + jnp.sum(ec, axis=-1, keepdims=True)
        o = o + _dot(ec.astype(BF16), vc_ref[...])
    o_ref[...] = o / den


def _attention(q, k, v, kc=None, vc=None):
    bsz, seq, _ = q.shape
    tq = min(256, seq)
    cache = kc is not None
    scale = float(MLA_NOPE_DIM + MLA_ROPE_DIM) ** -0.5
    ins = [q, k, v]
    specs = [pl.BlockSpec((None, tq, MLA_QK_PAD), lambda b, h, i: (b, i, h)),
             pl.BlockSpec((None, seq, MLA_QK_PAD), lambda b, h, i: (b, 0, h)),
             pl.BlockSpec((None, seq, MLA_V_DIM), lambda b, h, i: (b, 0, h))]
    if cache:
        past = kc.shape[1]
        ins += [kc, vc]
        specs += [pl.BlockSpec((None, past, MLA_QK_PAD), lambda b, h, i: (b, 0, h)),
                  pl.BlockSpec((None, past, MLA_V_DIM), lambda b, h, i: (b, 0, h))]
    return pl.pallas_call(
        functools.partial(_attn_kernel, cache=cache, scale=scale),
        grid=(bsz, MLA_HEADS, seq // tq),
        in_specs=specs,
        out_specs=pl.BlockSpec((None, tq, MLA_V_DIM), lambda b, h, i: (b, i, h)),
        out_shape=jax.ShapeDtypeStruct((bsz, seq, MLA_HEADS * MLA_V_DIM), F32),
        compiler_params=_cp("parallel", "parallel", "arbitrary"),
        name="mla_attention_cached" if cache else "mla_attention",
    )(*ins)


def _outproj_kernel(ys_ref, yr_ref, ym_ref, x_ref, mod_ref, nm_ref, w_ref, o_ref):
    ym = _rms(ym_ref[...], nm_ref[...]).astype(BF16)
    acc = _dot(ys_ref[...], w_ref[0:S5_WIDTH, :])
    acc += _dot(yr_ref[...], w_ref[S5_WIDTH:S5_WIDTH + RWKV_WIDTH, :])
    acc += _dot(ym, w_ref[S5_WIDTH + RWKV_WIDTH:, :])
    o_ref[...] = x_ref[...] + mod_ref[2:3, :] * acc


def _outproj(ys, yr, ym, x2, mod, layer, row_of_tile, tm, nm, w_out):
    n = x2.shape[0]
    full = lambda a: pl.BlockSpec(a.shape, lambda i: (0,) * a.ndim)
    row = lambda w: pl.BlockSpec((tm, w), lambda i: (i, 0))
    return pl.pallas_call(
        _outproj_kernel,
        grid=(n // tm,),
        in_specs=[row(S5_WIDTH), row(RWKV_WIDTH), row(MLA_WIDTH), row(D_MODEL), _mod_spec(layer, row_of_tile),
                  full(nm), full(w_out)],
        out_specs=row(D_MODEL),
        out_shape=jax.ShapeDtypeStruct((n, D_MODEL), F32),
        compiler_params=_cp("parallel"),
        name="out_projection",
    )(ys, yr, ym, x2, mod, nm, w_out)


def _mlp_kernel(x_ref, nw_ref, mod_ref, w1_ref, w2_ref, nf_ref, o_ref, h_scr, acc_scr, *, final_norm):
    j = pl.program_id(1)

    @pl.when(j == 0)
    def _():
        h = _rms(x_ref[...], nw_ref[...]) * (1.0 + mod_ref[4:5, :]) + mod_ref[3:4, :]
        h_scr[...] = h.astype(BF16)
        acc_scr[...] = jnp.zeros_like(acc_scr)

    a = _dot(h_scr[...], w1_ref[...])
    a = jnp.square(jnp.maximum(a, 0.0)).astype(BF16)
    acc_scr[...] += _dot(a, w2_ref[...])

    @pl.when(j == pl.num_programs(1) - 1)
    def _():
        y = x_ref[...] + mod_ref[5:6, :] * acc_scr[...]
        if final_norm:
            y = _rms(y, nf_ref[...])
        o_ref[...] = y


def _mlp(x2, nw, mod, layer, row_of_tile, tm, w1, w2, nf, final_norm):
    n = x2.shape[0]
    tf = 1024
    full = lambda a: pl.BlockSpec(a.shape, lambda i, j: (0,) * a.ndim)
    return pl.pallas_call(
        functools.partial(_mlp_kernel, final_norm=final_norm),
        grid=(n // tm, D_FF // tf),
        in_specs=[pl.BlockSpec((tm, D_MODEL), lambda i, j: (i, 0)), full(nw), _mod_spec(layer, row_of_tile),
                  pl.BlockSpec((D_MODEL, tf), lambda i, j: (0, j)),
                  pl.BlockSpec((tf, D_MODEL), lambda i, j: (j, 0)), full(nf)],
        out_specs=pl.BlockSpec((tm, D_MODEL), lambda i, j: (i, 0)),
        out_shape=jax.ShapeDtypeStruct((n, D_MODEL), F32),
        scratch_shapes=[pltpu.VMEM((tm, D_MODEL), BF16), pltpu.VMEM((tm, D_MODEL), F32)],
        compiler_params=_cp("parallel", "arbitrary"),
        name="mlp_final" if final_norm else "mlp",
    )(x2, nw, mod, w1, w2, nf)


def _rope_tables(length):
    rows = length // GRID_W
    row_pos = jnp.repeat(jnp.arange(rows, dtype=F32), GRID_W)
    col_pos = jnp.tile(jnp.arange(GRID_W, dtype=F32), rows)
    axis_dim = MLA_ROPE_DIM // 2
    inv_freq = 1.0 / (ROPE_THETA ** (jnp.arange(0, axis_dim, 2, dtype=F32) / axis_dim))
    ang_r = row_pos[:, None] * inv_freq[None, :]
    ang_c = col_pos[:, None] * inv_freq[None, :]
    ang = jnp.concatenate([ang_r, ang_r, ang_c, ang_c], axis=-1)
    cos, sin = jnp.cos(ang), jnp.sin(ang)
    z64 = jnp.zeros((length, 64), F32)
    cos_q = jnp.concatenate([jnp.ones((length, MLA_NOPE_DIM), F32), cos, z64], axis=1)
    sin_q = jnp.concatenate([jnp.zeros((length, MLA_NOPE_DIM), F32), sin, z64], axis=1)
    cos_k = jnp.concatenate([cos, z64], axis=1)
    sin_k = jnp.concatenate([sin, z64], axis=1)
    return cos_q, sin_q, cos_k, sin_k


def _rot_cols(w):
    a, b, c, d = w[..., 0:16], w[..., 16:32], w[..., 32:48], w[..., 48:64]
    return jnp.concatenate([-b, a, -d, c], axis=-1)


def _layer_weights(l, p):
    d = D_MODEL
    w_in = p["w_in"][l]
    z64 = jnp.zeros((d, 64), F32)
    o = S5_WIDTH
    rk = w_in[:, o:o + 3 * RWKV_WIDTH]
    o += 3 * RWKV_WIDTH
    wl, al, gl = w_in[:, o:o + 64], w_in[:, o + 64:o + 128], w_in[:, o + 128:o + 256]
    o += 256
    cq, ckv, kr = w_in[:, o:o + 512], w_in[:, o + 512:o + 768], w_in[:, o + 768:o + 832]
    mu = p["rwkv_mu"][l]
    z1 = jnp.zeros((64,), F32)
    mu_pad = jnp.concatenate([mu[:1536], mu[1536:1600], z1, mu[1600:1664], z1, mu[1664:1792]])[None]
    pad_rows = lambda w: jnp.concatenate([w, jnp.zeros_like(w)], axis=-2)
    w_uq = p["mla_w_uq"][l].reshape(MLA_Q_RANK, MLA_HEADS, MLA_NOPE_DIM + MLA_ROPE_DIM)
    zq = jnp.zeros((MLA_Q_RANK, MLA_HEADS, 64), F32)
    wq = jnp.concatenate([w_uq, zq], axis=-1).reshape(MLA_Q_RANK, MLA_HEADS * MLA_QK_PAD)
    wq_rot = jnp.concatenate([jnp.zeros((MLA_Q_RANK, MLA_HEADS, MLA_NOPE_DIM), F32),
                              _rot_cols(w_uq[..., MLA_NOPE_DIM:]), zq], axis=-1).reshape(MLA_Q_RANK, MLA_HEADS * MLA_QK_PAD)
    w_ukv = p["mla_w_ukv"][l].reshape(MLA_KV_RANK, MLA_HEADS, MLA_NOPE_DIM + MLA_V_DIM)
    row = lambda a: a.reshape(1, -1)
    return {
        "norm_mix": row(p["norm_mix"][l]), "norm_mlp": row(p["norm_mlp"][l]),
        "w_s5": w_in[:, 0:S5_WIDTH].astype(BF16),
        "w_rwkv": jnp.concatenate([rk, wl, z64, al, z64, gl], axis=1).astype(BF16),
        "w_mla": jnp.concatenate([cq, ckv, kr, z64, _rot_cols(kr), z64], axis=1).astype(BF16),
        "w_out": p["w_out"][l].astype(BF16),
        "s5_d": row(p["s5_d"][l]), "s5_w_glu": p["s5_w_glu"][l].astype(BF16), "s5_out_norm": row(p["s5_out_norm"][l]),
        "rwkv": {
            "mu": mu_pad, "w0": p["rwkv_w0"][l], "a0": p["rwkv_a0"][l],
            "w2": pad_rows(p["rwkv_w2"][l]).astype(BF16), "a2": pad_rows(p["rwkv_a2"][l]).astype(BF16),
            "g2": p["rwkv_g2"][l].astype(BF16), "k_k": row(p["rwkv_k_k"][l]), "k_a": row(p["rwkv_k_a"][l]),
            "r_k": row(p["rwkv_r_k"][l]),
        },
        "rwkv_ln_w": row(p["rwkv_ln_w"][l]), "rwkv_ln_b": row(p["rwkv_ln_b"][l]),
        "mla": {
            "q_norm": row(p["mla_q_norm"][l]), "kv_norm": row(p["mla_kv_norm"][l]),
            "wq": wq.astype(BF16), "wq_rot": wq_rot.astype(BF16),
            "wk": w_ukv[..., :MLA_NOPE_DIM].reshape(MLA_KV_RANK, -1).astype(BF16),
            "wv": w_ukv[..., MLA_NOPE_DIM:].reshape(MLA_KV_RANK, -1).astype(BF16),
        },
        "mla_out_norm": row(p["mla_out_norm"][l]),
        "mlp_w1": p["mlp_w1"][l].astype(BF16), "mlp_w2": p["mlp_w2"][l].astype(BF16),
    }


def _trunk_layer(x, mod, layer, lw, s5w, bd, row_of_tile_fn, s5_h0, rwkv_s0, cache, tables, norm_final, final_norm):
    bsz, seq, d = x.shape
    n = bsz * seq
    x2 = x.reshape(n, d)
    tm = min(512, seq) if cache is not None else min(512, n)
    row_of_tile = row_of_tile_fn(tm)
    u2, pr2, pm2 = _inproj(x2, lw["norm_mix"], mod, layer, row_of_tile, tm, lw["w_s5"], lw["w_rwkv"], lw["w_mla"])

    ys, s5_fin = _s5_scan(u2.reshape(bsz, seq, S5_WIDTH), s5_h0, *s5w)
    ys2 = _s5_out(ys.reshape(n, S5_WIDTH), u2, lw["s5_d"], lw["s5_w_glu"], lw["s5_out_norm"], min(512, n))

    v, nk, dec, kd, bb, qq, vkr, g, bonus = _rwkv_pre(pr2, bsz, seq, lw["rwkv"], bd)
    s0 = rwkv_s0.transpose(1, 0, 3, 2, 4).reshape(2, bsz, RWKV_HEAD_DIM, RWKV_WIDTH)
    yr, s_fin = _rwkv_scan(nk, v, dec, kd, bb, qq, vkr, s0)
    yr2 = _rwkv_post(yr, bonus, g, lw["rwkv_ln_w"], lw["rwkv_ln_b"], bd)
    rwkv_fin = s_fin.reshape(2, bsz, RWKV_HEAD_DIM, RWKV_HEADS, RWKV_HEAD_DIM).transpose(1, 0, 3, 2, 4)

    shape3 = lambda a: a.reshape(bsz, seq, a.shape[-1])
    if cache is None:
        q, k, v_, ckv_n, k_rope = _mla_prep(pm2, seq, lw["mla"], None)
        ym = _attention(shape3(q), shape3(k), shape3(v_))
        extras = (shape3(ckv_n), shape3(k_rope), s5_fin, rwkv_fin)
    else:
        q, k, v_ = _mla_prep(pm2, seq, lw["mla"], tables)
        kc, vc = _mla_cache(cache[0], cache[1], layer, lw["mla"]["wk"], lw["mla"]["wv"])
        ym = _attention(shape3(q), shape3(k), shape3(v_), kc, vc)
        extras = None

    x2 = _outproj(ys2, yr2, ym.reshape(n, MLA_WIDTH), x2, mod, layer, row_of_tile, tm, lw["mla_out_norm"], lw["w_out"])
    x2 = _mlp(x2, lw["norm_mlp"], mod, layer, row_of_tile, tm, lw["mlp_w1"], lw["mlp_w2"], norm_final, final_norm)
    return x2.reshape(bsz, seq, d), extras


def kernel(x_prompt, x_sample, cache_mla_ckv, cache_mla_krope, state_s5, state_rwkv, c, c_ctx, norm_mix, norm_mlp, norm_final, w_ada, b_ada, w_in, w_out, s5_a_re, s5_a_im, s5_log_dt, s5_b_re, s5_b_im, s5_c_re, s5_c_im, s5_d, s5_w_glu, s5_out_norm, rwkv_mu, rwkv_w0, rwkv_w2, rwkv_a0, rwkv_a2, rwkv_g2, rwkv_k_k, rwkv_k_a, rwkv_r_k, rwkv_ln_w, rwkv_ln_b, mla_q_norm, mla_w_uq, mla_kv_norm, mla_w_ukv, mla_out_norm, mlp_w1, mlp_w2):
    p = dict(norm_mix=norm_mix, norm_mlp=norm_mlp, w_in=w_in, w_out=w_out, s5_d=s5_d, s5_w_glu=s5_w_glu,
             s5_out_norm=s5_out_norm, rwkv_mu=rwkv_mu, rwkv_w0=rwkv_w0, rwkv_w2=rwkv_w2, rwkv_a0=rwkv_a0,
             rwkv_a2=rwkv_a2, rwkv_g2=rwkv_g2, rwkv_k_k=rwkv_k_k, rwkv_k_a=rwkv_k_a, rwkv_r_k=rwkv_r_k,
             rwkv_ln_w=rwkv_ln_w, rwkv_ln_b=rwkv_ln_b, mla_q_norm=mla_q_norm, mla_w_uq=mla_w_uq,
             mla_kv_norm=mla_kv_norm, mla_w_ukv=mla_w_ukv, mla_out_norm=mla_out_norm, mlp_w1=mlp_w1, mlp_w2=mlp_w2)
    depth = w_in.shape[0]
    b_ctx, l_ctx, d = x_prompt.shape
    b_dec, l_dec, _ = x_sample.shape

    rows = -(-(1 + b_dec) // 8) * 8
    cond = jnp.zeros((rows, d), F32).at[0].set(c_ctx).at[1:1 + b_dec].set(c)
    mod = _modulation(cond, w_ada, b_ada).reshape(depth, rows, N_MOD, d)

    bd = jnp.kron(jnp.eye(4, dtype=F32), jnp.ones((64, 64), F32)).astype(BF16)
    tables = _rope_tables(l_dec)
    kr_pad = jnp.pad(cache_mla_krope, ((0, 0), (0, 0), (0, 0), (0, 128 - MLA_ROPE_DIM)))
    zero_s5 = jnp.zeros((b_ctx, 2, S5_GROUPS, S5_STATE, 2), F32)
    zero_rwkv = jnp.zeros((b_ctx, 2, RWKV_HEADS, RWKV_HEAD_DIM, RWKV_HEAD_DIM), F32)
    nf = norm_final.reshape(1, d)

    ctx_rows = lambda tm: (lambda i: 0)
    dec_rows = lambda tm: (lambda i: 1 + i // (l_dec // tm))

    xp, xs = x_prompt, x_sample
    new_ckv, new_krope, new_s5, new_rwkv = [], [], [], []
    for l in range(depth):
        lw = _layer_weights(l, p)
        s5w = _s5_prep(s5_a_re[l], s5_a_im[l], s5_log_dt[l], s5_b_re[l], s5_b_im[l], s5_c_re[l], s5_c_im[l])
        last = l == depth - 1
        xp, (ckv_l, krope_l, s5_l, rwkv_l) = _trunk_layer(
            xp, mod, l, lw, s5w, bd, ctx_rows, zero_s5, zero_rwkv, None, None, nf, last)
        new_ckv.append(ckv_l)
        new_krope.append(krope_l)
        new_s5.append(s5_l)
        new_rwkv.append(rwkv_l)
        xs, _ = _trunk_layer(
            xs, mod, l, lw, s5w, bd, dec_rows, state_s5[:, l], state_rwkv[:, l], (cache_mla_ckv, kr_pad), tables, nf, last)
    return (xp, xs, jnp.stack(new_ckv, axis=1), jnp.stack(new_krope, axis=1),
            jnp.stack(new_s5, axis=1), jnp.stack(new_rwkv, axis=1))
```

```python
import functools
import math

import jax
import jax.numpy as jnp
from jax import lax
from jax.experimental import pallas as pl
from jax.experimental.pallas import tpu as pltpu

F32 = jnp.float32
BF16 = jnp.bfloat16

D_MODEL = 2048
N_MOD = 6
GRID_W = 64
S5_WIDTH = 512
S5_CH = 16
S5_GROUPS = 32
S5_STATE = 64
S5_CHUNK = 16
S5_TC = S5_CHUNK * S5_CH
RWKV_WIDTH = 512
RWKV_HEAD_DIM = 64
RWKV_HEADS = 8
LORA_PAD = 128
RWKV_COLS = 3 * RWKV_WIDTH + 3 * LORA_PAD
MLA_HEADS = 8
MLA_V_DIM = 128
MLA_NOPE_DIM = 128
MLA_ROPE_DIM = 64
MLA_QK_PAD = 256
MLA_Q_RANK = 512
MLA_KV_RANK = 256
MLA_WIDTH = 1024
MLA_COLS = MLA_Q_RANK + MLA_KV_RANK + 2 * 128
D_FF = 8192
ROPE_THETA = 10000.0
NORM_EPS = 1e-6
GN_EPS = 64e-5

VMEM_LIMIT_BYTES = 56 * 1024 * 1024


def _cp(*sem):
    return pltpu.CompilerParams(dimension_semantics=sem, vmem_limit_bytes=VMEM_LIMIT_BYTES)


def _dot(a, b):
    return jnp.dot(a, b, preferred_element_type=F32)


def _rms(x, g):
    ms = jnp.mean(x * x, axis=-1, keepdims=True)
    return x * lax.rsqrt(ms + NORM_EPS) * g


def _split_bf16(x):
    hi = x.astype(BF16)
    lo = (x - hi.astype(F32)).astype(BF16)
    return hi, lo


def _segsum(x, bd):
    hi, lo = _split_bf16(x)
    left = _dot(hi[:, :256], bd) + _dot(lo[:, :256], bd)
    right = _dot(hi[:, 256:], bd) + _dot(lo[:, 256:], bd)
    return jnp.concatenate([left, right], axis=1)


def _segsum_bf16(xb, bd):
    return jnp.concatenate([_dot(xb[:, :256], bd), _dot(xb[:, 256:], bd)], axis=1)


def _mod_kernel(c_ref, w_ref, b_ref, o_ref):
    c = c_ref[...]
    s = (c * jax.nn.sigmoid(c)).astype(BF16)
    o_ref[...] = _dot(s, w_ref[...].astype(BF16)) + b_ref[...]


def _modulation(cond, w_ada, b_ada):
    depth, d, n = w_ada.shape
    rows = cond.shape[0]
    tn = 1024
    return pl.pallas_call(
        _mod_kernel,
        grid=(depth, n // tn),
        in_specs=[
            pl.BlockSpec((rows, d), lambda l, j: (0, 0)),
            pl.BlockSpec((None, d, tn), lambda l, j: (l, 0, j)),
            pl.BlockSpec((None, 1, tn), lambda l, j: (l, 0, j)),
        ],
        out_specs=pl.BlockSpec((None, rows, tn), lambda l, j: (l, 0, j)),
        out_shape=jax.ShapeDtypeStruct((depth, rows, n), F32),
        compiler_params=_cp("parallel", "arbitrary"),
        name="adaln_modulation",
    )(cond, w_ada, b_ada.reshape(depth, 1, n))


def _mod_spec(layer, row_of_tile):
    return pl.BlockSpec((None, None, N_MOD, D_MODEL), lambda i, *_: (layer, row_of_tile(i), 0, 0))


def _inproj_kernel(x_ref, nw_ref, mod_ref, ws_ref, wr_ref, wm_ref, os_ref, or_ref, om_ref):
    h = _rms(x_ref[...], nw_ref[...]) * (1.0 + mod_ref[1:2, :]) + mod_ref[0:1, :]
    hb = h.astype(BF16)
    os_ref[...] = _dot(hb, ws_ref[...])
    or_ref[...] = _dot(hb, wr_ref[...])
    om_ref[...] = _dot(hb, wm_ref[...])


def _inproj(x2, nw, mod, layer, row_of_tile, tm, ws, wr, wm):
    n = x2.shape[0]
    full = lambda a: pl.BlockSpec(a.shape, lambda i: (0,) * a.ndim)
    return pl.pallas_call(
        _inproj_kernel,
        grid=(n // tm,),
        in_specs=[
            pl.BlockSpec((tm, D_MODEL), lambda i: (i, 0)),
            full(nw),
            _mod_spec(layer, row_of_tile),
            full(ws), full(wr), full(wm),
        ],
        out_specs=[
            pl.BlockSpec((tm, S5_WIDTH), lambda i: (i, 0)),
            pl.BlockSpec((tm, RWKV_COLS), lambda i: (i, 0)),
            pl.BlockSpec((tm, MLA_COLS), lambda i: (i, 0)),
        ],
        out_shape=[
            jax.ShapeDtypeStruct((n, S5_WIDTH), F32),
            jax.ShapeDtypeStruct((n, RWKV_COLS), F32),
            jax.ShapeDtypeStruct((n, MLA_COLS), F32),
        ],
        compiler_params=_cp("parallel"),
        name="in_projection",
    )(x2, nw, mod, ws, wr, wm)


def _s5_prep_kernel(are_ref, aim_ref, ldt_ref, bre_ref, bim_ref, cre_ref, cim_ref,
                    k_ref, pin_ref, poutt_ref, lam_ref):
    T = S5_CHUNK
    for d in range(2):
        are = are_ref[d:d + 1, :]
        aim = aim_ref[d:d + 1, :]
        dt = jnp.exp(ldt_ref[d:d + 1, :])
        lre = jnp.exp(are * dt) * jnp.cos(aim * dt)
        lim = jnp.exp(are * dt) * jnp.sin(aim * dt)
        den = are * are + aim * aim
        xr = lre - 1.0
        zre = (xr * are + lim * aim) / den
        zim = (lim * are - xr * aim) / den
        bre = bre_ref[d]
        bim = bim_ref[d]
        bbre = zre * bre - zim * bim
        bbim = zre * bim + zim * bre
        cre = cre_ref[d]
        cim = cim_ref[d]

        def powers(tau):
            mag = jnp.exp(tau * (are * dt))
            ang = tau * (aim * dt)
            return mag * jnp.cos(ang), mag * jnp.sin(ang)

        tau0 = lax.broadcasted_iota(jnp.int32, (T, 1), 0).astype(F32)
        ere, eim = powers(tau0)
        xre = (ere[:, None, :] * cre[None] - eim[:, None, :] * cim[None]).reshape(T * S5_CH, S5_STATE)
        xim = (ere[:, None, :] * cim[None] + eim[:, None, :] * cre[None]).reshape(T * S5_CH, S5_STATE)
        nt = (((1,), (1,)), ((), ()))
        k_ref[d] = (lax.dot_general(xre, bbre, nt, precision=lax.Precision.HIGHEST, preferred_element_type=F32)
                    - lax.dot_general(xim, bbim, nt, precision=lax.Precision.HIGHEST, preferred_element_type=F32))
        tau_out = tau0 + 1.0 if d == 0 else float(T) - tau0
        ore, oim = powers(tau_out)
        poutt_ref[2 * d] = (ore[:, None, :] * cre[None] - oim[:, None, :] * cim[None]).reshape(T * S5_CH, S5_STATE)
        poutt_ref[2 * d + 1] = -(ore[:, None, :] * cim[None] + oim[:, None, :] * cre[None]).reshape(T * S5_CH, S5_STATE)
        tau_in = float(T - 1) - tau0 if d == 0 else tau0
        ire, iim = powers(tau_in)
        pin_ref[2 * d] = (ire[:, None, :] * bbre[None] - iim[:, None, :] * bbim[None]).reshape(T * S5_CH, S5_STATE)
        pin_ref[2 * d + 1] = (ire[:, None, :] * bbim[None] + iim[:, None, :] * bbre[None]).reshape(T * S5_CH, S5_STATE)
        tre, tim = powers(jnp.full((1, 1), float(T), F32))
        lam_ref[2 * d:2 * d + 1, :] = tre
        lam_ref[2 * d + 1:2 * d + 2, :] = tim


def _s5_prep(a_re, a_im, log_dt, b_re, b_im, c_re, c_im):
    G, P, CH, T = S5_GROUPS, S5_STATE, S5_CH, S5_CHUNK
    g_first = lambda a: jnp.swapaxes(a, 0, 1)
    are = g_first(a_re)
    aim = g_first(a_im)
    ldt = g_first(log_dt)[..., None]
    bre = jnp.swapaxes(g_first(b_re), -1, -2)
    bim = jnp.swapaxes(g_first(b_im), -1, -2)
    cre = g_first(c_re)
    cim = g_first(c_im)
    spec3 = lambda s: pl.BlockSpec((None,) + s, lambda g: (g,) + (0,) * len(s))
    k, pin, poutt, lam = pl.pallas_call(
        _s5_prep_kernel,
        grid=(G,),
        in_specs=[spec3((2, P)), spec3((2, P)), spec3((2, 1)),
                  spec3((2, CH, P)), spec3((2, CH, P)), spec3((2, CH, P)), spec3((2, CH, P))],
        out_specs=[spec3((2, T * CH, CH)), spec3((4, T * CH, P)), spec3((4, T * CH, P)), spec3((4, P))],
        out_shape=[jax.ShapeDtypeStruct((G, 2, T * CH, CH), F32),
                   jax.ShapeDtypeStruct((G, 4, T * CH, P), F32),
                   jax.ShapeDtypeStruct((G, 4, T * CH, P), F32),
                   jax.ShapeDtypeStruct((G, 4, P), F32)],
        compiler_params=_cp("parallel"),
        name="s5_weight_prep",
    )(are, aim, ldt, bre, bim, cre, cim)
    k = k.reshape(G, 2, T, CH, CH)
    s_idx = jnp.arange(T)[:, None]
    t_idx = jnp.arange(T)[None, :]
    kf = jnp.where((t_idx >= s_idx)[None, :, :, None, None], k[:, 0][:, jnp.clip(t_idx - s_idx, 0, T - 1)], 0.0)
    kb = jnp.where((t_idx <= s_idx)[None, :, :, None, None], k[:, 1][:, jnp.clip(s_idx - t_idx, 0, T - 1)], 0.0)
    m = (kf + kb).transpose(0, 1, 4, 2, 3).reshape(G, T * CH, T * CH)
    pin_m = pin.transpose(0, 2, 1, 3).reshape(G, T * CH, 4 * P)
    pout_m = poutt.transpose(0, 1, 3, 2).reshape(G, 4 * P, T * CH)
    lre = jnp.concatenate([lam[:, 0], lam[:, 0], lam[:, 2], lam[:, 2]], axis=-1)
    lim = jnp.concatenate([-lam[:, 1], lam[:, 1], -lam[:, 3], lam[:, 3]], axis=-1)
    lam_rows = jnp.stack([lre, lim], axis=1)
    return m.astype(BF16), pin_m.astype(BF16), pout_m.astype(BF16), lam_rows


def _s5_scan_kernel(u_ref, m_ref, pin_ref, pout_ref, lam_ref, h0_ref, y_ref, hfin_ref, g_scr, hin_scr, *, n_chunks, bsz):
    ub = u_ref[...].astype(BF16)
    g_scr[...] = _dot(ub, pin_ref[...])
    lre = lam_ref[0:1, :]
    lim = lam_ref[1:2, :]

    def body(c, carry):
        hf, hb = carry
        rf = pl.multiple_of(c * bsz, bsz)
        rb = pl.multiple_of((n_chunks - 1 - c) * bsz, bsz)
        hin_scr[pl.ds(rf, bsz), 0:128] = hf
        hin_scr[pl.ds(rb, bsz), 128:256] = hb
        hf = lre[:, 0:128] * hf + lim[:, 0:128] * pltpu.roll(hf, 64, 1) + g_scr[pl.ds(rf, bsz), 0:128]
        hb = lre[:, 128:256] * hb + lim[:, 128:256] * pltpu.roll(hb, 64, 1) + g_scr[pl.ds(rb, bsz), 128:256]
        return hf, hb

    h0 = h0_ref[...]
    hf, hb = lax.fori_loop(0, n_chunks, body, (h0[:, 0:128], h0[:, 128:256]))
    hfin_ref[:, 0:128] = hf
    hfin_ref[:, 128:256] = hb
    hi, lo = _split_bf16(hin_scr[...])
    pout = pout_ref[...]
    y_ref[...] = _dot(ub, m_ref[...]) + _dot(hi, pout) + _dot(lo, pout)


def _s5_scan(u, h0, m, pin, pout, lam_rows):
    bsz, seq, _ = u.shape
    G, P, T = S5_GROUPS, S5_STATE, S5_CHUNK
    nc = seq // T
    rows = nc * bsz
    ug = u.reshape(bsz, nc, T, G, S5_CH).transpose(3, 1, 0, 2, 4).reshape(G, rows, S5_TC)
    h0g = h0.transpose(2, 0, 1, 4, 3).reshape(G, bsz, 4 * P)
    spec = lambda s: pl.BlockSpec((None,) + s, lambda g: (g,) + (0,) * len(s))
    y, hfin = pl.pallas_call(
        functools.partial(_s5_scan_kernel, n_chunks=nc, bsz=bsz),
        grid=(G,),
        in_specs=[spec((rows, S5_TC)), spec((S5_TC, S5_TC)), spec((S5_TC, 4 * P)), spec((4 * P, S5_TC)),
                  spec((2, 4 * P)), spec((bsz, 4 * P))],
        out_specs=[spec((rows, S5_TC)), spec((bsz, 4 * P))],
        out_shape=[jax.ShapeDtypeStruct((G, rows, S5_TC), F32), jax.ShapeDtypeStruct((G, bsz, 4 * P), F32)],
        scratch_shapes=[pltpu.VMEM((rows, 4 * P), F32), pltpu.VMEM((rows, 4 * P), F32)],
        compiler_params=_cp("parallel"),
        name="s5_chunk_scan",
    )(ug, m, pin, pout, lam_rows, h0g)
    y = y.reshape(G, nc, bsz, T, S5_CH).transpose(2, 1, 3, 0, 4).reshape(bsz, seq, S5_WIDTH)
    hfin = hfin.reshape(G, bsz, 2, 2, P).transpose(1, 2, 0, 4, 3)
    return y, hfin


def _s5_out_kernel(y_ref, u_ref, d_ref, w_ref, nw_ref, o_ref):
    y = y_ref[...] + u_ref[...] * d_ref[...]
    c = math.sqrt(2.0 / math.pi)
    y = y * (0.5 * (1.0 + jnp.tanh(c * (y + 0.044715 * (y * y * y)))))
    z = _dot(y.astype(BF16), w_ref[...])
    o = z[:, :S5_WIDTH] * jax.nn.sigmoid(z[:, S5_WIDTH:])
    o_ref[...] = _rms(o, nw_ref[...]).astype(BF16)


def _s5_out(y2, u2, d_skip, w_glu, nw, tm):
    n = y2.shape[0]
    full = lambda a: pl.BlockSpec(a.shape, lambda i: (0,) * a.ndim)
    row = pl.BlockSpec((tm, S5_WIDTH), lambda i: (i, 0))
    return pl.pallas_call(
        _s5_out_kernel,
        grid=(n // tm,),
        in_specs=[row, row, full(d_skip), full(w_glu), full(nw)],
        out_specs=row,
        out_shape=jax.ShapeDtypeStruct((n, S5_WIDTH), BF16),
        compiler_params=_cp("parallel"),
        name="s5_gelu_glu",
    )(y2, u2, d_skip, w_glu, nw)


def _rwkv_pre_kernel(p_ref, hp_ref, hn_ref, mu_ref, w0_ref, a0_ref, w2_ref, a2_ref, g2_ref,
                     kk_ref, ka_ref, rk_ref, bd_ref,
                     v_ref, nk_ref, dec_ref, kd_ref, bb_ref, qq_ref, vkr_ref, g_ref, bonus_ref, *, tiles_per_seq, tm):
    i = pl.program_id(0)
    j = i % tiles_per_seq
    p = p_ref[...]
    rows = lax.broadcasted_iota(jnp.int32, (tm, 1), 0)
    prev_edge = jnp.where(j == 0, 0.0, hp_ref[7:8, :])
    next_edge = jnp.where(j == tiles_per_seq - 1, 0.0, hn_ref[0:1, :])
    prev = jnp.where(rows == 0, prev_edge, pltpu.roll(p, 1, 0))
    nxt = jnp.where(rows == tm - 1, next_edge, pltpu.roll(p, tm - 1, 0))
    p = p + mu_ref[...] * (0.5 * (prev + nxt) - p)
    W = RWKV_WIDTH
    r = p[:, 0:W]
    k = p[:, W:2 * W]
    v = p[:, 2 * W:3 * W]
    wl = p[:, 3 * W:3 * W + LORA_PAD]
    al = p[:, 3 * W + LORA_PAD:3 * W + 2 * LORA_PAD]
    gl = p[:, 3 * W + 2 * LORA_PAD:3 * W + 3 * LORA_PAD]
    bd = bd_ref[...]
    kk = k * kk_ref[...]
    kk = kk * lax.rsqrt(_segsum(kk * kk, bd) + 1e-12)
    g_ref[...] = _dot(jax.nn.sigmoid(gl).astype(BF16), g2_ref[...])
    tw = jnp.tanh(wl).astype(BF16)
    alb = al.astype(BF16)
    ksum = None
    for d in range(2):
        z = -(w0_ref[d:d + 1, :] + _dot(tw, w2_ref[d]))
        w = -(jnp.maximum(z, 0.0) + jnp.log(1.0 + jnp.exp(-jnp.abs(z)))) - 0.5
        dec = jnp.exp(-jnp.exp(w))
        dec_ref[d] = dec
        a = jax.nn.sigmoid(a0_ref[d:d + 1, :] + _dot(alb, a2_ref[d]))
        kd = k * (1.0 + (a - 1.0) * ka_ref[...])
        kd_ref[d] = kd
        bb = kk * a
        bb_ref[d] = bb
        qq_ref[d] = dec * r - kk * _segsum(bb * r, bd)
        vkr_ref[d] = v * _segsum(kd * r, bd)
        ksum = kd if ksum is None else ksum + kd
    v_ref[...] = v
    nk_ref[...] = -kk
    bonus_ref[...] = _segsum(r * ksum * rk_ref[...], bd) * v


def _rwkv_pre(p2, bsz, seq, wts, bd):
    n = p2.shape[0]
    tm = min(256, seq)
    tps = seq // tm
    r8 = tm // 8
    nblk8 = n // 8
    full = lambda a: pl.BlockSpec(a.shape, lambda i: (0,) * a.ndim)
    o3 = pl.BlockSpec((None, tm, RWKV_WIDTH), lambda i: (i // tps, i % tps, 0))
    o4 = pl.BlockSpec((2, None, tm, RWKV_WIDTH), lambda i: (0, i // tps, i % tps, 0))
    s3 = jax.ShapeDtypeStruct((bsz, seq, RWKV_WIDTH), F32)
    s4 = jax.ShapeDtypeStruct((2, bsz, seq, RWKV_WIDTH), F32)
    names = ("mu", "w0", "a0", "w2", "a2", "g2", "k_k", "k_a", "r_k")
    return pl.pallas_call(
        functools.partial(_rwkv_pre_kernel, tiles_per_seq=tps, tm=tm),
        grid=(n // tm,),
        in_specs=[pl.BlockSpec((tm, RWKV_COLS), lambda i: (i, 0)),
                  pl.BlockSpec((8, RWKV_COLS), lambda i: (jnp.maximum(i * r8 - 1, 0), 0)),
                  pl.BlockSpec((8, RWKV_COLS), lambda i: (jnp.minimum((i + 1) * r8, nblk8 - 1), 0))]
                 + [full(wts[k]) for k in names] + [full(bd)],
        out_specs=[o3, o3, o4, o4, o4, o4, o4, o3, o3],
        out_shape=[s3, s3, s4, s4, s4, s4, s4, s3, s3],
        compiler_params=_cp("parallel"),
        name="rwkv_prepare",
    )(p2, p2, p2, *[wts[k] for k in names], bd)


RWKV_SCAN_COLS = 32


def _rwkv_scan_kernel(nkf, nkb, vf, vb, decf, decb, kdf, kdb, bbf, bbb, qqf, qqb, vkf, vkb, s0_ref,
                      yf_ref, yb_ref, sfin_ref, s_scr, v8_scr, v8k_scr, y8_scr, *, nb, tlen, n_chunks):
    c = pl.program_id(1)

    @pl.when(c == 0)
    def _():
        s_scr[...] = s0_ref[...].reshape(s_scr.shape)

    N = RWKV_HEAD_DIM
    W = RWKV_WIDTH
    H = RWKV_HEADS
    nt = (((1,), (1,)), ((), ()))
    chains = [(d, n) for d in range(2) for n in range(nb)]
    pick = lambda d, f, b: f if d == 0 else b
    base = lambda m: m * tlen * H

    for m, (d, n) in enumerate(chains):
        for src, dst in ((pick(d, vf, vb), v8_scr), (pick(d, vkf, vkb), v8k_scr)):
            for p in range(H // 2):
                tile = src[n, :, p * 128:(p + 1) * 128]
                dst[pl.ds(base(m) + 2 * p, tlen, stride=H), :] = tile
                dst[pl.ds(base(m) + 2 * p + 1, tlen, stride=H), :] = pltpu.roll(tile, N, 1)

    hmask = (lax.broadcasted_iota(jnp.int32, (H, W), 1) // N == lax.broadcasted_iota(jnp.int32, (H, W), 0)).astype(F32)
    eye_pad = (lax.broadcasted_iota(jnp.int32, (N, 128), 1) == lax.broadcasted_iota(jnp.int32, (N, 128), 0)).astype(BF16)
    sel = (lax.broadcasted_iota(jnp.int32, (H, RWKV_SCAN_COLS), 1)
           == lax.broadcasted_iota(jnp.int32, (H, RWKV_SCAN_COLS), 0) + 2 * H).astype(BF16)
    z_w = jnp.zeros((H, W), F32)
    z_e = jnp.zeros((H, 128), F32)
    z_row = jnp.zeros((H, W + 128), F32)

    def step(t, carry):
        tts = (t, tlen - 1 - t)
        row = lambda ref, n, tt: ref[n, pl.ds(tt, 1), :] * hmask
        tile8 = lambda m, tt: pl.ds(pl.multiple_of(base(m) + tt * H, H), H)
        obs = []
        for m, (d, n) in enumerate(chains):
            tt = tts[d]
            rhs = jnp.concatenate([
                jnp.concatenate([row(pick(d, nkf, nkb), n, tt), z_e], axis=1),
                jnp.concatenate([z_w, v8_scr[tile8(m, tt), :]], axis=1),
                jnp.concatenate([row(pick(d, qqf, qqb), n, tt), v8k_scr[tile8(m, tt), :]], axis=1),
                z_row], axis=0).astype(BF16)
            lhs = jnp.concatenate([s_scr[m].astype(BF16), eye_pad], axis=1)
            obs.append(lax.dot_general(lhs, rhs, nt, preferred_element_type=F32).astype(BF16))
        for m, (d, n) in enumerate(chains):
            tt = tts[d]
            w2 = jnp.concatenate([row(pick(d, bbf, bbb), n, tt), row(pick(d, kdf, kdb), n, tt), z_w, z_w],
                                 axis=0).astype(BF16)
            s_scr[m] = s_scr[m] * pick(d, decf, decb)[n, pl.ds(tt, 1), :] + _dot(obs[m], w2)
        for m, (d, n) in enumerate(chains):
            ob2 = jnp.concatenate([obs[m], obs[m]], axis=0)
            y8_scr[tile8(m, tts[d]), :] = lax.dot_general(sel, ob2, nt, preferred_element_type=F32)
        return carry

    lax.fori_loop(0, tlen, step, 0, unroll=2)

    left =lax.broadcasted_iota(jnp.int32, (tlen, 128), 1) < N
    for m, (d, n) in enumerate(chains):
        y_ref = pick(d, yf_ref, yb_ref)
        for p in range(H // 2):
            even = y8_scr[pl.ds(base(m) + 2 * p, tlen, stride=H), :]
            odd = y8_scr[pl.ds(base(m) + 2 * p + 1, tlen, stride=H), :]
            y_ref[n, :, p * 128:(p + 1) * 128] = jnp.where(left, even, odd)

    @pl.when(c == n_chunks - 1)
    def _():
        sfin_ref[...] = s_scr[...].reshape(sfin_ref.shape)


def _rwkv_scan(nk, v, dec, kd, bb, qq, vkr, s0):
    bsz, seq, W = nk.shape
    N, H = RWKV_HEAD_DIM, RWKV_HEADS
    nb = 8 if bsz % 8 == 0 else 4
    tlen = min(32, seq)
    nc = seq // tlen
    fwd = pl.BlockSpec((nb, tlen, W), lambda b, c: (b, c, 0))
    bwd = pl.BlockSpec((nb, tlen, W), lambda b, c: (b, nc - 1 - c, 0))
    fwd_d = pl.BlockSpec((None, nb, tlen, W), lambda b, c: (0, b, c, 0))
    bwd_d = pl.BlockSpec((None, nb, tlen, W), lambda b, c: (1, b, nc - 1 - c, 0))
    st = pl.BlockSpec((2, nb, N, W), lambda b, c: (0, b, 0, 0))
    tiles = pltpu.VMEM((2 * nb * tlen * H, 128), F32)
    return pl.pallas_call(
        functools.partial(_rwkv_scan_kernel, nb=nb, tlen=tlen, n_chunks=nc),
        grid=(bsz // nb, nc),
        in_specs=[fwd, bwd, fwd, bwd] + [fwd_d, bwd_d] * 5 + [st],
        out_specs=[fwd, bwd, st],
        out_shape=[jax.ShapeDtypeStruct((bsz, seq, W), F32)] * 2 + [jax.ShapeDtypeStruct((2, bsz, N, W), F32)],
        scratch_shapes=[pltpu.VMEM((2 * nb, N, W), F32), tiles, tiles, tiles],
        compiler_params=_cp("parallel", "arbitrary"),
        name="rwkv_scan",
    )(nk, nk, v, v, dec, dec, kd, kd, bb, bb, qq, qq, vkr, vkr, s0)


def _rwkv_post_kernel(yf_ref, yb_ref, bonus_ref, g_ref, lw_ref, lb_ref, bd_ref, o_ref):
    bd = bd_ref[...]
    y = yf_ref[...] + yb_ref[...]
    inv_n = 1.0 / RWKV_HEAD_DIM
    mean = _segsum(y, bd) * inv_n
    yc = y - mean
    var = _segsum(yc * yc, bd) * inv_n
    yn = yc * lax.rsqrt(var + GN_EPS) * lw_ref[...] + lb_ref[...]
    o_ref[...] = ((yn + bonus_ref[...]) * g_ref[...]).astype(BF16)


def _rwkv_post(yf, yb, bonus, g, ln_w, ln_b, bd):
    bsz, seq, W = yf.shape
    tm = min(256, seq)
    tps = seq // tm
    n = bsz * seq
    full = lambda a: pl.BlockSpec(a.shape, lambda i: (0,) * a.ndim)
    i3 = pl.BlockSpec((None, tm, W), lambda i: (i // tps, i % tps, 0))
    return pl.pallas_call(
        _rwkv_post_kernel,
        grid=(n // tm,),
        in_specs=[i3, i3, i3, i3, full(ln_w), full(ln_b), full(bd)],
        out_specs=pl.BlockSpec((tm, W), lambda i: (i, 0)),
        out_shape=jax.ShapeDtypeStruct((n, W), BF16),
        compiler_params=_cp("parallel"),
        name="rwkv_groupnorm_gate",
    )(yf, yb, bonus, g, ln_w, ln_b, bd)


def _store_k_heads(k_ref, kn, kpe):
    for h in range(MLA_HEADS):
        k_ref[:, h * MLA_QK_PAD:h * MLA_QK_PAD + 128] = kn[:, h * 128:(h + 1) * 128].astype(BF16)
        k_ref[:, h * MLA_QK_PAD + 128:(h + 1) * MLA_QK_PAD] = kpe


def _mla_prep_kernel(*refs, rope):
    if rope:
        (p_ref, qn_ref, kvn_ref, wq_ref, wqr_ref, wk_ref, wv_ref, cq_ref, sq_ref, ck_ref, sk_ref,
         q_ref, k_ref, v_ref) = refs
    else:
        (p_ref, qn_ref, kvn_ref, wq_ref, wk_ref, wv_ref, q_ref, k_ref, v_ref, ckv_ref, kr_ref) = refs
    p = p_ref[...]
    qn = _rms(p[:, 0:MLA_Q_RANK], qn_ref[...]).astype(BF16)
    q = _dot(qn, wq_ref[...])
    ckv = _rms(p[:, MLA_Q_RANK:MLA_Q_RANK + MLA_KV_RANK], kvn_ref[...])
    kr = p[:, MLA_Q_RANK + MLA_KV_RANK:MLA_Q_RANK + MLA_KV_RANK + 128]
    if rope:
        cq = jnp.concatenate([cq_ref[...]] * MLA_HEADS, axis=1)
        sq = jnp.concatenate([sq_ref[...]] * MLA_HEADS, axis=1)
        q = q * cq + _dot(qn, wqr_ref[...]) * sq
        krot = p[:, MLA_Q_RANK + MLA_KV_RANK + 128:MLA_Q_RANK + MLA_KV_RANK + 256]
        kpe = kr * ck_ref[...] + krot * sk_ref[...]
    else:
        kpe = kr
        ckv_ref[...] = ckv
        kr_ref[...] = kr[:, 0:MLA_ROPE_DIM]
    q_ref[...] = q.astype(BF16)
    cb = ckv.astype(BF16)
    _store_k_heads(k_ref, _dot(cb, wk_ref[...]), kpe.astype(BF16))
    v_ref[...] = _dot(cb, wv_ref[...]).astype(BF16)


def _mla_prep(p2, seq, wts, tables):
    n = p2.shape[0]
    rope = tables is not None
    tm = min(256, seq)
    tps = seq // tm
    full = lambda a: pl.BlockSpec(a.shape, lambda i: (0,) * a.ndim)
    row = lambda w: pl.BlockSpec((tm, w), lambda i: (i, 0))
    ins = [p2, wts["q_norm"], wts["kv_norm"], wts["wq"]]
    specs = [row(MLA_COLS), full(wts["q_norm"]), full(wts["kv_norm"]), full(wts["wq"])]
    if rope:
        ins.append(wts["wq_rot"])
        specs.append(full(wts["wq_rot"]))
    ins += [wts["wk"], wts["wv"]]
    specs += [full(wts["wk"]), full(wts["wv"])]
    outs = [row(MLA_HEADS * MLA_QK_PAD), row(MLA_HEADS * MLA_QK_PAD), row(MLA_HEADS * MLA_V_DIM)]
    shapes = [jax.ShapeDtypeStruct((n, MLA_HEADS * MLA_QK_PAD), BF16),
              jax.ShapeDtypeStruct((n, MLA_HEADS * MLA_QK_PAD), BF16),
              jax.ShapeDtypeStruct((n, MLA_HEADS * MLA_V_DIM), BF16)]
    if rope:
        ins += list(tables)
        specs += [pl.BlockSpec((tm, t.shape[1]), lambda i: (i % tps, 0)) for t in tables]
    else:
        outs += [row(MLA_KV_RANK), row(MLA_ROPE_DIM)]
        shapes += [jax.ShapeDtypeStruct((n, MLA_KV_RANK), F32), jax.ShapeDtypeStruct((n, MLA_ROPE_DIM), F32)]
    return pl.pallas_call(
        functools.partial(_mla_prep_kernel, rope=rope),
        grid=(n // tm,),
        in_specs=specs, out_specs=outs, out_shape=shapes,
        compiler_params=_cp("parallel"),
        name="mla_prepare_rope" if rope else "mla_prepare",
    )(*ins)


def _mla_cache_kernel(ckv_ref, kr_ref, wk_ref, wv_ref, k_ref, v_ref):
    cb = ckv_ref[...].astype(BF16)
    _store_k_heads(k_ref, _dot(cb, wk_ref[...]), kr_ref[...].astype(BF16))
    v_ref[...] = _dot(cb, wv_ref[...]).astype(BF16)


def _mla_cache(cache_ckv, cache_kr_pad, layer, wk, wv):
    bsz, _, past, _ = cache_ckv.shape
    full = lambda a: pl.BlockSpec(a.shape, lambda b: (0,) * a.ndim)
    return pl.pallas_call(
        _mla_cache_kernel,
        grid=(bsz,),
        in_specs=[pl.BlockSpec((None, None, past, MLA_KV_RANK), lambda b: (b, layer, 0, 0)),
                  pl.BlockSpec((None, None, past, 128), lambda b: (b, layer, 0, 0)),
                  full(wk), full(wv)],
        out_specs=[pl.BlockSpec((None, past, MLA_HEADS * MLA_QK_PAD), lambda b: (b, 0, 0)),
                   pl.BlockSpec((None, past, MLA_HEADS * MLA_V_DIM), lambda b: (b, 0, 0))],
        out_shape=[jax.ShapeDtypeStruct((bsz, past, MLA_HEADS * MLA_QK_PAD), BF16),
                   jax.ShapeDtypeStruct((bsz, past, MLA_HEADS * MLA_V_DIM), BF16)],
        compiler_params=_cp("parallel"),
        name="mla_cache_keys",
    )(cache_ckv, cache_kr_pad, wk, wv)


def _attn_kernel(*refs, cache, scale):
    if cache:
        q_ref, k_ref, v_ref, kc_ref, vc_ref, o_ref = refs
    else:
        q_ref, k_ref, v_ref, o_ref = refs
    nt = (((1,), (1,)), ((), ()))
    q = q_ref[...]
    s = lax.dot_general(q, k_ref[...], nt, preferred_element_type=F32) * scale
    m = jnp.max(s, axis=-1, keepdims=True)
    if cache:
        sc = lax.dot_general(q, kc_ref[...], nt, preferred_element_type=F32) * scale
        m = jnp.maximum(m, jnp.max(sc, axis=-1, keepdims=True))
    e = jnp.exp(s - m)
    den = jnp.sum(e, axis=-1, keepdims=True)
    o = _dot(e.astype(BF16), v_ref[...])
    if cache:
        ec = jnp.exp(sc - m)
        den = den + jnp.sum(ec, axis=-1, keepdims=True)
        o = o + _dot(ec.astype(BF16), vc_ref[...])
    o_ref[...] = o / den


def _attention(q, k, v, kc=None, vc=None):
    bsz, seq, _ = q.shape
    tq = min(256, seq)
    cache = kc is not None
    scale = float(MLA_NOPE_DIM + MLA_ROPE_DIM) ** -0.5
    ins = [q, k, v]
    specs = [pl.BlockSpec((None, tq, MLA_QK_PAD), lambda b, h, i: (b, i, h)),
             pl.BlockSpec((None, seq, MLA_QK_PAD), lambda b, h, i: (b, 0, h)),
             pl.BlockSpec((None, seq, MLA_V_DIM), lambda b, h, i: (b, 0, h))]
    if cache:
        past = kc.shape[1]
        ins += [kc, vc]
        specs += [pl.BlockSpec((None, past, MLA_QK_PAD), lambda b, h, i: (b, 0, h)),
                  pl.BlockSpec((None, past, MLA_V_DIM), lambda b, h, i: (b, 0, h))]
    return pl.pallas_call(
        functools.partial(_attn_kernel, cache=cache, scale=scale),
        grid=(bsz, MLA_HEADS, seq // tq),
        in_specs=specs,
        out_specs=pl.BlockSpec((None, tq, MLA_V_DIM), lambda b, h, i: (b, i, h)),
        out_shape=jax.ShapeDtypeStruct((bsz, seq, MLA_HEADS * MLA_V_DIM), F32),
        compiler_params=_cp("parallel", "parallel", "arbitrary"),
        name="mla_attention_cached" if cache else "mla_attention",
    )(*ins)


def _outproj_kernel(ys_ref, yr_ref, ym_ref, x_ref, mod_ref, nm_ref, w_ref, o_ref):
    ym = _rms(ym_ref[...], nm_ref[...]).astype(BF16)
    acc = _dot(ys_ref[...], w_ref[0:S5_WIDTH, :])
    acc += _dot(yr_ref[...], w_ref[S5_WIDTH:S5_WIDTH + RWKV_WIDTH, :])
    acc += _dot(ym, w_ref[S5_WIDTH + RWKV_WIDTH:, :])
    o_ref[...] = x_ref[...] + mod_ref[2:3, :] * acc


def _outproj(ys, yr, ym, x2, mod, layer, row_of_tile, tm, nm, w_out):
    n = x2.shape[0]
    full = lambda a: pl.BlockSpec(a.shape, lambda i: (0,) * a.ndim)
    row = lambda w: pl.BlockSpec((tm, w), lambda i: (i, 0))
    return pl.pallas_call(
        _outproj_kernel,
        grid=(n // tm,),
        in_specs=[row(S5_WIDTH), row(RWKV_WIDTH), row(MLA_WIDTH), row(D_MODEL), _mod_spec(layer, row_of_tile),
                  full(nm), full(w_out)],
        out_specs=row(D_MODEL),
        out_shape=jax.ShapeDtypeStruct((n, D_MODEL), F32),
        compiler_params=_cp("parallel"),
        name="out_projection",
    )(ys, yr, ym, x2, mod, nm, w_out)


def _mlp_kernel(x_ref, nw_ref, mod_ref, w1_ref, w2_ref, nf_ref, o_ref, h_scr, acc_scr, *, final_norm):
    j = pl.program_id(1)

    @pl.when(j == 0)
    def _():
        h = _rms(x_ref[...], nw_ref[...]) * (1.0 + mod_ref[4:5, :]) + mod_ref[3:4, :]
        h_scr[...] = h.astype(BF16)
        acc_scr[...] = jnp.zeros_like(acc_scr)

    a = _dot(h_scr[...], w1_ref[...])
    a = jnp.square(jnp.maximum(a, 0.0)).astype(BF16)
    acc_scr[...] += _dot(a, w2_ref[...])

    @pl.when(j == pl.num_programs(1) - 1)
    def _():
        y = x_ref[...] + mod_ref[5:6, :] * acc_scr[...]
        if final_norm:
            y = _rms(y, nf_ref[...])
        o_ref[...] = y


def _mlp(x2, nw, mod, layer, row_of_tile, tm, w1, w2, nf, final_norm):
    n = x2.shape[0]
    tf = 1024
    full = lambda a: pl.BlockSpec(a.shape, lambda i, j: (0,) * a.ndim)
    return pl.pallas_call(
        functools.partial(_mlp_kernel, final_norm=final_norm),
        grid=(n // tm, D_FF // tf),
        in_specs=[pl.BlockSpec((tm, D_MODEL), lambda i, j: (i, 0)), full(nw), _mod_spec(layer, row_of_tile),
                  pl.BlockSpec((D_MODEL, tf), lambda i, j: (0, j)),
                  pl.BlockSpec((tf, D_MODEL), lambda i, j: (j, 0)), full(nf)],
        out_specs=pl.BlockSpec((tm, D_MODEL), lambda i, j: (i, 0)),
        out_shape=jax.ShapeDtypeStruct((n, D_MODEL), F32),
        scratch_shapes=[pltpu.VMEM((tm, D_MODEL), BF16), pltpu.VMEM((tm, D_MODEL), F32)],
        compiler_params=_cp("parallel", "arbitrary"),
        name="mlp_final" if final_norm else "mlp",
    )(x2, nw, mod, w1, w2, nf)


def _rope_tables(length):
    rows = length // GRID_W
    row_pos = jnp.repeat(jnp.arange(rows, dtype=F32), GRID_W)
    col_pos = jnp.tile(jnp.arange(GRID_W, dtype=F32), rows)
    axis_dim = MLA_ROPE_DIM // 2
    inv_freq = 1.0 / (ROPE_THETA ** (jnp.arange(0, axis_dim, 2, dtype=F32) / axis_dim))
    ang_r = row_pos[:, None] * inv_freq[None, :]
    ang_c = col_pos[:, None] * inv_freq[None, :]
    ang = jnp.concatenate([ang_r, ang_r, ang_c, ang_c], axis=-1)
    cos, sin = jnp.cos(ang), jnp.sin(ang)
    z64 = jnp.zeros((length, 64), F32)
    cos_q = jnp.concatenate([jnp.ones((length, MLA_NOPE_DIM), F32), cos, z64], axis=1)
    sin_q = jnp.concatenate([jnp.zeros((length, MLA_NOPE_DIM), F32), sin, z64], axis=1)
    cos_k = jnp.concatenate([cos, z64], axis=1)
    sin_k = jnp.concatenate([sin, z64], axis=1)
    return cos_q, sin_q, cos_k, sin_k


def _rot_cols(w):
    a, b, c, d = w[..., 0:16], w[..., 16:32], w[..., 32:48], w[..., 48:64]
    return jnp.concatenate([-b, a, -d, c], axis=-1)


def _layer_weights(l, p):
    d = D_MODEL
    w_in = p["w_in"][l]
    z64 = jnp.zeros((d, 64), F32)
    o = S5_WIDTH
    rk = w_in[:, o:o + 3 * RWKV_WIDTH]
    o += 3 * RWKV_WIDTH
    wl, al, gl = w_in[:, o:o + 64], w_in[:, o + 64:o + 128], w_in[:, o + 128:o + 256]
    o += 256
    cq, ckv, kr = w_in[:, o:o + 512], w_in[:, o + 512:o + 768], w_in[:, o + 768:o + 832]
    mu = p["rwkv_mu"][l]
    z1 = jnp.zeros((64,), F32)
    mu_pad = jnp.concatenate([mu[:1536], mu[1536:1600], z1, mu[1600:1664], z1, mu[1664:1792]])[None]
    pad_rows = lambda w: jnp.concatenate([w, jnp.zeros_like(w)], axis=-2)
    w_uq = p["mla_w_uq"][l].reshape(MLA_Q_RANK, MLA_HEADS, MLA_NOPE_DIM + MLA_ROPE_DIM)
    zq = jnp.zeros((MLA_Q_RANK, MLA_HEADS, 64), F32)
    wq = jnp.concatenate([w_uq, zq], axis=-1).reshape(MLA_Q_RANK, MLA_HEADS * MLA_QK_PAD)
    wq_rot = jnp.concatenate([jnp.zeros((MLA_Q_RANK, MLA_HEADS, MLA_NOPE_DIM), F32),
                              _rot_cols(w_uq[..., MLA_NOPE_DIM:]), zq], axis=-1).reshape(MLA_Q_RANK, MLA_HEADS * MLA_QK_PAD)
    w_ukv = p["mla_w_ukv"][l].reshape(MLA_KV_RANK, MLA_HEADS, MLA_NOPE_DIM + MLA_V_DIM)
    row = lambda a: a.reshape(1, -1)
    return {
        "norm_mix": row(p["norm_mix"][l]), "norm_mlp": row(p["norm_mlp"][l]),
        "w_s5": w_in[:, 0:S5_WIDTH].astype(BF16),
        "w_rwkv": jnp.concatenate([rk, wl, z64, al, z64, gl], axis=1).astype(BF16),
        "w_mla": jnp.concatenate([cq, ckv, kr, z64, _rot_cols(kr), z64], axis=1).astype(BF16),
        "w_out": p["w_out"][l].astype(BF16),
        "s5_d": row(p["s5_d"][l]), "s5_w_glu": p["s5_w_glu"][l].astype(BF16), "s5_out_norm": row(p["s5_out_norm"][l]),
        "rwkv": {
            "mu": mu_pad, "w0": p["rwkv_w0"][l], "a0": p["rwkv_a0"][l],
            "w2": pad_rows(p["rwkv_w2"][l]).astype(BF16), "a2": pad_rows(p["rwkv_a2"][l]).astype(BF16),
            "g2": p["rwkv_g2"][l].astype(BF16), "k_k": row(p["rwkv_k_k"][l]), "k_a": row(p["rwkv_k_a"][l]),
            "r_k": row(p["rwkv_r_k"][l]),
        },
        "rwkv_ln_w": row(p["rwkv_ln_w"][l]), "rwkv_ln_b": row(p["rwkv_ln_b"][l]),
        "mla": {
            "q_norm": row(p["mla_q_norm"][l]), "kv_norm": row(p["mla_kv_norm"][l]),
            "wq": wq.astype(BF16), "wq_rot": wq_rot.astype(BF16),
            "wk": w_ukv[..., :MLA_NOPE_DIM].reshape(MLA_KV_RANK, -1).astype(BF16),
            "wv": w_ukv[..., MLA_NOPE_DIM:].reshape(MLA_KV_RANK, -1).astype(BF16),
        },
        "mla_out_norm": row(p["mla_out_norm"][l]),
        "mlp_w1": p["mlp_w1"][l].astype(BF16), "mlp_w2": p["mlp_w2"][l].astype(BF16),
    }


def _trunk_layer(x, mod, layer, lw, s5w, bd, row_of_tile_fn, s5_h0, rwkv_s0, cache, tables, norm_final, final_norm):
    bsz, seq, d = x.shape
    n = bsz * seq
    x2 = x.reshape(n, d)
    tm = min(512, seq) if cache is not None else min(512, n)
    row_of_tile = row_of_tile_fn(tm)
    u2, pr2, pm2 = _inproj(x2, lw["norm_mix"], mod, layer, row_of_tile, tm, lw["w_s5"], lw["w_rwkv"], lw["w_mla"])

    ys, s5_fin = _s5_scan(u2.reshape(bsz, seq, S5_WIDTH), s5_h0, *s5w)
    ys2 = _s5_out(ys.reshape(n, S5_WIDTH), u2, lw["s5_d"], lw["s5_w_glu"], lw["s5_out_norm"], min(512, n))

    v, nk, dec, kd, bb, qq, vkr, g, bonus = _rwkv_pre(pr2, bsz, seq, lw["rwkv"], bd)
    s0 = rwkv_s0.transpose(1, 0, 3, 2, 4).reshape(2, bsz, RWKV_HEAD_DIM, RWKV_WIDTH)
    yf, yb, s_fin = _rwkv_scan(nk, v, dec, kd, bb, qq, vkr, s0)
    yr2 = _rwkv_post(yf, yb, bonus, g, lw["rwkv_ln_w"], lw["rwkv_ln_b"], bd)
    rwkv_fin = s_fin.reshape(2, bsz, RWKV_HEAD_DIM, RWKV_HEADS, RWKV_HEAD_DIM).transpose(1, 0, 3, 2, 4)

    shape3 = lambda a: a.reshape(bsz, seq, a.shape[-1])
    if cache is None:
        q, k, v_, ckv_n, k_rope = _mla_prep(pm2, seq, lw["mla"], None)
        ym = _attention(shape3(q), shape3(k), shape3(v_))
        extras = (shape3(ckv_n), shape3(k_rope), s5_fin, rwkv_fin)
    else:
        q, k, v_ = _mla_prep(pm2, seq, lw["mla"], tables)
        kc, vc = _mla_cache(cache[0], cache[1], layer, lw["mla"]["wk"], lw["mla"]["wv"])
        ym = _attention(shape3(q), shape3(k), shape3(v_), kc, vc)
        extras = None

    x2 = _outproj(ys2, yr2, ym.reshape(n, MLA_WIDTH), x2, mod, layer, row_of_tile, tm, lw["mla_out_norm"], lw["w_out"])
    x2 = _mlp(x2, lw["norm_mlp"], mod, layer, row_of_tile, tm, lw["mlp_w1"], lw["mlp_w2"], norm_final, final_norm)
    return x2.reshape(bsz, seq, d), extras


def kernel(x_prompt, x_sample, cache_mla_ckv, cache_mla_krope, state_s5, state_rwkv, c, c_ctx, norm_mix, norm_mlp, norm_final, w_ada, b_ada, w_in, w_out, s5_a_re, s5_a_im, s5_log_dt, s5_b_re, s5_b_im, s5_c_re, s5_c_im, s5_d, s5_w_glu, s5_out_norm, rwkv_mu, rwkv_w0, rwkv_w2, rwkv_a0, rwkv_a2, rwkv_g2, rwkv_k_k, rwkv_k_a, rwkv_r_k, rwkv_ln_w, rwkv_ln_b, mla_q_norm, mla_w_uq, mla_kv_norm, mla_w_ukv, mla_out_norm, mlp_w1, mlp_w2):
    p = dict(norm_mix=norm_mix, norm_mlp=norm_mlp, w_in=w_in, w_out=w_out, s5_d=s5_d, s5_w_glu=s5_w_glu,
             s5_out_norm=s5_out_norm, rwkv_mu=rwkv_mu, rwkv_w0=rwkv_w0, rwkv_w2=rwkv_w2, rwkv_a0=rwkv_a0,
             rwkv_a2=rwkv_a2, rwkv_g2=rwkv_g2, rwkv_k_k=rwkv_k_k, rwkv_k_a=rwkv_k_a, rwkv_r_k=rwkv_r_k,
             rwkv_ln_w=rwkv_ln_w, rwkv_ln_b=rwkv_ln_b, mla_q_norm=mla_q_norm, mla_w_uq=mla_w_uq,
             mla_kv_norm=mla_kv_norm, mla_w_ukv=mla_w_ukv, mla_out_norm=mla_out_norm, mlp_w1=mlp_w1, mlp_w2=mlp_w2)
    depth = w_in.shape[0]
    b_ctx, l_ctx, d = x_prompt.shape
    b_dec, l_dec, _ = x_sample.shape

    rows = -(-(1 + b_dec) // 8) * 8
    cond = jnp.zeros((rows, d), F32).at[0].set(c_ctx).at[1:1 + b_dec].set(c)
    mod = _modulation(cond, w_ada, b_ada).reshape(depth, rows, N_MOD, d)

    bd = jnp.kron(jnp.eye(4, dtype=F32), jnp.ones((64, 64), F32)).astype(BF16)
    tables = _rope_tables(l_dec)
    kr_pad = jnp.pad(cache_mla_krope, ((0, 0), (0, 0), (0, 0), (0, 128 - MLA_ROPE_DIM)))
    zero_s5 = jnp.zeros((b_ctx, 2, S5_GROUPS, S5_STATE, 2), F32)
    zero_rwkv = jnp.zeros((b_ctx, 2, RWKV_HEADS, RWKV_HEAD_DIM, RWKV_HEAD_DIM), F32)
    nf = norm_final.reshape(1, d)

    ctx_rows = lambda tm: (lambda i: 0)
    dec_rows = lambda tm: (lambda i: 1 + i // (l_dec // tm))

    xp, xs = x_prompt, x_sample
    new_ckv, new_krope, new_s5, new_rwkv = [], [], [], []
    for l in range(depth):
        lw = _layer_weights(l, p)
        s5w = _s5_prep(s5_a_re[l], s5_a_im[l], s5_log_dt[l], s5_b_re[l], s5_b_im[l], s5_c_re[l], s5_c_im[l])
        last = l == depth - 1
        xp, (ckv_l, krope_l, s5_l, rwkv_l) = _trunk_layer(
            xp, mod, l, lw, s5w, bd, ctx_rows, zero_s5, zero_rwkv, None, None, nf, last)
        new_ckv.append(ckv_l)
        new_krope.append(krope_l)
        new_s5.append(s5_l)
        new_rwkv.append(rwkv_l)
        xs, _ = _trunk_layer(
            xs, mod, l, lw, s5w, bd, dec_rows, state_s5[:, l], state_rwkv[:, l], (cache_mla_ckv, kr_pad), tables, nf, last)
    return (xp, xs, jnp.stack(new_ckv, axis=1), jnp.stack(new_krope, axis=1),
            jnp.stack(new_s5, axis=1), jnp.stack(new_rwkv, axis=1))
```

```python
import functools
import math

import jax
import jax.numpy as jnp
from jax import lax
from jax.experimental import pallas as pl
from jax.experimental.pallas import tpu as pltpu

F32 = jnp.float32
BF16 = jnp.bfloat16

D_MODEL = 2048
N_MOD = 6
GRID_W = 64
S5_WIDTH = 512
S5_CH = 16
S5_GROUPS = 32
S5_STATE = 64
S5_CHUNK = 8
S5_TILE_GROUPS = 8
RWKV_WIDTH = 512
RWKV_HEAD_DIM = 64
RWKV_HEADS = 8
LORA_PAD = 128
RWKV_COLS = 3 * RWKV_WIDTH + 3 * LORA_PAD
MLA_HEADS = 8
MLA_V_DIM = 128
MLA_NOPE_DIM = 128
MLA_ROPE_DIM = 64
MLA_QK_PAD = 256
MLA_Q_RANK = 512
MLA_KV_RANK = 256
MLA_WIDTH = 1024
MLA_COLS = MLA_Q_RANK + MLA_KV_RANK + 2 * 128
D_FF = 8192
ROPE_THETA = 10000.0
NORM_EPS = 1e-6
GN_EPS = 64e-5

VMEM_LIMIT_BYTES = 56 * 1024 * 1024


def _cp(*sem):
    return pltpu.CompilerParams(dimension_semantics=sem, vmem_limit_bytes=VMEM_LIMIT_BYTES)


def _dot(a, b):
    return jnp.dot(a, b, preferred_element_type=F32)


def _rms(x, g):
    ms = jnp.mean(x * x, axis=-1, keepdims=True)
    return x * lax.rsqrt(ms + NORM_EPS) * g


def _split_bf16(x):
    hi = x.astype(BF16)
    lo = (x - hi.astype(F32)).astype(BF16)
    return hi, lo


def _segsum(x, bd):
    hi, lo = _split_bf16(x)
    left = _dot(hi[:, :256], bd) + _dot(lo[:, :256], bd)
    right = _dot(hi[:, 256:], bd) + _dot(lo[:, 256:], bd)
    return jnp.concatenate([left, right], axis=1)


def _segsum_bf16(xb, bd):
    return jnp.concatenate([_dot(xb[:, :256], bd), _dot(xb[:, 256:], bd)], axis=1)


def _mod_kernel(c_ref, w_ref, b_ref, o_ref):
    c = c_ref[...]
    s = (c * jax.nn.sigmoid(c)).astype(BF16)
    o_ref[...] = _dot(s, w_ref[...].astype(BF16)) + b_ref[...]


def _modulation(cond, w_ada, b_ada):
    depth, d, n = w_ada.shape
    rows = cond.shape[0]
    tn = 1024
    return pl.pallas_call(
        _mod_kernel,
        grid=(depth, n // tn),
        in_specs=[
            pl.BlockSpec((rows, d), lambda l, j: (0, 0)),
            pl.BlockSpec((None, d, tn), lambda l, j: (l, 0, j)),
            pl.BlockSpec((None, 1, tn), lambda l, j: (l, 0, j)),
        ],
        out_specs=pl.BlockSpec((None, rows, tn), lambda l, j: (l, 0, j)),
        out_shape=jax.ShapeDtypeStruct((depth, rows, n), F32),
        compiler_params=_cp("parallel", "arbitrary"),
        name="adaln_modulation",
    )(cond, w_ada, b_ada.reshape(depth, 1, n))


def _mod_spec(layer, row_of_tile):
    return pl.BlockSpec((None, None, N_MOD, D_MODEL), lambda i, *_: (layer, row_of_tile(i), 0, 0))


def _inproj_kernel(x_ref, nw_ref, mod_ref, ws_ref, wr_ref, wm_ref, os_ref, or_ref, om_ref):
    h = _rms(x_ref[...], nw_ref[...]) * (1.0 + mod_ref[1:2, :]) + mod_ref[0:1, :]
    hb = h.astype(BF16)
    os_ref[...] = _dot(hb, ws_ref[...])
    or_ref[...] = _dot(hb, wr_ref[...])
    om_ref[...] = _dot(hb, wm_ref[...])


def _inproj(x2, nw, mod, layer, row_of_tile, tm, ws, wr, wm):
    n = x2.shape[0]
    full = lambda a: pl.BlockSpec(a.shape, lambda i: (0,) * a.ndim)
    return pl.pallas_call(
        _inproj_kernel,
        grid=(n // tm,),
        in_specs=[
            pl.BlockSpec((tm, D_MODEL), lambda i: (i, 0)),
            full(nw),
            _mod_spec(layer, row_of_tile),
            full(ws), full(wr), full(wm),
        ],
        out_specs=[
            pl.BlockSpec((tm, S5_WIDTH), lambda i: (i, 0)),
            pl.BlockSpec((tm, RWKV_COLS), lambda i: (i, 0)),
            pl.BlockSpec((tm, MLA_COLS), lambda i: (i, 0)),
        ],
        out_shape=[
            jax.ShapeDtypeStruct((n, S5_WIDTH), F32),
            jax.ShapeDtypeStruct((n, RWKV_COLS), F32),
            jax.ShapeDtypeStruct((n, MLA_COLS), F32),
        ],
        compiler_params=_cp("parallel"),
        name="in_projection",
    )(x2, nw, mod, ws, wr, wm)


def _s5_prep_kernel(are_ref, aim_ref, ldt_ref, bre_ref, bim_ref, cre_ref, cim_ref,
                    k_ref, pin_ref, poutt_ref, lam_ref):
    T = S5_CHUNK
    for d in range(2):
        are = are_ref[d:d + 1, :]
        aim = aim_ref[d:d + 1, :]
        dt = jnp.exp(ldt_ref[d:d + 1, :])
        lre = jnp.exp(are * dt) * jnp.cos(aim * dt)
        lim = jnp.exp(are * dt) * jnp.sin(aim * dt)
        den = are * are + aim * aim
        xr = lre - 1.0
        zre = (xr * are + lim * aim) / den
        zim = (lim * are - xr * aim) / den
        bre = bre_ref[d]
        bim = bim_ref[d]
        bbre = zre * bre - zim * bim
        bbim = zre * bim + zim * bre
        cre = cre_ref[d]
        cim = cim_ref[d]

        def powers(tau):
            mag = jnp.exp(tau * (are * dt))
            ang = tau * (aim * dt)
            return mag * jnp.cos(ang), mag * jnp.sin(ang)

        tau0 = lax.broadcasted_iota(jnp.int32, (T, 1), 0).astype(F32)
        ere, eim = powers(tau0)
        xre = (ere[:, None, :] * cre[None] - eim[:, None, :] * cim[None]).reshape(T * S5_CH, S5_STATE)
        xim = (ere[:, None, :] * cim[None] + eim[:, None, :] * cre[None]).reshape(T * S5_CH, S5_STATE)
        nt = (((1,), (1,)), ((), ()))
        k_ref[d] = (lax.dot_general(xre, bbre, nt, precision=lax.Precision.HIGHEST, preferred_element_type=F32)
                    - lax.dot_general(xim, bbim, nt, precision=lax.Precision.HIGHEST, preferred_element_type=F32))
        tau_out = tau0 + 1.0 if d == 0 else float(T) - tau0
        ore, oim = powers(tau_out)
        poutt_ref[2 * d] = (ore[:, None, :] * cre[None] - oim[:, None, :] * cim[None]).reshape(T * S5_CH, S5_STATE)
        poutt_ref[2 * d + 1] = -(ore[:, None, :] * cim[None] + oim[:, None, :] * cre[None]).reshape(T * S5_CH, S5_STATE)
        tau_in = float(T - 1) - tau0 if d == 0 else tau0
        ire, iim = powers(tau_in)
        pin_ref[2 * d] = (ire[:, None, :] * bbre[None] - iim[:, None, :] * bbim[None]).reshape(T * S5_CH, S5_STATE)
        pin_ref[2 * d + 1] = (ire[:, None, :] * bbim[None] + iim[:, None, :] * bbre[None]).reshape(T * S5_CH, S5_STATE)
        tre, tim = powers(jnp.full((1, 1), float(T), F32))
        lam_ref[2 * d:2 * d + 1, :] = tre
        lam_ref[2 * d + 1:2 * d + 2, :] = tim


def _s5_prep(a_re, a_im, log_dt, b_re, b_im, c_re, c_im):
    G, P, CH, T = S5_GROUPS, S5_STATE, S5_CH, S5_CHUNK
    g_first = lambda a: jnp.swapaxes(a, 0, 1)
    are = g_first(a_re)
    aim = g_first(a_im)
    ldt = g_first(log_dt)[..., None]
    bre = jnp.swapaxes(g_first(b_re), -1, -2)
    bim = jnp.swapaxes(g_first(b_im), -1, -2)
    cre = g_first(c_re)
    cim = g_first(c_im)
    spec3 = lambda s: pl.BlockSpec((None,) + s, lambda g: (g,) + (0,) * len(s))
    k, pin, poutt, lam = pl.pallas_call(
        _s5_prep_kernel,
        grid=(G,),
        in_specs=[spec3((2, P)), spec3((2, P)), spec3((2, 1)),
                  spec3((2, CH, P)), spec3((2, CH, P)), spec3((2, CH, P)), spec3((2, CH, P))],
        out_specs=[spec3((2, T * CH, CH)), spec3((4, T * CH, P)), spec3((4, T * CH, P)), spec3((4, P))],
        out_shape=[jax.ShapeDtypeStruct((G, 2, T * CH, CH), F32),
                   jax.ShapeDtypeStruct((G, 4, T * CH, P), F32),
                   jax.ShapeDtypeStruct((G, 4, T * CH, P), F32),
                   jax.ShapeDtypeStruct((G, 4, P), F32)],
        compiler_params=_cp("parallel"),
        name="s5_weight_prep",
    )(are, aim, ldt, bre, bim, cre, cim)
    k = k.reshape(G, 2, T, CH, CH)
    s_idx = jnp.arange(T)[:, None]
    t_idx = jnp.arange(T)[None, :]
    kf = jnp.where((t_idx >= s_idx)[None, :, :, None, None], k[:, 0][:, jnp.clip(t_idx - s_idx, 0, T - 1)], 0.0)
    kb = jnp.where((t_idx <= s_idx)[None, :, :, None, None], k[:, 1][:, jnp.clip(s_idx - t_idx, 0, T - 1)], 0.0)
    m = (kf + kb).transpose(0, 1, 4, 2, 3)
    GT, G8 = G // S5_TILE_GROUPS, S5_TILE_GROUPS
    eye8 = jnp.eye(G8, dtype=F32)
    w_big = jnp.einsum("xgsitc,gh->xsgithc", m.reshape(GT, G8, T, CH, T, CH), eye8).reshape(GT, T * 128, T * 128)
    pin_big = jnp.einsum("xgqsip,gh->xsgihqp", pin.reshape(GT, G8, 4, T, CH, P), eye8).reshape(GT, T * 128, G8 * 4 * P)
    pout_big = jnp.einsum("xgqtcp,gh->xgqpthc", poutt.reshape(GT, G8, 4, T, CH, P), eye8).reshape(GT, G8 * 4 * P, T * 128)
    lre = jnp.concatenate([lam[:, 0], lam[:, 0], lam[:, 2], lam[:, 2]], axis=-1)
    lim = jnp.concatenate([-lam[:, 1], lam[:, 1], -lam[:, 3], lam[:, 3]], axis=-1)
    lam_rows = jnp.stack([lre.reshape(GT, G8 * 4 * P), lim.reshape(GT, G8 * 4 * P)], axis=1)
    return w_big.astype(BF16), pin_big.astype(BF16), pout_big.astype(BF16), lam_rows


def _s5_chunk_rows(u_ref, bsz, cblk):
    T = S5_CHUNK
    per_b = [jnp.concatenate([u_ref[b, pl.ds(s, cblk, stride=T), :] for s in range(T)], axis=1) for b in range(bsz)]
    return jnp.concatenate(per_b, axis=0).astype(BF16)


def _s5_summary_kernel(u_ref, pin_ref, g_ref, *, bsz, cblk):
    g = _dot(_s5_chunk_rows(u_ref, bsz, cblk), pin_ref[...])
    for k in range(g_ref.shape[0]):
        for b in range(bsz):
            g_ref[k, pl.ds(b, cblk, stride=bsz), :] = g[b * cblk:(b + 1) * cblk, k * 128:(k + 1) * 128]


def _s5_state_kernel(g_ref, lam_ref, h0_ref, hin_ref, hfin_ref, *, n_chunks, bsz):
    lre = lam_ref[0:1, :]
    lim = lam_ref[1:2, :]
    tiles = [slice(k * 128, (k + 1) * 128) for k in range(4)]

    def body(c, hs):
        rows = (pl.ds(pl.multiple_of(c * bsz, bsz), bsz), pl.ds(pl.multiple_of((n_chunks - 1 - c) * bsz, bsz), bsz))
        out = []
        for k, sl in enumerate(tiles):
            r = rows[k % 2]
            hin_ref[k, r, :] = hs[k]
            out.append(lre[:, sl] * hs[k] + lim[:, sl] * pltpu.roll(hs[k], 64, 1) + g_ref[k, r, :])
        return tuple(out)

    hs = lax.fori_loop(0, n_chunks, body, tuple(h0_ref[:, sl] for sl in tiles))
    for k, sl in enumerate(tiles):
        hfin_ref[:, sl] = hs[k]


def _s5_output_kernel(u_ref, hin_ref, w_ref, pout_ref, y_ref, *, bsz, cblk):
    T = S5_CHUNK
    x = _s5_chunk_rows(u_ref, bsz, cblk)
    hin = jnp.concatenate(
        [jnp.concatenate([hin_ref[k, pl.ds(b, cblk, stride=bsz), :] for k in range(hin_ref.shape[0])], axis=1)
         for b in range(bsz)], axis=0)
    hi, lo = _split_bf16(hin)
    pout = pout_ref[...]
    y = _dot(x, w_ref[...]) + _dot(hi, pout) + _dot(lo, pout)
    for b in range(bsz):
        for s in range(T):
            y_ref[b, pl.ds(s, cblk, stride=T), :] = y[b * cblk:(b + 1) * cblk, s * 128:(s + 1) * 128]


def _s5_scan(u, h0, w_big, pin_big, pout_big, lam_rows):
    bsz, seq, _ = u.shape
    G, P, T = S5_GROUPS, S5_STATE, S5_CHUNK
    GT, G8 = G // S5_TILE_GROUPS, S5_TILE_GROUPS
    SW = G8 * 4 * P
    nc = seq // T
    cblk = min(max(256 // bsz, 8), nc)
    nblk = nc // cblk
    h0g = h0.transpose(2, 0, 1, 4, 3).reshape(GT, G8, bsz, 4 * P).transpose(0, 2, 1, 3).reshape(GT, bsz, SW)
    u_spec = pl.BlockSpec((bsz, cblk * T, 128), lambda x, j: (0, j, x))
    n_tiles = SW // 128
    rows_spec = pl.BlockSpec((None, n_tiles, cblk * bsz, 128), lambda x, j: (x, 0, j, 0))
    per_tile = lambda a: pl.BlockSpec((None,) + a.shape[1:], lambda x, j: (x, 0, 0))
    g = pl.pallas_call(
        functools.partial(_s5_summary_kernel, bsz=bsz, cblk=cblk),
        grid=(GT, nblk),
        in_specs=[u_spec, per_tile(pin_big)],
        out_specs=rows_spec,
        out_shape=jax.ShapeDtypeStruct((GT, n_tiles, nc * bsz, 128), F32),
        compiler_params=_cp("parallel", "parallel"),
        name="s5_chunk_summary",
    )(u, pin_big)
    quarter = lambda r: pl.BlockSpec((None, r, 512), lambda x, q: (x, 0, q))
    quarter_rows = pl.BlockSpec((None, 4, nc * bsz, 128), lambda x, q: (x, q, 0, 0))
    hin, hfin = pl.pallas_call(
        functools.partial(_s5_state_kernel, n_chunks=nc, bsz=bsz),
        grid=(GT, n_tiles // 4),
        in_specs=[quarter_rows, quarter(2), quarter(bsz)],
        out_specs=[quarter_rows, quarter(bsz)],
        out_shape=[jax.ShapeDtypeStruct((GT, n_tiles, nc * bsz, 128), F32), jax.ShapeDtypeStruct((GT, bsz, SW), F32)],
        compiler_params=_cp("parallel", "parallel"),
        name="s5_state_scan",
    )(g, lam_rows, h0g)
    y = pl.pallas_call(
        functools.partial(_s5_output_kernel, bsz=bsz, cblk=cblk),
        grid=(GT, nblk),
        in_specs=[u_spec, rows_spec, per_tile(w_big), per_tile(pout_big)],
        out_specs=u_spec,
        out_shape=jax.ShapeDtypeStruct((bsz, seq, S5_WIDTH), F32),
        compiler_params=_cp("parallel", "parallel"),
        name="s5_chunk_output",
    )(u, hin, w_big, pout_big)
    hfin = hfin.reshape(GT, bsz, G8, 2, 2, P).transpose(1, 3, 0, 2, 5, 4).reshape(bsz, 2, G, P, 2)
    return y, hfin


def _s5_out_kernel(y_ref, u_ref, d_ref, w_ref, nw_ref, o_ref):
    y = y_ref[...] + u_ref[...] * d_ref[...]
    c = math.sqrt(2.0 / math.pi)
    y = y * (0.5 * (1.0 + jnp.tanh(c * (y + 0.044715 * (y * y * y)))))
    z = _dot(y.astype(BF16), w_ref[...])
    o = z[:, :S5_WIDTH] * jax.nn.sigmoid(z[:, S5_WIDTH:])
    o_ref[...] = _rms(o, nw_ref[...]).astype(BF16)


def _s5_out(y2, u2, d_skip, w_glu, nw, tm):
    n = y2.shape[0]
    full = lambda a: pl.BlockSpec(a.shape, lambda i: (0,) * a.ndim)
    row = pl.BlockSpec((tm, S5_WIDTH), lambda i: (i, 0))
    return pl.pallas_call(
        _s5_out_kernel,
        grid=(n // tm,),
        in_specs=[row, row, full(d_skip), full(w_glu), full(nw)],
        out_specs=row,
        out_shape=jax.ShapeDtypeStruct((n, S5_WIDTH), BF16),
        compiler_params=_cp("parallel"),
        name="s5_gelu_glu",
    )(y2, u2, d_skip, w_glu, nw)


def _rwkv_pre_kernel(p_ref, hp_ref, hn_ref, mu_ref, w0_ref, a0_ref, w2_ref, a2_ref, g2_ref,
                     kk_ref, ka_ref, rk_ref, bd_ref,
                     v_ref, nk_ref, dec_ref, kd_ref, bb_ref, qq_ref, vkr_ref, g_ref, bonus_ref, *, tiles_per_seq, tm):
    i = pl.program_id(0)
    j = i % tiles_per_seq
    p = p_ref[...]
    rows = lax.broadcasted_iota(jnp.int32, (tm, 1), 0)
    prev_edge = jnp.where(j == 0, 0.0, hp_ref[7:8, :])
    next_edge = jnp.where(j == tiles_per_seq - 1, 0.0, hn_ref[0:1, :])
    prev = jnp.where(rows == 0, prev_edge, pltpu.roll(p, 1, 0))
    nxt = jnp.where(rows == tm - 1, next_edge, pltpu.roll(p, tm - 1, 0))
    p = p + mu_ref[...] * (0.5 * (prev + nxt) - p)
    W = RWKV_WIDTH
    r = p[:, 0:W]
    k = p[:, W:2 * W]
    v = p[:, 2 * W:3 * W]
    wl = p[:, 3 * W:3 * W + LORA_PAD]
    al = p[:, 3 * W + LORA_PAD:3 * W + 2 * LORA_PAD]
    gl = p[:, 3 * W + 2 * LORA_PAD:3 * W + 3 * LORA_PAD]
    bd = bd_ref[...]
    kk = k * kk_ref[...]
    kk = kk * lax.rsqrt(_segsum(kk * kk, bd) + 1e-12)
    g_ref[...] = _dot(jax.nn.sigmoid(gl).astype(BF16), g2_ref[...])
    tw = jnp.tanh(wl).astype(BF16)
    alb = al.astype(BF16)
    ksum = None
    for d in range(2):
        z = -(w0_ref[d:d + 1, :] + _dot(tw, w2_ref[d]))
        w = -(jnp.maximum(z, 0.0) + jnp.log(1.0 + jnp.exp(-jnp.abs(z)))) - 0.5
        dec = jnp.exp(-jnp.exp(w))
        dec_ref[d] = dec
        a = jax.nn.sigmoid(a0_ref[d:d + 1, :] + _dot(alb, a2_ref[d]))
        kd = k * (1.0 + (a - 1.0) * ka_ref[...])
        kd_ref[d] = kd
        bb = kk * a
        bb_ref[d] = bb
        qq_ref[d] = dec * r - kk * _segsum(bb * r, bd)
        vkr_ref[d] = v * _segsum(kd * r, bd)
        ksum = kd if ksum is None else ksum + kd
    v_ref[...] = v
    nk_ref[...] = -kk
    bonus_ref[...] = _segsum(r * ksum * rk_ref[...], bd) * v


def _rwkv_pre(p2, bsz, seq, wts, bd):
    n = p2.shape[0]
    tm = min(256, seq)
    tps = seq // tm
    r8 = tm // 8
    nblk8 = n // 8
    full = lambda a: pl.BlockSpec(a.shape, lambda i: (0,) * a.ndim)
    o3 = pl.BlockSpec((None, tm, RWKV_WIDTH), lambda i: (i // tps, i % tps, 0))
    o4 = pl.BlockSpec((2, None, tm, RWKV_WIDTH), lambda i: (0, i // tps, i % tps, 0))
    s3 = jax.ShapeDtypeStruct((bsz, seq, RWKV_WIDTH), F32)
    s4 = jax.ShapeDtypeStruct((2, bsz, seq, RWKV_WIDTH), F32)
    names = ("mu", "w0", "a0", "w2", "a2", "g2", "k_k", "k_a", "r_k")
    return pl.pallas_call(
        functools.partial(_rwkv_pre_kernel, tiles_per_seq=tps, tm=tm),
        grid=(n // tm,),
        in_specs=[pl.BlockSpec((tm, RWKV_COLS), lambda i: (i, 0)),
                  pl.BlockSpec((8, RWKV_COLS), lambda i: (jnp.maximum(i * r8 - 1, 0), 0)),
                  pl.BlockSpec((8, RWKV_COLS), lambda i: (jnp.minimum((i + 1) * r8, nblk8 - 1), 0))]
                 + [full(wts[k]) for k in names] + [full(bd)],
        out_specs=[o3, o3, o4, o4, o4, o4, o4, o3, o3],
        out_shape=[s3, s3, s4, s4, s4, s4, s4, s3, s3],
        compiler_params=_cp("parallel"),
        name="rwkv_prepare",
    )(p2, p2, p2, *[wts[k] for k in names], bd)


RWKV_SCAN_COLS = 32


def _rwkv_scan_kernel(nkf, nkb, vf, vb, decf, decb, kdf, kdb, bbf, bbb, qqf, qqb, vkf, vkb, s0_ref,
                      yf_ref, yb_ref, sfin_ref, s_scr, v8_scr, v8k_scr, y8_scr, *, nb, tlen, n_chunks):
    c = pl.program_id(1)

    @pl.when(c == 0)
    def _():
        s_scr[...] = s0_ref[...].reshape(s_scr.shape)

    N = RWKV_HEAD_DIM
    W = RWKV_WIDTH
    H = RWKV_HEADS
    nt = (((1,), (1,)), ((), ()))
    chains = [(d, n) for d in range(2) for n in range(nb)]
    pick = lambda d, f, b: f if d == 0 else b
    base = lambda m: m * tlen * H

    for m, (d, n) in enumerate(chains):
        for src, dst in ((pick(d, vf, vb), v8_scr), (pick(d, vkf, vkb), v8k_scr)):
            for p in range(H // 2):
                tile = src[n, :, p * 128:(p + 1) * 128]
                dst[pl.ds(base(m) + 2 * p, tlen, stride=H), :] = tile
                dst[pl.ds(base(m) + 2 * p + 1, tlen, stride=H), :] = pltpu.roll(tile, N, 1)

    hmask = (lax.broadcasted_iota(jnp.int32, (H, W), 1) // N == lax.broadcasted_iota(jnp.int32, (H, W), 0)).astype(F32)
    eye_pad = (lax.broadcasted_iota(jnp.int32, (N, 128), 1) == lax.broadcasted_iota(jnp.int32, (N, 128), 0)).astype(BF16)
    sel = (lax.broadcasted_iota(jnp.int32, (H, RWKV_SCAN_COLS), 1)
           == lax.broadcasted_iota(jnp.int32, (H, RWKV_SCAN_COLS), 0) + 2 * H).astype(BF16)
    z_w = jnp.zeros((H, W), F32)
    z_e = jnp.zeros((H, 128), F32)
    z_row = jnp.zeros((H, W + 128), F32)

    def step(t, carry):
        tts = (t, tlen - 1 - t)
        row = lambda ref, n, tt: ref[n, pl.ds(tt, 1), :] * hmask
        tile8 = lambda m, tt: pl.ds(pl.multiple_of(base(m) + tt * H, H), H)
        obs = []
        for m, (d, n) in enumerate(chains):
            tt = tts[d]
            rhs = jnp.concatenate([
                jnp.concatenate([row(pick(d, nkf, nkb), n, tt), z_e], axis=1),
                jnp.concatenate([z_w, v8_scr[tile8(m, tt), :]], axis=1),
                jnp.concatenate([row(pick(d, qqf, qqb), n, tt), v8k_scr[tile8(m, tt), :]], axis=1),
                z_row], axis=0).astype(BF16)
            lhs = jnp.concatenate([s_scr[m].astype(BF16), eye_pad], axis=1)
            obs.append(lax.dot_general(lhs, rhs, nt, preferred_element_type=F32).astype(BF16))
        for m, (d, n) in enumerate(chains):
            tt = tts[d]
            w2 = jnp.concatenate([row(pick(d, bbf, bbb), n, tt), row(pick(d, kdf, kdb), n, tt), z_w, z_w],
                                 axis=0).astype(BF16)
            s_scr[m] = s_scr[m] * pick(d, decf, decb)[n, pl.ds(tt, 1), :] + _dot(obs[m], w2)
        for m, (d, n) in enumerate(chains):
            ob2 = jnp.concatenate([obs[m], obs[m]], axis=0)
            y8_scr[tile8(m, tts[d]), :] = lax.dot_general(sel, ob2, nt, preferred_element_type=F32)
        return carry

    lax.fori_loop(0, tlen, step, 0, unroll=2)

    left =lax.broadcasted_iota(jnp.int32, (tlen, 128), 1) < N
    for m, (d, n) in enumerate(chains):
        y_ref = pick(d, yf_ref, yb_ref)
        for p in range(H // 2):
            even = y8_scr[pl.ds(base(m) + 2 * p, tlen, stride=H), :]
            odd = y8_scr[pl.ds(base(m) + 2 * p + 1, tlen, stride=H), :]
            y_ref[n, :, p * 128:(p + 1) * 128] = jnp.where(left, even, odd)

    @pl.when(c == n_chunks - 1)
    def _():
        sfin_ref[...] = s_scr[...].reshape(sfin_ref.shape)


def _rwkv_scan(nk, v, dec, kd, bb, qq, vkr, s0):
    bsz, seq, W = nk.shape
    N, H = RWKV_HEAD_DIM, RWKV_HEADS
    nb = 8 if bsz % 8 == 0 else 4
    tlen = min(32, seq)
    nc = seq // tlen
    fwd = pl.BlockSpec((nb, tlen, W), lambda b, c: (b, c, 0))
    bwd = pl.BlockSpec((nb, tlen, W), lambda b, c: (b, nc - 1 - c, 0))
    fwd_d = pl.BlockSpec((None, nb, tlen, W), lambda b, c: (0, b, c, 0))
    bwd_d = pl.BlockSpec((None, nb, tlen, W), lambda b, c: (1, b, nc - 1 - c, 0))
    st = pl.BlockSpec((2, nb, N, W), lambda b, c: (0, b, 0, 0))
    tiles = pltpu.VMEM((2 * nb * tlen * H, 128), F32)
    return pl.pallas_call(
        functools.partial(_rwkv_scan_kernel, nb=nb, tlen=tlen, n_chunks=nc),
        grid=(bsz // nb, nc),
        in_specs=[fwd, bwd, fwd, bwd] + [fwd_d, bwd_d] * 5 + [st],
        out_specs=[fwd, bwd, st],
        out_shape=[jax.ShapeDtypeStruct((bsz, seq, W), F32)] * 2 + [jax.ShapeDtypeStruct((2, bsz, N, W), F32)],
        scratch_shapes=[pltpu.VMEM((2 * nb, N, W), F32), tiles, tiles, tiles],
        compiler_params=_cp("parallel", "arbitrary"),
        name="rwkv_scan",
    )(nk, nk, v, v, dec, dec, kd, kd, bb, bb, qq, qq, vkr, vkr, s0)


def _rwkv_post_kernel(yf_ref, yb_ref, bonus_ref, g_ref, lw_ref, lb_ref, bd_ref, o_ref):
    bd = bd_ref[...]
    y = yf_ref[...] + yb_ref[...]
    inv_n = 1.0 / RWKV_HEAD_DIM
    mean = _segsum(y, bd) * inv_n
    yc = y - mean
    var = _segsum(yc * yc, bd) * inv_n
    yn = yc * lax.rsqrt(var + GN_EPS) * lw_ref[...] + lb_ref[...]
    o_ref[...] = ((yn + bonus_ref[...]) * g_ref[...]).astype(BF16)


def _rwkv_post(yf, yb, bonus, g, ln_w, ln_b, bd):
    bsz, seq, W = yf.shape
    tm = min(256, seq)
    tps = seq // tm
    n = bsz * seq
    full = lambda a: pl.BlockSpec(a.shape, lambda i: (0,) * a.ndim)
    i3 = pl.BlockSpec((None, tm, W), lambda i: (i // tps, i % tps, 0))
    return pl.pallas_call(
        _rwkv_post_kernel,
        grid=(n // tm,),
        in_specs=[i3, i3, i3, i3, full(ln_w), full(ln_b), full(bd)],
        out_specs=pl.BlockSpec((tm, W), lambda i: (i, 0)),
        out_shape=jax.ShapeDtypeStruct((n, W), BF16),
        compiler_params=_cp("parallel"),
        name="rwkv_groupnorm_gate",
    )(yf, yb, bonus, g, ln_w, ln_b, bd)


def _store_k_heads(k_ref, kn, kpe):
    for h in range(MLA_HEADS):
        k_ref[:, h * MLA_QK_PAD:h * MLA_QK_PAD + 128] = kn[:, h * 128:(h + 1) * 128].astype(BF16)
        k_ref[:, h * MLA_QK_PAD + 128:(h + 1) * MLA_QK_PAD] = kpe


def _mla_prep_kernel(*refs, rope):
    if rope:
        (p_ref, qn_ref, kvn_ref, wq_ref, wqr_ref, wk_ref, wv_ref, cq_ref, sq_ref, ck_ref, sk_ref,
         q_ref, k_ref, v_ref) = refs
    else:
        (p_ref, qn_ref, kvn_ref, wq_ref, wk_ref, wv_ref, q_ref, k_ref, v_ref, ckv_ref, kr_ref) = refs
    p = p_ref[...]
    qn = _rms(p[:, 0:MLA_Q_RANK], qn_ref[...]).astype(BF16)
    q = _dot(qn, wq_ref[...])
    ckv = _rms(p[:, MLA_Q_RANK:MLA_Q_RANK + MLA_KV_RANK], kvn_ref[...])
    kr = p[:, MLA_Q_RANK + MLA_KV_RANK:MLA_Q_RANK + MLA_KV_RANK + 128]
    if rope:
        cq = jnp.concatenate([cq_ref[...]] * MLA_HEADS, axis=1)
        sq = jnp.concatenate([sq_ref[...]] * MLA_HEADS, axis=1)
        q = q * cq + _dot(qn, wqr_ref[...]) * sq
        krot = p[:, MLA_Q_RANK + MLA_KV_RANK + 128:MLA_Q_RANK + MLA_KV_RANK + 256]
        kpe = kr * ck_ref[...] + krot * sk_ref[...]
    else:
        kpe = kr
        ckv_ref[...] = ckv
        kr_ref[...] = kr[:, 0:MLA_ROPE_DIM]
    q_ref[...] = q.astype(BF16)
    cb = ckv.astype(BF16)
    _store_k_heads(k_ref, _dot(cb, wk_ref[...]), kpe.astype(BF16))
    v_ref[...] = _dot(cb, wv_ref[...]).astype(BF16)


def _mla_prep(p2, seq, wts, tables):
    n = p2.shape[0]
    rope = tables is not None
    tm = min(256, seq)
    tps = seq // tm
    full = lambda a: pl.BlockSpec(a.shape, lambda i: (0,) * a.ndim)
    row = lambda w: pl.BlockSpec((tm, w), lambda i: (i, 0))
    ins = [p2, wts["q_norm"], wts["kv_norm"], wts["wq"]]
    specs = [row(MLA_COLS), full(wts["q_norm"]), full(wts["kv_norm"]), full(wts["wq"])]
    if rope:
        ins.append(wts["wq_rot"])
        specs.append(full(wts["wq_rot"]))
    ins += [wts["wk"], wts["wv"]]
    specs += [full(wts["wk"]), full(wts["wv"])]
    outs = [row(MLA_HEADS * MLA_QK_PAD), row(MLA_HEADS * MLA_QK_PAD), row(MLA_HEADS * MLA_V_DIM)]
    shapes = [jax.ShapeDtypeStruct((n, MLA_HEADS * MLA_QK_PAD), BF16),
              jax.ShapeDtypeStruct((n, MLA_HEADS * MLA_QK_PAD), BF16),
              jax.ShapeDtypeStruct((n, MLA_HEADS * MLA_V_DIM), BF16)]
    if rope:
        ins += list(tables)
        specs += [pl.BlockSpec((tm, t.shape[1]), lambda i: (i % tps, 0)) for t in tables]
    else:
        outs += [row(MLA_KV_RANK), row(MLA_ROPE_DIM)]
        shapes += [jax.ShapeDtypeStruct((n, MLA_KV_RANK), F32), jax.ShapeDtypeStruct((n, MLA_ROPE_DIM), F32)]
    return pl.pallas_call(
        functools.partial(_mla_prep_kernel, rope=rope),
        grid=(n // tm,),
        in_specs=specs, out_specs=outs, out_shape=shapes,
        compiler_params=_cp("parallel"),
        name="mla_prepare_rope" if rope else "mla_prepare",
    )(*ins)


def _mla_cache_kernel(ckv_ref, kr_ref, wk_ref, wv_ref, k_ref, v_ref):
    cb = ckv_ref[...].astype(BF16)
    _store_k_heads(k_ref, _dot(cb, wk_ref[...]), kr_ref[...].astype(BF16))
    v_ref[...] = _dot(cb, wv_ref[...]).astype(BF16)


def _mla_cache(cache_ckv, cache_kr_pad, layer, wk, wv):
    bsz, _, past, _ = cache_ckv.shape
    full = lambda a: pl.BlockSpec(a.shape, lambda b: (0,) * a.ndim)
    return pl.pallas_call(
        _mla_cache_kernel,
        grid=(bsz,),
        in_specs=[pl.BlockSpec((None, None, past, MLA_KV_RANK), lambda b: (b, layer, 0, 0)),
                  pl.BlockSpec((None, None, past, 128), lambda b: (b, layer, 0, 0)),
                  full(wk), full(wv)],
        out_specs=[pl.BlockSpec((None, past, MLA_HEADS * MLA_QK_PAD), lambda b: (b, 0, 0)),
                   pl.BlockSpec((None, past, MLA_HEADS * MLA_V_DIM), lambda b: (b, 0, 0))],
        out_shape=[jax.ShapeDtypeStruct((bsz, past, MLA_HEADS * MLA_QK_PAD), BF16),
                   jax.ShapeDtypeStruct((bsz, past, MLA_HEADS * MLA_V_DIM), BF16)],
        compiler_params=_cp("parallel"),
        name="mla_cache_keys",
    )(cache_ckv, cache_kr_pad, wk, wv)


def _attn_kernel(*refs, cache, scale):
    if cache:
        q_ref, k_ref, v_ref, kc_ref, vc_ref, o_ref = refs
    else:
        q_ref, k_ref, v_ref, o_ref = refs
    nt = (((1,), (1,)), ((), ()))
    q = q_ref[...]
    s = lax.dot_general(q, k_ref[...], nt, preferred_element_type=F32) * scale
    m = jnp.max(s, axis=-1, keepdims=True)
    if cache:
        sc = lax.dot_general(q, kc_ref[...], nt, preferred_element_type=F32) * scale
        m = jnp.maximum(m, jnp.max(sc, axis=-1, keepdims=True))
    e = jnp.exp(s - m)
    den = jnp.sum(e, axis=-1, keepdims=True)
    o = _dot(e.astype(BF16), v_ref[...])
    if cache:
        ec = jnp.exp(sc - m)
        den = den + jnp.sum(ec, axis=-1, keepdims=True)
        o = o + _dot(ec.astype(BF16), vc_ref[...])
    o_ref[...] = o / den


def _attention(q, k, v, kc=None, vc=None):
    bsz, seq, _ = q.shape
    tq = min(256, seq)
    cache = kc is not None
    scale = float(MLA_NOPE_DIM + MLA_ROPE_DIM) ** -0.5
    ins = [q, k, v]
    specs = [pl.BlockSpec((None, tq, MLA_QK_PAD), lambda b, h, i: (b, i, h)),
             pl.BlockSpec((None, seq, MLA_QK_PAD), lambda b, h, i: (b, 0, h)),
             pl.BlockSpec((None, seq, MLA_V_DIM), lambda b, h, i: (b, 0, h))]
    if cache:
        past = kc.shape[1]
        ins += [kc, vc]
        specs += [pl.BlockSpec((None, past, MLA_QK_PAD), lambda b, h, i: (b, 0, h)),
                  pl.BlockSpec((None, past, MLA_V_DIM), lambda b, h, i: (b, 0, h))]
    return pl.pallas_call(
        functools.partial(_attn_kernel, cache=cache, scale=scale),
        grid=(bsz, MLA_HEADS, seq // tq),
        in_specs=specs,
        out_specs=pl.BlockSpec((None, tq, MLA_V_DIM), lambda b, h, i: (b, i, h)),
        out_shape=jax.ShapeDtypeStruct((bsz, seq, MLA_HEADS * MLA_V_DIM), F32),
        compiler_params=_cp("parallel", "parallel", "arbitrary"),
        name="mla_attention_cached" if cache else "mla_attention",
    )(*ins)


def _outproj_kernel(ys_ref, yr_ref, ym_ref, x_ref, mod_ref, nm_ref, w_ref, o_ref):
    ym = _rms(ym_ref[...], nm_ref[...]).astype(BF16)
    acc = _dot(ys_ref[...], w_ref[0:S5_WIDTH, :])
    acc += _dot(yr_ref[...], w_ref[S5_WIDTH:S5_WIDTH + RWKV_WIDTH, :])
    acc += _dot(ym, w_ref[S5_WIDTH + RWKV_WIDTH:, :])
    o_ref[...] = x_ref[...] + mod_ref[2:3, :] * acc


def _outproj(ys, yr, ym, x2, mod, layer, row_of_tile, tm, nm, w_out):
    n = x2.shape[0]
    full = lambda a: pl.BlockSpec(a.shape, lambda i: (0,) * a.ndim)
    row = lambda w: pl.BlockSpec((tm, w), lambda i: (i, 0))
    return pl.pallas_call(
        _outproj_kernel,
        grid=(n // tm,),
        in_specs=[row(S5_WIDTH), row(RWKV_WIDTH), row(MLA_WIDTH), row(D_MODEL), _mod_spec(layer, row_of_tile),
                  full(nm), full(w_out)],
        out_specs=row(D_MODEL),
        out_shape=jax.ShapeDtypeStruct((n, D_MODEL), F32),
        compiler_params=_cp("parallel"),
        name="out_projection",
    )(ys, yr, ym, x2, mod, nm, w_out)


def _mlp_kernel(x_ref, nw_ref, mod_ref, w1_ref, w2_ref, nf_ref, o_ref, h_scr, acc_scr, *, final_norm):
    j = pl.program_id(1)

    @pl.when(j == 0)
    def _():
        h = _rms(x_ref[...], nw_ref[...]) * (1.0 + mod_ref[4:5, :]) + mod_ref[3:4, :]
        h_scr[...] = h.astype(BF16)
        acc_scr[...] = jnp.zeros_like(acc_scr)

    a = _dot(h_scr[...], w1_ref[...])
    a = jnp.square(jnp.maximum(a, 0.0)).astype(BF16)
    acc_scr[...] += _dot(a, w2_ref[...])

    @pl.when(j == pl.num_programs(1) - 1)
    def _():
        y = x_ref[...] + mod_ref[5:6, :] * acc_scr[...]
        if final_norm:
            y = _rms(y, nf_ref[...])
        o_ref[...] = y


def _mlp(x2, nw, mod, layer, row_of_tile, tm, w1, w2, nf, final_norm):
    n = x2.shape[0]
    tf = 1024
    full = lambda a: pl.BlockSpec(a.shape, lambda i, j: (0,) * a.ndim)
    return pl.pallas_call(
        functools.partial(_mlp_kernel, final_norm=final_norm),
        grid=(n // tm, D_FF // tf),
        in_specs=[pl.BlockSpec((tm, D_MODEL), lambda i, j: (i, 0)), full(nw), _mod_spec(layer, row_of_tile),
                  pl.BlockSpec((D_MODEL, tf), lambda i, j: (0, j)),
                  pl.BlockSpec((tf, D_MODEL), lambda i, j: (j, 0)), full(nf)],
        out_specs=pl.BlockSpec((tm, D_MODEL), lambda i, j: (i, 0)),
        out_shape=jax.ShapeDtypeStruct((n, D_MODEL), F32),
        scratch_shapes=[pltpu.VMEM((tm, D_MODEL), BF16), pltpu.VMEM((tm, D_MODEL), F32)],
        compiler_params=_cp("parallel", "arbitrary"),
        name="mlp_final" if final_norm else "mlp",
    )(x2, nw, mod, w1, w2, nf)


def _rope_tables(length):
    rows = length // GRID_W
    row_pos = jnp.repeat(jnp.arange(rows, dtype=F32), GRID_W)
    col_pos = jnp.tile(jnp.arange(GRID_W, dtype=F32), rows)
    axis_dim = MLA_ROPE_DIM // 2
    inv_freq = 1.0 / (ROPE_THETA ** (jnp.arange(0, axis_dim, 2, dtype=F32) / axis_dim))
    ang_r = row_pos[:, None] * inv_freq[None, :]
    ang_c = col_pos[:, None] * inv_freq[None, :]
    ang = jnp.concatenate([ang_r, ang_r, ang_c, ang_c], axis=-1)
    cos, sin = jnp.cos(ang), jnp.sin(ang)
    z64 = jnp.zeros((length, 64), F32)
    cos_q = jnp.concatenate([jnp.ones((length, MLA_NOPE_DIM), F32), cos, z64], axis=1)
    sin_q = jnp.concatenate([jnp.zeros((length, MLA_NOPE_DIM), F32), sin, z64], axis=1)
    cos_k = jnp.concatenate([cos, z64], axis=1)
    sin_k = jnp.concatenate([sin, z64], axis=1)
    return cos_q, sin_q, cos_k, sin_k


def _rot_cols(w):
    a, b, c, d = w[..., 0:16], w[..., 16:32], w[..., 32:48], w[..., 48:64]
    return jnp.concatenate([-b, a, -d, c], axis=-1)


def _layer_weights(l, p):
    d = D_MODEL
    w_in = p["w_in"][l]
    z64 = jnp.zeros((d, 64), F32)
    o = S5_WIDTH
    rk = w_in[:, o:o + 3 * RWKV_WIDTH]
    o += 3 * RWKV_WIDTH
    wl, al, gl = w_in[:, o:o + 64], w_in[:, o + 64:o + 128], w_in[:, o + 128:o + 256]
    o += 256
    cq, ckv, kr = w_in[:, o:o + 512], w_in[:, o + 512:o + 768], w_in[:, o + 768:o + 832]
    mu = p["rwkv_mu"][l]
    z1 = jnp.zeros((64,), F32)
    mu_pad = jnp.concatenate([mu[:1536], mu[1536:1600], z1, mu[1600:1664], z1, mu[1664:1792]])[None]
    pad_rows = lambda w: jnp.concatenate([w, jnp.zeros_like(w)], axis=-2)
    w_uq = p["mla_w_uq"][l].reshape(MLA_Q_RANK, MLA_HEADS, MLA_NOPE_DIM + MLA_ROPE_DIM)
    zq = jnp.zeros((MLA_Q_RANK, MLA_HEADS, 64), F32)
    wq = jnp.concatenate([w_uq, zq], axis=-1).reshape(MLA_Q_RANK, MLA_HEADS * MLA_QK_PAD)
    wq_rot = jnp.concatenate([jnp.zeros((MLA_Q_RANK, MLA_HEADS, MLA_NOPE_DIM), F32),
                              _rot_cols(w_uq[..., MLA_NOPE_DIM:]), zq], axis=-1).reshape(MLA_Q_RANK, MLA_HEADS * MLA_QK_PAD)
    w_ukv = p["mla_w_ukv"][l].reshape(MLA_KV_RANK, MLA_HEADS, MLA_NOPE_DIM + MLA_V_DIM)
    row = lambda a: a.reshape(1, -1)
    return {
        "norm_mix": row(p["norm_mix"][l]), "norm_mlp": row(p["norm_mlp"][l]),
        "w_s5": w_in[:, 0:S5_WIDTH].astype(BF16),
        "w_rwkv": jnp.concatenate([rk, wl, z64, al, z64, gl], axis=1).astype(BF16),
        "w_mla": jnp.concatenate([cq, ckv, kr, z64, _rot_cols(kr), z64], axis=1).astype(BF16),
        "w_out": p["w_out"][l].astype(BF16),
        "s5_d": row(p["s5_d"][l]), "s5_w_glu": p["s5_w_glu"][l].astype(BF16), "s5_out_norm": row(p["s5_out_norm"][l]),
        "rwkv": {
            "mu": mu_pad, "w0": p["rwkv_w0"][l], "a0": p["rwkv_a0"][l],
            "w2": pad_rows(p["rwkv_w2"][l]).astype(BF16), "a2": pad_rows(p["rwkv_a2"][l]).astype(BF16),
            "g2": p["rwkv_g2"][l].astype(BF16), "k_k": row(p["rwkv_k_k"][l]), "k_a": row(p["rwkv_k_a"][l]),
            "r_k": row(p["rwkv_r_k"][l]),
        },
        "rwkv_ln_w": row(p["rwkv_ln_w"][l]), "rwkv_ln_b": row(p["rwkv_ln_b"][l]),
        "mla": {
            "q_norm": row(p["mla_q_norm"][l]), "kv_norm": row(p["mla_kv_norm"][l]),
            "wq": wq.astype(BF16), "wq_rot": wq_rot.astype(BF16),
            "wk": w_ukv[..., :MLA_NOPE_DIM].reshape(MLA_KV_RANK, -1).astype(BF16),
            "wv": w_ukv[..., MLA_NOPE_DIM:].reshape(MLA_KV_RANK, -1).astype(BF16),
        },
        "mla_out_norm": row(p["mla_out_norm"][l]),
        "mlp_w1": p["mlp_w1"][l].astype(BF16), "mlp_w2": p["mlp_w2"][l].astype(BF16),
    }


def _trunk_layer(x, mod, layer, lw, s5w, bd, row_of_tile_fn, s5_h0, rwkv_s0, cache, tables, norm_final, final_norm):
    bsz, seq, d = x.shape
    n = bsz * seq
    x2 = x.reshape(n, d)
    tm = min(512, seq) if cache is not None else min(512, n)
    row_of_tile = row_of_tile_fn(tm)
    u2, pr2, pm2 = _inproj(x2, lw["norm_mix"], mod, layer, row_of_tile, tm, lw["w_s5"], lw["w_rwkv"], lw["w_mla"])

    ys, s5_fin = _s5_scan(u2.reshape(bsz, seq, S5_WIDTH), s5_h0, *s5w)
    ys2 = _s5_out(ys.reshape(n, S5_WIDTH), u2, lw["s5_d"], lw["s5_w_glu"], lw["s5_out_norm"], min(512, n))

    v, nk, dec, kd, bb, qq, vkr, g, bonus = _rwkv_pre(pr2, bsz, seq, lw["rwkv"], bd)
    s0 = rwkv_s0.transpose(1, 0, 3, 2, 4).reshape(2, bsz, RWKV_HEAD_DIM, RWKV_WIDTH)
    yf, yb, s_fin = _rwkv_scan(nk, v, dec, kd, bb, qq, vkr, s0)
    yr2 = _rwkv_post(yf, yb, bonus, g, lw["rwkv_ln_w"], lw["rwkv_ln_b"], bd)
    rwkv_fin = s_fin.reshape(2, bsz, RWKV_HEAD_DIM, RWKV_HEADS, RWKV_HEAD_DIM).transpose(1, 0, 3, 2, 4)

    shape3 = lambda a: a.reshape(bsz, seq, a.shape[-1])
    if cache is None:
        q, k, v_, ckv_n, k_rope = _mla_prep(pm2, seq, lw["mla"], None)
        ym = _attention(shape3(q), shape3(k), shape3(v_))
        extras = (shape3(ckv_n), shape3(k_rope), s5_fin, rwkv_fin)
    else:
        q, k, v_ = _mla_prep(pm2, seq, lw["mla"], tables)
        kc, vc = _mla_cache(cache[0], cache[1], layer, lw["mla"]["wk"], lw["mla"]["wv"])
        ym = _attention(shape3(q), shape3(k), shape3(v_), kc, vc)
        extras = None

    x2 = _outproj(ys2, yr2, ym.reshape(n, MLA_WIDTH), x2, mod, layer, row_of_tile, tm, lw["mla_out_norm"], lw["w_out"])
    x2 = _mlp(x2, lw["norm_mlp"], mod, layer, row_of_tile, tm, lw["mlp_w1"], lw["mlp_w2"], norm_final, final_norm)
    return x2.reshape(bsz, seq, d), extras


def kernel(x_prompt, x_sample, cache_mla_ckv, cache_mla_krope, state_s5, state_rwkv, c, c_ctx, norm_mix, norm_mlp, norm_final, w_ada, b_ada, w_in, w_out, s5_a_re, s5_a_im, s5_log_dt, s5_b_re, s5_b_im, s5_c_re, s5_c_im, s5_d, s5_w_glu, s5_out_norm, rwkv_mu, rwkv_w0, rwkv_w2, rwkv_a0, rwkv_a2, rwkv_g2, rwkv_k_k, rwkv_k_a, rwkv_r_k, rwkv_ln_w, rwkv_ln_b, mla_q_norm, mla_w_uq, mla_kv_norm, mla_w_ukv, mla_out_norm, mlp_w1, mlp_w2):
    p = dict(norm_mix=norm_mix, norm_mlp=norm_mlp, w_in=w_in, w_out=w_out, s5_d=s5_d, s5_w_glu=s5_w_glu,
             s5_out_norm=s5_out_norm, rwkv_mu=rwkv_mu, rwkv_w0=rwkv_w0, rwkv_w2=rwkv_w2, rwkv_a0=rwkv_a0,
             rwkv_a2=rwkv_a2, rwkv_g2=rwkv_g2, rwkv_k_k=rwkv_k_k, rwkv_k_a=rwkv_k_a, rwkv_r_k=rwkv_r_k,
             rwkv_ln_w=rwkv_ln_w, rwkv_ln_b=rwkv_ln_b, mla_q_norm=mla_q_norm, mla_w_uq=mla_w_uq,
             mla_kv_norm=mla_kv_norm, mla_w_ukv=mla_w_ukv, mla_out_norm=mla_out_norm, mlp_w1=mlp_w1, mlp_w2=mlp_w2)
    depth = w_in.shape[0]
    b_ctx, l_ctx, d = x_prompt.shape
    b_dec, l_dec, _ = x_sample.shape

    rows = -(-(1 + b_dec) // 8) * 8
    cond = jnp.zeros((rows, d), F32).at[0].set(c_ctx).at[1:1 + b_dec].set(c)
    mod = _modulation(cond, w_ada, b_ada).reshape(depth, rows, N_MOD, d)

    bd = jnp.kron(jnp.eye(4, dtype=F32), jnp.ones((64, 64), F32)).astype(BF16)
    tables = _rope_tables(l_dec)
    kr_pad = jnp.pad(cache_mla_krope, ((0, 0), (0, 0), (0, 0), (0, 128 - MLA_ROPE_DIM)))
    zero_s5 = jnp.zeros((b_ctx, 2, S5_GROUPS, S5_STATE, 2), F32)
    zero_rwkv = jnp.zeros((b_ctx, 2, RWKV_HEADS, RWKV_HEAD_DIM, RWKV_HEAD_DIM), F32)
    nf = norm_final.reshape(1, d)

    ctx_rows = lambda tm: (lambda i: 0)
    dec_rows = lambda tm: (lambda i: 1 + i // (l_dec // tm))

    xp, xs = x_prompt, x_sample
    new_ckv, new_krope, new_s5, new_rwkv = [], [], [], []
    for l in range(depth):
        lw = _layer_weights(l, p)
        s5w = _s5_prep(s5_a_re[l], s5_a_im[l], s5_log_dt[l], s5_b_re[l], s5_b_im[l], s5_c_re[l], s5_c_im[l])
        last = l == depth - 1
        xp, (ckv_l, krope_l, s5_l, rwkv_l) = _trunk_layer(
            xp, mod, l, lw, s5w, bd, ctx_rows, zero_s5, zero_rwkv, None, None, nf, last)
        new_ckv.append(ckv_l)
        new_krope.append(krope_l)
        new_s5.append(s5_l)
        new_rwkv.append(rwkv_l)
        xs, _ = _trunk_layer(
            xs, mod, l, lw, s5w, bd, dec_rows, state_s5[:, l], state_rwkv[:, l], (cache_mla_ckv, kr_pad), tables, nf, last)
    return (xp, xs, jnp.stack(new_ckv, axis=1), jnp.stack(new_krope, axis=1),
            jnp.stack(new_s5, axis=1), jnp.stack(new_rwkv, axis=1))
```

```python
import functools
import math

import jax
import jax.numpy as jnp
from jax import lax
from jax.experimental import pallas as pl
from jax.experimental.pallas import tpu as pltpu

F32 = jnp.float32
BF16 = jnp.bfloat16

D_MODEL = 2048
N_MOD = 6
GRID_W = 64
S5_WIDTH = 512
S5_CH = 16
S5_GROUPS = 32
S5_STATE = 64
S5_CHUNK = 8
S5_TILE_GROUPS = 8
RWKV_WIDTH = 512
RWKV_HEAD_DIM = 64
RWKV_HEADS = 8
LORA_PAD = 128
RWKV_COLS = 3 * RWKV_WIDTH + 3 * LORA_PAD
MLA_HEADS = 8
MLA_V_DIM = 128
MLA_NOPE_DIM = 128
MLA_ROPE_DIM = 64
MLA_QK_PAD = 256
MLA_Q_RANK = 512
MLA_KV_RANK = 256
MLA_WIDTH = 1024
MLA_COLS = MLA_Q_RANK + MLA_KV_RANK + 2 * 128
D_FF = 8192
ROPE_THETA = 10000.0
NORM_EPS = 1e-6
GN_EPS = 64e-5

VMEM_LIMIT_BYTES = 56 * 1024 * 1024


def _cp(*sem):
    return pltpu.CompilerParams(dimension_semantics=sem, vmem_limit_bytes=VMEM_LIMIT_BYTES)


def _dot(a, b):
    return jnp.dot(a, b, preferred_element_type=F32)


def _rms(x, g):
    ms = jnp.mean(x * x, axis=-1, keepdims=True)
    return x * lax.rsqrt(ms + NORM_EPS) * g


def _split_bf16(x):
    hi = x.astype(BF16)
    lo = (x - hi.astype(F32)).astype(BF16)
    return hi, lo


def _segsum(x, bd):
    hi, lo = _split_bf16(x)
    left = _dot(hi[:, :256], bd) + _dot(lo[:, :256], bd)
    right = _dot(hi[:, 256:], bd) + _dot(lo[:, 256:], bd)
    return jnp.concatenate([left, right], axis=1)


def _segsum_bf16(xb, bd):
    return jnp.concatenate([_dot(xb[:, :256], bd), _dot(xb[:, 256:], bd)], axis=1)


def _mod_kernel(c_ref, w_ref, b_ref, o_ref):
    c = c_ref[...]
    s = (c * jax.nn.sigmoid(c)).astype(BF16)
    o_ref[...] = _dot(s, w_ref[...].astype(BF16)) + b_ref[...]


def _modulation(cond, w_ada, b_ada):
    depth, d, n = w_ada.shape
    rows = cond.shape[0]
    tn = 1024
    return pl.pallas_call(
        _mod_kernel,
        grid=(depth, n // tn),
        in_specs=[
            pl.BlockSpec((rows, d), lambda l, j: (0, 0)),
            pl.BlockSpec((None, d, tn), lambda l, j: (l, 0, j)),
            pl.BlockSpec((None, 1, tn), lambda l, j: (l, 0, j)),
        ],
        out_specs=pl.BlockSpec((None, rows, tn), lambda l, j: (l, 0, j)),
        out_shape=jax.ShapeDtypeStruct((depth, rows, n), F32),
        compiler_params=_cp("parallel", "arbitrary"),
        name="adaln_modulation",
    )(cond, w_ada, b_ada.reshape(depth, 1, n))


def _mod_spec(layer, row_of_tile):
    return pl.BlockSpec((None, None, N_MOD, D_MODEL), lambda i, *_: (layer, row_of_tile(i), 0, 0))


def _inproj_kernel(x_ref, nw_ref, mod_ref, ws_ref, wr_ref, wm_ref, os_ref, or_ref, om_ref):
    h = _rms(x_ref[...], nw_ref[...]) * (1.0 + mod_ref[1:2, :]) + mod_ref[0:1, :]
    hb = h.astype(BF16)
    os_ref[...] = _dot(hb, ws_ref[...])
    or_ref[...] = _dot(hb, wr_ref[...])
    om_ref[...] = _dot(hb, wm_ref[...])


def _inproj(x2, nw, mod, layer, row_of_tile, tm, ws, wr, wm):
    n = x2.shape[0]
    full = lambda a: pl.BlockSpec(a.shape, lambda i: (0,) * a.ndim)
    return pl.pallas_call(
        _inproj_kernel,
        grid=(n // tm,),
        in_specs=[
            pl.BlockSpec((tm, D_MODEL), lambda i: (i, 0)),
            full(nw),
            _mod_spec(layer, row_of_tile),
            full(ws), full(wr), full(wm),
        ],
        out_specs=[
            pl.BlockSpec((tm, S5_WIDTH), lambda i: (i, 0)),
            pl.BlockSpec((tm, RWKV_COLS), lambda i: (i, 0)),
            pl.BlockSpec((tm, MLA_COLS), lambda i: (i, 0)),
        ],
        out_shape=[
            jax.ShapeDtypeStruct((n, S5_WIDTH), F32),
            jax.ShapeDtypeStruct((n, RWKV_COLS), F32),
            jax.ShapeDtypeStruct((n, MLA_COLS), F32),
        ],
        compiler_params=_cp("parallel"),
        name="in_projection",
    )(x2, nw, mod, ws, wr, wm)


def _s5_prep_kernel(are_ref, aim_ref, ldt_ref, bre_ref, bim_ref, cre_ref, cim_ref,
                    k_ref, pin_ref, poutt_ref, lam_ref):
    T = S5_CHUNK
    for d in range(2):
        are = are_ref[d:d + 1, :]
        aim = aim_ref[d:d + 1, :]
        dt = jnp.exp(ldt_ref[d:d + 1, :])
        lre = jnp.exp(are * dt) * jnp.cos(aim * dt)
        lim = jnp.exp(are * dt) * jnp.sin(aim * dt)
        den = are * are + aim * aim
        xr = lre - 1.0
        zre = (xr * are + lim * aim) / den
        zim = (lim * are - xr * aim) / den
        bre = bre_ref[d]
        bim = bim_ref[d]
        bbre = zre * bre - zim * bim
        bbim = zre * bim + zim * bre
        cre = cre_ref[d]
        cim = cim_ref[d]

        def powers(tau):
            mag = jnp.exp(tau * (are * dt))
            ang = tau * (aim * dt)
            return mag * jnp.cos(ang), mag * jnp.sin(ang)

        tau0 = lax.broadcasted_iota(jnp.int32, (T, 1), 0).astype(F32)
        ere, eim = powers(tau0)
        xre = (ere[:, None, :] * cre[None] - eim[:, None, :] * cim[None]).reshape(T * S5_CH, S5_STATE)
        xim = (ere[:, None, :] * cim[None] + eim[:, None, :] * cre[None]).reshape(T * S5_CH, S5_STATE)
        nt = (((1,), (1,)), ((), ()))
        k_ref[d] = (lax.dot_general(xre, bbre, nt, precision=lax.Precision.HIGHEST, preferred_element_type=F32)
                    - lax.dot_general(xim, bbim, nt, precision=lax.Precision.HIGHEST, preferred_element_type=F32))
        tau_out = tau0 + 1.0 if d == 0 else float(T) - tau0
        ore, oim = powers(tau_out)
        poutt_ref[2 * d] = (ore[:, None, :] * cre[None] - oim[:, None, :] * cim[None]).reshape(T * S5_CH, S5_STATE)
        poutt_ref[2 * d + 1] = -(ore[:, None, :] * cim[None] + oim[:, None, :] * cre[None]).reshape(T * S5_CH, S5_STATE)
        tau_in = float(T - 1) - tau0 if d == 0 else tau0
        ire, iim = powers(tau_in)
        pin_ref[2 * d] = (ire[:, None, :] * bbre[None] - iim[:, None, :] * bbim[None]).reshape(T * S5_CH, S5_STATE)
        pin_ref[2 * d + 1] = (ire[:, None, :] * bbim[None] + iim[:, None, :] * bbre[None]).reshape(T * S5_CH, S5_STATE)
        tre, tim = powers(jnp.full((1, 1), float(T), F32))
        lam_ref[2 * d:2 * d + 1, :] = tre
        lam_ref[2 * d + 1:2 * d + 2, :] = tim


def _s5_prep(a_re, a_im, log_dt, b_re, b_im, c_re, c_im):
    G, P, CH, T = S5_GROUPS, S5_STATE, S5_CH, S5_CHUNK
    g_first = lambda a: jnp.swapaxes(a, 0, 1)
    are = g_first(a_re)
    aim = g_first(a_im)
    ldt = g_first(log_dt)[..., None]
    bre = jnp.swapaxes(g_first(b_re), -1, -2)
    bim = jnp.swapaxes(g_first(b_im), -1, -2)
    cre = g_first(c_re)
    cim = g_first(c_im)
    spec3 = lambda s: pl.BlockSpec((None,) + s, lambda g: (g,) + (0,) * len(s))
    k, pin, poutt, lam = pl.pallas_call(
        _s5_prep_kernel,
        grid=(G,),
        in_specs=[spec3((2, P)), spec3((2, P)), spec3((2, 1)),
                  spec3((2, CH, P)), spec3((2, CH, P)), spec3((2, CH, P)), spec3((2, CH, P))],
        out_specs=[spec3((2, T * CH, CH)), spec3((4, T * CH, P)), spec3((4, T * CH, P)), spec3((4, P))],
        out_shape=[jax.ShapeDtypeStruct((G, 2, T * CH, CH), F32),
                   jax.ShapeDtypeStruct((G, 4, T * CH, P), F32),
                   jax.ShapeDtypeStruct((G, 4, T * CH, P), F32),
                   jax.ShapeDtypeStruct((G, 4, P), F32)],
        compiler_params=_cp("parallel"),
        name="s5_weight_prep",
    )(are, aim, ldt, bre, bim, cre, cim)
    k = k.reshape(G, 2, T, CH, CH)
    s_idx = jnp.arange(T)[:, None]
    t_idx = jnp.arange(T)[None, :]
    kf = jnp.where((t_idx >= s_idx)[None, :, :, None, None], k[:, 0][:, jnp.clip(t_idx - s_idx, 0, T - 1)], 0.0)
    kb = jnp.where((t_idx <= s_idx)[None, :, :, None, None], k[:, 1][:, jnp.clip(s_idx - t_idx, 0, T - 1)], 0.0)
    m = (kf + kb).transpose(0, 1, 4, 2, 3)
    GT, G8 = G // S5_TILE_GROUPS, S5_TILE_GROUPS
    eye8 = jnp.eye(G8, dtype=F32)
    w_big = jnp.einsum("xgsitc,gh->xsgithc", m.reshape(GT, G8, T, CH, T, CH), eye8).reshape(GT, T * 128, T * 128)
    pin_big = jnp.einsum("xgqsip,gh->xsgihqp", pin.reshape(GT, G8, 4, T, CH, P), eye8).reshape(GT, T * 128, G8 * 4 * P)
    pout_big = jnp.einsum("xgqtcp,gh->xgqpthc", poutt.reshape(GT, G8, 4, T, CH, P), eye8).reshape(GT, G8 * 4 * P, T * 128)
    lre = jnp.concatenate([lam[:, 0], lam[:, 0], lam[:, 2], lam[:, 2]], axis=-1)
    lim = jnp.concatenate([-lam[:, 1], lam[:, 1], -lam[:, 3], lam[:, 3]], axis=-1)
    lam_rows = jnp.stack([lre.reshape(GT, G8 * 4 * P), lim.reshape(GT, G8 * 4 * P)], axis=1)
    return w_big.astype(BF16), pin_big.astype(BF16), pout_big.astype(BF16), lam_rows


def _s5_chunk_rows(u_ref, bsz, cblk):
    T = S5_CHUNK
    per_b = [jnp.concatenate([u_ref[b, pl.ds(s, cblk, stride=T), :] for s in range(T)], axis=1) for b in range(bsz)]
    return jnp.concatenate(per_b, axis=0).astype(BF16)


def _s5_summary_kernel(u_ref, pin_ref, g_ref, *, bsz, cblk):
    g = _dot(_s5_chunk_rows(u_ref, bsz, cblk), pin_ref[...])
    for k in range(g_ref.shape[0]):
        for b in range(bsz):
            g_ref[k, pl.ds(b, cblk, stride=bsz), :] = g[b * cblk:(b + 1) * cblk, k * 128:(k + 1) * 128]


def _s5_state_kernel(g_ref, lam_ref, h0_ref, hin_ref, hfin_ref, *, n_chunks, bsz):
    lre = lam_ref[0:1, :]
    lim = lam_ref[1:2, :]
    tiles = [slice(k * 128, (k + 1) * 128) for k in range(4)]

    def body(c, carry):
        hs, hx = carry
        rows = (pl.ds(pl.multiple_of(c * bsz, bsz), bsz), pl.ds(pl.multiple_of((n_chunks - 1 - c) * bsz, bsz), bsz))
        out, outx = [], []
        for k, sl in enumerate(tiles):
            r = rows[k % 2]
            hin_ref[k, r, :] = hs[k]
            g = g_ref[k, r, :]
            out.append(lre[:, sl] * hs[k] + lim[:, sl] * hx[k] + g)
            outx.append(lre[:, sl] * hx[k] - lim[:, sl] * hs[k] + pltpu.roll(g, 64, 1))
        return tuple(out), tuple(outx)

    h0 = tuple(h0_ref[:, sl] for sl in tiles)
    hs, _ = lax.fori_loop(0, n_chunks, body, (h0, tuple(pltpu.roll(h, 64, 1) for h in h0)))
    for k, sl in enumerate(tiles):
        hfin_ref[:, sl] = hs[k]


def _s5_output_kernel(u_ref, hin_ref, w_ref, pout_ref, y_ref, *, bsz, cblk):
    T = S5_CHUNK
    x = _s5_chunk_rows(u_ref, bsz, cblk)
    hin = jnp.concatenate(
        [jnp.concatenate([hin_ref[k, pl.ds(b, cblk, stride=bsz), :] for k in range(hin_ref.shape[0])], axis=1)
         for b in range(bsz)], axis=0)
    hi, lo = _split_bf16(hin)
    pout = pout_ref[...]
    y = _dot(x, w_ref[...]) + _dot(hi, pout) + _dot(lo, pout)
    for b in range(bsz):
        for s in range(T):
            y_ref[b, pl.ds(s, cblk, stride=T), :] = y[b * cblk:(b + 1) * cblk, s * 128:(s + 1) * 128]


def _s5_scan(u, h0, w_big, pin_big, pout_big, lam_rows):
    bsz, seq, _ = u.shape
    G, P, T = S5_GROUPS, S5_STATE, S5_CHUNK
    GT, G8 = G // S5_TILE_GROUPS, S5_TILE_GROUPS
    SW = G8 * 4 * P
    nc = seq // T
    cblk = min(max(256 // bsz, 8), nc)
    nblk = nc // cblk
    h0g = h0.transpose(2, 0, 1, 4, 3).reshape(GT, G8, bsz, 4 * P).transpose(0, 2, 1, 3).reshape(GT, bsz, SW)
    u_spec = pl.BlockSpec((bsz, cblk * T, 128), lambda x, j: (0, j, x))
    n_tiles = SW // 128
    rows_spec = pl.BlockSpec((None, n_tiles, cblk * bsz, 128), lambda x, j: (x, 0, j, 0))
    per_tile = lambda a: pl.BlockSpec((None,) + a.shape[1:], lambda x, j: (x, 0, 0))
    g = pl.pallas_call(
        functools.partial(_s5_summary_kernel, bsz=bsz, cblk=cblk),
        grid=(GT, nblk),
        in_specs=[u_spec, per_tile(pin_big)],
        out_specs=rows_spec,
        out_shape=jax.ShapeDtypeStruct((GT, n_tiles, nc * bsz, 128), F32),
        compiler_params=_cp("parallel", "parallel"),
        name="s5_chunk_summary",
    )(u, pin_big)
    quarter = lambda r: pl.BlockSpec((None, r, 512), lambda x, q: (x, 0, q))
    quarter_rows = pl.BlockSpec((None, 4, nc * bsz, 128), lambda x, q: (x, q, 0, 0))
    hin, hfin = pl.pallas_call(
        functools.partial(_s5_state_kernel, n_chunks=nc, bsz=bsz),
        grid=(GT, n_tiles // 4),
        in_specs=[quarter_rows, quarter(2), quarter(bsz)],
        out_specs=[quarter_rows, quarter(bsz)],
        out_shape=[jax.ShapeDtypeStruct((GT, n_tiles, nc * bsz, 128), F32), jax.ShapeDtypeStruct((GT, bsz, SW), F32)],
        compiler_params=_cp("parallel", "parallel"),
        name="s5_state_scan",
    )(g, lam_rows, h0g)
    y = pl.pallas_call(
        functools.partial(_s5_output_kernel, bsz=bsz, cblk=cblk),
        grid=(GT, nblk),
        in_specs=[u_spec, rows_spec, per_tile(w_big), per_tile(pout_big)],
        out_specs=u_spec,
        out_shape=jax.ShapeDtypeStruct((bsz, seq, S5_WIDTH), F32),
        compiler_params=_cp("parallel", "parallel"),
        name="s5_chunk_output",
    )(u, hin, w_big, pout_big)
    hfin = hfin.reshape(GT, bsz, G8, 2, 2, P).transpose(1, 3, 0, 2, 5, 4).reshape(bsz, 2, G, P, 2)
    return y, hfin


def _s5_out_kernel(y_ref, u_ref, d_ref, w_ref, nw_ref, o_ref):
    y = y_ref[...] + u_ref[...] * d_ref[...]
    c = math.sqrt(2.0 / math.pi)
    y = y * (0.5 * (1.0 + jnp.tanh(c * (y + 0.044715 * (y * y * y)))))
    z = _dot(y.astype(BF16), w_ref[...])
    o = z[:, :S5_WIDTH] * jax.nn.sigmoid(z[:, S5_WIDTH:])
    o_ref[...] = _rms(o, nw_ref[...]).astype(BF16)


def _s5_out(y2, u2, d_skip, w_glu, nw, tm):
    n = y2.shape[0]
    full = lambda a: pl.BlockSpec(a.shape, lambda i: (0,) * a.ndim)
    row = pl.BlockSpec((tm, S5_WIDTH), lambda i: (i, 0))
    return pl.pallas_call(
        _s5_out_kernel,
        grid=(n // tm,),
        in_specs=[row, row, full(d_skip), full(w_glu), full(nw)],
        out_specs=row,
        out_shape=jax.ShapeDtypeStruct((n, S5_WIDTH), BF16),
        compiler_params=_cp("parallel"),
        name="s5_gelu_glu",
    )(y2, u2, d_skip, w_glu, nw)


def _rwkv_pre_kernel(p_ref, hp_ref, hn_ref, mu_ref, w0_ref, a0_ref, w2_ref, a2_ref, g2_ref,
                     kk_ref, ka_ref, rk_ref, bd_ref,
                     v_ref, nk_ref, dec_ref, kd_ref, bb_ref, qq_ref, vkr_ref, g_ref, bonus_ref, *, tiles_per_seq, tm):
    i = pl.program_id(0)
    j = i % tiles_per_seq
    p = p_ref[...]
    rows = lax.broadcasted_iota(jnp.int32, (tm, 1), 0)
    prev_edge = jnp.where(j == 0, 0.0, hp_ref[7:8, :])
    next_edge = jnp.where(j == tiles_per_seq - 1, 0.0, hn_ref[0:1, :])
    prev = jnp.where(rows == 0, prev_edge, pltpu.roll(p, 1, 0))
    nxt = jnp.where(rows == tm - 1, next_edge, pltpu.roll(p, tm - 1, 0))
    p = p + mu_ref[...] * (0.5 * (prev + nxt) - p)
    W = RWKV_WIDTH
    r = p[:, 0:W]
    k = p[:, W:2 * W]
    v = p[:, 2 * W:3 * W]
    wl = p[:, 3 * W:3 * W + LORA_PAD]
    al = p[:, 3 * W + LORA_PAD:3 * W + 2 * LORA_PAD]
    gl = p[:, 3 * W + 2 * LORA_PAD:3 * W + 3 * LORA_PAD]
    bd = bd_ref[...]
    kk = k * kk_ref[...]
    kk = kk * lax.rsqrt(_segsum(kk * kk, bd) + 1e-12)
    g_ref[...] = _dot(jax.nn.sigmoid(gl).astype(BF16), g2_ref[...])
    tw = jnp.tanh(wl).astype(BF16)
    alb = al.astype(BF16)
    ksum = None
    for d in range(2):
        z = -(w0_ref[d:d + 1, :] + _dot(tw, w2_ref[d]))
        w = -(jnp.maximum(z, 0.0) + jnp.log(1.0 + jnp.exp(-jnp.abs(z)))) - 0.5
        dec = jnp.exp(-jnp.exp(w))
        dec_ref[d] = dec
        a = jax.nn.sigmoid(a0_ref[d:d + 1, :] + _dot(alb, a2_ref[d]))
        kd = k * (1.0 + (a - 1.0) * ka_ref[...])
        kd_ref[d] = kd
        bb = kk * a
        bb_ref[d] = bb
        qq_ref[d] = dec * r - kk * _segsum(bb * r, bd)
        vkr_ref[d] = v * _segsum(kd * r, bd)
        ksum = kd if ksum is None else ksum + kd
    v_ref[...] = v
    nk_ref[...] = -kk
    bonus_ref[...] = _segsum(r * ksum * rk_ref[...], bd) * v


def _rwkv_pre(p2, bsz, seq, wts, bd):
    n = p2.shape[0]
    tm = min(256, seq)
    tps = seq // tm
    r8 = tm // 8
    nblk8 = n // 8
    full = lambda a: pl.BlockSpec(a.shape, lambda i: (0,) * a.ndim)
    o3 = pl.BlockSpec((None, tm, RWKV_WIDTH), lambda i: (i // tps, i % tps, 0))
    o4 = pl.BlockSpec((2, None, tm, RWKV_WIDTH), lambda i: (0, i // tps, i % tps, 0))
    s3 = jax.ShapeDtypeStruct((bsz, seq, RWKV_WIDTH), F32)
    s4 = jax.ShapeDtypeStruct((2, bsz, seq, RWKV_WIDTH), F32)
    names = ("mu", "w0", "a0", "w2", "a2", "g2", "k_k", "k_a", "r_k")
    return pl.pallas_call(
        functools.partial(_rwkv_pre_kernel, tiles_per_seq=tps, tm=tm),
        grid=(n // tm,),
        in_specs=[pl.BlockSpec((tm, RWKV_COLS), lambda i: (i, 0)),
                  pl.BlockSpec((8, RWKV_COLS), lambda i: (jnp.maximum(i * r8 - 1, 0), 0)),
                  pl.BlockSpec((8, RWKV_COLS), lambda i: (jnp.minimum((i + 1) * r8, nblk8 - 1), 0))]
                 + [full(wts[k]) for k in names] + [full(bd)],
        out_specs=[o3, o3, o4, o4, o4, o4, o4, o3, o3],
        out_shape=[s3, s3, s4, s4, s4, s4, s4, s3, s3],
        compiler_params=_cp("parallel"),
        name="rwkv_prepare",
    )(p2, p2, p2, *[wts[k] for k in names], bd)


RWKV_COL_STEPS = 16


def _rwkv_scan_kernel(nkf, nkb, vf, vb, decf, decb, kdf, kdb, bbf, bbb, qqf, qqb, vkf, vkb, s0_ref,
                      yf_ref, yb_ref, sfin_ref, s_scr, v8_scr, v8k_scr, y8_scr, vc_scr, *, nb, tlen, n_chunks):
    c = pl.program_id(1)

    @pl.when(c == 0)
    def _():
        s_scr[...] = s0_ref[...].reshape(s_scr.shape)

    N = RWKV_HEAD_DIM
    W = RWKV_WIDTH
    H = RWKV_HEADS
    nt = (((1,), (1,)), ((), ()))
    chains = [(d, n) for d in range(2) for n in range(nb)]
    pick = lambda d, f, b: f if d == 0 else b
    base = lambda m: m * tlen * H

    left = lambda rows: lax.broadcasted_iota(jnp.int32, (rows, 128), 1) < N
    for m, (d, n) in enumerate(chains):
        for p in range(H // 2):
            tile = pick(d, vf, vb)[n, :, p * 128:(p + 1) * 128]
            v8_scr[pl.ds(base(m) + 2 * p, tlen, stride=H), :] = tile
            v8_scr[pl.ds(base(m) + 2 * p + 1, tlen, stride=H), :] = pltpu.roll(tile, N, 1)
            tile = pick(d, vkf, vkb)[n, :, p * 128:(p + 1) * 128]
            swapped = pltpu.roll(tile, N, 1)
            v8k_scr[pl.ds(base(m) + 2 * p, tlen, stride=H), :] = jnp.where(left(tlen), tile, swapped)
            v8k_scr[pl.ds(base(m) + 2 * p + 1, tlen, stride=H), :] = jnp.where(left(tlen), swapped, tile)

    n_col_tiles = tlen // RWKV_COL_STEPS
    eye = (lax.broadcasted_iota(jnp.int32, (N, N), 1) == lax.broadcasted_iota(jnp.int32, (N, N), 0)).astype(BF16)
    for m in range(len(chains)):
        for j in range(n_col_tiles):
            rows = v8_scr[pl.ds(base(m) + j * 128, 128), :][:, 0:N].astype(BF16)
            vc_scr[m * n_col_tiles + j] = lax.dot_general(eye, rows, nt, preferred_element_type=F32)

    hmask = (lax.broadcasted_iota(jnp.int32, (H, W), 1) // N == lax.broadcasted_iota(jnp.int32, (H, W), 0)).astype(F32)
    sel = (lax.broadcasted_iota(jnp.int32, (H, 2 * H), 1) == lax.broadcasted_iota(jnp.int32, (H, 2 * H), 0) + H).astype(BF16)
    v_cols = lax.broadcasted_iota(jnp.int32, (N, 2 * H), 1) >= H

    def step(t, carry):
        tts = (t, tlen - 1 - t)
        row = lambda ref, n, tt: ref[n, pl.ds(tt, 1), :] * hmask
        tile8 = lambda m, tt: pl.ds(pl.multiple_of(base(m) + tt * H, H), H)
        os_ = []
        for m, (d, n) in enumerate(chains):
            tt = tts[d]
            rhs = jnp.concatenate([row(pick(d, nkf, nkb), n, tt), row(pick(d, qqf, qqb), n, tt)], axis=0).astype(BF16)
            os_.append(lax.dot_general(s_scr[m].astype(BF16), rhs, nt, preferred_element_type=F32))
        for m, (d, n) in enumerate(chains):
            tt = tts[d]
            vt = pltpu.roll(vc_scr[m * n_col_tiles + tt // RWKV_COL_STEPS], (128 + H - (tt % RWKV_COL_STEPS) * H) % 128, 1)
            sa_v = jnp.where(v_cols, vt[:, 0:2 * H], os_[m]).astype(BF16)
            w2 = jnp.concatenate([row(pick(d, bbf, bbb), n, tt), row(pick(d, kdf, kdb), n, tt)], axis=0).astype(BF16)
            s_scr[m] = s_scr[m] * pick(d, decf, decb)[n, pl.ds(tt, 1), :] + _dot(sa_v, w2)
        for m, (d, n) in enumerate(chains):
            ob = os_[m].astype(BF16)
            ob2 = jnp.concatenate([ob, ob], axis=0)
            y8_scr[tile8(m, tts[d]), :] = (lax.dot_general(sel, ob2, nt, preferred_element_type=F32)
                                           + v8k_scr[tile8(m, tts[d]), :])
        return carry

    lax.fori_loop(0, tlen, step, 0, unroll=2)

    for m, (d, n) in enumerate(chains):
        y_ref = pick(d, yf_ref, yb_ref)
        for p in range(H // 2):
            even = y8_scr[pl.ds(base(m) + 2 * p, tlen, stride=H), :]
            odd = y8_scr[pl.ds(base(m) + 2 * p + 1, tlen, stride=H), :]
            y_ref[n, :, p * 128:(p + 1) * 128] = jnp.where(left(tlen), even, odd)

    @pl.when(c == n_chunks - 1)
    def _():
        sfin_ref[...] = s_scr[...].reshape(sfin_ref.shape)


def _rwkv_scan(nk, v, dec, kd, bb, qq, vkr, s0):
    bsz, seq, W = nk.shape
    N, H = RWKV_HEAD_DIM, RWKV_HEADS
    nb = 8 if bsz % 8 == 0 else 4
    tlen = min(32, seq)
    nc = seq // tlen
    fwd = pl.BlockSpec((nb, tlen, W), lambda b, c: (b, c, 0))
    bwd = pl.BlockSpec((nb, tlen, W), lambda b, c: (b, nc - 1 - c, 0))
    fwd_d = pl.BlockSpec((None, nb, tlen, W), lambda b, c: (0, b, c, 0))
    bwd_d = pl.BlockSpec((None, nb, tlen, W), lambda b, c: (1, b, nc - 1 - c, 0))
    st = pl.BlockSpec((2, nb, N, W), lambda b, c: (0, b, 0, 0))
    tiles = pltpu.VMEM((2 * nb * tlen * H, 128), F32)
    return pl.pallas_call(
        functools.partial(_rwkv_scan_kernel, nb=nb, tlen=tlen, n_chunks=nc),
        grid=(bsz // nb, nc),
        in_specs=[fwd, bwd, fwd, bwd] + [fwd_d, bwd_d] * 5 + [st],
        out_specs=[fwd, bwd, st],
        out_shape=[jax.ShapeDtypeStruct((bsz, seq, W), F32)] * 2 + [jax.ShapeDtypeStruct((2, bsz, N, W), F32)],
        scratch_shapes=[pltpu.VMEM((2 * nb, N, W), F32), tiles, tiles, tiles,
                        pltpu.VMEM((2 * nb * (tlen // RWKV_COL_STEPS), N, 128), F32)],
        compiler_params=_cp("parallel", "arbitrary"),
        name="rwkv_scan",
    )(nk, nk, v, v, dec, dec, kd, kd, bb, bb, qq, qq, vkr, vkr, s0)


def _rwkv_post_kernel(yf_ref, yb_ref, bonus_ref, g_ref, lw_ref, lb_ref, bd_ref, o_ref):
    bd = bd_ref[...]
    y = yf_ref[...] + yb_ref[...]
    inv_n = 1.0 / RWKV_HEAD_DIM
    mean = _segsum(y, bd) * inv_n
    yc = y - mean
    var = _segsum(yc * yc, bd) * inv_n
    yn = yc * lax.rsqrt(var + GN_EPS) * lw_ref[...] + lb_ref[...]
    o_ref[...] = ((yn + bonus_ref[...]) * g_ref[...]).astype(BF16)


def _rwkv_post(yf, yb, bonus, g, ln_w, ln_b, bd):
    bsz, seq, W = yf.shape
    tm = min(256, seq)
    tps = seq // tm
    n = bsz * seq
    full = lambda a: pl.BlockSpec(a.shape, lambda i: (0,) * a.ndim)
    i3 = pl.BlockSpec((None, tm, W), lambda i: (i // tps, i % tps, 0))
    return pl.pallas_call(
        _rwkv_post_kernel,
        grid=(n // tm,),
        in_specs=[i3, i3, i3, i3, full(ln_w), full(ln_b), full(bd)],
        out_specs=pl.BlockSpec((tm, W), lambda i: (i, 0)),
        out_shape=jax.ShapeDtypeStruct((n, W), BF16),
        compiler_params=_cp("parallel"),
        name="rwkv_groupnorm_gate",
    )(yf, yb, bonus, g, ln_w, ln_b, bd)


def _store_k_heads(k_ref, kn, kpe):
    for h in range(MLA_HEADS):
        k_ref[:, h * MLA_QK_PAD:h * MLA_QK_PAD + 128] = kn[:, h * 128:(h + 1) * 128].astype(BF16)
        k_ref[:, h * MLA_QK_PAD + 128:(h + 1) * MLA_QK_PAD] = kpe


def _mla_prep_kernel(*refs, rope):
    if rope:
        (p_ref, qn_ref, kvn_ref, wq_ref, wqr_ref, wk_ref, wv_ref, cq_ref, sq_ref, ck_ref, sk_ref,
         q_ref, k_ref, v_ref) = refs
    else:
        (p_ref, qn_ref, kvn_ref, wq_ref, wk_ref, wv_ref, q_ref, k_ref, v_ref, ckv_ref, kr_ref) = refs
    p = p_ref[...]
    qn = _rms(p[:, 0:MLA_Q_RANK], qn_ref[...]).astype(BF16)
    q = _dot(qn, wq_ref[...])
    ckv = _rms(p[:, MLA_Q_RANK:MLA_Q_RANK + MLA_KV_RANK], kvn_ref[...])
    kr = p[:, MLA_Q_RANK + MLA_KV_RANK:MLA_Q_RANK + MLA_KV_RANK + 128]
    if rope:
        cq = jnp.concatenate([cq_ref[...]] * MLA_HEADS, axis=1)
        sq = jnp.concatenate([sq_ref[...]] * MLA_HEADS, axis=1)
        q = q * cq + _dot(qn, wqr_ref[...]) * sq
        krot = p[:, MLA_Q_RANK + MLA_KV_RANK + 128:MLA_Q_RANK + MLA_KV_RANK + 256]
        kpe = kr * ck_ref[...] + krot * sk_ref[...]
    else:
        kpe = kr
        ckv_ref[...] = ckv
        kr_ref[...] = kr[:, 0:MLA_ROPE_DIM]
    q_ref[...] = q.astype(BF16)
    cb = ckv.astype(BF16)
    _store_k_heads(k_ref, _dot(cb, wk_ref[...]), kpe.astype(BF16))
    v_ref[...] = _dot(cb, wv_ref[...]).astype(BF16)


def _mla_prep(p2, seq, wts, tables):
    n = p2.shape[0]
    rope = tables is not None
    tm = min(256, seq)
    tps = seq // tm
    full = lambda a: pl.BlockSpec(a.shape, lambda i: (0,) * a.ndim)
    row = lambda w: pl.BlockSpec((tm, w), lambda i: (i, 0))
    ins = [p2, wts["q_norm"], wts["kv_norm"], wts["wq"]]
    specs = [row(MLA_COLS), full(wts["q_norm"]), full(wts["kv_norm"]), full(wts["wq"])]
    if rope:
        ins.append(wts["wq_rot"])
        specs.append(full(wts["wq_rot"]))
    ins += [wts["wk"], wts["wv"]]
    specs += [full(wts["wk"]), full(wts["wv"])]
    outs = [row(MLA_HEADS * MLA_QK_PAD), row(MLA_HEADS * MLA_QK_PAD), row(MLA_HEADS * MLA_V_DIM)]
    shapes = [jax.ShapeDtypeStruct((n, MLA_HEADS * MLA_QK_PAD), BF16),
              jax.ShapeDtypeStruct((n, MLA_HEADS * MLA_QK_PAD), BF16),
              jax.ShapeDtypeStruct((n, MLA_HEADS * MLA_V_DIM), BF16)]
    if rope:
        ins += list(tables)
        specs += [pl.BlockSpec((tm, t.shape[1]), lambda i: (i % tps, 0)) for t in tables]
    else:
        outs += [row(MLA_KV_RANK), row(MLA_ROPE_DIM)]
        shapes += [jax.ShapeDtypeStruct((n, MLA_KV_RANK), F32), jax.ShapeDtypeStruct((n, MLA_ROPE_DIM), F32)]
    return pl.pallas_call(
        functools.partial(_mla_prep_kernel, rope=rope),
        grid=(n // tm,),
        in_specs=specs, out_specs=outs, out_shape=shapes,
        compiler_params=_cp("parallel"),
        name="mla_prepare_rope" if rope else "mla_prepare",
    )(*ins)


def _mla_cache_kernel(ckv_ref, kr_ref, wk_ref, wv_ref, k_ref, v_ref):
    cb = ckv_ref[...].astype(BF16)
    _store_k_heads(k_ref, _dot(cb, wk_ref[...]), kr_ref[...].astype(BF16))
    v_ref[...] = _dot(cb, wv_ref[...]).astype(BF16)


def _mla_cache(cache_ckv, cache_kr_pad, layer, wk, wv):
    bsz, _, past, _ = cache_ckv.shape
    full = lambda a: pl.BlockSpec(a.shape, lambda b: (0,) * a.ndim)
    return pl.pallas_call(
        _mla_cache_kernel,
        grid=(bsz,),
        in_specs=[pl.BlockSpec((None, None, past, MLA_KV_RANK), lambda b: (b, layer, 0, 0)),
                  pl.BlockSpec((None, None, past, 128), lambda b: (b, layer, 0, 0)),
                  full(wk), full(wv)],
        out_specs=[pl.BlockSpec((None, past, MLA_HEADS * MLA_QK_PAD), lambda b: (b, 0, 0)),
                   pl.BlockSpec((None, past, MLA_HEADS * MLA_V_DIM), lambda b: (b, 0, 0))],
        out_shape=[jax.ShapeDtypeStruct((bsz, past, MLA_HEADS * MLA_QK_PAD), BF16),
                   jax.ShapeDtypeStruct((bsz, past, MLA_HEADS * MLA_V_DIM), BF16)],
        compiler_params=_cp("parallel"),
        name="mla_cache_keys",
    )(cache_ckv, cache_kr_pad, wk, wv)


def _attn_kernel(*refs, cache, scale):
    if cache:
        q_ref, k_ref, v_ref, kc_ref, vc_ref, o_ref = refs
    else:
        q_ref, k_ref, v_ref, o_ref = refs
    nt = (((1,), (1,)), ((), ()))
    q = q_ref[...]
    s = lax.dot_general(q, k_ref[...], nt, preferred_element_type=F32) * scale
    m = jnp.max(s, axis=-1, keepdims=True)
    if cache:
        sc = lax.dot_general(q, kc_ref[...], nt, preferred_element_type=F32) * scale
        m = jnp.maximum(m, jnp.max(sc, axis=-1, keepdims=True))
    e = jnp.exp(s - m)
    den = jnp.sum(e, axis=-1, keepdims=True)
    o = _dot(e.astype(BF16), v_ref[...])
    if cache:
        ec = jnp.exp(sc - m)
        den = den + jnp.sum(ec, axis=-1, keepdims=True)
        o = o + _dot(ec.astype(BF16), vc_ref[...])
    o_ref[...] = o / den


def _attention(q, k, v, kc=None, vc=None):
    bsz, seq, _ = q.shape
    tq = min(256, seq)
    cache = kc is not None
    scale = float(MLA_NOPE_DIM + MLA_ROPE_DIM) ** -0.5
    ins = [q, k, v]
    specs = [pl.BlockSpec((None, tq, MLA_QK_PAD), lambda b, h, i: (b, i, h)),
             pl.BlockSpec((None, seq, MLA_QK_PAD), lambda b, h, i: (b, 0, h)),
             pl.BlockSpec((None, seq, MLA_V_DIM), lambda b, h, i: (b, 0, h))]
    if cache:
        past = kc.shape[1]
        ins += [kc, vc]
        specs += [pl.BlockSpec((None, past, MLA_QK_PAD), lambda b, h, i: (b, 0, h)),
                  pl.BlockSpec((None, past, MLA_V_DIM), lambda b, h, i: (b, 0, h))]
    return pl.pallas_call(
        functools.partial(_attn_kernel, cache=cache, scale=scale),
        grid=(bsz, MLA_HEADS, seq // tq),
        in_specs=specs,
        out_specs=pl.BlockSpec((None, tq, MLA_V_DIM), lambda b, h, i: (b, i, h)),
        out_shape=jax.ShapeDtypeStruct((bsz, seq, MLA_HEADS * MLA_V_DIM), F32),
        compiler_params=_cp("parallel", "parallel", "arbitrary"),
        name="mla_attention_cached" if cache else "mla_attention",
    )(*ins)


def _outproj_kernel(ys_ref, yr_ref, ym_ref, x_ref, mod_ref, nm_ref, w_ref, o_ref):
    ym = _rms(ym_ref[...], nm_ref[...]).astype(BF16)
    acc = _dot(ys_ref[...], w_ref[0:S5_WIDTH, :])
    acc += _dot(yr_ref[...], w_ref[S5_WIDTH:S5_WIDTH + RWKV_WIDTH, :])
    acc += _dot(ym, w_ref[S5_WIDTH + RWKV_WIDTH:, :])
    o_ref[...] = x_ref[...] + mod_ref[2:3, :] * acc


def _outproj(ys, yr, ym, x2, mod, layer, row_of_tile, tm, nm, w_out):
    n = x2.shape[0]
    full = lambda a: pl.BlockSpec(a.shape, lambda i: (0,) * a.ndim)
    row = lambda w: pl.BlockSpec((tm, w), lambda i: (i, 0))
    return pl.pallas_call(
        _outproj_kernel,
        grid=(n // tm,),
        in_specs=[row(S5_WIDTH), row(RWKV_WIDTH), row(MLA_WIDTH), row(D_MODEL), _mod_spec(layer, row_of_tile),
                  full(nm), full(w_out)],
        out_specs=row(D_MODEL),
        out_shape=jax.ShapeDtypeStruct((n, D_MODEL), F32),
        compiler_params=_cp("parallel"),
        name="out_projection",
    )(ys, yr, ym, x2, mod, nm, w_out)


def _mlp_kernel(x_ref, nw_ref, mod_ref, w1_ref, w2_ref, nf_ref, o_ref, h_scr, acc_scr, *, final_norm):
    j = pl.program_id(1)

    @pl.when(j == 0)
    def _():
        h = _rms(x_ref[...], nw_ref[...]) * (1.0 + mod_ref[4:5, :]) + mod_ref[3:4, :]
        h_scr[...] = h.astype(BF16)
        acc_scr[...] = jnp.zeros_like(acc_scr)

    a = _dot(h_scr[...], w1_ref[...])
    a = jnp.square(jnp.maximum(a, 0.0)).astype(BF16)
    acc_scr[...] += _dot(a, w2_ref[...])

    @pl.when(j == pl.num_programs(1) - 1)
    def _():
        y = x_ref[...] + mod_ref[5:6, :] * acc_scr[...]
        if final_norm:
            y = _rms(y, nf_ref[...])
        o_ref[...] = y


def _mlp(x2, nw, mod, layer, row_of_tile, tm, w1, w2, nf, final_norm):
    n = x2.shape[0]
    tf = 1024
    full = lambda a: pl.BlockSpec(a.shape, lambda i, j: (0,) * a.ndim)
    return pl.pallas_call(
        functools.partial(_mlp_kernel, final_norm=final_norm),
        grid=(n // tm, D_FF // tf),
        in_specs=[pl.BlockSpec((tm, D_MODEL), lambda i, j: (i, 0)), full(nw), _mod_spec(layer, row_of_tile),
                  pl.BlockSpec((D_MODEL, tf), lambda i, j: (0, j)),
                  pl.BlockSpec((tf, D_MODEL), lambda i, j: (j, 0)), full(nf)],
        out_specs=pl.BlockSpec((tm, D_MODEL), lambda i, j: (i, 0)),
        out_shape=jax.ShapeDtypeStruct((n, D_MODEL), F32),
        scratch_shapes=[pltpu.VMEM((tm, D_MODEL), BF16), pltpu.VMEM((tm, D_MODEL), F32)],
        compiler_params=_cp("parallel", "arbitrary"),
        name="mlp_final" if final_norm else "mlp",
    )(x2, nw, mod, w1, w2, nf)


def _rope_tables(length):
    rows = length // GRID_W
    row_pos = jnp.repeat(jnp.arange(rows, dtype=F32), GRID_W)
    col_pos = jnp.tile(jnp.arange(GRID_W, dtype=F32), rows)
    axis_dim = MLA_ROPE_DIM // 2
    inv_freq = 1.0 / (ROPE_THETA ** (jnp.arange(0, axis_dim, 2, dtype=F32) / axis_dim))
    ang_r = row_pos[:, None] * inv_freq[None, :]
    ang_c = col_pos[:, None] * inv_freq[None, :]
    ang = jnp.concatenate([ang_r, ang_r, ang_c, ang_c], axis=-1)
    cos, sin = jnp.cos(ang), jnp.sin(ang)
    z64 = jnp.zeros((length, 64), F32)
    cos_q = jnp.concatenate([jnp.ones((length, MLA_NOPE_DIM), F32), cos, z64], axis=1)
    sin_q = jnp.concatenate([jnp.zeros((length, MLA_NOPE_DIM), F32), sin, z64], axis=1)
    cos_k = jnp.concatenate([cos, z64], axis=1)
    sin_k = jnp.concatenate([sin, z64], axis=1)
    return cos_q, sin_q, cos_k, sin_k


def _rot_cols(w):
    a, b, c, d = w[..., 0:16], w[..., 16:32], w[..., 32:48], w[..., 48:64]
    return jnp.concatenate([-b, a, -d, c], axis=-1)


def _layer_weights(l, p):
    d = D_MODEL
    w_in = p["w_in"][l]
    z64 = jnp.zeros((d, 64), F32)
    o = S5_WIDTH
    rk = w_in[:, o:o + 3 * RWKV_WIDTH]
    o += 3 * RWKV_WIDTH
    wl, al, gl = w_in[:, o:o + 64], w_in[:, o + 64:o + 128], w_in[:, o + 128:o + 256]
    o += 256
    cq, ckv, kr = w_in[:, o:o + 512], w_in[:, o + 512:o + 768], w_in[:, o + 768:o + 832]
    mu = p["rwkv_mu"][l]
    z1 = jnp.zeros((64,), F32)
    mu_pad = jnp.concatenate([mu[:1536], mu[1536:1600], z1, mu[1600:1664], z1, mu[1664:1792]])[None]
    pad_rows = lambda w: jnp.concatenate([w, jnp.zeros_like(w)], axis=-2)
    w_uq = p["mla_w_uq"][l].reshape(MLA_Q_RANK, MLA_HEADS, MLA_NOPE_DIM + MLA_ROPE_DIM)
    zq = jnp.zeros((MLA_Q_RANK, MLA_HEADS, 64), F32)
    wq = jnp.concatenate([w_uq, zq], axis=-1).reshape(MLA_Q_RANK, MLA_HEADS * MLA_QK_PAD)
    wq_rot = jnp.concatenate([jnp.zeros((MLA_Q_RANK, MLA_HEADS, MLA_NOPE_DIM), F32),
                              _rot_cols(w_uq[..., MLA_NOPE_DIM:]), zq], axis=-1).reshape(MLA_Q_RANK, MLA_HEADS * MLA_QK_PAD)
    w_ukv = p["mla_w_ukv"][l].reshape(MLA_KV_RANK, MLA_HEADS, MLA_NOPE_DIM + MLA_V_DIM)
    row = lambda a: a.reshape(1, -1)
    return {
        "norm_mix": row(p["norm_mix"][l]), "norm_mlp": row(p["norm_mlp"][l]),
        "w_s5": w_in[:, 0:S5_WIDTH].astype(BF16),
        "w_rwkv": jnp.concatenate([rk, wl, z64, al, z64, gl], axis=1).astype(BF16),
        "w_mla": jnp.concatenate([cq, ckv, kr, z64, _rot_cols(kr), z64], axis=1).astype(BF16),
        "w_out": p["w_out"][l].astype(BF16),
        "s5_d": row(p["s5_d"][l]), "s5_w_glu": p["s5_w_glu"][l].astype(BF16), "s5_out_norm": row(p["s5_out_norm"][l]),
        "rwkv": {
            "mu": mu_pad, "w0": p["rwkv_w0"][l], "a0": p["rwkv_a0"][l],
            "w2": pad_rows(p["rwkv_w2"][l]).astype(BF16), "a2": pad_rows(p["rwkv_a2"][l]).astype(BF16),
            "g2": p["rwkv_g2"][l].astype(BF16), "k_k": row(p["rwkv_k_k"][l]), "k_a": row(p["rwkv_k_a"][l]),
            "r_k": row(p["rwkv_r_k"][l]),
        },
        "rwkv_ln_w": row(p["rwkv_ln_w"][l]), "rwkv_ln_b": row(p["rwkv_ln_b"][l]),
        "mla": {
            "q_norm": row(p["mla_q_norm"][l]), "kv_norm": row(p["mla_kv_norm"][l]),
            "wq": wq.astype(BF16), "wq_rot": wq_rot.astype(BF16),
            "wk": w_ukv[..., :MLA_NOPE_DIM].reshape(MLA_KV_RANK, -1).astype(BF16),
            "wv": w_ukv[..., MLA_NOPE_DIM:].reshape(MLA_KV_RANK, -1).astype(BF16),
        },
        "mla_out_norm": row(p["mla_out_norm"][l]),
        "mlp_w1": p["mlp_w1"][l].astype(BF16), "mlp_w2": p["mlp_w2"][l].astype(BF16),
    }


def _trunk_layer(x, mod, layer, lw, s5w, bd, row_of_tile_fn, s5_h0, rwkv_s0, cache, tables, norm_final, final_norm):
    bsz, seq, d = x.shape
    n = bsz * seq
    x2 = x.reshape(n, d)
    tm = min(512, seq) if cache is not None else min(512, n)
    row_of_tile = row_of_tile_fn(tm)
    u2, pr2, pm2 = _inproj(x2, lw["norm_mix"], mod, layer, row_of_tile, tm, lw["w_s5"], lw["w_rwkv"], lw["w_mla"])

    ys, s5_fin = _s5_scan(u2.reshape(bsz, seq, S5_WIDTH), s5_h0, *s5w)
    ys2 = _s5_out(ys.reshape(n, S5_WIDTH), u2, lw["s5_d"], lw["s5_w_glu"], lw["s5_out_norm"], min(512, n))

    v, nk, dec, kd, bb, qq, vkr, g, bonus = _rwkv_pre(pr2, bsz, seq, lw["rwkv"], bd)
    s0 = rwkv_s0.transpose(1, 0, 3, 2, 4).reshape(2, bsz, RWKV_HEAD_DIM, RWKV_WIDTH)
    yf, yb, s_fin = _rwkv_scan(nk, v, dec, kd, bb, qq, vkr, s0)
    yr2 = _rwkv_post(yf, yb, bonus, g, lw["rwkv_ln_w"], lw["rwkv_ln_b"], bd)
    rwkv_fin = s_fin.reshape(2, bsz, RWKV_HEAD_DIM, RWKV_HEADS, RWKV_HEAD_DIM).transpose(1, 0, 3, 2, 4)

    shape3 = lambda a: a.reshape(bsz, seq, a.shape[-1])
    if cache is None:
        q, k, v_, ckv_n, k_rope = _mla_prep(pm2, seq, lw["mla"], None)
        ym = _attention(shape3(q), shape3(k), shape3(v_))
        extras = (shape3(ckv_n), shape3(k_rope), s5_fin, rwkv_fin)
    else:
        q, k, v_ = _mla_prep(pm2, seq, lw["mla"], tables)
        kc, vc = _mla_cache(cache[0], cache[1], layer, lw["mla"]["wk"], lw["mla"]["wv"])
        ym = _attention(shape3(q), shape3(k), shape3(v_), kc, vc)
        extras = None

    x2 = _outproj(ys2, yr2, ym.reshape(n, MLA_WIDTH), x2, mod, layer, row_of_tile, tm, lw["mla_out_norm"], lw["w_out"])
    x2 = _mlp(x2, lw["norm_mlp"], mod, layer, row_of_tile, tm, lw["mlp_w1"], lw["mlp_w2"], norm_final, final_norm)
    return x2.reshape(bsz, seq, d), extras


def kernel(x_prompt, x_sample, cache_mla_ckv, cache_mla_krope, state_s5, state_rwkv, c, c_ctx, norm_mix, norm_mlp, norm_final, w_ada, b_ada, w_in, w_out, s5_a_re, s5_a_im, s5_log_dt, s5_b_re, s5_b_im, s5_c_re, s5_c_im, s5_d, s5_w_glu, s5_out_norm, rwkv_mu, rwkv_w0, rwkv_w2, rwkv_a0, rwkv_a2, rwkv_g2, rwkv_k_k, rwkv_k_a, rwkv_r_k, rwkv_ln_w, rwkv_ln_b, mla_q_norm, mla_w_uq, mla_kv_norm, mla_w_ukv, mla_out_norm, mlp_w1, mlp_w2):
    p = dict(norm_mix=norm_mix, norm_mlp=norm_mlp, w_in=w_in, w_out=w_out, s5_d=s5_d, s5_w_glu=s5_w_glu,
             s5_out_norm=s5_out_norm, rwkv_mu=rwkv_mu, rwkv_w0=rwkv_w0, rwkv_w2=rwkv_w2, rwkv_a0=rwkv_a0,
             rwkv_a2=rwkv_a2, rwkv_g2=rwkv_g2, rwkv_k_k=rwkv_k_k, rwkv_k_a=rwkv_k_a, rwkv_r_k=rwkv_r_k,
             rwkv_ln_w=rwkv_ln_w, rwkv_ln_b=rwkv_ln_b, mla_q_norm=mla_q_norm, mla_w_uq=mla_w_uq,
             mla_kv_norm=mla_kv_norm, mla_w_ukv=mla_w_ukv, mla_out_norm=mla_out_norm, mlp_w1=mlp_w1, mlp_w2=mlp_w2)
    depth = w_in.shape[0]
    b_ctx, l_ctx, d = x_prompt.shape
    b_dec, l_dec, _ = x_sample.shape

    rows = -(-(1 + b_dec) // 8) * 8
    cond = jnp.zeros((rows, d), F32).at[0].set(c_ctx).at[1:1 + b_dec].set(c)
    mod = _modulation(cond, w_ada, b_ada).reshape(depth, rows, N_MOD, d)

    bd = jnp.kron(jnp.eye(4, dtype=F32), jnp.ones((64, 64), F32)).astype(BF16)
    tables = _rope_tables(l_dec)
    kr_pad = jnp.pad(cache_mla_krope, ((0, 0), (0, 0), (0, 0), (0, 128 - MLA_ROPE_DIM)))
    zero_s5 = jnp.zeros((b_ctx, 2, S5_GROUPS, S5_STATE, 2), F32)
    zero_rwkv = jnp.zeros((b_ctx, 2, RWKV_HEADS, RWKV_HEAD_DIM, RWKV_HEAD_DIM), F32)
    nf = norm_final.reshape(1, d)

    ctx_rows = lambda tm: (lambda i: 0)
    dec_rows = lambda tm: (lambda i: 1 + i // (l_dec // tm))

    xp, xs = x_prompt, x_sample
    new_ckv, new_krope, new_s5, new_rwkv = [], [], [], []
    for l in range(depth):
        lw = _layer_weights(l, p)
        s5w = _s5_prep(s5_a_re[l], s5_a_im[l], s5_log_dt[l], s5_b_re[l], s5_b_im[l], s5_c_re[l], s5_c_im[l])
        last = l == depth - 1
        xp, (ckv_l, krope_l, s5_l, rwkv_l) = _trunk_layer(
            xp, mod, l, lw, s5w, bd, ctx_rows, zero_s5, zero_rwkv, None, None, nf, last)
        new_ckv.append(ckv_l)
        new_krope.append(krope_l)
        new_s5.append(s5_l)
        new_rwkv.append(rwkv_l)
        xs, _ = _trunk_layer(
            xs, mod, l, lw, s5w, bd, dec_rows, state_s5[:, l], state_rwkv[:, l], (cache_mla_ckv, kr_pad), tables, nf, last)
    return (xp, xs, jnp.stack(new_ckv, axis=1), jnp.stack(new_krope, axis=1),
            jnp.stack(new_s5, axis=1), jnp.stack(new_rwkv, axis=1))
```

```python
import functools
import math

import jax
import jax.numpy as jnp
from jax import lax
from jax.experimental import pallas as pl
from jax.experimental.pallas import tpu as pltpu

F32 = jnp.float32
BF16 = jnp.bfloat16

D_MODEL = 2048
N_MOD = 6
GRID_W = 64
S5_WIDTH = 512
S5_CH = 16
S5_GROUPS = 32
S5_STATE = 64
S5_CHUNK = 8
S5_TILE_GROUPS = 8
RWKV_WIDTH = 512
RWKV_HEAD_DIM = 64
RWKV_HEADS = 8
LORA_PAD = 128
RWKV_COLS = 3 * RWKV_WIDTH + 3 * LORA_PAD
MLA_HEADS = 8
MLA_V_DIM = 128
MLA_NOPE_DIM = 128
MLA_ROPE_DIM = 64
MLA_QK_PAD = 256
MLA_Q_RANK = 512
MLA_KV_RANK = 256
MLA_WIDTH = 1024
MLA_COLS = MLA_Q_RANK + MLA_KV_RANK + 2 * 128
D_FF = 8192
MLP_ROW_TILE = 1024
MLP_FF_TILE = 512
ATTN_Q_TILE = 256
ROPE_THETA = 10000.0
NORM_EPS = 1e-6
GN_EPS = 64e-5

VMEM_LIMIT_BYTES = 56 * 1024 * 1024


def _cp(*sem):
    return pltpu.CompilerParams(dimension_semantics=sem, vmem_limit_bytes=VMEM_LIMIT_BYTES)


def _dot(a, b):
    return jnp.dot(a, b, preferred_element_type=F32)


def _rms(x, g):
    ms = jnp.mean(x * x, axis=-1, keepdims=True)
    return x * lax.rsqrt(ms + NORM_EPS) * g


def _split_bf16(x):
    hi = x.astype(BF16)
    lo = (x - hi.astype(F32)).astype(BF16)
    return hi, lo


def _segsum(x, bd):
    hi, lo = _split_bf16(x)
    left = _dot(hi[:, :256], bd) + _dot(lo[:, :256], bd)
    right = _dot(hi[:, 256:], bd) + _dot(lo[:, 256:], bd)
    return jnp.concatenate([left, right], axis=1)


def _segsum_bf16(xb, bd):
    return jnp.concatenate([_dot(xb[:, :256], bd), _dot(xb[:, 256:], bd)], axis=1)


def _mod_kernel(c_ref, w_ref, b_ref, o_ref):
    c = c_ref[...]
    s = (c * jax.nn.sigmoid(c)).astype(BF16)
    o_ref[...] = _dot(s, w_ref[...].astype(BF16)) + b_ref[...]


def _modulation(cond, w_ada, b_ada):
    depth, d, n = w_ada.shape
    rows = cond.shape[0]
    tn = 1024
    return pl.pallas_call(
        _mod_kernel,
        grid=(depth, n // tn),
        in_specs=[
            pl.BlockSpec((rows, d), lambda l, j: (0, 0)),
            pl.BlockSpec((None, d, tn), lambda l, j: (l, 0, j)),
            pl.BlockSpec((None, 1, tn), lambda l, j: (l, 0, j)),
        ],
        out_specs=pl.BlockSpec((None, rows, tn), lambda l, j: (l, 0, j)),
        out_shape=jax.ShapeDtypeStruct((depth, rows, n), F32),
        compiler_params=_cp("parallel", "arbitrary"),
        name="adaln_modulation",
    )(cond, w_ada, b_ada.reshape(depth, 1, n))


def _mod_spec(layer, row_of_tile):
    return pl.BlockSpec((None, None, N_MOD, D_MODEL), lambda i, *_: (layer, row_of_tile(i), 0, 0))


def _inproj_kernel(x_ref, nw_ref, mod_ref, ws_ref, wr_ref, wm_ref, os_ref, or_ref, om_ref):
    h = _rms(x_ref[...], nw_ref[...]) * (1.0 + mod_ref[1:2, :]) + mod_ref[0:1, :]
    hb = h.astype(BF16)
    os_ref[...] = _dot(hb, ws_ref[...])
    or_ref[...] = _dot(hb, wr_ref[...])
    om_ref[...] = _dot(hb, wm_ref[...])


def _inproj(x2, nw, mod, layer, row_of_tile, tm, ws, wr, wm):
    n = x2.shape[0]
    full = lambda a: pl.BlockSpec(a.shape, lambda i: (0,) * a.ndim)
    return pl.pallas_call(
        _inproj_kernel,
        grid=(n // tm,),
        in_specs=[
            pl.BlockSpec((tm, D_MODEL), lambda i: (i, 0)),
            full(nw),
            _mod_spec(layer, row_of_tile),
            full(ws), full(wr), full(wm),
        ],
        out_specs=[
            pl.BlockSpec((tm, S5_WIDTH), lambda i: (i, 0)),
            pl.BlockSpec((tm, RWKV_COLS), lambda i: (i, 0)),
            pl.BlockSpec((tm, MLA_COLS), lambda i: (i, 0)),
        ],
        out_shape=[
            jax.ShapeDtypeStruct((n, S5_WIDTH), F32),
            jax.ShapeDtypeStruct((n, RWKV_COLS), F32),
            jax.ShapeDtypeStruct((n, MLA_COLS), F32),
        ],
        compiler_params=_cp("parallel"),
        name="in_projection",
    )(x2, nw, mod, ws, wr, wm)


def _s5_prep_kernel(are_ref, aim_ref, ldt_ref, bre_ref, bim_ref, cre_ref, cim_ref,
                    k_ref, pin_ref, poutt_ref, lam_ref):
    T = S5_CHUNK
    for d in range(2):
        are = are_ref[d:d + 1, :]
        aim = aim_ref[d:d + 1, :]
        dt = jnp.exp(ldt_ref[d:d + 1, :])
        lre = jnp.exp(are * dt) * jnp.cos(aim * dt)
        lim = jnp.exp(are * dt) * jnp.sin(aim * dt)
        den = are * are + aim * aim
        xr = lre - 1.0
        zre = (xr * are + lim * aim) / den
        zim = (lim * are - xr * aim) / den
        bre = bre_ref[d]
        bim = bim_ref[d]
        bbre = zre * bre - zim * bim
        bbim = zre * bim + zim * bre
        cre = cre_ref[d]
        cim = cim_ref[d]

        def powers(tau):
            mag = jnp.exp(tau * (are * dt))
            ang = tau * (aim * dt)
            return mag * jnp.cos(ang), mag * jnp.sin(ang)

        tau0 = lax.broadcasted_iota(jnp.int32, (T, 1), 0).astype(F32)
        ere, eim = powers(tau0)
        xre = (ere[:, None, :] * cre[None] - eim[:, None, :] * cim[None]).reshape(T * S5_CH, S5_STATE)
        xim = (ere[:, None, :] * cim[None] + eim[:, None, :] * cre[None]).reshape(T * S5_CH, S5_STATE)
        nt = (((1,), (1,)), ((), ()))
        k_ref[d] = (lax.dot_general(xre, bbre, nt, precision=lax.Precision.HIGHEST, preferred_element_type=F32)
                    - lax.dot_general(xim, bbim, nt, precision=lax.Precision.HIGHEST, preferred_element_type=F32))
        tau_out = tau0 + 1.0 if d == 0 else float(T) - tau0
        ore, oim = powers(tau_out)
        poutt_ref[2 * d] = (ore[:, None, :] * cre[None] - oim[:, None, :] * cim[None]).reshape(T * S5_CH, S5_STATE)
        poutt_ref[2 * d + 1] = -(ore[:, None, :] * cim[None] + oim[:, None, :] * cre[None]).reshape(T * S5_CH, S5_STATE)
        tau_in = float(T - 1) - tau0 if d == 0 else tau0
        ire, iim = powers(tau_in)
        pin_ref[2 * d] = (ire[:, None, :] * bbre[None] - iim[:, None, :] * bbim[None]).reshape(T * S5_CH, S5_STATE)
        pin_ref[2 * d + 1] = (ire[:, None, :] * bbim[None] + iim[:, None, :] * bbre[None]).reshape(T * S5_CH, S5_STATE)
        tre, tim = powers(jnp.full((1, 1), float(T), F32))
        lam_ref[2 * d:2 * d + 1, :] = tre
        lam_ref[2 * d + 1:2 * d + 2, :] = tim


def _s5_prep(a_re, a_im, log_dt, b_re, b_im, c_re, c_im):
    G, P, CH, T = S5_GROUPS, S5_STATE, S5_CH, S5_CHUNK
    g_first = lambda a: jnp.swapaxes(a, 0, 1)
    are = g_first(a_re)
    aim = g_first(a_im)
    ldt = g_first(log_dt)[..., None]
    bre = jnp.swapaxes(g_first(b_re), -1, -2)
    bim = jnp.swapaxes(g_first(b_im), -1, -2)
    cre = g_first(c_re)
    cim = g_first(c_im)
    spec3 = lambda s: pl.BlockSpec((None,) + s, lambda g: (g,) + (0,) * len(s))
    k, pin, poutt, lam = pl.pallas_call(
        _s5_prep_kernel,
        grid=(G,),
        in_specs=[spec3((2, P)), spec3((2, P)), spec3((2, 1)),
                  spec3((2, CH, P)), spec3((2, CH, P)), spec3((2, CH, P)), spec3((2, CH, P))],
        out_specs=[spec3((2, T * CH, CH)), spec3((4, T * CH, P)), spec3((4, T * CH, P)), spec3((4, P))],
        out_shape=[jax.ShapeDtypeStruct((G, 2, T * CH, CH), F32),
                   jax.ShapeDtypeStruct((G, 4, T * CH, P), F32),
                   jax.ShapeDtypeStruct((G, 4, T * CH, P), F32),
                   jax.ShapeDtypeStruct((G, 4, P), F32)],
        compiler_params=_cp("parallel"),
        name="s5_weight_prep",
    )(are, aim, ldt, bre, bim, cre, cim)
    k = k.reshape(G, 2, T, CH, CH)
    s_idx = jnp.arange(T)[:, None]
    t_idx = jnp.arange(T)[None, :]
    kf = jnp.where((t_idx >= s_idx)[None, :, :, None, None], k[:, 0][:, jnp.clip(t_idx - s_idx, 0, T - 1)], 0.0)
    kb = jnp.where((t_idx <= s_idx)[None, :, :, None, None], k[:, 1][:, jnp.clip(s_idx - t_idx, 0, T - 1)], 0.0)
    m = (kf + kb).transpose(0, 1, 4, 2, 3)
    GT, G8 = G // S5_TILE_GROUPS, S5_TILE_GROUPS
    m_c = m.reshape(GT, G8, T * CH, T * CH)
    pin_c = pin.transpose(0, 2, 1, 3).reshape(GT, G8, T * CH, 4 * P)
    pout_c = poutt.transpose(0, 1, 3, 2).reshape(GT, G8, 4 * P, T * CH)
    lre = jnp.concatenate([lam[:, 0], lam[:, 0], lam[:, 2], lam[:, 2]], axis=-1)
    lim = jnp.concatenate([-lam[:, 1], lam[:, 1], -lam[:, 3], lam[:, 3]], axis=-1)
    lam_rows = jnp.stack([lre.reshape(GT, G8 * 4 * P), lim.reshape(GT, G8 * 4 * P)], axis=1)
    return m_c.astype(BF16), pin_c.astype(BF16), pout_c.astype(BF16), lam_rows


def _s5_lane_spread():
    T, CH, G8 = S5_CHUNK, S5_CH, S5_TILE_GROUPS
    src = jnp.arange(T * CH)
    dst = jnp.arange(T * G8 * CH)
    same = (src[:, None] // CH == dst[None, :] // (G8 * CH)) & (src[:, None] % CH == dst[None, :] % CH)
    g8_of_dst = (dst // CH) % G8
    return (same[None] & (g8_of_dst[None, None, :] == jnp.arange(G8)[:, None, None])).astype(BF16)


def _s5_chunk_rows(u_ref, bsz, cblk):
    T = S5_CHUNK
    per_b = [jnp.concatenate([u_ref[b, pl.ds(s, cblk, stride=T), :] for s in range(T)], axis=1) for b in range(bsz)]
    return jnp.concatenate(per_b, axis=0).astype(BF16)


def _s5_summary_kernel(u_ref, pin_ref, g_ref, pin_scr, *, bsz, cblk):
    T, CH, G8 = S5_CHUNK, S5_CH, S5_TILE_GROUPS
    SWG = pin_ref.shape[-1]

    @pl.when(pl.program_id(1) == 0)
    def _():
        pin_scr[...] = jnp.zeros_like(pin_scr)
        for g8 in range(G8):
            for s in range(T):
                pin_scr[pl.ds(s * G8 * CH + g8 * CH, CH), pl.ds(g8 * SWG, SWG)] = pin_ref[g8, pl.ds(s * CH, CH), :]

    g = _dot(_s5_chunk_rows(u_ref, bsz, cblk), pin_scr[...])
    for k in range(g_ref.shape[0]):
        for b in range(bsz):
            g_ref[k, pl.ds(b, cblk, stride=bsz), :] = g[b * cblk:(b + 1) * cblk, k * 128:(k + 1) * 128]


def _s5_state_kernel(g_ref, lam_ref, h0_ref, hin_ref, hfin_ref, *, n_chunks, bsz):
    lre = lam_ref[0:1, :]
    lim = lam_ref[1:2, :]
    tiles = [slice(k * 128, (k + 1) * 128) for k in range(4)]

    def body(c, carry):
        hs, hx = carry
        rows = (pl.ds(pl.multiple_of(c * bsz, bsz), bsz), pl.ds(pl.multiple_of((n_chunks - 1 - c) * bsz, bsz), bsz))
        out, outx = [], []
        for k, sl in enumerate(tiles):
            r = rows[k % 2]
            hin_ref[k, r, :] = hs[k]
            g = g_ref[k, r, :]
            out.append(lre[:, sl] * hs[k] + lim[:, sl] * hx[k] + g)
            outx.append(lre[:, sl] * hx[k] - lim[:, sl] * hs[k] + pltpu.roll(g, 64, 1))
        return tuple(out), tuple(outx)

    h0 = tuple(h0_ref[:, sl] for sl in tiles)
    hs, _ = lax.fori_loop(0, n_chunks, body, (h0, tuple(pltpu.roll(h, 64, 1) for h in h0)), unroll=8)
    for k, sl in enumerate(tiles):
        hfin_ref[:, sl] = hs[k]


def _s5_output_kernel(u_ref, hin_ref, m_ref, pout_ref, spread_ref, y_ref, w_scr, pout_scr, *, bsz, cblk):
    T, CH, G8 = S5_CHUNK, S5_CH, S5_TILE_GROUPS
    SWG = pout_ref.shape[-2]

    @pl.when(pl.program_id(1) == 0)
    def _():
        for g8 in range(G8):
            spread = spread_ref[g8]
            wide = _dot(m_ref[g8], spread).astype(BF16)
            for s in range(T):
                w_scr[pl.ds(s * G8 * CH + g8 * CH, CH), :] = wide[s * CH:(s + 1) * CH, :]
            pout_scr[pl.ds(g8 * SWG, SWG), :] = _dot(pout_ref[g8], spread).astype(BF16)

    x = _s5_chunk_rows(u_ref, bsz, cblk)
    hin = jnp.concatenate(
        [jnp.concatenate([hin_ref[k, pl.ds(b, cblk, stride=bsz), :] for k in range(hin_ref.shape[0])], axis=1)
         for b in range(bsz)], axis=0)
    hi, lo = _split_bf16(hin)
    pout = pout_scr[...]
    y = _dot(x, w_scr[...]) + _dot(hi, pout) + _dot(lo, pout)
    for b in range(bsz):
        for s in range(T):
            y_ref[b, pl.ds(s, cblk, stride=T), :] = y[b * cblk:(b + 1) * cblk, s * 128:(s + 1) * 128]


def _s5_scan(u, h0, m_c, pin_c, pout_c, lam_rows, spread):
    bsz, seq, _ = u.shape
    G, P, T = S5_GROUPS, S5_STATE, S5_CHUNK
    GT, G8 = G // S5_TILE_GROUPS, S5_TILE_GROUPS
    SW = G8 * 4 * P
    nc = seq // T
    cblk = min(max(256 // bsz, 8), nc)
    nblk = nc // cblk
    h0g = h0.transpose(2, 0, 1, 4, 3).reshape(GT, G8, bsz, 4 * P).transpose(0, 2, 1, 3).reshape(GT, bsz, SW)
    u_spec = pl.BlockSpec((bsz, cblk * T, 128), lambda x, j: (0, j, x))
    n_tiles = SW // 128
    rows_spec = pl.BlockSpec((None, n_tiles, cblk * bsz, 128), lambda x, j: (x, 0, j, 0))
    per_tile = lambda a: pl.BlockSpec((None,) + a.shape[1:], lambda x, j: (x, 0, 0, 0))
    xw = T * 128
    g = pl.pallas_call(
        functools.partial(_s5_summary_kernel, bsz=bsz, cblk=cblk),
        grid=(GT, nblk),
        in_specs=[u_spec, per_tile(pin_c)],
        scratch_shapes=[pltpu.VMEM((xw, SW), BF16)],
        out_specs=rows_spec,
        out_shape=jax.ShapeDtypeStruct((GT, n_tiles, nc * bsz, 128), F32),
        compiler_params=_cp("parallel", "arbitrary"),
        name="s5_chunk_summary",
    )(u, pin_c)
    quarter = lambda r: pl.BlockSpec((None, r, 512), lambda x, q: (x, 0, q))
    quarter_rows = pl.BlockSpec((None, 4, nc * bsz, 128), lambda x, q: (x, q, 0, 0))
    hin, hfin = pl.pallas_call(
        functools.partial(_s5_state_kernel, n_chunks=nc, bsz=bsz),
        grid=(GT, n_tiles // 4),
        in_specs=[quarter_rows, quarter(2), quarter(bsz)],
        out_specs=[quarter_rows, quarter(bsz)],
        out_shape=[jax.ShapeDtypeStruct((GT, n_tiles, nc * bsz, 128), F32), jax.ShapeDtypeStruct((GT, bsz, SW), F32)],
        compiler_params=_cp("parallel", "parallel"),
        name="s5_state_scan",
    )(g, lam_rows, h0g)
    y = pl.pallas_call(
        functools.partial(_s5_output_kernel, bsz=bsz, cblk=cblk),
        grid=(GT, nblk),
        in_specs=[u_spec, rows_spec, per_tile(m_c), per_tile(pout_c),
                  pl.BlockSpec(spread.shape, lambda x, j: (0, 0, 0))],
        scratch_shapes=[pltpu.VMEM((xw, xw), BF16), pltpu.VMEM((SW, xw), BF16)],
        out_specs=u_spec,
        out_shape=jax.ShapeDtypeStruct((bsz, seq, S5_WIDTH), F32),
        compiler_params=_cp("parallel", "arbitrary"),
        name="s5_chunk_output",
    )(u, hin, m_c, pout_c, spread)
    hfin = hfin.reshape(GT, bsz, G8, 2, 2, P).transpose(1, 3, 0, 2, 5, 4).reshape(bsz, 2, G, P, 2)
    return y, hfin


def _s5_out_kernel(y_ref, u_ref, d_ref, w_ref, nw_ref, o_ref):
    y = y_ref[...] + u_ref[...] * d_ref[...]
    c = math.sqrt(2.0 / math.pi)
    y = y * (0.5 * (1.0 + jnp.tanh(c * (y + 0.044715 * (y * y * y)))))
    z = _dot(y.astype(BF16), w_ref[...])
    o = z[:, :S5_WIDTH] * jax.nn.sigmoid(z[:, S5_WIDTH:])
    o_ref[...] = _rms(o, nw_ref[...]).astype(BF16)


def _s5_out(y2, u2, d_skip, w_glu, nw, tm):
    n = y2.shape[0]
    full = lambda a: pl.BlockSpec(a.shape, lambda i: (0,) * a.ndim)
    row = pl.BlockSpec((tm, S5_WIDTH), lambda i: (i, 0))
    return pl.pallas_call(
        _s5_out_kernel,
        grid=(n // tm,),
        in_specs=[row, row, full(d_skip), full(w_glu), full(nw)],
        out_specs=row,
        out_shape=jax.ShapeDtypeStruct((n, S5_WIDTH), BF16),
        compiler_params=_cp("parallel"),
        name="s5_gelu_glu",
    )(y2, u2, d_skip, w_glu, nw)


def _rwkv_pre_kernel(p_ref, hp_ref, hn_ref, mu_ref, w0_ref, a0_ref, w2_ref, a2_ref, g2_ref,
                     kk_ref, ka_ref, rk_ref, bd_ref,
                     v_ref, nk_ref, dec_ref, kd_ref, bb_ref, qq_ref, vkr_ref, g_ref, bonus_ref, *, tiles_per_seq, tm):
    i = pl.program_id(0)
    j = i % tiles_per_seq
    p = p_ref[...]
    rows = lax.broadcasted_iota(jnp.int32, (tm, 1), 0)
    prev_edge = jnp.where(j == 0, 0.0, hp_ref[7:8, :])
    next_edge = jnp.where(j == tiles_per_seq - 1, 0.0, hn_ref[0:1, :])
    prev = jnp.where(rows == 0, prev_edge, pltpu.roll(p, 1, 0))
    nxt = jnp.where(rows == tm - 1, next_edge, pltpu.roll(p, tm - 1, 0))
    p = p + mu_ref[...] * (0.5 * (prev + nxt) - p)
    W = RWKV_WIDTH
    r = p[:, 0:W]
    k = p[:, W:2 * W]
    v = p[:, 2 * W:3 * W]
    wl = p[:, 3 * W:3 * W + LORA_PAD]
    al = p[:, 3 * W + LORA_PAD:3 * W + 2 * LORA_PAD]
    gl = p[:, 3 * W + 2 * LORA_PAD:3 * W + 3 * LORA_PAD]
    bd = bd_ref[...]
    kk = k * kk_ref[...]
    kk = kk * lax.rsqrt(_segsum(kk * kk, bd) + 1e-12)
    g_ref[...] = _dot(jax.nn.sigmoid(gl).astype(BF16), g2_ref[...])
    tw = jnp.tanh(wl).astype(BF16)
    alb = al.astype(BF16)
    ksum = None
    for d in range(2):
        z = -(w0_ref[d:d + 1, :] + _dot(tw, w2_ref[d]))
        w = -(jnp.maximum(z, 0.0) + jnp.log(1.0 + jnp.exp(-jnp.abs(z)))) - 0.5
        dec = jnp.exp(-jnp.exp(w))
        dec_ref[d] = dec
        a = jax.nn.sigmoid(a0_ref[d:d + 1, :] + _dot(alb, a2_ref[d]))
        kd = k * (1.0 + (a - 1.0) * ka_ref[...])
        kd_ref[d] = kd
        bb = kk * a
        bb_ref[d] = bb
        qq_ref[d] = dec * r - kk * _segsum(bb * r, bd)
        vkr_ref[d] = v * _segsum(kd * r, bd)
        ksum = kd if ksum is None else ksum + kd
    v_ref[...] = v
    nk_ref[...] = -kk
    bonus_ref[...] = _segsum(r * ksum * rk_ref[...], bd) * v


def _rwkv_pre(p2, bsz, seq, wts, bd):
    n = p2.shape[0]
    tm = min(256, seq)
    tps = seq // tm
    r8 = tm // 8
    nblk8 = n // 8
    full = lambda a: pl.BlockSpec(a.shape, lambda i: (0,) * a.ndim)
    o3 = pl.BlockSpec((None, tm, RWKV_WIDTH), lambda i: (i // tps, i % tps, 0))
    o4 = pl.BlockSpec((2, None, tm, RWKV_WIDTH), lambda i: (0, i // tps, i % tps, 0))
    s3 = jax.ShapeDtypeStruct((bsz, seq, RWKV_WIDTH), F32)
    s4 = jax.ShapeDtypeStruct((2, bsz, seq, RWKV_WIDTH), F32)
    names = ("mu", "w0", "a0", "w2", "a2", "g2", "k_k", "k_a", "r_k")
    return pl.pallas_call(
        functools.partial(_rwkv_pre_kernel, tiles_per_seq=tps, tm=tm),
        grid=(n // tm,),
        in_specs=[pl.BlockSpec((tm, RWKV_COLS), lambda i: (i, 0)),
                  pl.BlockSpec((8, RWKV_COLS), lambda i: (jnp.maximum(i * r8 - 1, 0), 0)),
                  pl.BlockSpec((8, RWKV_COLS), lambda i: (jnp.minimum((i + 1) * r8, nblk8 - 1), 0))]
                 + [full(wts[k]) for k in names] + [full(bd)],
        out_specs=[o3, o3, o4, o4, o4, o4, o4, o3, o3],
        out_shape=[s3, s3, s4, s4, s4, s4, s4, s3, s3],
        compiler_params=_cp("parallel"),
        name="rwkv_prepare",
    )(p2, p2, p2, *[wts[k] for k in names], bd)


RWKV_COL_STEPS = 16


def _rwkv_scan_kernel(nkf, nkb, vf, vb, decf, decb, kdf, kdb, bbf, bbb, qqf, qqb, vkf, vkb, s0_ref,
                      yf_ref, yb_ref, sfin_ref, s_scr, v8_scr, v8k_scr, y8_scr, vc_scr, *, nb, tlen, n_chunks):
    c = pl.program_id(1)

    @pl.when(c == 0)
    def _():
        s_scr[...] = s0_ref[...].reshape(s_scr.shape)

    N = RWKV_HEAD_DIM
    W = RWKV_WIDTH
    H = RWKV_HEADS
    nt = (((1,), (1,)), ((), ()))
    chains = [(d, n) for d in range(2) for n in range(nb)]
    pick = lambda d, f, b: f if d == 0 else b
    base = lambda m: m * tlen * H

    left = lambda rows: lax.broadcasted_iota(jnp.int32, (rows, 128), 1) < N
    for m, (d, n) in enumerate(chains):
        for p in range(H // 2):
            tile = pick(d, vf, vb)[n, :, p * 128:(p + 1) * 128]
            v8_scr[pl.ds(base(m) + 2 * p, tlen, stride=H), :] = tile
            v8_scr[pl.ds(base(m) + 2 * p + 1, tlen, stride=H), :] = pltpu.roll(tile, N, 1)
            tile = pick(d, vkf, vkb)[n, :, p * 128:(p + 1) * 128]
            swapped = pltpu.roll(tile, N, 1)
            v8k_scr[pl.ds(base(m) + 2 * p, tlen, stride=H), :] = jnp.where(left(tlen), tile, swapped)
            v8k_scr[pl.ds(base(m) + 2 * p + 1, tlen, stride=H), :] = jnp.where(left(tlen), swapped, tile)

    n_col_tiles = tlen // RWKV_COL_STEPS
    eye = (lax.broadcasted_iota(jnp.int32, (N, N), 1) == lax.broadcasted_iota(jnp.int32, (N, N), 0)).astype(BF16)
    for m in range(len(chains)):
        for j in range(n_col_tiles):
            rows = v8_scr[pl.ds(base(m) + j * 128, 128), :][:, 0:N].astype(BF16)
            vc_scr[m * n_col_tiles + j] = lax.dot_general(eye, rows, nt, preferred_element_type=F32)

    hmask = (lax.broadcasted_iota(jnp.int32, (H, W), 1) // N == lax.broadcasted_iota(jnp.int32, (H, W), 0)).astype(F32)
    sel = (lax.broadcasted_iota(jnp.int32, (H, 2 * H), 1) == lax.broadcasted_iota(jnp.int32, (H, 2 * H), 0) + H).astype(BF16)
    v_cols = lax.broadcasted_iota(jnp.int32, (N, 2 * H), 1) >= H

    def step(t, carry):
        tts = (t, tlen - 1 - t)
        row = lambda ref, n, tt: ref[n, pl.ds(tt, 1), :] * hmask
        tile8 = lambda m, tt: pl.ds(pl.multiple_of(base(m) + tt * H, H), H)
        os_ = []
        for m, (d, n) in enumerate(chains):
            tt = tts[d]
            rhs = jnp.concatenate([row(pick(d, nkf, nkb), n, tt), row(pick(d, qqf, qqb), n, tt)], axis=0).astype(BF16)
            os_.append(lax.dot_general(s_scr[m].astype(BF16), rhs, nt, preferred_element_type=F32))
        for m, (d, n) in enumerate(chains):
            tt = tts[d]
            vt = pltpu.roll(vc_scr[m * n_col_tiles + tt // RWKV_COL_STEPS], (128 + H - (tt % RWKV_COL_STEPS) * H) % 128, 1)
            sa_v = jnp.where(v_cols, vt[:, 0:2 * H], os_[m]).astype(BF16)
            w2 = jnp.concatenate([row(pick(d, bbf, bbb), n, tt), row(pick(d, kdf, kdb), n, tt)], axis=0).astype(BF16)
            s_scr[m] = s_scr[m] * pick(d, decf, decb)[n, pl.ds(tt, 1), :] + _dot(sa_v, w2)
        for m, (d, n) in enumerate(chains):
            ob = os_[m].astype(BF16)
            ob2 = jnp.concatenate([ob, ob], axis=0)
            y8_scr[tile8(m, tts[d]), :] = (lax.dot_general(sel, ob2, nt, preferred_element_type=F32)
                                           + v8k_scr[tile8(m, tts[d]), :])
        return carry

    lax.fori_loop(0, tlen, step, 0, unroll=2)

    for m, (d, n) in enumerate(chains):
        y_ref = pick(d, yf_ref, yb_ref)
        for p in range(H // 2):
            even = y8_scr[pl.ds(base(m) + 2 * p, tlen, stride=H), :]
            odd = y8_scr[pl.ds(base(m) + 2 * p + 1, tlen, stride=H), :]
            y_ref[n, :, p * 128:(p + 1) * 128] = jnp.where(left(tlen), even, odd)

    @pl.when(c == n_chunks - 1)
    def _():
        sfin_ref[...] = s_scr[...].reshape(sfin_ref.shape)


def _rwkv_scan(nk, v, dec, kd, bb, qq, vkr, s0):
    bsz, seq, W = nk.shape
    N, H = RWKV_HEAD_DIM, RWKV_HEADS
    nb = 8 if bsz % 8 == 0 else 4
    tlen = min(32, seq)
    nc = seq // tlen
    fwd = pl.BlockSpec((nb, tlen, W), lambda b, c: (b, c, 0))
    bwd = pl.BlockSpec((nb, tlen, W), lambda b, c: (b, nc - 1 - c, 0))
    fwd_d = pl.BlockSpec((None, nb, tlen, W), lambda b, c: (0, b, c, 0))
    bwd_d = pl.BlockSpec((None, nb, tlen, W), lambda b, c: (1, b, nc - 1 - c, 0))
    st = pl.BlockSpec((2, nb, N, W), lambda b, c: (0, b, 0, 0))
    tiles = pltpu.VMEM((2 * nb * tlen * H, 128), F32)
    return pl.pallas_call(
        functools.partial(_rwkv_scan_kernel, nb=nb, tlen=tlen, n_chunks=nc),
        grid=(bsz // nb, nc),
        in_specs=[fwd, bwd, fwd, bwd] + [fwd_d, bwd_d] * 5 + [st],
        out_specs=[fwd, bwd, st],
        out_shape=[jax.ShapeDtypeStruct((bsz, seq, W), F32)] * 2 + [jax.ShapeDtypeStruct((2, bsz, N, W), F32)],
        scratch_shapes=[pltpu.VMEM((2 * nb, N, W), F32), tiles, tiles, tiles,
                        pltpu.VMEM((2 * nb * (tlen // RWKV_COL_STEPS), N, 128), F32)],
        compiler_params=_cp("parallel", "arbitrary"),
        name="rwkv_scan",
    )(nk, nk, v, v, dec, dec, kd, kd, bb, bb, qq, qq, vkr, vkr, s0)


def _rwkv_post_kernel(yf_ref, yb_ref, bonus_ref, g_ref, lw_ref, lb_ref, bd_ref, o_ref):
    bd = bd_ref[...]
    y = yf_ref[...] + yb_ref[...]
    inv_n = 1.0 / RWKV_HEAD_DIM
    mean = _segsum(y, bd) * inv_n
    yc = y - mean
    var = _segsum(yc * yc, bd) * inv_n
    yn = yc * lax.rsqrt(var + GN_EPS) * lw_ref[...] + lb_ref[...]
    o_ref[...] = ((yn + bonus_ref[...]) * g_ref[...]).astype(BF16)


def _rwkv_post(yf, yb, bonus, g, ln_w, ln_b, bd):
    bsz, seq, W = yf.shape
    tm = min(256, seq)
    tps = seq // tm
    n = bsz * seq
    full = lambda a: pl.BlockSpec(a.shape, lambda i: (0,) * a.ndim)
    i3 = pl.BlockSpec((None, tm, W), lambda i: (i // tps, i % tps, 0))
    return pl.pallas_call(
        _rwkv_post_kernel,
        grid=(n // tm,),
        in_specs=[i3, i3, i3, i3, full(ln_w), full(ln_b), full(bd)],
        out_specs=pl.BlockSpec((tm, W), lambda i: (i, 0)),
        out_shape=jax.ShapeDtypeStruct((n, W), BF16),
        compiler_params=_cp("parallel"),
        name="rwkv_groupnorm_gate",
    )(yf, yb, bonus, g, ln_w, ln_b, bd)


def _store_k_heads(k_ref, kn, kpe):
    for h in range(MLA_HEADS):
        k_ref[:, h * MLA_QK_PAD:h * MLA_QK_PAD + 128] = kn[:, h * 128:(h + 1) * 128].astype(BF16)
        k_ref[:, h * MLA_QK_PAD + 128:(h + 1) * MLA_QK_PAD] = kpe


def _mla_prep_kernel(*refs, rope):
    if rope:
        (p_ref, qn_ref, kvn_ref, wq_ref, wqr_ref, wk_ref, wv_ref, cq_ref, sq_ref, ck_ref, sk_ref,
         q_ref, k_ref, v_ref) = refs
    else:
        (p_ref, qn_ref, kvn_ref, wq_ref, wk_ref, wv_ref, q_ref, k_ref, v_ref, ckv_ref, kr_ref) = refs
    p = p_ref[...]
    qn = _rms(p[:, 0:MLA_Q_RANK], qn_ref[...]).astype(BF16)
    q = _dot(qn, wq_ref[...])
    ckv = _rms(p[:, MLA_Q_RANK:MLA_Q_RANK + MLA_KV_RANK], kvn_ref[...])
    kr = p[:, MLA_Q_RANK + MLA_KV_RANK:MLA_Q_RANK + MLA_KV_RANK + 128]
    if rope:
        cq = jnp.concatenate([cq_ref[...]] * MLA_HEADS, axis=1)
        sq = jnp.concatenate([sq_ref[...]] * MLA_HEADS, axis=1)
        q = q * cq + _dot(qn, wqr_ref[...]) * sq
        krot = p[:, MLA_Q_RANK + MLA_KV_RANK + 128:MLA_Q_RANK + MLA_KV_RANK + 256]
        kpe = kr * ck_ref[...] + krot * sk_ref[...]
    else:
        kpe = kr
        ckv_ref[...] = ckv
        kr_ref[...] = kr[:, 0:MLA_ROPE_DIM]
    q_ref[...] = q.astype(BF16)
    cb = ckv.astype(BF16)
    _store_k_heads(k_ref, _dot(cb, wk_ref[...]), kpe.astype(BF16))
    v_ref[...] = _dot(cb, wv_ref[...]).astype(BF16)


def _mla_prep(p2, seq, wts, tables):
    n = p2.shape[0]
    rope = tables is not None
    tm = min(256, seq)
    tps = seq // tm
    full = lambda a: pl.BlockSpec(a.shape, lambda i: (0,) * a.ndim)
    row = lambda w: pl.BlockSpec((tm, w), lambda i: (i, 0))
    ins = [p2, wts["q_norm"], wts["kv_norm"], wts["wq"]]
    specs = [row(MLA_COLS), full(wts["q_norm"]), full(wts["kv_norm"]), full(wts["wq"])]
    if rope:
        ins.append(wts["wq_rot"])
        specs.append(full(wts["wq_rot"]))
    ins += [wts["wk"], wts["wv"]]
    specs += [full(wts["wk"]), full(wts["wv"])]
    outs = [row(MLA_HEADS * MLA_QK_PAD), row(MLA_HEADS * MLA_QK_PAD), row(MLA_HEADS * MLA_V_DIM)]
    shapes = [jax.ShapeDtypeStruct((n, MLA_HEADS * MLA_QK_PAD), BF16),
              jax.ShapeDtypeStruct((n, MLA_HEADS * MLA_QK_PAD), BF16),
              jax.ShapeDtypeStruct((n, MLA_HEADS * MLA_V_DIM), BF16)]
    if rope:
        ins += list(tables)
        specs += [pl.BlockSpec((tm, t.shape[1]), lambda i: (i % tps, 0)) for t in tables]
    else:
        outs += [row(MLA_KV_RANK), row(MLA_ROPE_DIM)]
        shapes += [jax.ShapeDtypeStruct((n, MLA_KV_RANK), F32), jax.ShapeDtypeStruct((n, MLA_ROPE_DIM), F32)]
    return pl.pallas_call(
        functools.partial(_mla_prep_kernel, rope=rope),
        grid=(n // tm,),
        in_specs=specs, out_specs=outs, out_shape=shapes,
        compiler_params=_cp("parallel"),
        name="mla_prepare_rope" if rope else "mla_prepare",
    )(*ins)


def _mla_cache_kernel(ckv_ref, kr_ref, wk_ref, wv_ref, k_ref, v_ref):
    cb = ckv_ref[...].astype(BF16)
    _store_k_heads(k_ref, _dot(cb, wk_ref[...]), kr_ref[...].astype(BF16))
    v_ref[...] = _dot(cb, wv_ref[...]).astype(BF16)


def _mla_cache(cache_ckv, cache_kr_pad, layer, wk, wv):
    bsz, _, past, _ = cache_ckv.shape
    full = lambda a: pl.BlockSpec(a.shape, lambda b: (0,) * a.ndim)
    return pl.pallas_call(
        _mla_cache_kernel,
        grid=(bsz,),
        in_specs=[pl.BlockSpec((None, None, past, MLA_KV_RANK), lambda b: (b, layer, 0, 0)),
                  pl.BlockSpec((None, None, past, 128), lambda b: (b, layer, 0, 0)),
                  full(wk), full(wv)],
        out_specs=[pl.BlockSpec((None, past, MLA_HEADS * MLA_QK_PAD), lambda b: (b, 0, 0)),
                   pl.BlockSpec((None, past, MLA_HEADS * MLA_V_DIM), lambda b: (b, 0, 0))],
        out_shape=[jax.ShapeDtypeStruct((bsz, past, MLA_HEADS * MLA_QK_PAD), BF16),
                   jax.ShapeDtypeStruct((bsz, past, MLA_HEADS * MLA_V_DIM), BF16)],
        compiler_params=_cp("parallel"),
        name="mla_cache_keys",
    )(cache_ckv, cache_kr_pad, wk, wv)


def _attn_kernel(*refs, cache, scale):
    if cache:
        q_ref, k_ref, v_ref, kc_ref, vc_ref, o_ref = refs
    else:
        q_ref, k_ref, v_ref, o_ref = refs
    nt = (((1,), (1,)), ((), ()))
    q = q_ref[...]
    s = lax.dot_general(q, k_ref[...], nt, preferred_element_type=F32) * scale
    m = jnp.max(s, axis=-1, keepdims=True)
    if cache:
        sc = lax.dot_general(q, kc_ref[...], nt, preferred_element_type=F32) * scale
        m = jnp.maximum(m, jnp.max(sc, axis=-1, keepdims=True))
    e = jnp.exp(s - m)
    den = jnp.sum(e, axis=-1, keepdims=True)
    o = _dot(e.astype(BF16), v_ref[...])
    if cache:
        ec = jnp.exp(sc - m)
        den = den + jnp.sum(ec, axis=-1, keepdims=True)
        o = o + _dot(ec.astype(BF16), vc_ref[...])
    o_ref[...] = o / den


def _attention(q, k, v, kc=None, vc=None):
    bsz, seq, _ = q.shape
    tq = min(ATTN_Q_TILE, seq)
    cache = kc is not None
    scale = float(MLA_NOPE_DIM + MLA_ROPE_DIM) ** -0.5
    ins = [q, k, v]
    specs = [pl.BlockSpec((None, tq, MLA_QK_PAD), lambda b, h, i: (b, i, h)),
             pl.BlockSpec((None, seq, MLA_QK_PAD), lambda b, h, i: (b, 0, h)),
             pl.BlockSpec((None, seq, MLA_V_DIM), lambda b, h, i: (b, 0, h))]
    if cache:
        past = kc.shape[1]
        ins += [kc, vc]
        specs += [pl.BlockSpec((None, past, MLA_QK_PAD), lambda b, h, i: (b, 0, h)),
                  pl.BlockSpec((None, past, MLA_V_DIM), lambda b, h, i: (b, 0, h))]
    return pl.pallas_call(
        functools.partial(_attn_kernel, cache=cache, scale=scale),
        grid=(bsz, MLA_HEADS, seq // tq),
        in_specs=specs,
        out_specs=pl.BlockSpec((None, tq, MLA_V_DIM), lambda b, h, i: (b, i, h)),
        out_shape=jax.ShapeDtypeStruct((bsz, seq, MLA_HEADS * MLA_V_DIM), F32),
        compiler_params=_cp("parallel", "parallel", "arbitrary"),
        name="mla_attention_cached" if cache else "mla_attention",
    )(*ins)


def _outproj_kernel(ys_ref, yr_ref, ym_ref, x_ref, mod_ref, nm_ref, w_ref, o_ref):
    ym = _rms(ym_ref[...], nm_ref[...]).astype(BF16)
    acc = _dot(ys_ref[...], w_ref[0:S5_WIDTH, :])
    acc += _dot(yr_ref[...], w_ref[S5_WIDTH:S5_WIDTH + RWKV_WIDTH, :])
    acc += _dot(ym, w_ref[S5_WIDTH + RWKV_WIDTH:, :])
    o_ref[...] = x_ref[...] + mod_ref[2:3, :] * acc


def _outproj(ys, yr, ym, x2, mod, layer, row_of_tile, tm, nm, w_out):
    n = x2.shape[0]
    full = lambda a: pl.BlockSpec(a.shape, lambda i: (0,) * a.ndim)
    row = lambda w: pl.BlockSpec((tm, w), lambda i: (i, 0))
    return pl.pallas_call(
        _outproj_kernel,
        grid=(n // tm,),
        in_specs=[row(S5_WIDTH), row(RWKV_WIDTH), row(MLA_WIDTH), row(D_MODEL), _mod_spec(layer, row_of_tile),
                  full(nm), full(w_out)],
        out_specs=row(D_MODEL),
        out_shape=jax.ShapeDtypeStruct((n, D_MODEL), F32),
        compiler_params=_cp("parallel"),
        name="out_projection",
    )(ys, yr, ym, x2, mod, nm, w_out)


def _mlp_kernel(x_ref, nw_ref, mod_ref, w1_ref, w2_ref, nf_ref, o_ref, h_scr, *, final_norm):
    j = pl.program_id(1)

    @pl.when(j == 0)
    def _():
        h = _rms(x_ref[...], nw_ref[...]) * (1.0 + mod_ref[4:5, :]) + mod_ref[3:4, :]
        h_scr[...] = h.astype(BF16)
        o_ref[...] = jnp.zeros_like(o_ref)

    a = _dot(h_scr[...], w1_ref[...])
    a = jnp.square(jnp.maximum(a, 0.0)).astype(BF16)
    o_ref[...] += _dot(a, w2_ref[...])

    @pl.when(j == pl.num_programs(1) - 1)
    def _():
        y = x_ref[...] + mod_ref[5:6, :] * o_ref[...]
        if final_norm:
            y = _rms(y, nf_ref[...])
        o_ref[...] = y


def _mlp(x2, nw, mod, layer, row_of_tile, tm, w1, w2, nf, final_norm):
    n = x2.shape[0]
    tf = MLP_FF_TILE
    full = lambda a: pl.BlockSpec(a.shape, lambda i, j: (0,) * a.ndim)
    return pl.pallas_call(
        functools.partial(_mlp_kernel, final_norm=final_norm),
        grid=(n // tm, D_FF // tf),
        in_specs=[pl.BlockSpec((tm, D_MODEL), lambda i, j: (i, 0)), full(nw), _mod_spec(layer, row_of_tile),
                  pl.BlockSpec((D_MODEL, tf), lambda i, j: (0, j)),
                  pl.BlockSpec((tf, D_MODEL), lambda i, j: (j, 0)), full(nf)],
        out_specs=pl.BlockSpec((tm, D_MODEL), lambda i, j: (i, 0)),
        out_shape=jax.ShapeDtypeStruct((n, D_MODEL), F32),
        scratch_shapes=[pltpu.VMEM((tm, D_MODEL), BF16)],
        compiler_params=_cp("parallel", "arbitrary"),
        name="mlp_final" if final_norm else "mlp",
    )(x2, nw, mod, w1, w2, nf)


def _rope_tables(length):
    rows = length // GRID_W
    row_pos = jnp.repeat(jnp.arange(rows, dtype=F32), GRID_W)
    col_pos = jnp.tile(jnp.arange(GRID_W, dtype=F32), rows)
    axis_dim = MLA_ROPE_DIM // 2
    inv_freq = 1.0 / (ROPE_THETA ** (jnp.arange(0, axis_dim, 2, dtype=F32) / axis_dim))
    ang_r = row_pos[:, None] * inv_freq[None, :]
    ang_c = col_pos[:, None] * inv_freq[None, :]
    ang = jnp.concatenate([ang_r, ang_r, ang_c, ang_c], axis=-1)
    cos, sin = jnp.cos(ang), jnp.sin(ang)
    z64 = jnp.zeros((length, 64), F32)
    cos_q = jnp.concatenate([jnp.ones((length, MLA_NOPE_DIM), F32), cos, z64], axis=1)
    sin_q = jnp.concatenate([jnp.zeros((length, MLA_NOPE_DIM), F32), sin, z64], axis=1)
    cos_k = jnp.concatenate([cos, z64], axis=1)
    sin_k = jnp.concatenate([sin, z64], axis=1)
    return cos_q, sin_q, cos_k, sin_k


def _rot_cols(w):
    a, b, c, d = w[..., 0:16], w[..., 16:32], w[..., 32:48], w[..., 48:64]
    return jnp.concatenate([-b, a, -d, c], axis=-1)


def _layer_weights(l, p):
    d = D_MODEL
    w_in = p["w_in"][l]
    z64 = jnp.zeros((d, 64), F32)
    o = S5_WIDTH
    rk = w_in[:, o:o + 3 * RWKV_WIDTH]
    o += 3 * RWKV_WIDTH
    wl, al, gl = w_in[:, o:o + 64], w_in[:, o + 64:o + 128], w_in[:, o + 128:o + 256]
    o += 256
    cq, ckv, kr = w_in[:, o:o + 512], w_in[:, o + 512:o + 768], w_in[:, o + 768:o + 832]
    mu = p["rwkv_mu"][l]
    z1 = jnp.zeros((64,), F32)
    mu_pad = jnp.concatenate([mu[:1536], mu[1536:1600], z1, mu[1600:1664], z1, mu[1664:1792]])[None]
    pad_rows = lambda w: jnp.concatenate([w, jnp.zeros_like(w)], axis=-2)
    w_uq = p["mla_w_uq"][l].reshape(MLA_Q_RANK, MLA_HEADS, MLA_NOPE_DIM + MLA_ROPE_DIM)
    zq = jnp.zeros((MLA_Q_RANK, MLA_HEADS, 64), F32)
    wq = jnp.concatenate([w_uq, zq], axis=-1).reshape(MLA_Q_RANK, MLA_HEADS * MLA_QK_PAD)
    wq_rot = jnp.concatenate([jnp.zeros((MLA_Q_RANK, MLA_HEADS, MLA_NOPE_DIM), F32),
                              _rot_cols(w_uq[..., MLA_NOPE_DIM:]), zq], axis=-1).reshape(MLA_Q_RANK, MLA_HEADS * MLA_QK_PAD)
    w_ukv = p["mla_w_ukv"][l].reshape(MLA_KV_RANK, MLA_HEADS, MLA_NOPE_DIM + MLA_V_DIM)
    row = lambda a: a.reshape(1, -1)
    return {
        "norm_mix": row(p["norm_mix"][l]), "norm_mlp": row(p["norm_mlp"][l]),
        "w_s5": w_in[:, 0:S5_WIDTH].astype(BF16),
        "w_rwkv": jnp.concatenate([rk, wl, z64, al, z64, gl], axis=1).astype(BF16),
        "w_mla": jnp.concatenate([cq, ckv, kr, z64, _rot_cols(kr), z64], axis=1).astype(BF16),
        "w_out": p["w_out"][l].astype(BF16),
        "s5_d": row(p["s5_d"][l]), "s5_w_glu": p["s5_w_glu"][l].astype(BF16), "s5_out_norm": row(p["s5_out_norm"][l]),
        "rwkv": {
            "mu": mu_pad, "w0": p["rwkv_w0"][l], "a0": p["rwkv_a0"][l],
            "w2": pad_rows(p["rwkv_w2"][l]).astype(BF16), "a2": pad_rows(p["rwkv_a2"][l]).astype(BF16),
            "g2": p["rwkv_g2"][l].astype(BF16), "k_k": row(p["rwkv_k_k"][l]), "k_a": row(p["rwkv_k_a"][l]),
            "r_k": row(p["rwkv_r_k"][l]),
        },
        "rwkv_ln_w": row(p["rwkv_ln_w"][l]), "rwkv_ln_b": row(p["rwkv_ln_b"][l]),
        "mla": {
            "q_norm": row(p["mla_q_norm"][l]), "kv_norm": row(p["mla_kv_norm"][l]),
            "wq": wq.astype(BF16), "wq_rot": wq_rot.astype(BF16),
            "wk": w_ukv[..., :MLA_NOPE_DIM].reshape(MLA_KV_RANK, -1).astype(BF16),
            "wv": w_ukv[..., MLA_NOPE_DIM:].reshape(MLA_KV_RANK, -1).astype(BF16),
        },
        "mla_out_norm": row(p["mla_out_norm"][l]),
        "mlp_w1": p["mlp_w1"][l].astype(BF16), "mlp_w2": p["mlp_w2"][l].astype(BF16),
    }


def _trunk_layer(x, mod, layer, lw, s5w, bd, row_of_tile_fn, s5_h0, rwkv_s0, cache, tables, norm_final, final_norm):
    bsz, seq, d = x.shape
    n = bsz * seq
    x2 = x.reshape(n, d)
    tm = min(512, seq) if cache is not None else min(512, n)
    row_of_tile = row_of_tile_fn(tm)
    u2, pr2, pm2 = _inproj(x2, lw["norm_mix"], mod, layer, row_of_tile, tm, lw["w_s5"], lw["w_rwkv"], lw["w_mla"])

    ys, s5_fin = _s5_scan(u2.reshape(bsz, seq, S5_WIDTH), s5_h0, *s5w)
    ys2 = _s5_out(ys.reshape(n, S5_WIDTH), u2, lw["s5_d"], lw["s5_w_glu"], lw["s5_out_norm"], min(512, n))

    v, nk, dec, kd, bb, qq, vkr, g, bonus = _rwkv_pre(pr2, bsz, seq, lw["rwkv"], bd)
    s0 = rwkv_s0.transpose(1, 0, 3, 2, 4).reshape(2, bsz, RWKV_HEAD_DIM, RWKV_WIDTH)
    yf, yb, s_fin = _rwkv_scan(nk, v, dec, kd, bb, qq, vkr, s0)
    yr2 = _rwkv_post(yf, yb, bonus, g, lw["rwkv_ln_w"], lw["rwkv_ln_b"], bd)
    rwkv_fin = s_fin.reshape(2, bsz, RWKV_HEAD_DIM, RWKV_HEADS, RWKV_HEAD_DIM).transpose(1, 0, 3, 2, 4)

    shape3 = lambda a: a.reshape(bsz, seq, a.shape[-1])
    if cache is None:
        q, k, v_, ckv_n, k_rope = _mla_prep(pm2, seq, lw["mla"], None)
        ym = _attention(shape3(q), shape3(k), shape3(v_))
        extras = (shape3(ckv_n), shape3(k_rope), s5_fin, rwkv_fin)
    else:
        q, k, v_ = _mla_prep(pm2, seq, lw["mla"], tables)
        kc, vc = _mla_cache(cache[0], cache[1], layer, lw["mla"]["wk"], lw["mla"]["wv"])
        ym = _attention(shape3(q), shape3(k), shape3(v_), kc, vc)
        extras = None

    x2 = _outproj(ys2, yr2, ym.reshape(n, MLA_WIDTH), x2, mod, layer, row_of_tile, tm, lw["mla_out_norm"], lw["w_out"])
    tm_mlp = min(MLP_ROW_TILE, seq) if cache is not None else min(MLP_ROW_TILE, n)
    x2 = _mlp(x2, lw["norm_mlp"], mod, layer, row_of_tile_fn(tm_mlp), tm_mlp, lw["mlp_w1"], lw["mlp_w2"], norm_final,
              final_norm)
    return x2.reshape(bsz, seq, d), extras


def kernel(x_prompt, x_sample, cache_mla_ckv, cache_mla_krope, state_s5, state_rwkv, c, c_ctx, norm_mix, norm_mlp, norm_final, w_ada, b_ada, w_in, w_out, s5_a_re, s5_a_im, s5_log_dt, s5_b_re, s5_b_im, s5_c_re, s5_c_im, s5_d, s5_w_glu, s5_out_norm, rwkv_mu, rwkv_w0, rwkv_w2, rwkv_a0, rwkv_a2, rwkv_g2, rwkv_k_k, rwkv_k_a, rwkv_r_k, rwkv_ln_w, rwkv_ln_b, mla_q_norm, mla_w_uq, mla_kv_norm, mla_w_ukv, mla_out_norm, mlp_w1, mlp_w2):
    p = dict(norm_mix=norm_mix, norm_mlp=norm_mlp, w_in=w_in, w_out=w_out, s5_d=s5_d, s5_w_glu=s5_w_glu,
             s5_out_norm=s5_out_norm, rwkv_mu=rwkv_mu, rwkv_w0=rwkv_w0, rwkv_w2=rwkv_w2, rwkv_a0=rwkv_a0,
             rwkv_a2=rwkv_a2, rwkv_g2=rwkv_g2, rwkv_k_k=rwkv_k_k, rwkv_k_a=rwkv_k_a, rwkv_r_k=rwkv_r_k,
             rwkv_ln_w=rwkv_ln_w, rwkv_ln_b=rwkv_ln_b, mla_q_norm=mla_q_norm, mla_w_uq=mla_w_uq,
             mla_kv_norm=mla_kv_norm, mla_w_ukv=mla_w_ukv, mla_out_norm=mla_out_norm, mlp_w1=mlp_w1, mlp_w2=mlp_w2)
    depth = w_in.shape[0]
    b_ctx, l_ctx, d = x_prompt.shape
    b_dec, l_dec, _ = x_sample.shape

    rows = -(-(1 + b_dec) // 8) * 8
    cond = jnp.zeros((rows, d), F32).at[0].set(c_ctx).at[1:1 + b_dec].set(c)
    mod = _modulation(cond, w_ada, b_ada).reshape(depth, rows, N_MOD, d)

    bd = jnp.kron(jnp.eye(4, dtype=F32), jnp.ones((64, 64), F32)).astype(BF16)
    spread = _s5_lane_spread()
    tables = _rope_tables(l_dec)
    kr_pad = jnp.pad(cache_mla_krope, ((0, 0), (0, 0), (0, 0), (0, 128 - MLA_ROPE_DIM)))
    zero_s5 = jnp.zeros((b_ctx, 2, S5_GROUPS, S5_STATE, 2), F32)
    zero_rwkv = jnp.zeros((b_ctx, 2, RWKV_HEADS, RWKV_HEAD_DIM, RWKV_HEAD_DIM), F32)
    nf = norm_final.reshape(1, d)

    ctx_rows = lambda tm: (lambda i: 0)
    dec_rows = lambda tm: (lambda i: 1 + i // (l_dec // tm))

    xp, xs = x_prompt, x_sample
    new_ckv, new_krope, new_s5, new_rwkv = [], [], [], []
    for l in range(depth):
        lw = _layer_weights(l, p)
        s5w = _s5_prep(s5_a_re[l], s5_a_im[l], s5_log_dt[l], s5_b_re[l], s5_b_im[l], s5_c_re[l], s5_c_im[l]) + (spread,)
        last = l == depth - 1
        xp, (ckv_l, krope_l, s5_l, rwkv_l) = _trunk_layer(
            xp, mod, l, lw, s5w, bd, ctx_rows, zero_s5, zero_rwkv, None, None, nf, last)
        new_ckv.append(ckv_l)
        new_krope.append(krope_l)
        new_s5.append(s5_l)
        new_rwkv.append(rwkv_l)
        xs, _ = _trunk_layer(
            xs, mod, l, lw, s5w, bd, dec_rows, state_s5[:, l], state_rwkv[:, l], (cache_mla_ckv, kr_pad), tables, nf, last)
    return (xp, xs, jnp.stack(new_ckv, axis=1), jnp.stack(new_krope, axis=1),
            jnp.stack(new_s5, axis=1), jnp.stack(new_rwkv, axis=1))
```

```python
import functools
import math

import jax
import jax.numpy as jnp
from jax import lax
from jax.experimental import pallas as pl
from jax.experimental.pallas import tpu as pltpu

F32 = jnp.float32
BF16 = jnp.bfloat16

D_MODEL = 2048
N_MOD = 6
GRID_W = 64
S5_WIDTH = 512
S5_CH = 16
S5_GROUPS = 32
S5_STATE = 64
S5_CHUNK = 8
S5_TILE_GROUPS = 8
RWKV_WIDTH = 512
RWKV_HEAD_DIM = 64
RWKV_HEADS = 8
LORA_PAD = 128
RWKV_COLS = 3 * RWKV_WIDTH + 3 * LORA_PAD
MLA_HEADS = 8
MLA_V_DIM = 128
MLA_NOPE_DIM = 128
MLA_ROPE_DIM = 64
MLA_QK_PAD = 256
MLA_Q_RANK = 512
MLA_KV_RANK = 256
MLA_WIDTH = 1024
MLA_COLS = MLA_Q_RANK + MLA_KV_RANK + 2 * 128
D_FF = 8192
MLP_ROW_TILE = 512
MLP_FF_TILE = 1024
ATTN_Q_TILE = 256
ROPE_THETA = 10000.0
NORM_EPS = 1e-6
GN_EPS = 64e-5

VMEM_LIMIT_BYTES = 56 * 1024 * 1024


def _cp(*sem):
    return pltpu.CompilerParams(dimension_semantics=sem, vmem_limit_bytes=VMEM_LIMIT_BYTES)


def _dot(a, b):
    return jnp.dot(a, b, preferred_element_type=F32)


def _rms(x, g):
    ms = jnp.mean(x * x, axis=-1, keepdims=True)
    return x * lax.rsqrt(ms + NORM_EPS) * g


def _split_bf16(x):
    hi = x.astype(BF16)
    lo = (x - hi.astype(F32)).astype(BF16)
    return hi, lo


def _segsum(x, bd):
    hi, lo = _split_bf16(x)
    left = _dot(hi[:, :256], bd) + _dot(lo[:, :256], bd)
    right = _dot(hi[:, 256:], bd) + _dot(lo[:, 256:], bd)
    return jnp.concatenate([left, right], axis=1)


def _segsum_bf16(xb, bd):
    return jnp.concatenate([_dot(xb[:, :256], bd), _dot(xb[:, 256:], bd)], axis=1)


def _mod_kernel(c_ref, w_ref, b_ref, o_ref):
    c = c_ref[...]
    s = (c * jax.nn.sigmoid(c)).astype(BF16)
    o_ref[...] = _dot(s, w_ref[...].astype(BF16)) + b_ref[...]


def _modulation(cond, w_ada, b_ada):
    depth, d, n = w_ada.shape
    rows = cond.shape[0]
    tn = 1024
    return pl.pallas_call(
        _mod_kernel,
        grid=(depth, n // tn),
        in_specs=[
            pl.BlockSpec((rows, d), lambda l, j: (0, 0)),
            pl.BlockSpec((None, d, tn), lambda l, j: (l, 0, j)),
            pl.BlockSpec((None, 1, tn), lambda l, j: (l, 0, j)),
        ],
        out_specs=pl.BlockSpec((None, rows, tn), lambda l, j: (l, 0, j)),
        out_shape=jax.ShapeDtypeStruct((depth, rows, n), F32),
        compiler_params=_cp("parallel", "arbitrary"),
        name="adaln_modulation",
    )(cond, w_ada, b_ada.reshape(depth, 1, n))


def _mod_spec(layer, row_of_tile):
    return pl.BlockSpec((None, None, N_MOD, D_MODEL), lambda i, *_: (layer, row_of_tile(i), 0, 0))


def _inproj_kernel(x_ref, nw_ref, mod_ref, ws_ref, wr_ref, wm_ref, os_ref, or_ref, om_ref):
    h = _rms(x_ref[...], nw_ref[...]) * (1.0 + mod_ref[1:2, :]) + mod_ref[0:1, :]
    hb = h.astype(BF16)
    os_ref[...] = _dot(hb, ws_ref[...])
    or_ref[...] = _dot(hb, wr_ref[...])
    om_ref[...] = _dot(hb, wm_ref[...])


def _inproj(x2, nw, mod, layer, row_of_tile, tm, ws, wr, wm):
    n = x2.shape[0]
    full = lambda a: pl.BlockSpec(a.shape, lambda i: (0,) * a.ndim)
    return pl.pallas_call(
        _inproj_kernel,
        grid=(n // tm,),
        in_specs=[
            pl.BlockSpec((tm, D_MODEL), lambda i: (i, 0)),
            full(nw),
            _mod_spec(layer, row_of_tile),
            full(ws), full(wr), full(wm),
        ],
        out_specs=[
            pl.BlockSpec((tm, S5_WIDTH), lambda i: (i, 0)),
            pl.BlockSpec((tm, RWKV_COLS), lambda i: (i, 0)),
            pl.BlockSpec((tm, MLA_COLS), lambda i: (i, 0)),
        ],
        out_shape=[
            jax.ShapeDtypeStruct((n, S5_WIDTH), F32),
            jax.ShapeDtypeStruct((n, RWKV_COLS), F32),
            jax.ShapeDtypeStruct((n, MLA_COLS), F32),
        ],
        compiler_params=_cp("parallel"),
        name="in_projection",
    )(x2, nw, mod, ws, wr, wm)


def _s5_prep_kernel(are_ref, aim_ref, ldt_ref, bre_ref, bim_ref, cre_ref, cim_ref,
                    k_ref, pin_ref, poutt_ref, lam_ref):
    T = S5_CHUNK
    for d in range(2):
        are = are_ref[d:d + 1, :]
        aim = aim_ref[d:d + 1, :]
        dt = jnp.exp(ldt_ref[d:d + 1, :])
        lre = jnp.exp(are * dt) * jnp.cos(aim * dt)
        lim = jnp.exp(are * dt) * jnp.sin(aim * dt)
        den = are * are + aim * aim
        xr = lre - 1.0
        zre = (xr * are + lim * aim) / den
        zim = (lim * are - xr * aim) / den
        bre = bre_ref[d]
        bim = bim_ref[d]
        bbre = zre * bre - zim * bim
        bbim = zre * bim + zim * bre
        cre = cre_ref[d]
        cim = cim_ref[d]

        def powers(tau):
            mag = jnp.exp(tau * (are * dt))
            ang = tau * (aim * dt)
            return mag * jnp.cos(ang), mag * jnp.sin(ang)

        tau0 = lax.broadcasted_iota(jnp.int32, (T, 1), 0).astype(F32)
        ere, eim = powers(tau0)
        xre = (ere[:, None, :] * cre[None] - eim[:, None, :] * cim[None]).reshape(T * S5_CH, S5_STATE)
        xim = (ere[:, None, :] * cim[None] + eim[:, None, :] * cre[None]).reshape(T * S5_CH, S5_STATE)
        nt = (((1,), (1,)), ((), ()))
        k_ref[d] = (lax.dot_general(xre, bbre, nt, precision=lax.Precision.HIGHEST, preferred_element_type=F32)
                    - lax.dot_general(xim, bbim, nt, precision=lax.Precision.HIGHEST, preferred_element_type=F32))
        tau_out = tau0 + 1.0 if d == 0 else float(T) - tau0
        ore, oim = powers(tau_out)
        poutt_ref[2 * d] = (ore[:, None, :] * cre[None] - oim[:, None, :] * cim[None]).reshape(T * S5_CH, S5_STATE)
        poutt_ref[2 * d + 1] = -(ore[:, None, :] * cim[None] + oim[:, None, :] * cre[None]).reshape(T * S5_CH, S5_STATE)
        tau_in = float(T - 1) - tau0 if d == 0 else tau0
        ire, iim = powers(tau_in)
        pin_ref[2 * d] = (ire[:, None, :] * bbre[None] - iim[:, None, :] * bbim[None]).reshape(T * S5_CH, S5_STATE)
        pin_ref[2 * d + 1] = (ire[:, None, :] * bbim[None] + iim[:, None, :] * bbre[None]).reshape(T * S5_CH, S5_STATE)
        tre, tim = powers(jnp.full((1, 1), float(T), F32))
        lam_ref[2 * d:2 * d + 1, :] = tre
        lam_ref[2 * d + 1:2 * d + 2, :] = tim


def _s5_prep(a_re, a_im, log_dt, b_re, b_im, c_re, c_im):
    G, P, CH, T = S5_GROUPS, S5_STATE, S5_CH, S5_CHUNK
    g_first = lambda a: jnp.swapaxes(a, 0, 1)
    are = g_first(a_re)
    aim = g_first(a_im)
    ldt = g_first(log_dt)[..., None]
    bre = jnp.swapaxes(g_first(b_re), -1, -2)
    bim = jnp.swapaxes(g_first(b_im), -1, -2)
    cre = g_first(c_re)
    cim = g_first(c_im)
    spec3 = lambda s: pl.BlockSpec((None,) + s, lambda g: (g,) + (0,) * len(s))
    k, pin, poutt, lam = pl.pallas_call(
        _s5_prep_kernel,
        grid=(G,),
        in_specs=[spec3((2, P)), spec3((2, P)), spec3((2, 1)),
                  spec3((2, CH, P)), spec3((2, CH, P)), spec3((2, CH, P)), spec3((2, CH, P))],
        out_specs=[spec3((2, T * CH, CH)), spec3((4, T * CH, P)), spec3((4, T * CH, P)), spec3((4, P))],
        out_shape=[jax.ShapeDtypeStruct((G, 2, T * CH, CH), F32),
                   jax.ShapeDtypeStruct((G, 4, T * CH, P), F32),
                   jax.ShapeDtypeStruct((G, 4, T * CH, P), F32),
                   jax.ShapeDtypeStruct((G, 4, P), F32)],
        compiler_params=_cp("parallel"),
        name="s5_weight_prep",
    )(are, aim, ldt, bre, bim, cre, cim)
    k = k.reshape(G, 2, T, CH, CH)
    s_idx = jnp.arange(T)[:, None]
    t_idx = jnp.arange(T)[None, :]
    kf = jnp.where((t_idx >= s_idx)[None, :, :, None, None], k[:, 0][:, jnp.clip(t_idx - s_idx, 0, T - 1)], 0.0)
    kb = jnp.where((t_idx <= s_idx)[None, :, :, None, None], k[:, 1][:, jnp.clip(s_idx - t_idx, 0, T - 1)], 0.0)
    m = (kf + kb).transpose(0, 1, 4, 2, 3)
    GT, G8 = G // S5_TILE_GROUPS, S5_TILE_GROUPS
    m_c = m.reshape(GT, G8, T * CH, T * CH)
    pin_c = pin.transpose(0, 2, 1, 3).reshape(GT, G8, T * CH, 4 * P)
    pout_c = poutt.transpose(0, 1, 3, 2).reshape(GT, G8, 4 * P, T * CH)
    lre = jnp.concatenate([lam[:, 0], lam[:, 0], lam[:, 2], lam[:, 2]], axis=-1)
    lim = jnp.concatenate([-lam[:, 1], lam[:, 1], -lam[:, 3], lam[:, 3]], axis=-1)
    lam_rows = jnp.stack([lre.reshape(GT, G8 * 4 * P), lim.reshape(GT, G8 * 4 * P)], axis=1)
    return m_c.astype(BF16), pin_c.astype(BF16), pout_c.astype(BF16), lam_rows


def _s5_lane_spread():
    T, CH, G8 = S5_CHUNK, S5_CH, S5_TILE_GROUPS
    src = jnp.arange(T * CH)
    dst = jnp.arange(T * G8 * CH)
    same = (src[:, None] // CH == dst[None, :] // (G8 * CH)) & (src[:, None] % CH == dst[None, :] % CH)
    g8_of_dst = (dst // CH) % G8
    return (same[None] & (g8_of_dst[None, None, :] == jnp.arange(G8)[:, None, None])).astype(BF16)


def _s5_chunk_rows(u_ref, bsz, cblk):
    T = S5_CHUNK
    per_b = [jnp.concatenate([u_ref[b, pl.ds(s, cblk, stride=T), :] for s in range(T)], axis=1) for b in range(bsz)]
    return jnp.concatenate(per_b, axis=0).astype(BF16)


def _s5_summary_kernel(u_ref, pin_ref, g_ref, pin_scr, *, bsz, cblk):
    T, CH, G8 = S5_CHUNK, S5_CH, S5_TILE_GROUPS
    SWG = pin_ref.shape[-1]

    @pl.when(pl.program_id(1) == 0)
    def _():
        pin_scr[...] = jnp.zeros_like(pin_scr)
        for g8 in range(G8):
            for s in range(T):
                pin_scr[pl.ds(s * G8 * CH + g8 * CH, CH), pl.ds(g8 * SWG, SWG)] = pin_ref[g8, pl.ds(s * CH, CH), :]

    g = _dot(_s5_chunk_rows(u_ref, bsz, cblk), pin_scr[...])
    for k in range(g_ref.shape[0]):
        for b in range(bsz):
            g_ref[k, pl.ds(b, cblk, stride=bsz), :] = g[b * cblk:(b + 1) * cblk, k * 128:(k + 1) * 128]


def _s5_state_kernel(g_ref, lam_ref, h0_ref, hin_ref, hfin_ref, *, n_chunks, bsz):
    lre = lam_ref[0:1, :]
    lim = lam_ref[1:2, :]
    tiles = [slice(k * 128, (k + 1) * 128) for k in range(4)]

    def body(c, carry):
        hs, hx = carry
        rows = (pl.ds(pl.multiple_of(c * bsz, bsz), bsz), pl.ds(pl.multiple_of((n_chunks - 1 - c) * bsz, bsz), bsz))
        out, outx = [], []
        for k, sl in enumerate(tiles):
            r = rows[k % 2]
            hin_ref[k, r, :] = hs[k]
            g = g_ref[k, r, :]
            out.append(lre[:, sl] * hs[k] + lim[:, sl] * hx[k] + g)
            outx.append(lre[:, sl] * hx[k] - lim[:, sl] * hs[k] + pltpu.roll(g, 64, 1))
        return tuple(out), tuple(outx)

    h0 = tuple(h0_ref[:, sl] for sl in tiles)
    hs, _ = lax.fori_loop(0, n_chunks, body, (h0, tuple(pltpu.roll(h, 64, 1) for h in h0)), unroll=8)
    for k, sl in enumerate(tiles):
        hfin_ref[:, sl] = hs[k]


def _s5_output_kernel(u_ref, hin_ref, m_ref, pout_ref, spread_ref, y_ref, w_scr, pout_scr, *, bsz, cblk):
    T, CH, G8 = S5_CHUNK, S5_CH, S5_TILE_GROUPS
    SWG = pout_ref.shape[-2]

    @pl.when(pl.program_id(1) == 0)
    def _():
        for g8 in range(G8):
            spread = spread_ref[g8]
            wide = _dot(m_ref[g8], spread).astype(BF16)
            for s in range(T):
                w_scr[pl.ds(s * G8 * CH + g8 * CH, CH), :] = wide[s * CH:(s + 1) * CH, :]
            pout_scr[pl.ds(g8 * SWG, SWG), :] = _dot(pout_ref[g8], spread).astype(BF16)

    x = _s5_chunk_rows(u_ref, bsz, cblk)
    hin = jnp.concatenate(
        [jnp.concatenate([hin_ref[k, pl.ds(b, cblk, stride=bsz), :] for k in range(hin_ref.shape[0])], axis=1)
         for b in range(bsz)], axis=0)
    hi, lo = _split_bf16(hin)
    pout = pout_scr[...]
    y = _dot(x, w_scr[...]) + _dot(hi, pout) + _dot(lo, pout)
    for b in range(bsz):
        for s in range(T):
            y_ref[b, pl.ds(s, cblk, stride=T), :] = y[b * cblk:(b + 1) * cblk, s * 128:(s + 1) * 128]


def _s5_scan(u, h0, m_c, pin_c, pout_c, lam_rows, spread):
    bsz, seq, _ = u.shape
    G, P, T = S5_GROUPS, S5_STATE, S5_CHUNK
    GT, G8 = G // S5_TILE_GROUPS, S5_TILE_GROUPS
    SW = G8 * 4 * P
    nc = seq // T
    cblk = min(max(256 // bsz, 8), nc)
    nblk = nc // cblk
    h0g = h0.transpose(2, 0, 1, 4, 3).reshape(GT, G8, bsz, 4 * P).transpose(0, 2, 1, 3).reshape(GT, bsz, SW)
    u_spec = pl.BlockSpec((bsz, cblk * T, 128), lambda x, j: (0, j, x))
    n_tiles = SW // 128
    rows_spec = pl.BlockSpec((None, n_tiles, cblk * bsz, 128), lambda x, j: (x, 0, j, 0))
    per_tile = lambda a: pl.BlockSpec((None,) + a.shape[1:], lambda x, j: (x, 0, 0, 0))
    xw = T * 128
    g = pl.pallas_call(
        functools.partial(_s5_summary_kernel, bsz=bsz, cblk=cblk),
        grid=(GT, nblk),
        in_specs=[u_spec, per_tile(pin_c)],
        scratch_shapes=[pltpu.VMEM((xw, SW), BF16)],
        out_specs=rows_spec,
        out_shape=jax.ShapeDtypeStruct((GT, n_tiles, nc * bsz, 128), F32),
        compiler_params=_cp("parallel", "arbitrary"),
        name="s5_chunk_summary",
    )(u, pin_c)
    quarter = lambda r: pl.BlockSpec((None, r, 512), lambda x, q: (x, 0, q))
    quarter_rows = pl.BlockSpec((None, 4, nc * bsz, 128), lambda x, q: (x, q, 0, 0))
    hin, hfin = pl.pallas_call(
        functools.partial(_s5_state_kernel, n_chunks=nc, bsz=bsz),
        grid=(GT, n_tiles // 4),
        in_specs=[quarter_rows, quarter(2), quarter(bsz)],
        out_specs=[quarter_rows, quarter(bsz)],
        out_shape=[jax.ShapeDtypeStruct((GT, n_tiles, nc * bsz, 128), F32), jax.ShapeDtypeStruct((GT, bsz, SW), F32)],
        compiler_params=_cp("parallel", "parallel"),
        name="s5_state_scan",
    )(g, lam_rows, h0g)
    y = pl.pallas_call(
        functools.partial(_s5_output_kernel, bsz=bsz, cblk=cblk),
        grid=(GT, nblk),
        in_specs=[u_spec, rows_spec, per_tile(m_c), per_tile(pout_c),
                  pl.BlockSpec(spread.shape, lambda x, j: (0, 0, 0))],
        scratch_shapes=[pltpu.VMEM((xw, xw), BF16), pltpu.VMEM((SW, xw), BF16)],
        out_specs=u_spec,
        out_shape=jax.ShapeDtypeStruct((bsz, seq, S5_WIDTH), F32),
        compiler_params=_cp("parallel", "arbitrary"),
        name="s5_chunk_output",
    )(u, hin, m_c, pout_c, spread)
    hfin = hfin.reshape(GT, bsz, G8, 2, 2, P).transpose(1, 3, 0, 2, 5, 4).reshape(bsz, 2, G, P, 2)
    return y, hfin


def _s5_out_kernel(y_ref, u_ref, d_ref, w_ref, nw_ref, o_ref):
    y = y_ref[...] + u_ref[...] * d_ref[...]
    c = math.sqrt(2.0 / math.pi)
    y = y * (0.5 * (1.0 + jnp.tanh(c * (y + 0.044715 * (y * y * y)))))
    z = _dot(y.astype(BF16), w_ref[...])
    o = z[:, :S5_WIDTH] * jax.nn.sigmoid(z[:, S5_WIDTH:])
    o_ref[...] = _rms(o, nw_ref[...]).astype(BF16)


def _s5_out(y2, u2, d_skip, w_glu, nw, tm):
    n = y2.shape[0]
    full = lambda a: pl.BlockSpec(a.shape, lambda i: (0,) * a.ndim)
    row = pl.BlockSpec((tm, S5_WIDTH), lambda i: (i, 0))
    return pl.pallas_call(
        _s5_out_kernel,
        grid=(n // tm,),
        in_specs=[row, row, full(d_skip), full(w_glu), full(nw)],
        out_specs=row,
        out_shape=jax.ShapeDtypeStruct((n, S5_WIDTH), BF16),
        compiler_params=_cp("parallel"),
        name="s5_gelu_glu",
    )(y2, u2, d_skip, w_glu, nw)


def _rwkv_pre_kernel(p_ref, hp_ref, hn_ref, mu_ref, w0_ref, a0_ref, w2_ref, a2_ref, g2_ref,
                     kk_ref, ka_ref, rk_ref, bd_ref,
                     v_ref, nk_ref, dec_ref, kd_ref, bb_ref, qq_ref, vkr_ref, g_ref, bonus_ref, *, tiles_per_seq, tm):
    i = pl.program_id(0)
    j = i % tiles_per_seq
    p = p_ref[...]
    rows = lax.broadcasted_iota(jnp.int32, (tm, 1), 0)
    prev_edge = jnp.where(j == 0, 0.0, hp_ref[7:8, :])
    next_edge = jnp.where(j == tiles_per_seq - 1, 0.0, hn_ref[0:1, :])
    prev = jnp.where(rows == 0, prev_edge, pltpu.roll(p, 1, 0))
    nxt = jnp.where(rows == tm - 1, next_edge, pltpu.roll(p, tm - 1, 0))
    p = p + mu_ref[...] * (0.5 * (prev + nxt) - p)
    W = RWKV_WIDTH
    r = p[:, 0:W]
    k = p[:, W:2 * W]
    v = p[:, 2 * W:3 * W]
    wl = p[:, 3 * W:3 * W + LORA_PAD]
    al = p[:, 3 * W + LORA_PAD:3 * W + 2 * LORA_PAD]
    gl = p[:, 3 * W + 2 * LORA_PAD:3 * W + 3 * LORA_PAD]
    bd = bd_ref[...]
    kk = k * kk_ref[...]
    kk = kk * lax.rsqrt(_segsum(kk * kk, bd) + 1e-12)
    g_ref[...] = _dot(jax.nn.sigmoid(gl).astype(BF16), g2_ref[...])
    tw = jnp.tanh(wl).astype(BF16)
    alb = al.astype(BF16)
    ksum = None
    for d in range(2):
        z = -(w0_ref[d:d + 1, :] + _dot(tw, w2_ref[d]))
        w = -(jnp.maximum(z, 0.0) + jnp.log(1.0 + jnp.exp(-jnp.abs(z)))) - 0.5
        dec = jnp.exp(-jnp.exp(w))
        dec_ref[d] = dec
        a = jax.nn.sigmoid(a0_ref[d:d + 1, :] + _dot(alb, a2_ref[d]))
        kd = k * (1.0 + (a - 1.0) * ka_ref[...])
        kd_ref[d] = kd
        bb = kk * a
        bb_ref[d] = bb
        qq_ref[d] = dec * r - kk * _segsum(bb * r, bd)
        vkr_ref[d] = v * _segsum(kd * r, bd)
        ksum = kd if ksum is None else ksum + kd
    v_ref[...] = v
    nk_ref[...] = -kk
    bonus_ref[...] = _segsum(r * ksum * rk_ref[...], bd) * v


def _rwkv_pre(p2, bsz, seq, wts, bd):
    n = p2.shape[0]
    tm = min(256, seq)
    tps = seq // tm
    r8 = tm // 8
    nblk8 = n // 8
    full = lambda a: pl.BlockSpec(a.shape, lambda i: (0,) * a.ndim)
    o3 = pl.BlockSpec((None, tm, RWKV_WIDTH), lambda i: (i // tps, i % tps, 0))
    o4 = pl.BlockSpec((2, None, tm, RWKV_WIDTH), lambda i: (0, i // tps, i % tps, 0))
    s3 = jax.ShapeDtypeStruct((bsz, seq, RWKV_WIDTH), F32)
    s4 = jax.ShapeDtypeStruct((2, bsz, seq, RWKV_WIDTH), F32)
    names = ("mu", "w0", "a0", "w2", "a2", "g2", "k_k", "k_a", "r_k")
    return pl.pallas_call(
        functools.partial(_rwkv_pre_kernel, tiles_per_seq=tps, tm=tm),
        grid=(n // tm,),
        in_specs=[pl.BlockSpec((tm, RWKV_COLS), lambda i: (i, 0)),
                  pl.BlockSpec((8, RWKV_COLS), lambda i: (jnp.maximum(i * r8 - 1, 0), 0)),
                  pl.BlockSpec((8, RWKV_COLS), lambda i: (jnp.minimum((i + 1) * r8, nblk8 - 1), 0))]
                 + [full(wts[k]) for k in names] + [full(bd)],
        out_specs=[o3, o3, o4, o4, o4, o4, o4, o3, o3],
        out_shape=[s3, s3, s4, s4, s4, s4, s4, s3, s3],
        compiler_params=_cp("parallel"),
        name="rwkv_prepare",
    )(p2, p2, p2, *[wts[k] for k in names], bd)


def _rwkv_scan_kernel(nkf, nkb, vf, vb, decf, decb, kdf, kdb, bbf, bbb, qqf, qqb, vkf, vkb, s0_ref,
                      yf_ref, yb_ref, sfin_ref, s_scr, v8_scr, v8k_scr, y8_scr, *, nb, tlen, n_chunks):
    c = pl.program_id(1)

    @pl.when(c == 0)
    def _():
        s_scr[...] = s0_ref[...].reshape(s_scr.shape)

    N = RWKV_HEAD_DIM
    W = RWKV_WIDTH
    H = RWKV_HEADS
    nt = (((1,), (1,)), ((), ()))
    chains = [(d, n) for d in range(2) for n in range(nb)]
    pick = lambda d, f, b: f if d == 0 else b
    base = lambda m: m * tlen * H

    left = lambda rows: lax.broadcasted_iota(jnp.int32, (rows, 128), 1) < N
    for m, (d, n) in enumerate(chains):
        for src, dst in ((pick(d, vf, vb), v8_scr), (pick(d, vkf, vkb), v8k_scr)):
            for p in range(H // 2):
                tile = src[n, :, p * 128:(p + 1) * 128]
                dst[pl.ds(base(m) + 2 * p, tlen, stride=H), :] = tile
                dst[pl.ds(base(m) + 2 * p + 1, tlen, stride=H), :] = pltpu.roll(tile, N, 1)

    @pl.when(c == 0)
    def _():
        y8_scr[...] = jnp.zeros_like(y8_scr)

    hmask = (lax.broadcasted_iota(jnp.int32, (H, W), 1) // N == lax.broadcasted_iota(jnp.int32, (H, W), 0)).astype(F32)
    tn = (((0,), (0,)), ((), ()))

    def step(t, carry):
        tts = (t, tlen - 1 - t)
        row = lambda ref, n, tt: ref[n, pl.ds(tt, 1), :] * hmask
        tile8 = lambda m, tt: pl.ds(pl.multiple_of(base(m) + tt * H, H), H)
        os_ = []
        for m, (d, n) in enumerate(chains):
            tt = tts[d]
            rows = jnp.concatenate([row(pick(d, nkf, nkb), n, tt), row(pick(d, qqf, qqb), n, tt)], axis=0).astype(BF16)
            os_.append(lax.dot_general(rows, s_scr[m].astype(BF16), nt, preferred_element_type=F32))
        for m, (d, n) in enumerate(chains):
            tt = tts[d]
            sa_v = jnp.concatenate([os_[m][0:H], v8_scr[tile8(m, tt), :][:, 0:N]], axis=0).astype(BF16)
            w2 = jnp.concatenate([row(pick(d, bbf, bbb), n, tt), row(pick(d, kdf, kdb), n, tt)], axis=0).astype(BF16)
            s_scr[m] = s_scr[m] * pick(d, decf, decb)[n, pl.ds(tt, 1), :] + lax.dot_general(
                sa_v, w2, tn, preferred_element_type=F32)
        for m, (d, n) in enumerate(chains):
            y8_scr[tile8(m, tts[d]), 0:N] = os_[m][H:2 * H] + v8k_scr[tile8(m, tts[d]), :][:, 0:N]
        return carry

    lax.fori_loop(0, tlen, step, 0, unroll=2)

    for m, (d, n) in enumerate(chains):
        y_ref = pick(d, yf_ref, yb_ref)
        for p in range(H // 2):
            even = y8_scr[pl.ds(base(m) + 2 * p, tlen, stride=H), :]
            odd = pltpu.roll(y8_scr[pl.ds(base(m) + 2 * p + 1, tlen, stride=H), :], N, 1)
            y_ref[n, :, p * 128:(p + 1) * 128] = jnp.where(left(tlen), even, odd)

    @pl.when(c == n_chunks - 1)
    def _():
        sfin_ref[...] = s_scr[...].reshape(sfin_ref.shape)


def _rwkv_scan(nk, v, dec, kd, bb, qq, vkr, s0):
    bsz, seq, W = nk.shape
    N, H = RWKV_HEAD_DIM, RWKV_HEADS
    nb = 8 if bsz % 8 == 0 else 4
    tlen = min(32, seq)
    nc = seq // tlen
    fwd = pl.BlockSpec((nb, tlen, W), lambda b, c: (b, c, 0))
    bwd = pl.BlockSpec((nb, tlen, W), lambda b, c: (b, nc - 1 - c, 0))
    fwd_d = pl.BlockSpec((None, nb, tlen, W), lambda b, c: (0, b, c, 0))
    bwd_d = pl.BlockSpec((None, nb, tlen, W), lambda b, c: (1, b, nc - 1 - c, 0))
    st = pl.BlockSpec((2, nb, N, W), lambda b, c: (0, b, 0, 0))
    tiles = pltpu.VMEM((2 * nb * tlen * H, 128), F32)
    return pl.pallas_call(
        functools.partial(_rwkv_scan_kernel, nb=nb, tlen=tlen, n_chunks=nc),
        grid=(bsz // nb, nc),
        in_specs=[fwd, bwd, fwd, bwd] + [fwd_d, bwd_d] * 5 + [st],
        out_specs=[fwd, bwd, st],
        out_shape=[jax.ShapeDtypeStruct((bsz, seq, W), F32)] * 2 + [jax.ShapeDtypeStruct((2, bsz, N, W), F32)],
        scratch_shapes=[pltpu.VMEM((2 * nb, N, W), F32), tiles, tiles, tiles],
        compiler_params=_cp("parallel", "arbitrary"),
        name="rwkv_scan",
    )(nk, nk, v, v, dec, dec, kd, kd, bb, bb, qq, qq, vkr, vkr, s0)


def _rwkv_post_kernel(yf_ref, yb_ref, bonus_ref, g_ref, lw_ref, lb_ref, bd_ref, o_ref):
    bd = bd_ref[...]
    y = yf_ref[...] + yb_ref[...]
    inv_n = 1.0 / RWKV_HEAD_DIM
    mean = _segsum(y, bd) * inv_n
    yc = y - mean
    var = _segsum(yc * yc, bd) * inv_n
    yn = yc * lax.rsqrt(var + GN_EPS) * lw_ref[...] + lb_ref[...]
    o_ref[...] = ((yn + bonus_ref[...]) * g_ref[...]).astype(BF16)


def _rwkv_post(yf, yb, bonus, g, ln_w, ln_b, bd):
    bsz, seq, W = yf.shape
    tm = min(256, seq)
    tps = seq // tm
    n = bsz * seq
    full = lambda a: pl.BlockSpec(a.shape, lambda i: (0,) * a.ndim)
    i3 = pl.BlockSpec((None, tm, W), lambda i: (i // tps, i % tps, 0))
    return pl.pallas_call(
        _rwkv_post_kernel,
        grid=(n // tm,),
        in_specs=[i3, i3, i3, i3, full(ln_w), full(ln_b), full(bd)],
        out_specs=pl.BlockSpec((tm, W), lambda i: (i, 0)),
        out_shape=jax.ShapeDtypeStruct((n, W), BF16),
        compiler_params=_cp("parallel"),
        name="rwkv_groupnorm_gate",
    )(yf, yb, bonus, g, ln_w, ln_b, bd)


def _store_k_heads(k_ref, kn, kpe):
    for h in range(MLA_HEADS):
        k_ref[:, h * MLA_QK_PAD:h * MLA_QK_PAD + 128] = kn[:, h * 128:(h + 1) * 128].astype(BF16)
        k_ref[:, h * MLA_QK_PAD + 128:(h + 1) * MLA_QK_PAD] = kpe


def _mla_prep_kernel(*refs, rope):
    if rope:
        (p_ref, qn_ref, kvn_ref, wq_ref, wqr_ref, wk_ref, wv_ref, cq_ref, sq_ref, ck_ref, sk_ref,
         q_ref, k_ref, v_ref) = refs
    else:
        (p_ref, qn_ref, kvn_ref, wq_ref, wk_ref, wv_ref, q_ref, k_ref, v_ref, ckv_ref, kr_ref) = refs
    p = p_ref[...]
    qn = _rms(p[:, 0:MLA_Q_RANK], qn_ref[...]).astype(BF16)
    q = _dot(qn, wq_ref[...])
    ckv = _rms(p[:, MLA_Q_RANK:MLA_Q_RANK + MLA_KV_RANK], kvn_ref[...])
    kr = p[:, MLA_Q_RANK + MLA_KV_RANK:MLA_Q_RANK + MLA_KV_RANK + 128]
    if rope:
        cq = jnp.concatenate([cq_ref[...]] * MLA_HEADS, axis=1)
        sq = jnp.concatenate([sq_ref[...]] * MLA_HEADS, axis=1)
        q = q * cq + _dot(qn, wqr_ref[...]) * sq
        krot = p[:, MLA_Q_RANK + MLA_KV_RANK + 128:MLA_Q_RANK + MLA_KV_RANK + 256]
        kpe = kr * ck_ref[...] + krot * sk_ref[...]
    else:
        kpe = kr
        ckv_ref[...] = ckv
        kr_ref[...] = kr[:, 0:MLA_ROPE_DIM]
    q_ref[...] = q.astype(BF16)
    cb = ckv.astype(BF16)
    _store_k_heads(k_ref, _dot(cb, wk_ref[...]), kpe.astype(BF16))
    v_ref[...] = _dot(cb, wv_ref[...]).astype(BF16)


def _mla_prep(p2, seq, wts, tables):
    n = p2.shape[0]
    rope = tables is not None
    tm = min(256, seq)
    tps = seq // tm
    full = lambda a: pl.BlockSpec(a.shape, lambda i: (0,) * a.ndim)
    row = lambda w: pl.BlockSpec((tm, w), lambda i: (i, 0))
    ins = [p2, wts["q_norm"], wts["kv_norm"], wts["wq"]]
    specs = [row(MLA_COLS), full(wts["q_norm"]), full(wts["kv_norm"]), full(wts["wq"])]
    if rope:
        ins.append(wts["wq_rot"])
        specs.append(full(wts["wq_rot"]))
    ins += [wts["wk"], wts["wv"]]
    specs += [full(wts["wk"]), full(wts["wv"])]
    outs = [row(MLA_HEADS * MLA_QK_PAD), row(MLA_HEADS * MLA_QK_PAD), row(MLA_HEADS * MLA_V_DIM)]
    shapes = [jax.ShapeDtypeStruct((n, MLA_HEADS * MLA_QK_PAD), BF16),
              jax.ShapeDtypeStruct((n, MLA_HEADS * MLA_QK_PAD), BF16),
              jax.ShapeDtypeStruct((n, MLA_HEADS * MLA_V_DIM), BF16)]
    if rope:
        ins += list(tables)
        specs += [pl.BlockSpec((tm, t.shape[1]), lambda i: (i % tps, 0)) for t in tables]
    else:
        outs += [row(MLA_KV_RANK), row(MLA_ROPE_DIM)]
        shapes += [jax.ShapeDtypeStruct((n, MLA_KV_RANK), F32), jax.ShapeDtypeStruct((n, MLA_ROPE_DIM), F32)]
    return pl.pallas_call(
        functools.partial(_mla_prep_kernel, rope=rope),
        grid=(n // tm,),
        in_specs=specs, out_specs=outs, out_shape=shapes,
        compiler_params=_cp("parallel"),
        name="mla_prepare_rope" if rope else "mla_prepare",
    )(*ins)


def _mla_cache_kernel(ckv_ref, kr_ref, wk_ref, wv_ref, k_ref, v_ref):
    cb = ckv_ref[...].astype(BF16)
    _store_k_heads(k_ref, _dot(cb, wk_ref[...]), kr_ref[...].astype(BF16))
    v_ref[...] = _dot(cb, wv_ref[...]).astype(BF16)


def _mla_cache(cache_ckv, cache_kr_pad, layer, wk, wv):
    bsz, _, past, _ = cache_ckv.shape
    full = lambda a: pl.BlockSpec(a.shape, lambda b: (0,) * a.ndim)
    return pl.pallas_call(
        _mla_cache_kernel,
        grid=(bsz,),
        in_specs=[pl.BlockSpec((None, None, past, MLA_KV_RANK), lambda b: (b, layer, 0, 0)),
                  pl.BlockSpec((None, None, past, 128), lambda b: (b, layer, 0, 0)),
                  full(wk), full(wv)],
        out_specs=[pl.BlockSpec((None, past, MLA_HEADS * MLA_QK_PAD), lambda b: (b, 0, 0)),
                   pl.BlockSpec((None, past, MLA_HEADS * MLA_V_DIM), lambda b: (b, 0, 0))],
        out_shape=[jax.ShapeDtypeStruct((bsz, past, MLA_HEADS * MLA_QK_PAD), BF16),
                   jax.ShapeDtypeStruct((bsz, past, MLA_HEADS * MLA_V_DIM), BF16)],
        compiler_params=_cp("parallel"),
        name="mla_cache_keys",
    )(cache_ckv, cache_kr_pad, wk, wv)


def _attn_kernel(*refs, cache, scale):
    if cache:
        q_ref, k_ref, v_ref, kc_ref, vc_ref, o_ref = refs
    else:
        q_ref, k_ref, v_ref, o_ref = refs
    nt = (((1,), (1,)), ((), ()))
    q = q_ref[...]
    s = lax.dot_general(q, k_ref[...], nt, preferred_element_type=F32) * scale
    m = jnp.max(s, axis=-1, keepdims=True)
    if cache:
        sc = lax.dot_general(q, kc_ref[...], nt, preferred_element_type=F32) * scale
        m = jnp.maximum(m, jnp.max(sc, axis=-1, keepdims=True))
    e = jnp.exp(s - m)
    den = jnp.sum(e, axis=-1, keepdims=True)
    o = _dot(e.astype(BF16), v_ref[...])
    if cache:
        ec = jnp.exp(sc - m)
        den = den + jnp.sum(ec, axis=-1, keepdims=True)
        o = o + _dot(ec.astype(BF16), vc_ref[...])
    o_ref[...] = o / den


def _attention(q, k, v, kc=None, vc=None):
    bsz, seq, _ = q.shape
    tq = min(ATTN_Q_TILE, seq)
    cache = kc is not None
    scale = float(MLA_NOPE_DIM + MLA_ROPE_DIM) ** -0.5
    ins = [q, k, v]
    specs = [pl.BlockSpec((None, tq, MLA_QK_PAD), lambda b, h, i: (b, i, h)),
             pl.BlockSpec((None, seq, MLA_QK_PAD), lambda b, h, i: (b, 0, h)),
             pl.BlockSpec((None, seq, MLA_V_DIM), lambda b, h, i: (b, 0, h))]
    if cache:
        past = kc.shape[1]
        ins += [kc, vc]
        specs += [pl.BlockSpec((None, past, MLA_QK_PAD), lambda b, h, i: (b, 0, h)),
                  pl.BlockSpec((None, past, MLA_V_DIM), lambda b, h, i: (b, 0, h))]
    return pl.pallas_call(
        functools.partial(_attn_kernel, cache=cache, scale=scale),
        grid=(bsz, MLA_HEADS, seq // tq),
        in_specs=specs,
        out_specs=pl.BlockSpec((None, tq, MLA_V_DIM), lambda b, h, i: (b, i, h)),
        out_shape=jax.ShapeDtypeStruct((bsz, seq, MLA_HEADS * MLA_V_DIM), F32),
        compiler_params=_cp("parallel", "parallel", "arbitrary"),
        name="mla_attention_cached" if cache else "mla_attention",
    )(*ins)


def _outproj_kernel(ys_ref, yr_ref, ym_ref, x_ref, mod_ref, nm_ref, w_ref, o_ref):
    ym = _rms(ym_ref[...], nm_ref[...]).astype(BF16)
    acc = _dot(ys_ref[...], w_ref[0:S5_WIDTH, :])
    acc += _dot(yr_ref[...], w_ref[S5_WIDTH:S5_WIDTH + RWKV_WIDTH, :])
    acc += _dot(ym, w_ref[S5_WIDTH + RWKV_WIDTH:, :])
    o_ref[...] = x_ref[...] + mod_ref[2:3, :] * acc


def _outproj(ys, yr, ym, x2, mod, layer, row_of_tile, tm, nm, w_out):
    n = x2.shape[0]
    full = lambda a: pl.BlockSpec(a.shape, lambda i: (0,) * a.ndim)
    row = lambda w: pl.BlockSpec((tm, w), lambda i: (i, 0))
    return pl.pallas_call(
        _outproj_kernel,
        grid=(n // tm,),
        in_specs=[row(S5_WIDTH), row(RWKV_WIDTH), row(MLA_WIDTH), row(D_MODEL), _mod_spec(layer, row_of_tile),
                  full(nm), full(w_out)],
        out_specs=row(D_MODEL),
        out_shape=jax.ShapeDtypeStruct((n, D_MODEL), F32),
        compiler_params=_cp("parallel"),
        name="out_projection",
    )(ys, yr, ym, x2, mod, nm, w_out)


def _mlp_kernel(x_ref, nw_ref, mod_ref, w1_ref, w2_ref, nf_ref, o_ref, h_scr, *, final_norm):
    j = pl.program_id(1)

    @pl.when(j == 0)
    def _():
        h = _rms(x_ref[...], nw_ref[...]) * (1.0 + mod_ref[4:5, :]) + mod_ref[3:4, :]
        h_scr[...] = h.astype(BF16)
        o_ref[...] = jnp.zeros_like(o_ref)

    a = _dot(h_scr[...], w1_ref[...])
    a = jnp.square(jnp.maximum(a, 0.0)).astype(BF16)
    o_ref[...] += _dot(a, w2_ref[...])

    @pl.when(j == pl.num_programs(1) - 1)
    def _():
        y = x_ref[...] + mod_ref[5:6, :] * o_ref[...]
        if final_norm:
            y = _rms(y, nf_ref[...])
        o_ref[...] = y


def _mlp(x2, nw, mod, layer, row_of_tile, tm, w1, w2, nf, final_norm):
    n = x2.shape[0]
    tf = MLP_FF_TILE
    full = lambda a: pl.BlockSpec(a.shape, lambda i, j: (0,) * a.ndim)
    return pl.pallas_call(
        functools.partial(_mlp_kernel, final_norm=final_norm),
        grid=(n // tm, D_FF // tf),
        in_specs=[pl.BlockSpec((tm, D_MODEL), lambda i, j: (i, 0)), full(nw), _mod_spec(layer, row_of_tile),
                  pl.BlockSpec((D_MODEL, tf), lambda i, j: (0, j)),
                  pl.BlockSpec((tf, D_MODEL), lambda i, j: (j, 0)), full(nf)],
        out_specs=pl.BlockSpec((tm, D_MODEL), lambda i, j: (i, 0)),
        out_shape=jax.ShapeDtypeStruct((n, D_MODEL), F32),
        scratch_shapes=[pltpu.VMEM((tm, D_MODEL), BF16)],
        compiler_params=_cp("parallel", "arbitrary"),
        name="mlp_final" if final_norm else "mlp",
    )(x2, nw, mod, w1, w2, nf)


def _rope_tables(length):
    rows = length // GRID_W
    row_pos = jnp.repeat(jnp.arange(rows, dtype=F32), GRID_W)
    col_pos = jnp.tile(jnp.arange(GRID_W, dtype=F32), rows)
    axis_dim = MLA_ROPE_DIM // 2
    inv_freq = 1.0 / (ROPE_THETA ** (jnp.arange(0, axis_dim, 2, dtype=F32) / axis_dim))
    ang_r = row_pos[:, None] * inv_freq[None, :]
    ang_c = col_pos[:, None] * inv_freq[None, :]
    ang = jnp.concatenate([ang_r, ang_r, ang_c, ang_c], axis=-1)
    cos, sin = jnp.cos(ang), jnp.sin(ang)
    z64 = jnp.zeros((length, 64), F32)
    cos_q = jnp.concatenate([jnp.ones((length, MLA_NOPE_DIM), F32), cos, z64], axis=1)
    sin_q = jnp.concatenate([jnp.zeros((length, MLA_NOPE_DIM), F32), sin, z64], axis=1)
    cos_k = jnp.concatenate([cos, z64], axis=1)
    sin_k = jnp.concatenate([sin, z64], axis=1)
    return cos_q, sin_q, cos_k, sin_k


def _rot_cols(w):
    a, b, c, d = w[..., 0:16], w[..., 16:32], w[..., 32:48], w[..., 48:64]
    return jnp.concatenate([-b, a, -d, c], axis=-1)


def _layer_weights(l, p):
    d = D_MODEL
    w_in = p["w_in"][l]
    z64 = jnp.zeros((d, 64), F32)
    o = S5_WIDTH
    rk = w_in[:, o:o + 3 * RWKV_WIDTH]
    o += 3 * RWKV_WIDTH
    wl, al, gl = w_in[:, o:o + 64], w_in[:, o + 64:o + 128], w_in[:, o + 128:o + 256]
    o += 256
    cq, ckv, kr = w_in[:, o:o + 512], w_in[:, o + 512:o + 768], w_in[:, o + 768:o + 832]
    mu = p["rwkv_mu"][l]
    z1 = jnp.zeros((64,), F32)
    mu_pad = jnp.concatenate([mu[:1536], mu[1536:1600], z1, mu[1600:1664], z1, mu[1664:1792]])[None]
    pad_rows = lambda w: jnp.concatenate([w, jnp.zeros_like(w)], axis=-2)
    w_uq = p["mla_w_uq"][l].reshape(MLA_Q_RANK, MLA_HEADS, MLA_NOPE_DIM + MLA_ROPE_DIM)
    zq = jnp.zeros((MLA_Q_RANK, MLA_HEADS, 64), F32)
    wq = jnp.concatenate([w_uq, zq], axis=-1).reshape(MLA_Q_RANK, MLA_HEADS * MLA_QK_PAD)
    wq_rot = jnp.concatenate([jnp.zeros((MLA_Q_RANK, MLA_HEADS, MLA_NOPE_DIM), F32),
                              _rot_cols(w_uq[..., MLA_NOPE_DIM:]), zq], axis=-1).reshape(MLA_Q_RANK, MLA_HEADS * MLA_QK_PAD)
    w_ukv = p["mla_w_ukv"][l].reshape(MLA_KV_RANK, MLA_HEADS, MLA_NOPE_DIM + MLA_V_DIM)
    row = lambda a: a.reshape(1, -1)
    return {
        "norm_mix": row(p["norm_mix"][l]), "norm_mlp": row(p["norm_mlp"][l]),
        "w_s5": w_in[:, 0:S5_WIDTH].astype(BF16),
        "w_rwkv": jnp.concatenate([rk, wl, z64, al, z64, gl], axis=1).astype(BF16),
        "w_mla": jnp.concatenate([cq, ckv, kr, z64, _rot_cols(kr), z64], axis=1).astype(BF16),
        "w_out": p["w_out"][l].astype(BF16),
        "s5_d": row(p["s5_d"][l]), "s5_w_glu": p["s5_w_glu"][l].astype(BF16), "s5_out_norm": row(p["s5_out_norm"][l]),
        "rwkv": {
            "mu": mu_pad, "w0": p["rwkv_w0"][l], "a0": p["rwkv_a0"][l],
            "w2": pad_rows(p["rwkv_w2"][l]).astype(BF16), "a2": pad_rows(p["rwkv_a2"][l]).astype(BF16),
            "g2": p["rwkv_g2"][l].astype(BF16), "k_k": row(p["rwkv_k_k"][l]), "k_a": row(p["rwkv_k_a"][l]),
            "r_k": row(p["rwkv_r_k"][l]),
        },
        "rwkv_ln_w": row(p["rwkv_ln_w"][l]), "rwkv_ln_b": row(p["rwkv_ln_b"][l]),
        "mla": {
            "q_norm": row(p["mla_q_norm"][l]), "kv_norm": row(p["mla_kv_norm"][l]),
            "wq": wq.astype(BF16), "wq_rot": wq_rot.astype(BF16),
            "wk": w_ukv[..., :MLA_NOPE_DIM].reshape(MLA_KV_RANK, -1).astype(BF16),
            "wv": w_ukv[..., MLA_NOPE_DIM:].reshape(MLA_KV_RANK, -1).astype(BF16),
        },
        "mla_out_norm": row(p["mla_out_norm"][l]),
        "mlp_w1": p["mlp_w1"][l].astype(BF16), "mlp_w2": p["mlp_w2"][l].astype(BF16),
    }


def _trunk_layer(x, mod, layer, lw, s5w, bd, row_of_tile_fn, s5_h0, rwkv_s0, cache, tables, norm_final, final_norm):
    bsz, seq, d = x.shape
    n = bsz * seq
    x2 = x.reshape(n, d)
    tm = min(512, seq) if cache is not None else min(512, n)
    row_of_tile = row_of_tile_fn(tm)
    u2, pr2, pm2 = _inproj(x2, lw["norm_mix"], mod, layer, row_of_tile, tm, lw["w_s5"], lw["w_rwkv"], lw["w_mla"])

    ys, s5_fin = _s5_scan(u2.reshape(bsz, seq, S5_WIDTH), s5_h0, *s5w)
    ys2 = _s5_out(ys.reshape(n, S5_WIDTH), u2, lw["s5_d"], lw["s5_w_glu"], lw["s5_out_norm"], min(512, n))

    v, nk, dec, kd, bb, qq, vkr, g, bonus = _rwkv_pre(pr2, bsz, seq, lw["rwkv"], bd)
    s0 = rwkv_s0.transpose(1, 0, 3, 2, 4).reshape(2, bsz, RWKV_HEAD_DIM, RWKV_WIDTH)
    yf, yb, s_fin = _rwkv_scan(nk, v, dec, kd, bb, qq, vkr, s0)
    yr2 = _rwkv_post(yf, yb, bonus, g, lw["rwkv_ln_w"], lw["rwkv_ln_b"], bd)
    rwkv_fin = s_fin.reshape(2, bsz, RWKV_HEAD_DIM, RWKV_HEADS, RWKV_HEAD_DIM).transpose(1, 0, 3, 2, 4)

    shape3 = lambda a: a.reshape(bsz, seq, a.shape[-1])
    if cache is None:
        q, k, v_, ckv_n, k_rope = _mla_prep(pm2, seq, lw["mla"], None)
        ym = _attention(shape3(q), shape3(k), shape3(v_))
        extras = (shape3(ckv_n), shape3(k_rope), s5_fin, rwkv_fin)
    else:
        q, k, v_ = _mla_prep(pm2, seq, lw["mla"], tables)
        kc, vc = _mla_cache(cache[0], cache[1], layer, lw["mla"]["wk"], lw["mla"]["wv"])
        ym = _attention(shape3(q), shape3(k), shape3(v_), kc, vc)
        extras = None

    x2 = _outproj(ys2, yr2, ym.reshape(n, MLA_WIDTH), x2, mod, layer, row_of_tile, tm, lw["mla_out_norm"], lw["w_out"])
    tm_mlp = min(MLP_ROW_TILE, seq) if cache is not None else min(MLP_ROW_TILE, n)
    x2 = _mlp(x2, lw["norm_mlp"], mod, layer, row_of_tile_fn(tm_mlp), tm_mlp, lw["mlp_w1"], lw["mlp_w2"], norm_final,
              final_norm)
    return x2.reshape(bsz, seq, d), extras


def kernel(x_prompt, x_sample, cache_mla_ckv, cache_mla_krope, state_s5, state_rwkv, c, c_ctx, norm_mix, norm_mlp, norm_final, w_ada, b_ada, w_in, w_out, s5_a_re, s5_a_im, s5_log_dt, s5_b_re, s5_b_im, s5_c_re, s5_c_im, s5_d, s5_w_glu, s5_out_norm, rwkv_mu, rwkv_w0, rwkv_w2, rwkv_a0, rwkv_a2, rwkv_g2, rwkv_k_k, rwkv_k_a, rwkv_r_k, rwkv_ln_w, rwkv_ln_b, mla_q_norm, mla_w_uq, mla_kv_norm, mla_w_ukv, mla_out_norm, mlp_w1, mlp_w2):
    p = dict(norm_mix=norm_mix, norm_mlp=norm_mlp, w_in=w_in, w_out=w_out, s5_d=s5_d, s5_w_glu=s5_w_glu,
             s5_out_norm=s5_out_norm, rwkv_mu=rwkv_mu, rwkv_w0=rwkv_w0, rwkv_w2=rwkv_w2, rwkv_a0=rwkv_a0,
             rwkv_a2=rwkv_a2, rwkv_g2=rwkv_g2, rwkv_k_k=rwkv_k_k, rwkv_k_a=rwkv_k_a, rwkv_r_k=rwkv_r_k,
             rwkv_ln_w=rwkv_ln_w, rwkv_ln_b=rwkv_ln_b, mla_q_norm=mla_q_norm, mla_w_uq=mla_w_uq,
             mla_kv_norm=mla_kv_norm, mla_w_ukv=mla_w_ukv, mla_out_norm=mla_out_norm, mlp_w1=mlp_w1, mlp_w2=mlp_w2)
    depth = w_in.shape[0]
    b_ctx, l_ctx, d = x_prompt.shape
    b_dec, l_dec, _ = x_sample.shape

    rows = -(-(1 + b_dec) // 8) * 8
    cond = jnp.zeros((rows, d), F32).at[0].set(c_ctx).at[1:1 + b_dec].set(c)
    mod = _modulation(cond, w_ada, b_ada).reshape(depth, rows, N_MOD, d)

    bd = jnp.kron(jnp.eye(4, dtype=F32), jnp.ones((64, 64), F32)).astype(BF16)
    spread = _s5_lane_spread()
    tables = _rope_tables(l_dec)
    kr_pad = jnp.pad(cache_mla_krope, ((0, 0), (0, 0), (0, 0), (0, 128 - MLA_ROPE_DIM)))
    zero_s5 = jnp.zeros((b_ctx, 2, S5_GROUPS, S5_STATE, 2), F32)
    zero_rwkv = jnp.zeros((b_ctx, 2, RWKV_HEADS, RWKV_HEAD_DIM, RWKV_HEAD_DIM), F32)
    nf = norm_final.reshape(1, d)

    ctx_rows = lambda tm: (lambda i: 0)
    dec_rows = lambda tm: (lambda i: 1 + i // (l_dec // tm))

    xp, xs = x_prompt, x_sample
    new_ckv, new_krope, new_s5, new_rwkv = [], [], [], []
    for l in range(depth):
        lw = _layer_weights(l, p)
        s5w = _s5_prep(s5_a_re[l], s5_a_im[l], s5_log_dt[l], s5_b_re[l], s5_b_im[l], s5_c_re[l], s5_c_im[l]) + (spread,)
        last = l == depth - 1
        xp, (ckv_l, krope_l, s5_l, rwkv_l) = _trunk_layer(
            xp, mod, l, lw, s5w, bd, ctx_rows, zero_s5, zero_rwkv, None, None, nf, last)
        new_ckv.append(ckv_l)
        new_krope.append(krope_l)
        new_s5.append(s5_l)
        new_rwkv.append(rwkv_l)
        xs, _ = _trunk_layer(
            xs, mod, l, lw, s5w, bd, dec_rows, state_s5[:, l], state_rwkv[:, l], (cache_mla_ckv, kr_pad), tables, nf, last)
    return (xp, xs, jnp.stack(new_ckv, axis=1), jnp.stack(new_krope, axis=1),
            jnp.stack(new_s5, axis=1), jnp.stack(new_rwkv, axis=1))
```

```python
import functools
import math

import jax
import jax.numpy as jnp
from jax import lax
from jax.experimental import pallas as pl
from jax.experimental.pallas import tpu as pltpu

F32 = jnp.float32
BF16 = jnp.bfloat16

D_MODEL = 2048
N_MOD = 6
GRID_W = 64
S5_WIDTH = 512
S5_CH = 16
S5_GROUPS = 32
S5_STATE = 64
S5_CHUNK = 8
S5_TILE_GROUPS = 8
RWKV_WIDTH = 512
RWKV_HEAD_DIM = 64
RWKV_HEADS = 8
LORA_PAD = 128
RWKV_COLS = 3 * RWKV_WIDTH + 3 * LORA_PAD
MLA_HEADS = 8
MLA_V_DIM = 128
MLA_NOPE_DIM = 128
MLA_ROPE_DIM = 64
MLA_QK_PAD = 256
MLA_Q_RANK = 512
MLA_KV_RANK = 256
MLA_WIDTH = 1024
MLA_COLS = MLA_Q_RANK + MLA_KV_RANK + 2 * 128
D_FF = 8192
MLP_ROW_TILE = 512
MLP_FF_TILE = 1024
ATTN_Q_TILE = 256
ATTN_HEADS_PER_STEP = 4
ROPE_THETA = 10000.0
NORM_EPS = 1e-6
GN_EPS = 64e-5

VMEM_LIMIT_BYTES = 56 * 1024 * 1024


def _cp(*sem):
    return pltpu.CompilerParams(dimension_semantics=sem, vmem_limit_bytes=VMEM_LIMIT_BYTES)


def _dot(a, b):
    return jnp.dot(a, b, preferred_element_type=F32)


def _rms(x, g):
    ms = jnp.mean(x * x, axis=-1, keepdims=True)
    return x * lax.rsqrt(ms + NORM_EPS) * g


def _split_bf16(x):
    hi = x.astype(BF16)
    lo = (x - hi.astype(F32)).astype(BF16)
    return hi, lo


def _segsum(x, bd):
    hi, lo = _split_bf16(x)
    left = _dot(hi[:, :256], bd) + _dot(lo[:, :256], bd)
    right = _dot(hi[:, 256:], bd) + _dot(lo[:, 256:], bd)
    return jnp.concatenate([left, right], axis=1)


def _segsum_bf16(xb, bd):
    return jnp.concatenate([_dot(xb[:, :256], bd), _dot(xb[:, 256:], bd)], axis=1)


def _mod_kernel(c_ref, w_ref, b_ref, o_ref):
    c = c_ref[...]
    s = (c * jax.nn.sigmoid(c)).astype(BF16)
    o_ref[...] = _dot(s, w_ref[...].astype(BF16)) + b_ref[...]


def _modulation(cond, w_ada, b_ada):
    depth, d, n = w_ada.shape
    rows = cond.shape[0]
    tn = 1024
    return pl.pallas_call(
        _mod_kernel,
        grid=(depth, n // tn),
        in_specs=[
            pl.BlockSpec((rows, d), lambda l, j: (0, 0)),
            pl.BlockSpec((None, d, tn), lambda l, j: (l, 0, j)),
            pl.BlockSpec((None, 1, tn), lambda l, j: (l, 0, j)),
        ],
        out_specs=pl.BlockSpec((None, rows, tn), lambda l, j: (l, 0, j)),
        out_shape=jax.ShapeDtypeStruct((depth, rows, n), F32),
        compiler_params=_cp("parallel", "arbitrary"),
        name="adaln_modulation",
    )(cond, w_ada, b_ada.reshape(depth, 1, n))


def _mod_spec(layer, row_of_tile):
    return pl.BlockSpec((None, None, N_MOD, D_MODEL), lambda i, *_: (layer, row_of_tile(i), 0, 0))


def _inproj_kernel(x_ref, nw_ref, mod_ref, ws_ref, wr_ref, wm_ref, os_ref, or_ref, om_ref):
    h = _rms(x_ref[...], nw_ref[...]) * (1.0 + mod_ref[1:2, :]) + mod_ref[0:1, :]
    hb = h.astype(BF16)
    os_ref[...] = _dot(hb, ws_ref[...])
    or_ref[...] = _dot(hb, wr_ref[...])
    om_ref[...] = _dot(hb, wm_ref[...])


def _inproj(x2, nw, mod, layer, row_of_tile, tm, ws, wr, wm):
    n = x2.shape[0]
    full = lambda a: pl.BlockSpec(a.shape, lambda i: (0,) * a.ndim)
    return pl.pallas_call(
        _inproj_kernel,
        grid=(n // tm,),
        in_specs=[
            pl.BlockSpec((tm, D_MODEL), lambda i: (i, 0)),
            full(nw),
            _mod_spec(layer, row_of_tile),
            full(ws), full(wr), full(wm),
        ],
        out_specs=[
            pl.BlockSpec((tm, S5_WIDTH), lambda i: (i, 0)),
            pl.BlockSpec((tm, RWKV_COLS), lambda i: (i, 0)),
            pl.BlockSpec((tm, MLA_COLS), lambda i: (i, 0)),
        ],
        out_shape=[
            jax.ShapeDtypeStruct((n, S5_WIDTH), F32),
            jax.ShapeDtypeStruct((n, RWKV_COLS), F32),
            jax.ShapeDtypeStruct((n, MLA_COLS), F32),
        ],
        compiler_params=_cp("parallel"),
        name="in_projection",
    )(x2, nw, mod, ws, wr, wm)


def _s5_prep_kernel(are_ref, aim_ref, ldt_ref, bre_ref, bim_ref, cre_ref, cim_ref,
                    k_ref, pin_ref, poutt_ref, lam_ref):
    T = S5_CHUNK
    for d in range(2):
        are = are_ref[d:d + 1, :]
        aim = aim_ref[d:d + 1, :]
        dt = jnp.exp(ldt_ref[d:d + 1, :])
        lre = jnp.exp(are * dt) * jnp.cos(aim * dt)
        lim = jnp.exp(are * dt) * jnp.sin(aim * dt)
        den = are * are + aim * aim
        xr = lre - 1.0
        zre = (xr * are + lim * aim) / den
        zim = (lim * are - xr * aim) / den
        bre = bre_ref[d]
        bim = bim_ref[d]
        bbre = zre * bre - zim * bim
        bbim = zre * bim + zim * bre
        cre = cre_ref[d]
        cim = cim_ref[d]

        def powers(tau):
            mag = jnp.exp(tau * (are * dt))
            ang = tau * (aim * dt)
            return mag * jnp.cos(ang), mag * jnp.sin(ang)

        tau0 = lax.broadcasted_iota(jnp.int32, (T, 1), 0).astype(F32)
        ere, eim = powers(tau0)
        xre = (ere[:, None, :] * cre[None] - eim[:, None, :] * cim[None]).reshape(T * S5_CH, S5_STATE)
        xim = (ere[:, None, :] * cim[None] + eim[:, None, :] * cre[None]).reshape(T * S5_CH, S5_STATE)
        nt = (((1,), (1,)), ((), ()))
        k_ref[d] = (lax.dot_general(xre, bbre, nt, precision=lax.Precision.HIGHEST, preferred_element_type=F32)
                    - lax.dot_general(xim, bbim, nt, precision=lax.Precision.HIGHEST, preferred_element_type=F32))
        tau_out = tau0 + 1.0 if d == 0 else float(T) - tau0
        ore, oim = powers(tau_out)
        poutt_ref[2 * d] = (ore[:, None, :] * cre[None] - oim[:, None, :] * cim[None]).reshape(T * S5_CH, S5_STATE)
        poutt_ref[2 * d + 1] = -(ore[:, None, :] * cim[None] + oim[:, None, :] * cre[None]).reshape(T * S5_CH, S5_STATE)
        tau_in = float(T - 1) - tau0 if d == 0 else tau0
        ire, iim = powers(tau_in)
        pin_ref[2 * d] = (ire[:, None, :] * bbre[None] - iim[:, None, :] * bbim[None]).reshape(T * S5_CH, S5_STATE)
        pin_ref[2 * d + 1] = (ire[:, None, :] * bbim[None] + iim[:, None, :] * bbre[None]).reshape(T * S5_CH, S5_STATE)
        tre, tim = powers(jnp.full((1, 1), float(T), F32))
        lam_ref[2 * d:2 * d + 1, :] = tre
        lam_ref[2 * d + 1:2 * d + 2, :] = tim


def _s5_prep(a_re, a_im, log_dt, b_re, b_im, c_re, c_im):
    G, P, CH, T = S5_GROUPS, S5_STATE, S5_CH, S5_CHUNK
    g_first = lambda a: jnp.swapaxes(a, 0, 1)
    are = g_first(a_re)
    aim = g_first(a_im)
    ldt = g_first(log_dt)[..., None]
    bre = jnp.swapaxes(g_first(b_re), -1, -2)
    bim = jnp.swapaxes(g_first(b_im), -1, -2)
    cre = g_first(c_re)
    cim = g_first(c_im)
    spec3 = lambda s: pl.BlockSpec((None,) + s, lambda g: (g,) + (0,) * len(s))
    k, pin, poutt, lam = pl.pallas_call(
        _s5_prep_kernel,
        grid=(G,),
        in_specs=[spec3((2, P)), spec3((2, P)), spec3((2, 1)),
                  spec3((2, CH, P)), spec3((2, CH, P)), spec3((2, CH, P)), spec3((2, CH, P))],
        out_specs=[spec3((2, T * CH, CH)), spec3((4, T * CH, P)), spec3((4, T * CH, P)), spec3((4, P))],
        out_shape=[jax.ShapeDtypeStruct((G, 2, T * CH, CH), F32),
                   jax.ShapeDtypeStruct((G, 4, T * CH, P), F32),
                   jax.ShapeDtypeStruct((G, 4, T * CH, P), F32),
                   jax.ShapeDtypeStruct((G, 4, P), F32)],
        compiler_params=_cp("parallel"),
        name="s5_weight_prep",
    )(are, aim, ldt, bre, bim, cre, cim)
    k = k.reshape(G, 2, T, CH, CH)
    s_idx = jnp.arange(T)[:, None]
    t_idx = jnp.arange(T)[None, :]
    kf = jnp.where((t_idx >= s_idx)[None, :, :, None, None], k[:, 0][:, jnp.clip(t_idx - s_idx, 0, T - 1)], 0.0)
    kb = jnp.where((t_idx <= s_idx)[None, :, :, None, None], k[:, 1][:, jnp.clip(s_idx - t_idx, 0, T - 1)], 0.0)
    m = (kf + kb).transpose(0, 1, 4, 2, 3)
    GT, G8 = G // S5_TILE_GROUPS, S5_TILE_GROUPS
    m_c = m.reshape(GT, G8, T * CH, T * CH)
    pin_c = pin.transpose(0, 2, 1, 3).reshape(GT, G8, T * CH, 4 * P)
    pout_c = poutt.transpose(0, 1, 3, 2).reshape(GT, G8, 4 * P, T * CH)
    lre = jnp.concatenate([lam[:, 0], lam[:, 0], lam[:, 2], lam[:, 2]], axis=-1)
    lim = jnp.concatenate([-lam[:, 1], lam[:, 1], -lam[:, 3], lam[:, 3]], axis=-1)
    lam_rows = jnp.stack([lre.reshape(GT, G8 * 4 * P), lim.reshape(GT, G8 * 4 * P)], axis=1)
    return m_c.astype(BF16), pin_c.astype(BF16), pout_c.astype(BF16), lam_rows


def _s5_lane_spread():
    T, CH, G8 = S5_CHUNK, S5_CH, S5_TILE_GROUPS
    src = jnp.arange(T * CH)
    dst = jnp.arange(T * G8 * CH)
    same = (src[:, None] // CH == dst[None, :] // (G8 * CH)) & (src[:, None] % CH == dst[None, :] % CH)
    g8_of_dst = (dst // CH) % G8
    return (same[None] & (g8_of_dst[None, None, :] == jnp.arange(G8)[:, None, None])).astype(BF16)


def _s5_chunk_rows(u_ref, bsz, cblk):
    T = S5_CHUNK
    per_b = [jnp.concatenate([u_ref[b, pl.ds(s, cblk, stride=T), :] for s in range(T)], axis=1) for b in range(bsz)]
    return jnp.concatenate(per_b, axis=0).astype(BF16)


def _s5_summary_kernel(u_ref, pin_ref, g_ref, pin_scr, *, bsz, cblk):
    T, CH, G8 = S5_CHUNK, S5_CH, S5_TILE_GROUPS
    SWG = pin_ref.shape[-1]

    @pl.when(pl.program_id(1) == 0)
    def _():
        pin_scr[...] = jnp.zeros_like(pin_scr)
        for g8 in range(G8):
            for s in range(T):
                pin_scr[pl.ds(s * G8 * CH + g8 * CH, CH), pl.ds(g8 * SWG, SWG)] = pin_ref[g8, pl.ds(s * CH, CH), :]

    g = _dot(_s5_chunk_rows(u_ref, bsz, cblk), pin_scr[...])
    for k in range(g_ref.shape[0]):
        for b in range(bsz):
            g_ref[k, pl.ds(b, cblk, stride=bsz), :] = g[b * cblk:(b + 1) * cblk, k * 128:(k + 1) * 128]


def _s5_state_kernel(g_ref, lam_ref, h0_ref, hin_ref, hfin_ref, *, n_chunks, bsz):
    lre = lam_ref[0:1, :]
    lim = lam_ref[1:2, :]
    tiles = [slice(k * 128, (k + 1) * 128) for k in range(4)]

    def body(c, carry):
        hs, hx = carry
        rows = (pl.ds(pl.multiple_of(c * bsz, bsz), bsz), pl.ds(pl.multiple_of((n_chunks - 1 - c) * bsz, bsz), bsz))
        out, outx = [], []
        for k, sl in enumerate(tiles):
            r = rows[k % 2]
            hin_ref[k, r, :] = hs[k]
            g = g_ref[k, r, :]
            out.append(lre[:, sl] * hs[k] + lim[:, sl] * hx[k] + g)
            outx.append(lre[:, sl] * hx[k] - lim[:, sl] * hs[k] + pltpu.roll(g, 64, 1))
        return tuple(out), tuple(outx)

    h0 = tuple(h0_ref[:, sl] for sl in tiles)
    hs, _ = lax.fori_loop(0, n_chunks, body, (h0, tuple(pltpu.roll(h, 64, 1) for h in h0)), unroll=8)
    for k, sl in enumerate(tiles):
        hfin_ref[:, sl] = hs[k]


def _s5_output_kernel(u_ref, hin_ref, m_ref, pout_ref, spread_ref, y_ref, w_scr, pout_scr, *, bsz, cblk):
    T, CH, G8 = S5_CHUNK, S5_CH, S5_TILE_GROUPS
    SWG = pout_ref.shape[-2]

    @pl.when(pl.program_id(1) == 0)
    def _():
        for g8 in range(G8):
            spread = spread_ref[g8]
            wide = _dot(m_ref[g8], spread).astype(BF16)
            for s in range(T):
                w_scr[pl.ds(s * G8 * CH + g8 * CH, CH), :] = wide[s * CH:(s + 1) * CH, :]
            pout_scr[pl.ds(g8 * SWG, SWG), :] = _dot(pout_ref[g8], spread).astype(BF16)

    x = _s5_chunk_rows(u_ref, bsz, cblk)
    hin = jnp.concatenate(
        [jnp.concatenate([hin_ref[k, pl.ds(b, cblk, stride=bsz), :] for k in range(hin_ref.shape[0])], axis=1)
         for b in range(bsz)], axis=0)
    hi, lo = _split_bf16(hin)
    pout = pout_scr[...]
    y = _dot(x, w_scr[...]) + _dot(hi, pout) + _dot(lo, pout)
    for b in range(bsz):
        for s in range(T):
            y_ref[b, pl.ds(s, cblk, stride=T), :] = y[b * cblk:(b + 1) * cblk, s * 128:(s + 1) * 128]


def _s5_scan(u, h0, m_c, pin_c, pout_c, lam_rows, spread):
    bsz, seq, _ = u.shape
    G, P, T = S5_GROUPS, S5_STATE, S5_CHUNK
    GT, G8 = G // S5_TILE_GROUPS, S5_TILE_GROUPS
    SW = G8 * 4 * P
    nc = seq // T
    cblk = min(max(256 // bsz, 8), nc)
    nblk = nc // cblk
    h0g = h0.transpose(2, 0, 1, 4, 3).reshape(GT, G8, bsz, 4 * P).transpose(0, 2, 1, 3).reshape(GT, bsz, SW)
    u_spec = pl.BlockSpec((bsz, cblk * T, 128), lambda x, j: (0, j, x))
    n_tiles = SW // 128
    rows_spec = pl.BlockSpec((None, n_tiles, cblk * bsz, 128), lambda x, j: (x, 0, j, 0))
    per_tile = lambda a: pl.BlockSpec((None,) + a.shape[1:], lambda x, j: (x, 0, 0, 0))
    xw = T * 128
    g = pl.pallas_call(
        functools.partial(_s5_summary_kernel, bsz=bsz, cblk=cblk),
        grid=(GT, nblk),
        in_specs=[u_spec, per_tile(pin_c)],
        scratch_shapes=[pltpu.VMEM((xw, SW), BF16)],
        out_specs=rows_spec,
        out_shape=jax.ShapeDtypeStruct((GT, n_tiles, nc * bsz, 128), F32),
        compiler_params=_cp("parallel", "arbitrary"),
        name="s5_chunk_summary",
    )(u, pin_c)
    quarter = lambda r: pl.BlockSpec((None, r, 512), lambda x, q: (x, 0, q))
    quarter_rows = pl.BlockSpec((None, 4, nc * bsz, 128), lambda x, q: (x, q, 0, 0))
    hin, hfin = pl.pallas_call(
        functools.partial(_s5_state_kernel, n_chunks=nc, bsz=bsz),
        grid=(GT, n_tiles // 4),
        in_specs=[quarter_rows, quarter(2), quarter(bsz)],
        out_specs=[quarter_rows, quarter(bsz)],
        out_shape=[jax.ShapeDtypeStruct((GT, n_tiles, nc * bsz, 128), F32), jax.ShapeDtypeStruct((GT, bsz, SW), F32)],
        compiler_params=_cp("parallel", "parallel"),
        name="s5_state_scan",
    )(g, lam_rows, h0g)
    y = pl.pallas_call(
        functools.partial(_s5_output_kernel, bsz=bsz, cblk=cblk),
        grid=(GT, nblk),
        in_specs=[u_spec, rows_spec, per_tile(m_c), per_tile(pout_c),
                  pl.BlockSpec(spread.shape, lambda x, j: (0, 0, 0))],
        scratch_shapes=[pltpu.VMEM((xw, xw), BF16), pltpu.VMEM((SW, xw), BF16)],
        out_specs=u_spec,
        out_shape=jax.ShapeDtypeStruct((bsz, seq, S5_WIDTH), F32),
        compiler_params=_cp("parallel", "arbitrary"),
        name="s5_chunk_output",
    )(u, hin, m_c, pout_c, spread)
    hfin = hfin.reshape(GT, bsz, G8, 2, 2, P).transpose(1, 3, 0, 2, 5, 4).reshape(bsz, 2, G, P, 2)
    return y, hfin


def _s5_out_kernel(y_ref, u_ref, d_ref, w_ref, nw_ref, o_ref):
    y = y_ref[...] + u_ref[...] * d_ref[...]
    c = math.sqrt(2.0 / math.pi)
    y = y * (0.5 * (1.0 + jnp.tanh(c * (y + 0.044715 * (y * y * y)))))
    z = _dot(y.astype(BF16), w_ref[...])
    o = z[:, :S5_WIDTH] * jax.nn.sigmoid(z[:, S5_WIDTH:])
    o_ref[...] = _rms(o, nw_ref[...]).astype(BF16)


def _s5_out(y2, u2, d_skip, w_glu, nw, tm):
    n = y2.shape[0]
    full = lambda a: pl.BlockSpec(a.shape, lambda i: (0,) * a.ndim)
    row = pl.BlockSpec((tm, S5_WIDTH), lambda i: (i, 0))
    return pl.pallas_call(
        _s5_out_kernel,
        grid=(n // tm,),
        in_specs=[row, row, full(d_skip), full(w_glu), full(nw)],
        out_specs=row,
        out_shape=jax.ShapeDtypeStruct((n, S5_WIDTH), BF16),
        compiler_params=_cp("parallel"),
        name="s5_gelu_glu",
    )(y2, u2, d_skip, w_glu, nw)


def _rwkv_pre_kernel(p_ref, hp_ref, hn_ref, mu_ref, w0_ref, a0_ref, w2_ref, a2_ref, g2_ref,
                     kk_ref, ka_ref, rk_ref, bd_ref,
                     v_ref, nk_ref, dec_ref, kd_ref, bb_ref, qq_ref, vkr_ref, g_ref, bonus_ref, *, tiles_per_seq, tm):
    i = pl.program_id(0)
    j = i % tiles_per_seq
    p = p_ref[...]
    rows = lax.broadcasted_iota(jnp.int32, (tm, 1), 0)
    prev_edge = jnp.where(j == 0, 0.0, hp_ref[7:8, :])
    next_edge = jnp.where(j == tiles_per_seq - 1, 0.0, hn_ref[0:1, :])
    prev = jnp.where(rows == 0, prev_edge, pltpu.roll(p, 1, 0))
    nxt = jnp.where(rows == tm - 1, next_edge, pltpu.roll(p, tm - 1, 0))
    p = p + mu_ref[...] * (0.5 * (prev + nxt) - p)
    W = RWKV_WIDTH
    r = p[:, 0:W]
    k = p[:, W:2 * W]
    v = p[:, 2 * W:3 * W]
    wl = p[:, 3 * W:3 * W + LORA_PAD]
    al = p[:, 3 * W + LORA_PAD:3 * W + 2 * LORA_PAD]
    gl = p[:, 3 * W + 2 * LORA_PAD:3 * W + 3 * LORA_PAD]
    bd = bd_ref[...]
    kk = k * kk_ref[...]
    kk = kk * lax.rsqrt(_segsum(kk * kk, bd) + 1e-12)
    g_ref[...] = _dot(jax.nn.sigmoid(gl).astype(BF16), g2_ref[...])
    tw = jnp.tanh(wl).astype(BF16)
    alb = al.astype(BF16)
    ksum = None
    for d in range(2):
        z = -(w0_ref[d:d + 1, :] + _dot(tw, w2_ref[d]))
        w = -(jnp.maximum(z, 0.0) + jnp.log(1.0 + jnp.exp(-jnp.abs(z)))) - 0.5
        dec = jnp.exp(-jnp.exp(w))
        dec_ref[d] = dec
        a = jax.nn.sigmoid(a0_ref[d:d + 1, :] + _dot(alb, a2_ref[d]))
        kd = k * (1.0 + (a - 1.0) * ka_ref[...])
        kd_ref[d] = kd
        bb = kk * a
        bb_ref[d] = bb
        qq_ref[d] = dec * r - kk * _segsum(bb * r, bd)
        vkr_ref[d] = v * _segsum(kd * r, bd)
        ksum = kd if ksum is None else ksum + kd
    v_ref[...] = v
    nk_ref[...] = -kk
    bonus_ref[...] = _segsum(r * ksum * rk_ref[...], bd) * v


def _rwkv_pre(p2, bsz, seq, wts, bd):
    n = p2.shape[0]
    tm = min(256, seq)
    tps = seq // tm
    r8 = tm // 8
    nblk8 = n // 8
    full = lambda a: pl.BlockSpec(a.shape, lambda i: (0,) * a.ndim)
    o3 = pl.BlockSpec((None, tm, RWKV_WIDTH), lambda i: (i // tps, i % tps, 0))
    o4 = pl.BlockSpec((2, None, tm, RWKV_WIDTH), lambda i: (0, i // tps, i % tps, 0))
    s3 = jax.ShapeDtypeStruct((bsz, seq, RWKV_WIDTH), F32)
    s4 = jax.ShapeDtypeStruct((2, bsz, seq, RWKV_WIDTH), F32)
    names = ("mu", "w0", "a0", "w2", "a2", "g2", "k_k", "k_a", "r_k")
    return pl.pallas_call(
        functools.partial(_rwkv_pre_kernel, tiles_per_seq=tps, tm=tm),
        grid=(n // tm,),
        in_specs=[pl.BlockSpec((tm, RWKV_COLS), lambda i: (i, 0)),
                  pl.BlockSpec((8, RWKV_COLS), lambda i: (jnp.maximum(i * r8 - 1, 0), 0)),
                  pl.BlockSpec((8, RWKV_COLS), lambda i: (jnp.minimum((i + 1) * r8, nblk8 - 1), 0))]
                 + [full(wts[k]) for k in names] + [full(bd)],
        out_specs=[o3, o3, o4, o4, o4, o4, o4, o3, o3],
        out_shape=[s3, s3, s4, s4, s4, s4, s4, s3, s3],
        compiler_params=_cp("parallel"),
        name="rwkv_prepare",
    )(p2, p2, p2, *[wts[k] for k in names], bd)


def _rwkv_scan_kernel(nkf, nkb, vf, vb, decf, decb, kdf, kdb, bbf, bbb, qqf, qqb, vkf, vkb, s0_ref,
                      yf_ref, yb_ref, sfin_ref, s_scr, v8_scr, v8k_scr, y8_scr, *, nb, tlen, n_chunks):
    c = pl.program_id(1)

    @pl.when(c == 0)
    def _():
        s_scr[...] = s0_ref[...].reshape(s_scr.shape)

    N = RWKV_HEAD_DIM
    W = RWKV_WIDTH
    H = RWKV_HEADS
    nt = (((1,), (1,)), ((), ()))
    chains = [(d, n) for d in range(2) for n in range(nb)]
    pick = lambda d, f, b: f if d == 0 else b
    base = lambda m: m * tlen * H

    left = lambda rows: lax.broadcasted_iota(jnp.int32, (rows, 128), 1) < N
    for m, (d, n) in enumerate(chains):
        for src, dst in ((pick(d, vf, vb), v8_scr), (pick(d, vkf, vkb), v8k_scr)):
            for p in range(H // 2):
                tile = src[n, :, p * 128:(p + 1) * 128]
                dst[pl.ds(base(m) + 2 * p, tlen, stride=H), :] = tile
                dst[pl.ds(base(m) + 2 * p + 1, tlen, stride=H), :] = pltpu.roll(tile, N, 1)

    @pl.when(c == 0)
    def _():
        y8_scr[...] = jnp.zeros_like(y8_scr)

    hmask = (lax.broadcasted_iota(jnp.int32, (H, W), 1) // N == lax.broadcasted_iota(jnp.int32, (H, W), 0)).astype(F32)
    tn = (((0,), (0,)), ((), ()))

    def step(t, carry):
        tts = (t, tlen - 1 - t)
        row = lambda ref, n, tt: ref[n, pl.ds(tt, 1), :] * hmask
        tile8 = lambda m, tt: pl.ds(pl.multiple_of(base(m) + tt * H, H), H)
        os_ = []
        for m, (d, n) in enumerate(chains):
            tt = tts[d]
            rows = jnp.concatenate([row(pick(d, nkf, nkb), n, tt), row(pick(d, qqf, qqb), n, tt)], axis=0).astype(BF16)
            os_.append(lax.dot_general(rows, s_scr[m].astype(BF16), nt, preferred_element_type=F32))
        for m, (d, n) in enumerate(chains):
            tt = tts[d]
            sa_v = jnp.concatenate([os_[m][0:H], v8_scr[tile8(m, tt), :][:, 0:N]], axis=0).astype(BF16)
            w2 = jnp.concatenate([row(pick(d, bbf, bbb), n, tt), row(pick(d, kdf, kdb), n, tt)], axis=0).astype(BF16)
            s_scr[m] = s_scr[m] * pick(d, decf, decb)[n, pl.ds(tt, 1), :] + lax.dot_general(
                sa_v, w2, tn, preferred_element_type=F32)
        for m, (d, n) in enumerate(chains):
            y8_scr[tile8(m, tts[d]), 0:N] = os_[m][H:2 * H] + v8k_scr[tile8(m, tts[d]), :][:, 0:N]
        return carry

    lax.fori_loop(0, tlen, step, 0, unroll=2)

    for m, (d, n) in enumerate(chains):
        y_ref = pick(d, yf_ref, yb_ref)
        for p in range(H // 2):
            even = y8_scr[pl.ds(base(m) + 2 * p, tlen, stride=H), :]
            odd = pltpu.roll(y8_scr[pl.ds(base(m) + 2 * p + 1, tlen, stride=H), :], N, 1)
            y_ref[n, :, p * 128:(p + 1) * 128] = jnp.where(left(tlen), even, odd)

    @pl.when(c == n_chunks - 1)
    def _():
        sfin_ref[...] = s_scr[...].reshape(sfin_ref.shape)


def _rwkv_scan(nk, v, dec, kd, bb, qq, vkr, s0):
    bsz, seq, W = nk.shape
    N, H = RWKV_HEAD_DIM, RWKV_HEADS
    nb = 8 if bsz % 8 == 0 else 4
    tlen = min(32, seq)
    nc = seq // tlen
    fwd = pl.BlockSpec((nb, tlen, W), lambda b, c: (b, c, 0))
    bwd = pl.BlockSpec((nb, tlen, W), lambda b, c: (b, nc - 1 - c, 0))
    fwd_d = pl.BlockSpec((None, nb, tlen, W), lambda b, c: (0, b, c, 0))
    bwd_d = pl.BlockSpec((None, nb, tlen, W), lambda b, c: (1, b, nc - 1 - c, 0))
    st = pl.BlockSpec((2, nb, N, W), lambda b, c: (0, b, 0, 0))
    tiles = pltpu.VMEM((2 * nb * tlen * H, 128), F32)
    return pl.pallas_call(
        functools.partial(_rwkv_scan_kernel, nb=nb, tlen=tlen, n_chunks=nc),
        grid=(bsz // nb, nc),
        in_specs=[fwd, bwd, fwd, bwd] + [fwd_d, bwd_d] * 5 + [st],
        out_specs=[fwd, bwd, st],
        out_shape=[jax.ShapeDtypeStruct((bsz, seq, W), F32)] * 2 + [jax.ShapeDtypeStruct((2, bsz, N, W), F32)],
        scratch_shapes=[pltpu.VMEM((2 * nb, N, W), F32), tiles, tiles, tiles],
        compiler_params=_cp("parallel", "arbitrary"),
        name="rwkv_scan",
    )(nk, nk, v, v, dec, dec, kd, kd, bb, bb, qq, qq, vkr, vkr, s0)


def _rwkv_post_kernel(yf_ref, yb_ref, bonus_ref, g_ref, lw_ref, lb_ref, bd_ref, o_ref):
    bd = bd_ref[...]
    y = yf_ref[...] + yb_ref[...]
    inv_n = 1.0 / RWKV_HEAD_DIM
    mean = _segsum(y, bd) * inv_n
    yc = y - mean
    var = _segsum(yc * yc, bd) * inv_n
    yn = yc * lax.rsqrt(var + GN_EPS) * lw_ref[...] + lb_ref[...]
    o_ref[...] = ((yn + bonus_ref[...]) * g_ref[...]).astype(BF16)


def _rwkv_post(yf, yb, bonus, g, ln_w, ln_b, bd):
    bsz, seq, W = yf.shape
    tm = min(256, seq)
    tps = seq // tm
    n = bsz * seq
    full = lambda a: pl.BlockSpec(a.shape, lambda i: (0,) * a.ndim)
    i3 = pl.BlockSpec((None, tm, W), lambda i: (i // tps, i % tps, 0))
    return pl.pallas_call(
        _rwkv_post_kernel,
        grid=(n // tm,),
        in_specs=[i3, i3, i3, i3, full(ln_w), full(ln_b), full(bd)],
        out_specs=pl.BlockSpec((tm, W), lambda i: (i, 0)),
        out_shape=jax.ShapeDtypeStruct((n, W), BF16),
        compiler_params=_cp("parallel"),
        name="rwkv_groupnorm_gate",
    )(yf, yb, bonus, g, ln_w, ln_b, bd)


def _store_k_heads(k_ref, kn, kpe):
    for h in range(MLA_HEADS):
        k_ref[:, h * MLA_QK_PAD:h * MLA_QK_PAD + 128] = kn[:, h * 128:(h + 1) * 128].astype(BF16)
        k_ref[:, h * MLA_QK_PAD + 128:(h + 1) * MLA_QK_PAD] = kpe


def _mla_prep_kernel(*refs, rope):
    if rope:
        (p_ref, qn_ref, kvn_ref, wq_ref, wqr_ref, wk_ref, wv_ref, cq_ref, sq_ref, ck_ref, sk_ref,
         q_ref, k_ref, v_ref) = refs
    else:
        (p_ref, qn_ref, kvn_ref, wq_ref, wk_ref, wv_ref, q_ref, k_ref, v_ref, ckv_ref, kr_ref) = refs
    p = p_ref[...]
    qn = _rms(p[:, 0:MLA_Q_RANK], qn_ref[...]).astype(BF16)
    q = _dot(qn, wq_ref[...])
    ckv = _rms(p[:, MLA_Q_RANK:MLA_Q_RANK + MLA_KV_RANK], kvn_ref[...])
    kr = p[:, MLA_Q_RANK + MLA_KV_RANK:MLA_Q_RANK + MLA_KV_RANK + 128]
    if rope:
        cq = jnp.concatenate([cq_ref[...]] * MLA_HEADS, axis=1)
        sq = jnp.concatenate([sq_ref[...]] * MLA_HEADS, axis=1)
        q = q * cq + _dot(qn, wqr_ref[...]) * sq
        krot = p[:, MLA_Q_RANK + MLA_KV_RANK + 128:MLA_Q_RANK + MLA_KV_RANK + 256]
        kpe = kr * ck_ref[...] + krot * sk_ref[...]
    else:
        kpe = kr
        ckv_ref[...] = ckv
        kr_ref[...] = kr[:, 0:MLA_ROPE_DIM]
    q_ref[...] = q.astype(BF16)
    cb = ckv.astype(BF16)
    _store_k_heads(k_ref, _dot(cb, wk_ref[...]), kpe.astype(BF16))
    v_ref[...] = _dot(cb, wv_ref[...]).astype(BF16)


def _mla_prep(p2, seq, wts, tables):
    n = p2.shape[0]
    rope = tables is not None
    tm = min(256, seq)
    tps = seq // tm
    full = lambda a: pl.BlockSpec(a.shape, lambda i: (0,) * a.ndim)
    row = lambda w: pl.BlockSpec((tm, w), lambda i: (i, 0))
    ins = [p2, wts["q_norm"], wts["kv_norm"], wts["wq"]]
    specs = [row(MLA_COLS), full(wts["q_norm"]), full(wts["kv_norm"]), full(wts["wq"])]
    if rope:
        ins.append(wts["wq_rot"])
        specs.append(full(wts["wq_rot"]))
    ins += [wts["wk"], wts["wv"]]
    specs += [full(wts["wk"]), full(wts["wv"])]
    outs = [row(MLA_HEADS * MLA_QK_PAD), row(MLA_HEADS * MLA_QK_PAD), row(MLA_HEADS * MLA_V_DIM)]
    shapes = [jax.ShapeDtypeStruct((n, MLA_HEADS * MLA_QK_PAD), BF16),
              jax.ShapeDtypeStruct((n, MLA_HEADS * MLA_QK_PAD), BF16),
              jax.ShapeDtypeStruct((n, MLA_HEADS * MLA_V_DIM), BF16)]
    if rope:
        ins += list(tables)
        specs += [pl.BlockSpec((tm, t.shape[1]), lambda i: (i % tps, 0)) for t in tables]
    else:
        outs += [row(MLA_KV_RANK), row(MLA_ROPE_DIM)]
        shapes += [jax.ShapeDtypeStruct((n, MLA_KV_RANK), F32), jax.ShapeDtypeStruct((n, MLA_ROPE_DIM), F32)]
    return pl.pallas_call(
        functools.partial(_mla_prep_kernel, rope=rope),
        grid=(n // tm,),
        in_specs=specs, out_specs=outs, out_shape=shapes,
        compiler_params=_cp("parallel"),
        name="mla_prepare_rope" if rope else "mla_prepare",
    )(*ins)


def _mla_cache_kernel(ckv_ref, kr_ref, wk_ref, wv_ref, k_ref, v_ref):
    cb = ckv_ref[...].astype(BF16)
    _store_k_heads(k_ref, _dot(cb, wk_ref[...]), kr_ref[...].astype(BF16))
    v_ref[...] = _dot(cb, wv_ref[...]).astype(BF16)


def _mla_cache(cache_ckv, cache_kr_pad, layer, wk, wv):
    bsz, _, past, _ = cache_ckv.shape
    full = lambda a: pl.BlockSpec(a.shape, lambda b: (0,) * a.ndim)
    return pl.pallas_call(
        _mla_cache_kernel,
        grid=(bsz,),
        in_specs=[pl.BlockSpec((None, None, past, MLA_KV_RANK), lambda b: (b, layer, 0, 0)),
                  pl.BlockSpec((None, None, past, 128), lambda b: (b, layer, 0, 0)),
                  full(wk), full(wv)],
        out_specs=[pl.BlockSpec((None, past, MLA_HEADS * MLA_QK_PAD), lambda b: (b, 0, 0)),
                   pl.BlockSpec((None, past, MLA_HEADS * MLA_V_DIM), lambda b: (b, 0, 0))],
        out_shape=[jax.ShapeDtypeStruct((bsz, past, MLA_HEADS * MLA_QK_PAD), BF16),
                   jax.ShapeDtypeStruct((bsz, past, MLA_HEADS * MLA_V_DIM), BF16)],
        compiler_params=_cp("parallel"),
        name="mla_cache_keys",
    )(cache_ckv, cache_kr_pad, wk, wv)


def _attn_kernel(*refs, cache, scale):
    if cache:
        q_ref, k_ref, v_ref, kc_ref, vc_ref, o_ref = refs
    else:
        q_ref, k_ref, v_ref, o_ref = refs
    nt = (((1,), (1,)), ((), ()))
    qk = lambda j: pl.ds(j * MLA_QK_PAD, MLA_QK_PAD)
    vv = lambda j: pl.ds(j * MLA_V_DIM, MLA_V_DIM)
    heads = range(ATTN_HEADS_PER_STEP)
    s = [lax.dot_general(q_ref[:, qk(j)], k_ref[:, qk(j)], nt, preferred_element_type=F32) * scale for j in heads]
    if cache:
        sc = [lax.dot_general(q_ref[:, qk(j)], kc_ref[:, qk(j)], nt, preferred_element_type=F32) * scale for j in heads]
    for j in heads:
        m = jnp.max(s[j], axis=-1, keepdims=True)
        if cache:
            m = jnp.maximum(m, jnp.max(sc[j], axis=-1, keepdims=True))
        e = jnp.exp(s[j] - m)
        den = jnp.sum(e, axis=-1, keepdims=True)
        o = _dot(e.astype(BF16), v_ref[:, vv(j)])
        if cache:
            ec = jnp.exp(sc[j] - m)
            den = den + jnp.sum(ec, axis=-1, keepdims=True)
            o = o + _dot(ec.astype(BF16), vc_ref[:, vv(j)])
        o_ref[:, vv(j)] = o / den


def _attention(q, k, v, kc=None, vc=None):
    bsz, seq, _ = q.shape
    tq = min(ATTN_Q_TILE, seq)
    cache = kc is not None
    scale = float(MLA_NOPE_DIM + MLA_ROPE_DIM) ** -0.5
    hq, hv = ATTN_HEADS_PER_STEP * MLA_QK_PAD, ATTN_HEADS_PER_STEP * MLA_V_DIM
    ins = [q, k, v]
    specs = [pl.BlockSpec((None, tq, hq), lambda b, h, i: (b, i, h)),
             pl.BlockSpec((None, seq, hq), lambda b, h, i: (b, 0, h)),
             pl.BlockSpec((None, seq, hv), lambda b, h, i: (b, 0, h))]
    if cache:
        past = kc.shape[1]
        ins += [kc, vc]
        specs += [pl.BlockSpec((None, past, hq), lambda b, h, i: (b, 0, h)),
                  pl.BlockSpec((None, past, hv), lambda b, h, i: (b, 0, h))]
    return pl.pallas_call(
        functools.partial(_attn_kernel, cache=cache, scale=scale),
        grid=(bsz, MLA_HEADS // ATTN_HEADS_PER_STEP, seq // tq),
        in_specs=specs,
        out_specs=pl.BlockSpec((None, tq, hv), lambda b, h, i: (b, i, h)),
        out_shape=jax.ShapeDtypeStruct((bsz, seq, MLA_HEADS * MLA_V_DIM), F32),
        compiler_params=_cp("parallel", "parallel", "arbitrary"),
        name="mla_attention_cached" if cache else "mla_attention",
    )(*ins)


def _outproj_kernel(ys_ref, yr_ref, ym_ref, x_ref, mod_ref, nm_ref, w_ref, o_ref):
    ym = _rms(ym_ref[...], nm_ref[...]).astype(BF16)
    acc = _dot(ys_ref[...], w_ref[0:S5_WIDTH, :])
    acc += _dot(yr_ref[...], w_ref[S5_WIDTH:S5_WIDTH + RWKV_WIDTH, :])
    acc += _dot(ym, w_ref[S5_WIDTH + RWKV_WIDTH:, :])
    o_ref[...] = x_ref[...] + mod_ref[2:3, :] * acc


def _outproj(ys, yr, ym, x2, mod, layer, row_of_tile, tm, nm, w_out):
    n = x2.shape[0]
    full = lambda a: pl.BlockSpec(a.shape, lambda i: (0,) * a.ndim)
    row = lambda w: pl.BlockSpec((tm, w), lambda i: (i, 0))
    return pl.pallas_call(
        _outproj_kernel,
        grid=(n // tm,),
        in_specs=[row(S5_WIDTH), row(RWKV_WIDTH), row(MLA_WIDTH), row(D_MODEL), _mod_spec(layer, row_of_tile),
                  full(nm), full(w_out)],
        out_specs=row(D_MODEL),
        out_shape=jax.ShapeDtypeStruct((n, D_MODEL), F32),
        compiler_params=_cp("parallel"),
        name="out_projection",
    )(ys, yr, ym, x2, mod, nm, w_out)


def _mlp_kernel(x_ref, nw_ref, mod_ref, w1_ref, w2_ref, nf_ref, o_ref, h_scr, *, final_norm):
    j = pl.program_id(1)

    @pl.when(j == 0)
    def _():
        h = _rms(x_ref[...], nw_ref[...]) * (1.0 + mod_ref[4:5, :]) + mod_ref[3:4, :]
        h_scr[...] = h.astype(BF16)
        o_ref[...] = jnp.zeros_like(o_ref)

    a = _dot(h_scr[...], w1_ref[...])
    a = jnp.square(jnp.maximum(a, 0.0)).astype(BF16)
    o_ref[...] += _dot(a, w2_ref[...])

    @pl.when(j == pl.num_programs(1) - 1)
    def _():
        y = x_ref[...] + mod_ref[5:6, :] * o_ref[...]
        if final_norm:
            y = _rms(y, nf_ref[...])
        o_ref[...] = y


def _mlp(x2, nw, mod, layer, row_of_tile, tm, w1, w2, nf, final_norm):
    n = x2.shape[0]
    tf = MLP_FF_TILE
    full = lambda a: pl.BlockSpec(a.shape, lambda i, j: (0,) * a.ndim)
    return pl.pallas_call(
        functools.partial(_mlp_kernel, final_norm=final_norm),
        grid=(n // tm, D_FF // tf),
        in_specs=[pl.BlockSpec((tm, D_MODEL), lambda i, j: (i, 0)), full(nw), _mod_spec(layer, row_of_tile),
                  pl.BlockSpec((D_MODEL, tf), lambda i, j: (0, j)),
                  pl.BlockSpec((tf, D_MODEL), lambda i, j: (j, 0)), full(nf)],
        out_specs=pl.BlockSpec((tm, D_MODEL), lambda i, j: (i, 0)),
        out_shape=jax.ShapeDtypeStruct((n, D_MODEL), F32),
        scratch_shapes=[pltpu.VMEM((tm, D_MODEL), BF16)],
        compiler_params=_cp("parallel", "arbitrary"),
        name="mlp_final" if final_norm else "mlp",
    )(x2, nw, mod, w1, w2, nf)


def _rope_tables(length):
    rows = length // GRID_W
    row_pos = jnp.repeat(jnp.arange(rows, dtype=F32), GRID_W)
    col_pos = jnp.tile(jnp.arange(GRID_W, dtype=F32), rows)
    axis_dim = MLA_ROPE_DIM // 2
    inv_freq = 1.0 / (ROPE_THETA ** (jnp.arange(0, axis_dim, 2, dtype=F32) / axis_dim))
    ang_r = row_pos[:, None] * inv_freq[None, :]
    ang_c = col_pos[:, None] * inv_freq[None, :]
    ang = jnp.concatenate([ang_r, ang_r, ang_c, ang_c], axis=-1)
    cos, sin = jnp.cos(ang), jnp.sin(ang)
    z64 = jnp.zeros((length, 64), F32)
    cos_q = jnp.concatenate([jnp.ones((length, MLA_NOPE_DIM), F32), cos, z64], axis=1)
    sin_q = jnp.concatenate([jnp.zeros((length, MLA_NOPE_DIM), F32), sin, z64], axis=1)
    cos_k = jnp.concatenate([cos, z64], axis=1)
    sin_k = jnp.concatenate([sin, z64], axis=1)
    return cos_q, sin_q, cos_k, sin_k


def _rot_cols(w):
    a, b, c, d = w[..., 0:16], w[..., 16:32], w[..., 32:48], w[..., 48:64]
    return jnp.concatenate([-b, a, -d, c], axis=-1)


def _layer_weights(l, p):
    d = D_MODEL
    w_in = p["w_in"][l]
    z64 = jnp.zeros((d, 64), F32)
    o = S5_WIDTH
    rk = w_in[:, o:o + 3 * RWKV_WIDTH]
    o += 3 * RWKV_WIDTH
    wl, al, gl = w_in[:, o:o + 64], w_in[:, o + 64:o + 128], w_in[:, o + 128:o + 256]
    o += 256
    cq, ckv, kr = w_in[:, o:o + 512], w_in[:, o + 512:o + 768], w_in[:, o + 768:o + 832]
    mu = p["rwkv_mu"][l]
    z1 = jnp.zeros((64,), F32)
    mu_pad = jnp.concatenate([mu[:1536], mu[1536:1600], z1, mu[1600:1664], z1, mu[1664:1792]])[None]
    pad_rows = lambda w: jnp.concatenate([w, jnp.zeros_like(w)], axis=-2)
    w_uq = p["mla_w_uq"][l].reshape(MLA_Q_RANK, MLA_HEADS, MLA_NOPE_DIM + MLA_ROPE_DIM)
    zq = jnp.zeros((MLA_Q_RANK, MLA_HEADS, 64), F32)
    wq = jnp.concatenate([w_uq, zq], axis=-1).reshape(MLA_Q_RANK, MLA_HEADS * MLA_QK_PAD)
    wq_rot = jnp.concatenate([jnp.zeros((MLA_Q_RANK, MLA_HEADS, MLA_NOPE_DIM), F32),
                              _rot_cols(w_uq[..., MLA_NOPE_DIM:]), zq], axis=-1).reshape(MLA_Q_RANK, MLA_HEADS * MLA_QK_PAD)
    w_ukv = p["mla_w_ukv"][l].reshape(MLA_KV_RANK, MLA_HEADS, MLA_NOPE_DIM + MLA_V_DIM)
    row = lambda a: a.reshape(1, -1)
    return {
        "norm_mix": row(p["norm_mix"][l]), "norm_mlp": row(p["norm_mlp"][l]),
        "w_s5": w_in[:, 0:S5_WIDTH].astype(BF16),
        "w_rwkv": jnp.concatenate([rk, wl, z64, al, z64, gl], axis=1).astype(BF16),
        "w_mla": jnp.concatenate([cq, ckv, kr, z64, _rot_cols(kr), z64], axis=1).astype(BF16),
        "w_out": p["w_out"][l].astype(BF16),
        "s5_d": row(p["s5_d"][l]), "s5_w_glu": p["s5_w_glu"][l].astype(BF16), "s5_out_norm": row(p["s5_out_norm"][l]),
        "rwkv": {
            "mu": mu_pad, "w0": p["rwkv_w0"][l], "a0": p["rwkv_a0"][l],
            "w2": pad_rows(p["rwkv_w2"][l]).astype(BF16), "a2": pad_rows(p["rwkv_a2"][l]).astype(BF16),
            "g2": p["rwkv_g2"][l].astype(BF16), "k_k": row(p["rwkv_k_k"][l]), "k_a": row(p["rwkv_k_a"][l]),
            "r_k": row(p["rwkv_r_k"][l]),
        },
        "rwkv_ln_w": row(p["rwkv_ln_w"][l]), "rwkv_ln_b": row(p["rwkv_ln_b"][l]),
        "mla": {
            "q_norm": row(p["mla_q_norm"][l]), "kv_norm": row(p["mla_kv_norm"][l]),
            "wq": wq.astype(BF16), "wq_rot": wq_rot.astype(BF16),
            "wk": w_ukv[..., :MLA_NOPE_DIM].reshape(MLA_KV_RANK, -1).astype(BF16),
            "wv": w_ukv[..., MLA_NOPE_DIM:].reshape(MLA_KV_RANK, -1).astype(BF16),
        },
        "mla_out_norm": row(p["mla_out_norm"][l]),
        "mlp_w1": p["mlp_w1"][l].astype(BF16), "mlp_w2": p["mlp_w2"][l].astype(BF16),
    }


def _trunk_layer(x, mod, layer, lw, s5w, bd, row_of_tile_fn, s5_h0, rwkv_s0, cache, tables, norm_final, final_norm):
    bsz, seq, d = x.shape
    n = bsz * seq
    x2 = x.reshape(n, d)
    tm = min(512, seq) if cache is not None else min(512, n)
    row_of_tile = row_of_tile_fn(tm)
    u2, pr2, pm2 = _inproj(x2, lw["norm_mix"], mod, layer, row_of_tile, tm, lw["w_s5"], lw["w_rwkv"], lw["w_mla"])

    ys, s5_fin = _s5_scan(u2.reshape(bsz, seq, S5_WIDTH), s5_h0, *s5w)
    ys2 = _s5_out(ys.reshape(n, S5_WIDTH), u2, lw["s5_d"], lw["s5_w_glu"], lw["s5_out_norm"], min(512, n))

    v, nk, dec, kd, bb, qq, vkr, g, bonus = _rwkv_pre(pr2, bsz, seq, lw["rwkv"], bd)
    s0 = rwkv_s0.transpose(1, 0, 3, 2, 4).reshape(2, bsz, RWKV_HEAD_DIM, RWKV_WIDTH)
    yf, yb, s_fin = _rwkv_scan(nk, v, dec, kd, bb, qq, vkr, s0)
    yr2 = _rwkv_post(yf, yb, bonus, g, lw["rwkv_ln_w"], lw["rwkv_ln_b"], bd)
    rwkv_fin = s_fin.reshape(2, bsz, RWKV_HEAD_DIM, RWKV_HEADS, RWKV_HEAD_DIM).transpose(1, 0, 3, 2, 4)

    shape3 = lambda a: a.reshape(bsz, seq, a.shape[-1])
    if cache is None:
        q, k, v_, ckv_n, k_rope = _mla_prep(pm2, seq, lw["mla"], None)
        ym = _attention(shape3(q), shape3(k), shape3(v_))
        extras = (shape3(ckv_n), shape3(k_rope), s5_fin, rwkv_fin)
    else:
        q, k, v_ = _mla_prep(pm2, seq, lw["mla"], tables)
        kc, vc = _mla_cache(cache[0], cache[1], layer, lw["mla"]["wk"], lw["mla"]["wv"])
        ym = _attention(shape3(q), shape3(k), shape3(v_), kc, vc)
        extras = None

    x2 = _outproj(ys2, yr2, ym.reshape(n, MLA_WIDTH), x2, mod, layer, row_of_tile, tm, lw["mla_out_norm"], lw["w_out"])
    tm_mlp = min(MLP_ROW_TILE, seq) if cache is not None else min(MLP_ROW_TILE, n)
    x2 = _mlp(x2, lw["norm_mlp"], mod, layer, row_of_tile_fn(tm_mlp), tm_mlp, lw["mlp_w1"], lw["mlp_w2"], norm_final,
              final_norm)
    return x2.reshape(bsz, seq, d), extras


def kernel(x_prompt, x_sample, cache_mla_ckv, cache_mla_krope, state_s5, state_rwkv, c, c_ctx, norm_mix, norm_mlp, norm_final, w_ada, b_ada, w_in, w_out, s5_a_re, s5_a_im, s5_log_dt, s5_b_re, s5_b_im, s5_c_re, s5_c_im, s5_d, s5_w_glu, s5_out_norm, rwkv_mu, rwkv_w0, rwkv_w2, rwkv_a0, rwkv_a2, rwkv_g2, rwkv_k_k, rwkv_k_a, rwkv_r_k, rwkv_ln_w, rwkv_ln_b, mla_q_norm, mla_w_uq, mla_kv_norm, mla_w_ukv, mla_out_norm, mlp_w1, mlp_w2):
    p = dict(norm_mix=norm_mix, norm_mlp=norm_mlp, w_in=w_in, w_out=w_out, s5_d=s5_d, s5_w_glu=s5_w_glu,
             s5_out_norm=s5_out_norm, rwkv_mu=rwkv_mu, rwkv_w0=rwkv_w0, rwkv_w2=rwkv_w2, rwkv_a0=rwkv_a0,
             rwkv_a2=rwkv_a2, rwkv_g2=rwkv_g2, rwkv_k_k=rwkv_k_k, rwkv_k_a=rwkv_k_a, rwkv_r_k=rwkv_r_k,
             rwkv_ln_w=rwkv_ln_w, rwkv_ln_b=rwkv_ln_b, mla_q_norm=mla_q_norm, mla_w_uq=mla_w_uq,
             mla_kv_norm=mla_kv_norm, mla_w_ukv=mla_w_ukv, mla_out_norm=mla_out_norm, mlp_w1=mlp_w1, mlp_w2=mlp_w2)
    depth = w_in.shape[0]
    b_ctx, l_ctx, d = x_prompt.shape
    b_dec, l_dec, _ = x_sample.shape

    rows = -(-(1 + b_dec) // 8) * 8
    cond = jnp.zeros((rows, d), F32).at[0].set(c_ctx).at[1:1 + b_dec].set(c)
    mod = _modulation(cond, w_ada, b_ada).reshape(depth, rows, N_MOD, d)

    bd = jnp.kron(jnp.eye(4, dtype=F32), jnp.ones((64, 64), F32)).astype(BF16)
    spread = _s5_lane_spread()
    tables = _rope_tables(l_dec)
    kr_pad = jnp.pad(cache_mla_krope, ((0, 0), (0, 0), (0, 0), (0, 128 - MLA_ROPE_DIM)))
    zero_s5 = jnp.zeros((b_ctx, 2, S5_GROUPS, S5_STATE, 2), F32)
    zero_rwkv = jnp.zeros((b_ctx, 2, RWKV_HEADS, RWKV_HEAD_DIM, RWKV_HEAD_DIM), F32)
    nf = norm_final.reshape(1, d)

    ctx_rows = lambda tm: (lambda i: 0)
    dec_rows = lambda tm: (lambda i: 1 + i // (l_dec // tm))

    xp, xs = x_prompt, x_sample
    new_ckv, new_krope, new_s5, new_rwkv = [], [], [], []
    for l in range(depth):
        lw = _layer_weights(l, p)
        s5w = _s5_prep(s5_a_re[l], s5_a_im[l], s5_log_dt[l], s5_b_re[l], s5_b_im[l], s5_c_re[l], s5_c_im[l]) + (spread,)
        last = l == depth - 1
        xp, (ckv_l, krope_l, s5_l, rwkv_l) = _trunk_layer(
            xp, mod, l, lw, s5w, bd, ctx_rows, zero_s5, zero_rwkv, None, None, nf, last)
        new_ckv.append(ckv_l)
        new_krope.append(krope_l)
        new_s5.append(s5_l)
        new_rwkv.append(rwkv_l)
        xs, _ = _trunk_layer(
            xs, mod, l, lw, s5w, bd, dec_rows, state_s5[:, l], state_rwkv[:, l], (cache_mla_ckv, kr_pad), tables, nf, last)
    return (xp, xs, jnp.stack(new_ckv, axis=1), jnp.stack(new_krope, axis=1),
            jnp.stack(new_s5, axis=1), jnp.stack(new_rwkv, axis=1))
```

```python
import functools
import math

import jax
import jax.numpy as jnp
from jax import lax
from jax.experimental import pallas as pl
from jax.experimental.pallas import tpu as pltpu

F32 = jnp.float32
BF16 = jnp.bfloat16

D_MODEL = 2048
N_MOD = 6
GRID_W = 64
S5_WIDTH = 512
S5_CH = 16
S5_GROUPS = 32
S5_STATE = 64
S5_CHUNK = 8
S5_TILE_GROUPS = 8
RWKV_WIDTH = 512
RWKV_HEAD_DIM = 64
RWKV_HEADS = 8
LORA_PAD = 128
RWKV_COLS = 3 * RWKV_WIDTH + 3 * LORA_PAD
MLA_HEADS = 8
MLA_V_DIM = 128
MLA_NOPE_DIM = 128
MLA_ROPE_DIM = 64
MLA_QK_PAD = 256
MLA_Q_RANK = 512
MLA_KV_RANK = 256
MLA_WIDTH = 1024
MLA_COLS = MLA_Q_RANK + MLA_KV_RANK + 2 * 128
D_FF = 8192
MLP_ROW_TILE = 512
MLP_FF_TILE = 1024
ATTN_Q_TILE = 256
ATTN_HEADS_PER_STEP = 4
ROPE_THETA = 10000.0
NORM_EPS = 1e-6
GN_EPS = 64e-5

VMEM_LIMIT_BYTES = 56 * 1024 * 1024


def _cp(*sem):
    return pltpu.CompilerParams(dimension_semantics=sem, vmem_limit_bytes=VMEM_LIMIT_BYTES)


def _dot(a, b):
    return jnp.dot(a, b, preferred_element_type=F32)


def _rms(x, g):
    ms = jnp.mean(x * x, axis=-1, keepdims=True)
    return x * lax.rsqrt(ms + NORM_EPS) * g


def _split_bf16(x):
    hi = x.astype(BF16)
    lo = (x - hi.astype(F32)).astype(BF16)
    return hi, lo


def _segsum(x, bd):
    hi, lo = _split_bf16(x)
    left = _dot(hi[:, :256], bd) + _dot(lo[:, :256], bd)
    right = _dot(hi[:, 256:], bd) + _dot(lo[:, 256:], bd)
    return jnp.concatenate([left, right], axis=1)


def _segsum_bf16(xb, bd):
    return jnp.concatenate([_dot(xb[:, :256], bd), _dot(xb[:, 256:], bd)], axis=1)


def _mod_kernel(c_ref, w_ref, b_ref, o_ref):
    c = c_ref[...]
    s = (c * jax.nn.sigmoid(c)).astype(BF16)
    o_ref[...] = _dot(s, w_ref[...].astype(BF16)) + b_ref[...]


def _modulation(cond, w_ada, b_ada):
    depth, d, n = w_ada.shape
    rows = cond.shape[0]
    tn = 1024
    return pl.pallas_call(
        _mod_kernel,
        grid=(depth, n // tn),
        in_specs=[
            pl.BlockSpec((rows, d), lambda l, j: (0, 0)),
            pl.BlockSpec((None, d, tn), lambda l, j: (l, 0, j)),
            pl.BlockSpec((None, 1, tn), lambda l, j: (l, 0, j)),
        ],
        out_specs=pl.BlockSpec((None, rows, tn), lambda l, j: (l, 0, j)),
        out_shape=jax.ShapeDtypeStruct((depth, rows, n), F32),
        compiler_params=_cp("parallel", "arbitrary"),
        name="adaln_modulation",
    )(cond, w_ada, b_ada.reshape(depth, 1, n))


def _mod_spec(layer, row_of_tile):
    return pl.BlockSpec((None, None, N_MOD, D_MODEL), lambda i, *_: (layer, row_of_tile(i), 0, 0))


def _inproj_kernel(x_ref, nw_ref, mod_ref, ws_ref, wr_ref, wm_ref, os_ref, or_ref, om_ref):
    h = _rms(x_ref[...], nw_ref[...]) * (1.0 + mod_ref[1:2, :]) + mod_ref[0:1, :]
    hb = h.astype(BF16)
    os_ref[...] = _dot(hb, ws_ref[...])
    or_ref[...] = _dot(hb, wr_ref[...])
    om_ref[...] = _dot(hb, wm_ref[...])


def _inproj(x2, nw, mod, layer, row_of_tile, tm, ws, wr, wm):
    n = x2.shape[0]
    full = lambda a: pl.BlockSpec(a.shape, lambda i: (0,) * a.ndim)
    return pl.pallas_call(
        _inproj_kernel,
        grid=(n // tm,),
        in_specs=[
            pl.BlockSpec((tm, D_MODEL), lambda i: (i, 0)),
            full(nw),
            _mod_spec(layer, row_of_tile),
            full(ws), full(wr), full(wm),
        ],
        out_specs=[
            pl.BlockSpec((tm, S5_WIDTH), lambda i: (i, 0)),
            pl.BlockSpec((tm, RWKV_COLS), lambda i: (i, 0)),
            pl.BlockSpec((tm, MLA_COLS), lambda i: (i, 0)),
        ],
        out_shape=[
            jax.ShapeDtypeStruct((n, S5_WIDTH), F32),
            jax.ShapeDtypeStruct((n, RWKV_COLS), F32),
            jax.ShapeDtypeStruct((n, MLA_COLS), F32),
        ],
        compiler_params=_cp("parallel"),
        name="in_projection",
    )(x2, nw, mod, ws, wr, wm)


def _s5_prep_kernel(are_ref, aim_ref, ldt_ref, bre_ref, bim_ref, cre_ref, cim_ref,
                    k_ref, pin_ref, poutt_ref, lam_ref):
    T = S5_CHUNK
    for d in range(2):
        are = are_ref[d:d + 1, :]
        aim = aim_ref[d:d + 1, :]
        dt = jnp.exp(ldt_ref[d:d + 1, :])
        lre = jnp.exp(are * dt) * jnp.cos(aim * dt)
        lim = jnp.exp(are * dt) * jnp.sin(aim * dt)
        den = are * are + aim * aim
        xr = lre - 1.0
        zre = (xr * are + lim * aim) / den
        zim = (lim * are - xr * aim) / den
        bre = bre_ref[d]
        bim = bim_ref[d]
        bbre = zre * bre - zim * bim
        bbim = zre * bim + zim * bre
        cre = cre_ref[d]
        cim = cim_ref[d]

        def powers(tau):
            mag = jnp.exp(tau * (are * dt))
            ang = tau * (aim * dt)
            return mag * jnp.cos(ang), mag * jnp.sin(ang)

        tau0 = lax.broadcasted_iota(jnp.int32, (T, 1), 0).astype(F32)
        ere, eim = powers(tau0)
        xre = (ere[:, None, :] * cre[None] - eim[:, None, :] * cim[None]).reshape(T * S5_CH, S5_STATE)
        xim = (ere[:, None, :] * cim[None] + eim[:, None, :] * cre[None]).reshape(T * S5_CH, S5_STATE)
        nt = (((1,), (1,)), ((), ()))
        k_ref[d] = (lax.dot_general(xre, bbre, nt, precision=lax.Precision.HIGHEST, preferred_element_type=F32)
                    - lax.dot_general(xim, bbim, nt, precision=lax.Precision.HIGHEST, preferred_element_type=F32))
        tau_out = tau0 + 1.0 if d == 0 else float(T) - tau0
        ore, oim = powers(tau_out)
        poutt_ref[2 * d] = (ore[:, None, :] * cre[None] - oim[:, None, :] * cim[None]).reshape(T * S5_CH, S5_STATE)
        poutt_ref[2 * d + 1] = -(ore[:, None, :] * cim[None] + oim[:, None, :] * cre[None]).reshape(T * S5_CH, S5_STATE)
        tau_in = float(T - 1) - tau0 if d == 0 else tau0
        ire, iim = powers(tau_in)
        pin_ref[2 * d] = (ire[:, None, :] * bbre[None] - iim[:, None, :] * bbim[None]).reshape(T * S5_CH, S5_STATE)
        pin_ref[2 * d + 1] = (ire[:, None, :] * bbim[None] + iim[:, None, :] * bbre[None]).reshape(T * S5_CH, S5_STATE)
        tre, tim = powers(jnp.full((1, 1), float(T), F32))
        lam_ref[2 * d:2 * d + 1, :] = tre
        lam_ref[2 * d + 1:2 * d + 2, :] = tim


def _s5_prep(a_re, a_im, log_dt, b_re, b_im, c_re, c_im):
    G, P, CH, T = S5_GROUPS, S5_STATE, S5_CH, S5_CHUNK
    g_first = lambda a: jnp.swapaxes(a, 0, 1)
    are = g_first(a_re)
    aim = g_first(a_im)
    ldt = g_first(log_dt)[..., None]
    bre = jnp.swapaxes(g_first(b_re), -1, -2)
    bim = jnp.swapaxes(g_first(b_im), -1, -2)
    cre = g_first(c_re)
    cim = g_first(c_im)
    spec3 = lambda s: pl.BlockSpec((None,) + s, lambda g: (g,) + (0,) * len(s))
    k, pin, poutt, lam = pl.pallas_call(
        _s5_prep_kernel,
        grid=(G,),
        in_specs=[spec3((2, P)), spec3((2, P)), spec3((2, 1)),
                  spec3((2, CH, P)), spec3((2, CH, P)), spec3((2, CH, P)), spec3((2, CH, P))],
        out_specs=[spec3((2, T * CH, CH)), spec3((4, T * CH, P)), spec3((4, T * CH, P)), spec3((4, P))],
        out_shape=[jax.ShapeDtypeStruct((G, 2, T * CH, CH), F32),
                   jax.ShapeDtypeStruct((G, 4, T * CH, P), F32),
                   jax.ShapeDtypeStruct((G, 4, T * CH, P), F32),
                   jax.ShapeDtypeStruct((G, 4, P), F32)],
        compiler_params=_cp("parallel"),
        name="s5_weight_prep",
    )(are, aim, ldt, bre, bim, cre, cim)
    k = k.reshape(G, 2, T, CH, CH)
    s_idx = jnp.arange(T)[:, None]
    t_idx = jnp.arange(T)[None, :]
    kf = jnp.where((t_idx >= s_idx)[None, :, :, None, None], k[:, 0][:, jnp.clip(t_idx - s_idx, 0, T - 1)], 0.0)
    kb = jnp.where((t_idx <= s_idx)[None, :, :, None, None], k[:, 1][:, jnp.clip(s_idx - t_idx, 0, T - 1)], 0.0)
    m = (kf + kb).transpose(0, 1, 4, 2, 3)
    GT, G8 = G // S5_TILE_GROUPS, S5_TILE_GROUPS
    m_c = m.reshape(GT, G8, T * CH, T * CH)
    pin_c = pin.transpose(0, 2, 1, 3).reshape(GT, G8, T * CH, 4 * P)
    pout_c = poutt.transpose(0, 1, 3, 2).reshape(GT, G8, 4 * P, T * CH)
    lre = jnp.concatenate([lam[:, 0], lam[:, 0], lam[:, 2], lam[:, 2]], axis=-1)
    lim = jnp.concatenate([-lam[:, 1], lam[:, 1], -lam[:, 3], lam[:, 3]], axis=-1)
    lam_rows = jnp.stack([lre.reshape(GT, G8 * 4 * P), lim.reshape(GT, G8 * 4 * P)], axis=1)
    return m_c.astype(BF16), pin_c.astype(BF16), pout_c.astype(BF16), lam_rows


def _s5_lane_spread():
    T, CH, G8 = S5_CHUNK, S5_CH, S5_TILE_GROUPS
    src = jnp.arange(T * CH)
    dst = jnp.arange(T * G8 * CH)
    same = (src[:, None] // CH == dst[None, :] // (G8 * CH)) & (src[:, None] % CH == dst[None, :] % CH)
    g8_of_dst = (dst // CH) % G8
    return (same[None] & (g8_of_dst[None, None, :] == jnp.arange(G8)[:, None, None])).astype(BF16)


def _s5_chunk_rows(u_ref, bsz, cblk):
    T = S5_CHUNK
    per_b = [jnp.concatenate([u_ref[b, pl.ds(s, cblk, stride=T), :] for s in range(T)], axis=1) for b in range(bsz)]
    return jnp.concatenate(per_b, axis=0).astype(BF16)


def _s5_summary_kernel(u_ref, pin_ref, g_ref, pin_scr, *, bsz, cblk):
    T, CH, G8 = S5_CHUNK, S5_CH, S5_TILE_GROUPS
    SWG = pin_ref.shape[-1]

    @pl.when(pl.program_id(1) == 0)
    def _():
        pin_scr[...] = jnp.zeros_like(pin_scr)
        for g8 in range(G8):
            for s in range(T):
                pin_scr[pl.ds(s * G8 * CH + g8 * CH, CH), pl.ds(g8 * SWG, SWG)] = pin_ref[g8, pl.ds(s * CH, CH), :]

    g = _dot(_s5_chunk_rows(u_ref, bsz, cblk), pin_scr[...])
    for k in range(g_ref.shape[0]):
        for b in range(bsz):
            g_ref[k, pl.ds(b, cblk, stride=bsz), :] = g[b * cblk:(b + 1) * cblk, k * 128:(k + 1) * 128]


def _s5_state_kernel(g_ref, lam_ref, h0_ref, hin_ref, hfin_ref, *, n_chunks, bsz):
    lre = lam_ref[0:1, :]
    lim = lam_ref[1:2, :]
    tiles = [slice(k * 128, (k + 1) * 128) for k in range(4)]

    def body(c, carry):
        hs, hx = carry
        rows = (pl.ds(pl.multiple_of(c * bsz, bsz), bsz), pl.ds(pl.multiple_of((n_chunks - 1 - c) * bsz, bsz), bsz))
        out, outx = [], []
        for k, sl in enumerate(tiles):
            r = rows[k % 2]
            hin_ref[k, r, :] = hs[k]
            g = g_ref[k, r, :]
            out.append(lre[:, sl] * hs[k] + lim[:, sl] * hx[k] + g)
            outx.append(lre[:, sl] * hx[k] - lim[:, sl] * hs[k] + pltpu.roll(g, 64, 1))
        return tuple(out), tuple(outx)

    h0 = tuple(h0_ref[:, sl] for sl in tiles)
    hs, _ = lax.fori_loop(0, n_chunks, body, (h0, tuple(pltpu.roll(h, 64, 1) for h in h0)), unroll=8)
    for k, sl in enumerate(tiles):
        hfin_ref[:, sl] = hs[k]


def _s5_output_kernel(u_ref, hin_ref, m_ref, pout_ref, spread_ref, y_ref, w_scr, pout_scr, *, bsz, cblk):
    T, CH, G8 = S5_CHUNK, S5_CH, S5_TILE_GROUPS
    SWG = pout_ref.shape[-2]

    @pl.when(pl.program_id(1) == 0)
    def _():
        for g8 in range(G8):
            spread = spread_ref[g8]
            wide = _dot(m_ref[g8], spread).astype(BF16)
            for s in range(T):
                w_scr[pl.ds(s * G8 * CH + g8 * CH, CH), :] = wide[s * CH:(s + 1) * CH, :]
            pout_scr[pl.ds(g8 * SWG, SWG), :] = _dot(pout_ref[g8], spread).astype(BF16)

    x = _s5_chunk_rows(u_ref, bsz, cblk)
    hin = jnp.concatenate(
        [jnp.concatenate([hin_ref[k, pl.ds(b, cblk, stride=bsz), :] for k in range(hin_ref.shape[0])], axis=1)
         for b in range(bsz)], axis=0)
    y = _dot(x, w_scr[...]) + _dot(hin.astype(BF16), pout_scr[...])
    for b in range(bsz):
        for s in range(T):
            y_ref[b, pl.ds(s, cblk, stride=T), :] = y[b * cblk:(b + 1) * cblk, s * 128:(s + 1) * 128]


def _s5_scan(u, h0, m_c, pin_c, pout_c, lam_rows, spread):
    bsz, seq, _ = u.shape
    G, P, T = S5_GROUPS, S5_STATE, S5_CHUNK
    GT, G8 = G // S5_TILE_GROUPS, S5_TILE_GROUPS
    SW = G8 * 4 * P
    nc = seq // T
    cblk = min(max(256 // bsz, 8), nc)
    nblk = nc // cblk
    h0g = h0.transpose(2, 0, 1, 4, 3).reshape(GT, G8, bsz, 4 * P).transpose(0, 2, 1, 3).reshape(GT, bsz, SW)
    u_spec = pl.BlockSpec((bsz, cblk * T, 128), lambda x, j: (0, j, x))
    n_tiles = SW // 128
    rows_spec = pl.BlockSpec((None, n_tiles, cblk * bsz, 128), lambda x, j: (x, 0, j, 0))
    per_tile = lambda a: pl.BlockSpec((None,) + a.shape[1:], lambda x, j: (x, 0, 0, 0))
    xw = T * 128
    g = pl.pallas_call(
        functools.partial(_s5_summary_kernel, bsz=bsz, cblk=cblk),
        grid=(GT, nblk),
        in_specs=[u_spec, per_tile(pin_c)],
        scratch_shapes=[pltpu.VMEM((xw, SW), BF16)],
        out_specs=rows_spec,
        out_shape=jax.ShapeDtypeStruct((GT, n_tiles, nc * bsz, 128), F32),
        compiler_params=_cp("parallel", "arbitrary"),
        name="s5_chunk_summary",
    )(u, pin_c)
    quarter = lambda r: pl.BlockSpec((None, r, 512), lambda x, q: (x, 0, q))
    quarter_rows = pl.BlockSpec((None, 4, nc * bsz, 128), lambda x, q: (x, q, 0, 0))
    hin, hfin = pl.pallas_call(
        functools.partial(_s5_state_kernel, n_chunks=nc, bsz=bsz),
        grid=(GT, n_tiles // 4),
        in_specs=[quarter_rows, quarter(2), quarter(bsz)],
        out_specs=[quarter_rows, quarter(bsz)],
        out_shape=[jax.ShapeDtypeStruct((GT, n_tiles, nc * bsz, 128), F32), jax.ShapeDtypeStruct((GT, bsz, SW), F32)],
        compiler_params=_cp("parallel", "parallel"),
        name="s5_state_scan",
    )(g, lam_rows, h0g)
    y = pl.pallas_call(
        functools.partial(_s5_output_kernel, bsz=bsz, cblk=cblk),
        grid=(GT, nblk),
        in_specs=[u_spec, rows_spec, per_tile(m_c), per_tile(pout_c),
                  pl.BlockSpec(spread.shape, lambda x, j: (0, 0, 0))],
        scratch_shapes=[pltpu.VMEM((xw, xw), BF16), pltpu.VMEM((SW, xw), BF16)],
        out_specs=u_spec,
        out_shape=jax.ShapeDtypeStruct((bsz, seq, S5_WIDTH), F32),
        compiler_params=_cp("parallel", "arbitrary"),
        name="s5_chunk_output",
    )(u, hin, m_c, pout_c, spread)
    hfin = hfin.reshape(GT, bsz, G8, 2, 2, P).transpose(1, 3, 0, 2, 5, 4).reshape(bsz, 2, G, P, 2)
    return y, hfin


def _s5_out_kernel(y_ref, u_ref, d_ref, w_ref, nw_ref, o_ref):
    y = y_ref[...] + u_ref[...] * d_ref[...]
    c = math.sqrt(2.0 / math.pi)
    y = y * (0.5 * (1.0 + jnp.tanh(c * (y + 0.044715 * (y * y * y)))))
    z = _dot(y.astype(BF16), w_ref[...])
    o = z[:, :S5_WIDTH] * jax.nn.sigmoid(z[:, S5_WIDTH:])
    o_ref[...] = _rms(o, nw_ref[...]).astype(BF16)


def _s5_out(y2, u2, d_skip, w_glu, nw, tm):
    n = y2.shape[0]
    full = lambda a: pl.BlockSpec(a.shape, lambda i: (0,) * a.ndim)
    row = pl.BlockSpec((tm, S5_WIDTH), lambda i: (i, 0))
    return pl.pallas_call(
        _s5_out_kernel,
        grid=(n // tm,),
        in_specs=[row, row, full(d_skip), full(w_glu), full(nw)],
        out_specs=row,
        out_shape=jax.ShapeDtypeStruct((n, S5_WIDTH), BF16),
        compiler_params=_cp("parallel"),
        name="s5_gelu_glu",
    )(y2, u2, d_skip, w_glu, nw)


def _rwkv_pre_kernel(p_ref, hp_ref, hn_ref, mu_ref, w0_ref, a0_ref, w2_ref, a2_ref, g2_ref,
                     kk_ref, ka_ref, rk_ref, bd_ref,
                     v_ref, nk_ref, dec_ref, kd_ref, bb_ref, qq_ref, vkr_ref, g_ref, bonus_ref, *, tiles_per_seq, tm):
    i = pl.program_id(0)
    j = i % tiles_per_seq
    p = p_ref[...]
    rows = lax.broadcasted_iota(jnp.int32, (tm, 1), 0)
    prev_edge = jnp.where(j == 0, 0.0, hp_ref[7:8, :])
    next_edge = jnp.where(j == tiles_per_seq - 1, 0.0, hn_ref[0:1, :])
    prev = jnp.where(rows == 0, prev_edge, pltpu.roll(p, 1, 0))
    nxt = jnp.where(rows == tm - 1, next_edge, pltpu.roll(p, tm - 1, 0))
    p = p + mu_ref[...] * (0.5 * (prev + nxt) - p)
    W = RWKV_WIDTH
    r = p[:, 0:W]
    k = p[:, W:2 * W]
    v = p[:, 2 * W:3 * W]
    wl = p[:, 3 * W:3 * W + LORA_PAD]
    al = p[:, 3 * W + LORA_PAD:3 * W + 2 * LORA_PAD]
    gl = p[:, 3 * W + 2 * LORA_PAD:3 * W + 3 * LORA_PAD]
    bd = bd_ref[...]
    kk = k * kk_ref[...]
    kk = kk * lax.rsqrt(_segsum(kk * kk, bd) + 1e-12)
    g_ref[...] = _dot(jax.nn.sigmoid(gl).astype(BF16), g2_ref[...])
    tw = jnp.tanh(wl).astype(BF16)
    alb = al.astype(BF16)
    ksum = None
    for d in range(2):
        z = -(w0_ref[d:d + 1, :] + _dot(tw, w2_ref[d]))
        w = -(jnp.maximum(z, 0.0) + jnp.log(1.0 + jnp.exp(-jnp.abs(z)))) - 0.5
        dec = jnp.exp(-jnp.exp(w))
        dec_ref[d] = dec
        a = jax.nn.sigmoid(a0_ref[d:d + 1, :] + _dot(alb, a2_ref[d]))
        kd = k * (1.0 + (a - 1.0) * ka_ref[...])
        kd_ref[d] = kd
        bb = kk * a
        bb_ref[d] = bb
        qq_ref[d] = dec * r - kk * _segsum(bb * r, bd)
        vkr_ref[d] = v * _segsum(kd * r, bd)
        ksum = kd if ksum is None else ksum + kd
    v_ref[...] = v
    nk_ref[...] = -kk
    bonus_ref[...] = _segsum(r * ksum * rk_ref[...], bd) * v


def _rwkv_pre(p2, bsz, seq, wts, bd):
    n = p2.shape[0]
    tm = min(256, seq)
    tps = seq // tm
    r8 = tm // 8
    nblk8 = n // 8
    full = lambda a: pl.BlockSpec(a.shape, lambda i: (0,) * a.ndim)
    o3 = pl.BlockSpec((None, tm, RWKV_WIDTH), lambda i: (i // tps, i % tps, 0))
    o4 = pl.BlockSpec((2, None, tm, RWKV_WIDTH), lambda i: (0, i // tps, i % tps, 0))
    s3 = jax.ShapeDtypeStruct((bsz, seq, RWKV_WIDTH), F32)
    s4 = jax.ShapeDtypeStruct((2, bsz, seq, RWKV_WIDTH), F32)
    names = ("mu", "w0", "a0", "w2", "a2", "g2", "k_k", "k_a", "r_k")
    return pl.pallas_call(
        functools.partial(_rwkv_pre_kernel, tiles_per_seq=tps, tm=tm),
        grid=(n // tm,),
        in_specs=[pl.BlockSpec((tm, RWKV_COLS), lambda i: (i, 0)),
                  pl.BlockSpec((8, RWKV_COLS), lambda i: (jnp.maximum(i * r8 - 1, 0), 0)),
                  pl.BlockSpec((8, RWKV_COLS), lambda i: (jnp.minimum((i + 1) * r8, nblk8 - 1), 0))]
                 + [full(wts[k]) for k in names] + [full(bd)],
        out_specs=[o3, o3, o4, o4, o4, o4, o4, o3, o3],
        out_shape=[s3, s3, s4, s4, s4, s4, s4, s3, s3],
        compiler_params=_cp("parallel"),
        name="rwkv_prepare",
    )(p2, p2, p2, *[wts[k] for k in names], bd)


def _rwkv_scan_kernel(nkf, nkb, vf, vb, decf, decb, kdf, kdb, bbf, bbb, qqf, qqb, vkf, vkb, s0_ref,
                      yf_ref, yb_ref, sfin_ref, s_scr, v8_scr, v8k_scr, y8_scr, *, nb, tlen, n_chunks):
    c = pl.program_id(1)

    @pl.when(c == 0)
    def _():
        s_scr[...] = s0_ref[...].reshape(s_scr.shape)

    N = RWKV_HEAD_DIM
    W = RWKV_WIDTH
    H = RWKV_HEADS
    nt = (((1,), (1,)), ((), ()))
    chains = [(d, n) for d in range(2) for n in range(nb)]
    pick = lambda d, f, b: f if d == 0 else b
    base = lambda m: m * tlen * H

    left = lambda rows: lax.broadcasted_iota(jnp.int32, (rows, 128), 1) < N
    for m, (d, n) in enumerate(chains):
        for src, dst in ((pick(d, vf, vb), v8_scr), (pick(d, vkf, vkb), v8k_scr)):
            for p in range(H // 2):
                tile = src[n, :, p * 128:(p + 1) * 128]
                dst[pl.ds(base(m) + 2 * p, tlen, stride=H), :] = tile
                dst[pl.ds(base(m) + 2 * p + 1, tlen, stride=H), :] = pltpu.roll(tile, N, 1)

    @pl.when(c == 0)
    def _():
        y8_scr[...] = jnp.zeros_like(y8_scr)

    hmask = (lax.broadcasted_iota(jnp.int32, (H, W), 1) // N == lax.broadcasted_iota(jnp.int32, (H, W), 0)).astype(F32)
    tn = (((0,), (0,)), ((), ()))

    def step(t, carry):
        tts = (t, tlen - 1 - t)
        row = lambda ref, n, tt: ref[n, pl.ds(tt, 1), :] * hmask
        tile8 = lambda m, tt: pl.ds(pl.multiple_of(base(m) + tt * H, H), H)
        os_ = []
        for m, (d, n) in enumerate(chains):
            tt = tts[d]
            rows = jnp.concatenate([row(pick(d, nkf, nkb), n, tt), row(pick(d, qqf, qqb), n, tt)], axis=0).astype(BF16)
            os_.append(lax.dot_general(rows, s_scr[m].astype(BF16), nt, preferred_element_type=F32))
        for m, (d, n) in enumerate(chains):
            tt = tts[d]
            sa_v = jnp.concatenate([os_[m][0:H], v8_scr[tile8(m, tt), :][:, 0:N]], axis=0).astype(BF16)
            w2 = jnp.concatenate([row(pick(d, bbf, bbb), n, tt), row(pick(d, kdf, kdb), n, tt)], axis=0).astype(BF16)
            s_scr[m] = s_scr[m] * pick(d, decf, decb)[n, pl.ds(tt, 1), :] + lax.dot_general(
                sa_v, w2, tn, preferred_element_type=F32)
        for m, (d, n) in enumerate(chains):
            y8_scr[tile8(m, tts[d]), 0:N] = os_[m][H:2 * H] + v8k_scr[tile8(m, tts[d]), :][:, 0:N]
        return carry

    lax.fori_loop(0, tlen, step, 0, unroll=8)

    for m, (d, n) in enumerate(chains):
        y_ref = pick(d, yf_ref, yb_ref)
        for p in range(H // 2):
            even = y8_scr[pl.ds(base(m) + 2 * p, tlen, stride=H), :]
            odd = pltpu.roll(y8_scr[pl.ds(base(m) + 2 * p + 1, tlen, stride=H), :], N, 1)
            y_ref[n, :, p * 128:(p + 1) * 128] = jnp.where(left(tlen), even, odd)

    @pl.when(c == n_chunks - 1)
    def _():
        sfin_ref[...] = s_scr[...].reshape(sfin_ref.shape)


def _rwkv_scan(nk, v, dec, kd, bb, qq, vkr, s0):
    bsz, seq, W = nk.shape
    N, H = RWKV_HEAD_DIM, RWKV_HEADS
    nb = 8 if bsz % 8 == 0 else 4
    tlen = min(32, seq)
    nc = seq // tlen
    fwd = pl.BlockSpec((nb, tlen, W), lambda b, c: (b, c, 0))
    bwd = pl.BlockSpec((nb, tlen, W), lambda b, c: (b, nc - 1 - c, 0))
    fwd_d = pl.BlockSpec((None, nb, tlen, W), lambda b, c: (0, b, c, 0))
    bwd_d = pl.BlockSpec((None, nb, tlen, W), lambda b, c: (1, b, nc - 1 - c, 0))
    st = pl.BlockSpec((2, nb, N, W), lambda b, c: (0, b, 0, 0))
    tiles = pltpu.VMEM((2 * nb * tlen * H, 128), F32)
    return pl.pallas_call(
        functools.partial(_rwkv_scan_kernel, nb=nb, tlen=tlen, n_chunks=nc),
        grid=(bsz // nb, nc),
        in_specs=[fwd, bwd, fwd, bwd] + [fwd_d, bwd_d] * 5 + [st],
        out_specs=[fwd, bwd, st],
        out_shape=[jax.ShapeDtypeStruct((bsz, seq, W), F32)] * 2 + [jax.ShapeDtypeStruct((2, bsz, N, W), F32)],
        scratch_shapes=[pltpu.VMEM((2 * nb, N, W), F32), tiles, tiles, tiles],
        compiler_params=_cp("parallel", "arbitrary"),
        name="rwkv_scan",
    )(nk, nk, v, v, dec, dec, kd, kd, bb, bb, qq, qq, vkr, vkr, s0)


def _rwkv_post_kernel(yf_ref, yb_ref, bonus_ref, g_ref, lw_ref, lb_ref, bd_ref, o_ref):
    bd = bd_ref[...]
    y = yf_ref[...] + yb_ref[...]
    inv_n = 1.0 / RWKV_HEAD_DIM
    mean = _segsum(y, bd) * inv_n
    yc = y - mean
    var = _segsum(yc * yc, bd) * inv_n
    yn = yc * lax.rsqrt(var + GN_EPS) * lw_ref[...] + lb_ref[...]
    o_ref[...] = ((yn + bonus_ref[...]) * g_ref[...]).astype(BF16)


def _rwkv_post(yf, yb, bonus, g, ln_w, ln_b, bd):
    bsz, seq, W = yf.shape
    tm = min(256, seq)
    tps = seq // tm
    n = bsz * seq
    full = lambda a: pl.BlockSpec(a.shape, lambda i: (0,) * a.ndim)
    i3 = pl.BlockSpec((None, tm, W), lambda i: (i // tps, i % tps, 0))
    return pl.pallas_call(
        _rwkv_post_kernel,
        grid=(n // tm,),
        in_specs=[i3, i3, i3, i3, full(ln_w), full(ln_b), full(bd)],
        out_specs=pl.BlockSpec((tm, W), lambda i: (i, 0)),
        out_shape=jax.ShapeDtypeStruct((n, W), BF16),
        compiler_params=_cp("parallel"),
        name="rwkv_groupnorm_gate",
    )(yf, yb, bonus, g, ln_w, ln_b, bd)


def _store_k_heads(k_ref, kn, kpe):
    for h in range(MLA_HEADS):
        k_ref[:, h * MLA_QK_PAD:h * MLA_QK_PAD + 128] = kn[:, h * 128:(h + 1) * 128].astype(BF16)
        k_ref[:, h * MLA_QK_PAD + 128:(h + 1) * MLA_QK_PAD] = kpe


def _mla_prep_kernel(*refs, rope):
    if rope:
        (p_ref, qn_ref, kvn_ref, wq_ref, wqr_ref, wk_ref, wv_ref, cq_ref, sq_ref, ck_ref, sk_ref,
         q_ref, k_ref, v_ref) = refs
    else:
        (p_ref, qn_ref, kvn_ref, wq_ref, wk_ref, wv_ref, q_ref, k_ref, v_ref, ckv_ref, kr_ref) = refs
    p = p_ref[...]
    qn = _rms(p[:, 0:MLA_Q_RANK], qn_ref[...]).astype(BF16)
    q = _dot(qn, wq_ref[...])
    ckv = _rms(p[:, MLA_Q_RANK:MLA_Q_RANK + MLA_KV_RANK], kvn_ref[...])
    kr = p[:, MLA_Q_RANK + MLA_KV_RANK:MLA_Q_RANK + MLA_KV_RANK + 128]
    if rope:
        cq = jnp.concatenate([cq_ref[...]] * MLA_HEADS, axis=1)
        sq = jnp.concatenate([sq_ref[...]] * MLA_HEADS, axis=1)
        q = q * cq + _dot(qn, wqr_ref[...]) * sq
        krot = p[:, MLA_Q_RANK + MLA_KV_RANK + 128:MLA_Q_RANK + MLA_KV_RANK + 256]
        kpe = kr * ck_ref[...] + krot * sk_ref[...]
    else:
        kpe = kr
        ckv_ref[...] = ckv
        kr_ref[...] = kr[:, 0:MLA_ROPE_DIM]
    q_ref[...] = q.astype(BF16)
    cb = ckv.astype(BF16)
    _store_k_heads(k_ref, _dot(cb, wk_ref[...]), kpe.astype(BF16))
    v_ref[...] = _dot(cb, wv_ref[...]).astype(BF16)


def _mla_prep(p2, seq, wts, tables):
    n = p2.shape[0]
    rope = tables is not None
    tm = min(256, seq)
    tps = seq // tm
    full = lambda a: pl.BlockSpec(a.shape, lambda i: (0,) * a.ndim)
    row = lambda w: pl.BlockSpec((tm, w), lambda i: (i, 0))
    ins = [p2, wts["q_norm"], wts["kv_norm"], wts["wq"]]
    specs = [row(MLA_COLS), full(wts["q_norm"]), full(wts["kv_norm"]), full(wts["wq"])]
    if rope:
        ins.append(wts["wq_rot"])
        specs.append(full(wts["wq_rot"]))
    ins += [wts["wk"], wts["wv"]]
    specs += [full(wts["wk"]), full(wts["wv"])]
    outs = [row(MLA_HEADS * MLA_QK_PAD), row(MLA_HEADS * MLA_QK_PAD), row(MLA_HEADS * MLA_V_DIM)]
    shapes = [jax.ShapeDtypeStruct((n, MLA_HEADS * MLA_QK_PAD), BF16),
              jax.ShapeDtypeStruct((n, MLA_HEADS * MLA_QK_PAD), BF16),
              jax.ShapeDtypeStruct((n, MLA_HEADS * MLA_V_DIM), BF16)]
    if rope:
        ins += list(tables)
        specs += [pl.BlockSpec((tm, t.shape[1]), lambda i: (i % tps, 0)) for t in tables]
    else:
        outs += [row(MLA_KV_RANK), row(MLA_ROPE_DIM)]
        shapes += [jax.ShapeDtypeStruct((n, MLA_KV_RANK), F32), jax.ShapeDtypeStruct((n, MLA_ROPE_DIM), F32)]
    return pl.pallas_call(
        functools.partial(_mla_prep_kernel, rope=rope),
        grid=(n // tm,),
        in_specs=specs, out_specs=outs, out_shape=shapes,
        compiler_params=_cp("parallel"),
        name="mla_prepare_rope" if rope else "mla_prepare",
    )(*ins)


def _mla_cache_kernel(ckv_ref, kr_ref, wk_ref, wv_ref, k_ref, v_ref):
    cb = ckv_ref[...].astype(BF16)
    _store_k_heads(k_ref, _dot(cb, wk_ref[...]), kr_ref[...].astype(BF16))
    v_ref[...] = _dot(cb, wv_ref[...]).astype(BF16)


def _mla_cache(cache_ckv, cache_kr_pad, layer, wk, wv):
    bsz, _, past, _ = cache_ckv.shape
    full = lambda a: pl.BlockSpec(a.shape, lambda b: (0,) * a.ndim)
    return pl.pallas_call(
        _mla_cache_kernel,
        grid=(bsz,),
        in_specs=[pl.BlockSpec((None, None, past, MLA_KV_RANK), lambda b: (b, layer, 0, 0)),
                  pl.BlockSpec((None, None, past, 128), lambda b: (b, layer, 0, 0)),
                  full(wk), full(wv)],
        out_specs=[pl.BlockSpec((None, past, MLA_HEADS * MLA_QK_PAD), lambda b: (b, 0, 0)),
                   pl.BlockSpec((None, past, MLA_HEADS * MLA_V_DIM), lambda b: (b, 0, 0))],
        out_shape=[jax.ShapeDtypeStruct((bsz, past, MLA_HEADS * MLA_QK_PAD), BF16),
                   jax.ShapeDtypeStruct((bsz, past, MLA_HEADS * MLA_V_DIM), BF16)],
        compiler_params=_cp("parallel"),
        name="mla_cache_keys",
    )(cache_ckv, cache_kr_pad, wk, wv)


def _attn_kernel(*refs, cache, scale):
    if cache:
        q_ref, k_ref, v_ref, kc_ref, vc_ref, o_ref = refs
    else:
        q_ref, k_ref, v_ref, o_ref = refs
    nt = (((1,), (1,)), ((), ()))
    qk = lambda j: pl.ds(j * MLA_QK_PAD, MLA_QK_PAD)
    vv = lambda j: pl.ds(j * MLA_V_DIM, MLA_V_DIM)
    heads = range(ATTN_HEADS_PER_STEP)
    s = [lax.dot_general(q_ref[:, qk(j)], k_ref[:, qk(j)], nt, preferred_element_type=F32) * scale for j in heads]
    if cache:
        sc = [lax.dot_general(q_ref[:, qk(j)], kc_ref[:, qk(j)], nt, preferred_element_type=F32) * scale for j in heads]
    for j in heads:
        m = jnp.max(s[j], axis=-1, keepdims=True)
        if cache:
            m = jnp.maximum(m, jnp.max(sc[j], axis=-1, keepdims=True))
        e = jnp.exp(s[j] - m)
        den = jnp.sum(e, axis=-1, keepdims=True)
        o = _dot(e.astype(BF16), v_ref[:, vv(j)])
        if cache:
            ec = jnp.exp(sc[j] - m)
            den = den + jnp.sum(ec, axis=-1, keepdims=True)
            o = o + _dot(ec.astype(BF16), vc_ref[:, vv(j)])
        o_ref[:, vv(j)] = o / den


def _attention(q, k, v, kc=None, vc=None):
    bsz, seq, _ = q.shape
    tq = min(ATTN_Q_TILE, seq)
    cache = kc is not None
    scale = float(MLA_NOPE_DIM + MLA_ROPE_DIM) ** -0.5
    hq, hv = ATTN_HEADS_PER_STEP * MLA_QK_PAD, ATTN_HEADS_PER_STEP * MLA_V_DIM
    ins = [q, k, v]
    specs = [pl.BlockSpec((None, tq, hq), lambda b, h, i: (b, i, h)),
             pl.BlockSpec((None, seq, hq), lambda b, h, i: (b, 0, h)),
             pl.BlockSpec((None, seq, hv), lambda b, h, i: (b, 0, h))]
    if cache:
        past = kc.shape[1]
        ins += [kc, vc]
        specs += [pl.BlockSpec((None, past, hq), lambda b, h, i: (b, 0, h)),
                  pl.BlockSpec((None, past, hv), lambda b, h, i: (b, 0, h))]
    return pl.pallas_call(
        functools.partial(_attn_kernel, cache=cache, scale=scale),
        grid=(bsz, MLA_HEADS // ATTN_HEADS_PER_STEP, seq // tq),
        in_specs=specs,
        out_specs=pl.BlockSpec((None, tq, hv), lambda b, h, i: (b, i, h)),
        out_shape=jax.ShapeDtypeStruct((bsz, seq, MLA_HEADS * MLA_V_DIM), F32),
        compiler_params=_cp("parallel", "parallel", "arbitrary"),
        name="mla_attention_cached" if cache else "mla_attention",
    )(*ins)


def _outproj_kernel(ys_ref, yr_ref, ym_ref, x_ref, mod_ref, nm_ref, w_ref, o_ref):
    ym = _rms(ym_ref[...], nm_ref[...]).astype(BF16)
    acc = _dot(ys_ref[...], w_ref[0:S5_WIDTH, :])
    acc += _dot(yr_ref[...], w_ref[S5_WIDTH:S5_WIDTH + RWKV_WIDTH, :])
    acc += _dot(ym, w_ref[S5_WIDTH + RWKV_WIDTH:, :])
    o_ref[...] = x_ref[...] + mod_ref[2:3, :] * acc


def _outproj(ys, yr, ym, x2, mod, layer, row_of_tile, tm, nm, w_out):
    n = x2.shape[0]
    full = lambda a: pl.BlockSpec(a.shape, lambda i: (0,) * a.ndim)
    row = lambda w: pl.BlockSpec((tm, w), lambda i: (i, 0))
    return pl.pallas_call(
        _outproj_kernel,
        grid=(n // tm,),
        in_specs=[row(S5_WIDTH), row(RWKV_WIDTH), row(MLA_WIDTH), row(D_MODEL), _mod_spec(layer, row_of_tile),
                  full(nm), full(w_out)],
        out_specs=row(D_MODEL),
        out_shape=jax.ShapeDtypeStruct((n, D_MODEL), F32),
        compiler_params=_cp("parallel"),
        name="out_projection",
    )(ys, yr, ym, x2, mod, nm, w_out)


def _mlp_kernel(x_ref, nw_ref, mod_ref, w1_ref, w2_ref, nf_ref, o_ref, h_scr, *, final_norm):
    j = pl.program_id(1)

    @pl.when(j == 0)
    def _():
        h = _rms(x_ref[...], nw_ref[...]) * (1.0 + mod_ref[4:5, :]) + mod_ref[3:4, :]
        h_scr[...] = h.astype(BF16)
        o_ref[...] = jnp.zeros_like(o_ref)

    a = _dot(h_scr[...], w1_ref[...])
    a = jnp.square(jnp.maximum(a, 0.0)).astype(BF16)
    o_ref[...] += _dot(a, w2_ref[...])

    @pl.when(j == pl.num_programs(1) - 1)
    def _():
        y = x_ref[...] + mod_ref[5:6, :] * o_ref[...]
        if final_norm:
            y = _rms(y, nf_ref[...])
        o_ref[...] = y


def _mlp(x2, nw, mod, layer, row_of_tile, tm, w1, w2, nf, final_norm):
    n = x2.shape[0]
    tf = MLP_FF_TILE
    full = lambda a: pl.BlockSpec(a.shape, lambda i, j: (0,) * a.ndim)
    return pl.pallas_call(
        functools.partial(_mlp_kernel, final_norm=final_norm),
        grid=(n // tm, D_FF // tf),
        in_specs=[pl.BlockSpec((tm, D_MODEL), lambda i, j: (i, 0)), full(nw), _mod_spec(layer, row_of_tile),
                  pl.BlockSpec((D_MODEL, tf), lambda i, j: (0, j)),
                  pl.BlockSpec((tf, D_MODEL), lambda i, j: (j, 0)), full(nf)],
        out_specs=pl.BlockSpec((tm, D_MODEL), lambda i, j: (i, 0)),
        out_shape=jax.ShapeDtypeStruct((n, D_MODEL), F32),
        scratch_shapes=[pltpu.VMEM((tm, D_MODEL), BF16)],
        compiler_params=_cp("parallel", "arbitrary"),
        name="mlp_final" if final_norm else "mlp",
    )(x2, nw, mod, w1, w2, nf)


def _rope_tables(length):
    rows = length // GRID_W
    row_pos = jnp.repeat(jnp.arange(rows, dtype=F32), GRID_W)
    col_pos = jnp.tile(jnp.arange(GRID_W, dtype=F32), rows)
    axis_dim = MLA_ROPE_DIM // 2
    inv_freq = 1.0 / (ROPE_THETA ** (jnp.arange(0, axis_dim, 2, dtype=F32) / axis_dim))
    ang_r = row_pos[:, None] * inv_freq[None, :]
    ang_c = col_pos[:, None] * inv_freq[None, :]
    ang = jnp.concatenate([ang_r, ang_r, ang_c, ang_c], axis=-1)
    cos, sin = jnp.cos(ang), jnp.sin(ang)
    z64 = jnp.zeros((length, 64), F32)
    cos_q = jnp.concatenate([jnp.ones((length, MLA_NOPE_DIM), F32), cos, z64], axis=1)
    sin_q = jnp.concatenate([jnp.zeros((length, MLA_NOPE_DIM), F32), sin, z64], axis=1)
    cos_k = jnp.concatenate([cos, z64], axis=1)
    sin_k = jnp.concatenate([sin, z64], axis=1)
    return cos_q, sin_q, cos_k, sin_k


def _rot_cols(w):
    a, b, c, d = w[..., 0:16], w[..., 16:32], w[..., 32:48], w[..., 48:64]
    return jnp.concatenate([-b, a, -d, c], axis=-1)


def _layer_weights(l, p):
    d = D_MODEL
    w_in = p["w_in"][l]
    z64 = jnp.zeros((d, 64), F32)
    o = S5_WIDTH
    rk = w_in[:, o:o + 3 * RWKV_WIDTH]
    o += 3 * RWKV_WIDTH
    wl, al, gl = w_in[:, o:o + 64], w_in[:, o + 64:o + 128], w_in[:, o + 128:o + 256]
    o += 256
    cq, ckv, kr = w_in[:, o:o + 512], w_in[:, o + 512:o + 768], w_in[:, o + 768:o + 832]
    mu = p["rwkv_mu"][l]
    z1 = jnp.zeros((64,), F32)
    mu_pad = jnp.concatenate([mu[:1536], mu[1536:1600], z1, mu[1600:1664], z1, mu[1664:1792]])[None]
    pad_rows = lambda w: jnp.concatenate([w, jnp.zeros_like(w)], axis=-2)
    w_uq = p["mla_w_uq"][l].reshape(MLA_Q_RANK, MLA_HEADS, MLA_NOPE_DIM + MLA_ROPE_DIM)
    zq = jnp.zeros((MLA_Q_RANK, MLA_HEADS, 64), F32)
    wq = jnp.concatenate([w_uq, zq], axis=-1).reshape(MLA_Q_RANK, MLA_HEADS * MLA_QK_PAD)
    wq_rot = jnp.concatenate([jnp.zeros((MLA_Q_RANK, MLA_HEADS, MLA_NOPE_DIM), F32),
                              _rot_cols(w_uq[..., MLA_NOPE_DIM:]), zq], axis=-1).reshape(MLA_Q_RANK, MLA_HEADS * MLA_QK_PAD)
    w_ukv = p["mla_w_ukv"][l].reshape(MLA_KV_RANK, MLA_HEADS, MLA_NOPE_DIM + MLA_V_DIM)
    row = lambda a: a.reshape(1, -1)
    return {
        "norm_mix": row(p["norm_mix"][l]), "norm_mlp": row(p["norm_mlp"][l]),
        "w_s5": w_in[:, 0:S5_WIDTH].astype(BF16),
        "w_rwkv": jnp.concatenate([rk, wl, z64, al, z64, gl], axis=1).astype(BF16),
        "w_mla": jnp.concatenate([cq, ckv, kr, z64, _rot_cols(kr), z64], axis=1).astype(BF16),
        "w_out": p["w_out"][l].astype(BF16),
        "s5_d": row(p["s5_d"][l]), "s5_w_glu": p["s5_w_glu"][l].astype(BF16), "s5_out_norm": row(p["s5_out_norm"][l]),
        "rwkv": {
            "mu": mu_pad, "w0": p["rwkv_w0"][l], "a0": p["rwkv_a0"][l],
            "w2": pad_rows(p["rwkv_w2"][l]).astype(BF16), "a2": pad_rows(p["rwkv_a2"][l]).astype(BF16),
            "g2": p["rwkv_g2"][l].astype(BF16), "k_k": row(p["rwkv_k_k"][l]), "k_a": row(p["rwkv_k_a"][l]),
            "r_k": row(p["rwkv_r_k"][l]),
        },
        "rwkv_ln_w": row(p["rwkv_ln_w"][l]), "rwkv_ln_b": row(p["rwkv_ln_b"][l]),
        "mla": {
            "q_norm": row(p["mla_q_norm"][l]), "kv_norm": row(p["mla_kv_norm"][l]),
            "wq": wq.astype(BF16), "wq_rot": wq_rot.astype(BF16),
            "wk": w_ukv[..., :MLA_NOPE_DIM].reshape(MLA_KV_RANK, -1).astype(BF16),
            "wv": w_ukv[..., MLA_NOPE_DIM:].reshape(MLA_KV_RANK, -1).astype(BF16),
        },
        "mla_out_norm": row(p["mla_out_norm"][l]),
        "mlp_w1": p["mlp_w1"][l].astype(BF16), "mlp_w2": p["mlp_w2"][l].astype(BF16),
    }


def _trunk_layer(x, mod, layer, lw, s5w, bd, row_of_tile_fn, s5_h0, rwkv_s0, cache, tables, norm_final, final_norm):
    bsz, seq, d = x.shape
    n = bsz * seq
    x2 = x.reshape(n, d)
    tm = min(512, seq) if cache is not None else min(512, n)
    row_of_tile = row_of_tile_fn(tm)
    u2, pr2, pm2 = _inproj(x2, lw["norm_mix"], mod, layer, row_of_tile, tm, lw["w_s5"], lw["w_rwkv"], lw["w_mla"])

    ys, s5_fin = _s5_scan(u2.reshape(bsz, seq, S5_WIDTH), s5_h0, *s5w)
    ys2 = _s5_out(ys.reshape(n, S5_WIDTH), u2, lw["s5_d"], lw["s5_w_glu"], lw["s5_out_norm"], min(512, n))

    v, nk, dec, kd, bb, qq, vkr, g, bonus = _rwkv_pre(pr2, bsz, seq, lw["rwkv"], bd)
    s0 = rwkv_s0.transpose(1, 0, 3, 2, 4).reshape(2, bsz, RWKV_HEAD_DIM, RWKV_WIDTH)
    yf, yb, s_fin = _rwkv_scan(nk, v, dec, kd, bb, qq, vkr, s0)
    yr2 = _rwkv_post(yf, yb, bonus, g, lw["rwkv_ln_w"], lw["rwkv_ln_b"], bd)
    rwkv_fin = s_fin.reshape(2, bsz, RWKV_HEAD_DIM, RWKV_HEADS, RWKV_HEAD_DIM).transpose(1, 0, 3, 2, 4)

    shape3 = lambda a: a.reshape(bsz, seq, a.shape[-1])
    if cache is None:
        q, k, v_, ckv_n, k_rope = _mla_prep(pm2, seq, lw["mla"], None)
        ym = _attention(shape3(q), shape3(k), shape3(v_))
        extras = (shape3(ckv_n), shape3(k_rope), s5_fin, rwkv_fin)
    else:
        q, k, v_ = _mla_prep(pm2, seq, lw["mla"], tables)
        kc, vc = _mla_cache(cache[0], cache[1], layer, lw["mla"]["wk"], lw["mla"]["wv"])
        ym = _attention(shape3(q), shape3(k), shape3(v_), kc, vc)
        extras = None

    x2 = _outproj(ys2, yr2, ym.reshape(n, MLA_WIDTH), x2, mod, layer, row_of_tile, tm, lw["mla_out_norm"], lw["w_out"])
    tm_mlp = min(MLP_ROW_TILE, seq) if cache is not None else min(MLP_ROW_TILE, n)
    x2 = _mlp(x2, lw["norm_mlp"], mod, layer, row_of_tile_fn(tm_mlp), tm_mlp, lw["mlp_w1"], lw["mlp_w2"], norm_final,
              final_norm)
    return x2.reshape(bsz, seq, d), extras


def kernel(x_prompt, x_sample, cache_mla_ckv, cache_mla_krope, state_s5, state_rwkv, c, c_ctx, norm_mix, norm_mlp, norm_final, w_ada, b_ada, w_in, w_out, s5_a_re, s5_a_im, s5_log_dt, s5_b_re, s5_b_im, s5_c_re, s5_c_im, s5_d, s5_w_glu, s5_out_norm, rwkv_mu, rwkv_w0, rwkv_w2, rwkv_a0, rwkv_a2, rwkv_g2, rwkv_k_k, rwkv_k_a, rwkv_r_k, rwkv_ln_w, rwkv_ln_b, mla_q_norm, mla_w_uq, mla_kv_norm, mla_w_ukv, mla_out_norm, mlp_w1, mlp_w2):
    p = dict(norm_mix=norm_mix, norm_mlp=norm_mlp, w_in=w_in, w_out=w_out, s5_d=s5_d, s5_w_glu=s5_w_glu,
             s5_out_norm=s5_out_norm, rwkv_mu=rwkv_mu, rwkv_w0=rwkv_w0, rwkv_w2=rwkv_w2, rwkv_a0=rwkv_a0,
             rwkv_a2=rwkv_a2, rwkv_g2=rwkv_g2, rwkv_k_k=rwkv_k_k, rwkv_k_a=rwkv_k_a, rwkv_r_k=rwkv_r_k,
             rwkv_ln_w=rwkv_ln_w, rwkv_ln_b=rwkv_ln_b, mla_q_norm=mla_q_norm, mla_w_uq=mla_w_uq,
             mla_kv_norm=mla_kv_norm, mla_w_ukv=mla_w_ukv, mla_out_norm=mla_out_norm, mlp_w1=mlp_w1, mlp_w2=mlp_w2)
    depth = w_in.shape[0]
    b_ctx, l_ctx, d = x_prompt.shape
    b_dec, l_dec, _ = x_sample.shape

    rows = -(-(1 + b_dec) // 8) * 8
    cond = jnp.zeros((rows, d), F32).at[0].set(c_ctx).at[1:1 + b_dec].set(c)
    mod = _modulation(cond, w_ada, b_ada).reshape(depth, rows, N_MOD, d)

    bd = jnp.kron(jnp.eye(4, dtype=F32), jnp.ones((64, 64), F32)).astype(BF16)
    spread = _s5_lane_spread()
    tables = _rope_tables(l_dec)
    kr_pad = jnp.pad(cache_mla_krope, ((0, 0), (0, 0), (0, 0), (0, 128 - MLA_ROPE_DIM)))
    zero_s5 = jnp.zeros((b_ctx, 2, S5_GROUPS, S5_STATE, 2), F32)
    zero_rwkv = jnp.zeros((b_ctx, 2, RWKV_HEADS, RWKV_HEAD_DIM, RWKV_HEAD_DIM), F32)
    nf = norm_final.reshape(1, d)

    ctx_rows = lambda tm: (lambda i: 0)
    dec_rows = lambda tm: (lambda i: 1 + i // (l_dec // tm))

    xp, xs = x_prompt, x_sample
    new_ckv, new_krope, new_s5, new_rwkv = [], [], [], []
    for l in range(depth):
        lw = _layer_weights(l, p)
        s5w = _s5_prep(s5_a_re[l], s5_a_im[l], s5_log_dt[l], s5_b_re[l], s5_b_im[l], s5_c_re[l], s5_c_im[l]) + (spread,)
        last = l == depth - 1
        xp, (ckv_l, krope_l, s5_l, rwkv_l) = _trunk_layer(
            xp, mod, l, lw, s5w, bd, ctx_rows, zero_s5, zero_rwkv, None, None, nf, last)
        new_ckv.append(ckv_l)
        new_krope.append(krope_l)
        new_s5.append(s5_l)
        new_rwkv.append(rwkv_l)
        xs, _ = _trunk_layer(
            xs, mod, l, lw, s5w, bd, dec_rows, state_s5[:, l], state_rwkv[:, l], (cache_mla_ckv, kr_pad), tables, nf, last)
    return (xp, xs, jnp.stack(new_ckv, axis=1), jnp.stack(new_krope, axis=1),
            jnp.stack(new_s5, axis=1), jnp.stack(new_rwkv, axis=1))
```

```python
import functools
import math

import jax
import jax.numpy as jnp
from jax import lax
from jax.experimental import pallas as pl
from jax.experimental.pallas import tpu as pltpu

F32 = jnp.float32
BF16 = jnp.bfloat16

D_MODEL = 2048
N_MOD = 6
GRID_W = 64
S5_WIDTH = 512
S5_CH = 16
S5_GROUPS = 32
S5_STATE = 64
S5_CHUNK = 8
S5_TILE_GROUPS = 8
RWKV_WIDTH = 512
RWKV_HEAD_DIM = 64
RWKV_HEADS = 8
LORA_PAD = 128
RWKV_COLS = 3 * RWKV_WIDTH + 3 * LORA_PAD
MLA_HEADS = 8
MLA_V_DIM = 128
MLA_NOPE_DIM = 128
MLA_ROPE_DIM = 64
MLA_QK_PAD = 256
MLA_Q_RANK = 512
MLA_KV_RANK = 256
MLA_WIDTH = 1024
MLA_COLS = MLA_Q_RANK + MLA_KV_RANK + 2 * 128
D_FF = 8192
MLP_ROW_TILE = 512
MLP_FF_TILE = 1024
ATTN_Q_TILE = 256
ATTN_HEADS_PER_STEP = 4
ROPE_THETA = 10000.0
NORM_EPS = 1e-6
GN_EPS = 64e-5

VMEM_LIMIT_BYTES = 56 * 1024 * 1024


def _cp(*sem):
    return pltpu.CompilerParams(dimension_semantics=sem, vmem_limit_bytes=VMEM_LIMIT_BYTES)


def _dot(a, b):
    return jnp.dot(a, b, preferred_element_type=F32)


def _rms(x, g):
    ms = jnp.mean(x * x, axis=-1, keepdims=True)
    return x * lax.rsqrt(ms + NORM_EPS) * g


def _split_bf16(x):
    hi = x.astype(BF16)
    lo = (x - hi.astype(F32)).astype(BF16)
    return hi, lo


def _segsum(x, bd):
    hi, lo = _split_bf16(x)
    left = _dot(hi[:, :256], bd) + _dot(lo[:, :256], bd)
    right = _dot(hi[:, 256:], bd) + _dot(lo[:, 256:], bd)
    return jnp.concatenate([left, right], axis=1)


def _segsum_bf16(xb, bd):
    return jnp.concatenate([_dot(xb[:, :256], bd), _dot(xb[:, 256:], bd)], axis=1)


def _mod_kernel(c_ref, w_ref, b_ref, o_ref):
    c = c_ref[...]
    s = (c * jax.nn.sigmoid(c)).astype(BF16)
    o_ref[...] = _dot(s, w_ref[...].astype(BF16)) + b_ref[...]


def _modulation(cond, w_ada, b_ada):
    depth, d, n = w_ada.shape
    rows = cond.shape[0]
    tn = 1024
    return pl.pallas_call(
        _mod_kernel,
        grid=(depth, n // tn),
        in_specs=[
            pl.BlockSpec((rows, d), lambda l, j: (0, 0)),
            pl.BlockSpec((None, d, tn), lambda l, j: (l, 0, j)),
            pl.BlockSpec((None, 1, tn), lambda l, j: (l, 0, j)),
        ],
        out_specs=pl.BlockSpec((None, rows, tn), lambda l, j: (l, 0, j)),
        out_shape=jax.ShapeDtypeStruct((depth, rows, n), F32),
        compiler_params=_cp("parallel", "arbitrary"),
        name="adaln_modulation",
    )(cond, w_ada, b_ada.reshape(depth, 1, n))


def _mod_spec(layer, row_of_tile):
    return pl.BlockSpec((None, None, N_MOD, D_MODEL), lambda i, *_: (layer, row_of_tile(i), 0, 0))


def _inproj_kernel(x_ref, nw_ref, mod_ref, ws_ref, wr_ref, wm_ref, os_ref, or_ref, om_ref):
    h = _rms(x_ref[...], nw_ref[...]) * (1.0 + mod_ref[1:2, :]) + mod_ref[0:1, :]
    hb = h.astype(BF16)
    os_ref[...] = _dot(hb, ws_ref[...])
    or_ref[...] = _dot(hb, wr_ref[...])
    om_ref[...] = _dot(hb, wm_ref[...])


def _inproj(x2, nw, mod, layer, row_of_tile, tm, ws, wr, wm):
    n = x2.shape[0]
    full = lambda a: pl.BlockSpec(a.shape, lambda i: (0,) * a.ndim)
    return pl.pallas_call(
        _inproj_kernel,
        grid=(n // tm,),
        in_specs=[
            pl.BlockSpec((tm, D_MODEL), lambda i: (i, 0)),
            full(nw),
            _mod_spec(layer, row_of_tile),
            full(ws), full(wr), full(wm),
        ],
        out_specs=[
            pl.BlockSpec((tm, S5_WIDTH), lambda i: (i, 0)),
            pl.BlockSpec((tm, RWKV_COLS), lambda i: (i, 0)),
            pl.BlockSpec((tm, MLA_COLS), lambda i: (i, 0)),
        ],
        out_shape=[
            jax.ShapeDtypeStruct((n, S5_WIDTH), F32),
            jax.ShapeDtypeStruct((n, RWKV_COLS), F32),
            jax.ShapeDtypeStruct((n, MLA_COLS), F32),
        ],
        compiler_params=_cp("parallel"),
        name="in_projection",
    )(x2, nw, mod, ws, wr, wm)


def _s5_prep_kernel(are_ref, aim_ref, ldt_ref, bre_ref, bim_ref, cre_ref, cim_ref,
                    k_ref, pin_ref, poutt_ref, lam_ref):
    T = S5_CHUNK
    for d in range(2):
        are = are_ref[d:d + 1, :]
        aim = aim_ref[d:d + 1, :]
        dt = jnp.exp(ldt_ref[d:d + 1, :])
        lre = jnp.exp(are * dt) * jnp.cos(aim * dt)
        lim = jnp.exp(are * dt) * jnp.sin(aim * dt)
        den = are * are + aim * aim
        xr = lre - 1.0
        zre = (xr * are + lim * aim) / den
        zim = (lim * are - xr * aim) / den
        bre = bre_ref[d]
        bim = bim_ref[d]
        bbre = zre * bre - zim * bim
        bbim = zre * bim + zim * bre
        cre = cre_ref[d]
        cim = cim_ref[d]

        def powers(tau):
            mag = jnp.exp(tau * (are * dt))
            ang = tau * (aim * dt)
            return mag * jnp.cos(ang), mag * jnp.sin(ang)

        tau0 = lax.broadcasted_iota(jnp.int32, (T, 1), 0).astype(F32)
        ere, eim = powers(tau0)
        xre = (ere[:, None, :] * cre[None] - eim[:, None, :] * cim[None]).reshape(T * S5_CH, S5_STATE)
        xim = (ere[:, None, :] * cim[None] + eim[:, None, :] * cre[None]).reshape(T * S5_CH, S5_STATE)
        nt = (((1,), (1,)), ((), ()))
        k_ref[d] = (lax.dot_general(xre, bbre, nt, precision=lax.Precision.HIGHEST, preferred_element_type=F32)
                    - lax.dot_general(xim, bbim, nt, precision=lax.Precision.HIGHEST, preferred_element_type=F32))
        tau_out = tau0 + 1.0 if d == 0 else float(T) - tau0
        ore, oim = powers(tau_out)
        poutt_ref[2 * d] = (ore[:, None, :] * cre[None] - oim[:, None, :] * cim[None]).reshape(T * S5_CH, S5_STATE)
        poutt_ref[2 * d + 1] = -(ore[:, None, :] * cim[None] + oim[:, None, :] * cre[None]).reshape(T * S5_CH, S5_STATE)
        tau_in = float(T - 1) - tau0 if d == 0 else tau0
        ire, iim = powers(tau_in)
        pin_ref[2 * d] = (ire[:, None, :] * bbre[None] - iim[:, None, :] * bbim[None]).reshape(T * S5_CH, S5_STATE)
        pin_ref[2 * d + 1] = (ire[:, None, :] * bbim[None] + iim[:, None, :] * bbre[None]).reshape(T * S5_CH, S5_STATE)
        tre, tim = powers(jnp.full((1, 1), float(T), F32))
        lam_ref[2 * d:2 * d + 1, :] = tre
        lam_ref[2 * d + 1:2 * d + 2, :] = tim


def _s5_prep(a_re, a_im, log_dt, b_re, b_im, c_re, c_im):
    G, P, CH, T = S5_GROUPS, S5_STATE, S5_CH, S5_CHUNK
    g_first = lambda a: jnp.swapaxes(a, 0, 1)
    are = g_first(a_re)
    aim = g_first(a_im)
    ldt = g_first(log_dt)[..., None]
    bre = jnp.swapaxes(g_first(b_re), -1, -2)
    bim = jnp.swapaxes(g_first(b_im), -1, -2)
    cre = g_first(c_re)
    cim = g_first(c_im)
    spec3 = lambda s: pl.BlockSpec((None,) + s, lambda g: (g,) + (0,) * len(s))
    k, pin, poutt, lam = pl.pallas_call(
        _s5_prep_kernel,
        grid=(G,),
        in_specs=[spec3((2, P)), spec3((2, P)), spec3((2, 1)),
                  spec3((2, CH, P)), spec3((2, CH, P)), spec3((2, CH, P)), spec3((2, CH, P))],
        out_specs=[spec3((2, T * CH, CH)), spec3((4, T * CH, P)), spec3((4, T * CH, P)), spec3((4, P))],
        out_shape=[jax.ShapeDtypeStruct((G, 2, T * CH, CH), F32),
                   jax.ShapeDtypeStruct((G, 4, T * CH, P), F32),
                   jax.ShapeDtypeStruct((G, 4, T * CH, P), F32),
                   jax.ShapeDtypeStruct((G, 4, P), F32)],
        compiler_params=_cp("parallel"),
        name="s5_weight_prep",
    )(are, aim, ldt, bre, bim, cre, cim)
    k = k.reshape(G, 2, T, CH, CH)
    s_idx = jnp.arange(T)[:, None]
    t_idx = jnp.arange(T)[None, :]
    kf = jnp.where((t_idx >= s_idx)[None, :, :, None, None], k[:, 0][:, jnp.clip(t_idx - s_idx, 0, T - 1)], 0.0)
    kb = jnp.where((t_idx <= s_idx)[None, :, :, None, None], k[:, 1][:, jnp.clip(s_idx - t_idx, 0, T - 1)], 0.0)
    m = (kf + kb).transpose(0, 1, 4, 2, 3)
    GT, G8 = G // S5_TILE_GROUPS, S5_TILE_GROUPS
    m_c = m.reshape(GT, G8, T * CH, T * CH)
    pin_c = pin.transpose(0, 2, 1, 3).reshape(GT, G8, T * CH, 4 * P)
    pout_c = poutt.transpose(0, 1, 3, 2).reshape(GT, G8, 4 * P, T * CH)
    lre = jnp.concatenate([lam[:, 0], lam[:, 0], lam[:, 2], lam[:, 2]], axis=-1)
    lim = jnp.concatenate([-lam[:, 1], lam[:, 1], -lam[:, 3], lam[:, 3]], axis=-1)
    lam_rows = jnp.stack([lre.reshape(GT, G8 * 4 * P), lim.reshape(GT, G8 * 4 * P)], axis=1)
    return m_c.astype(BF16), pin_c.astype(BF16), pout_c.astype(BF16), lam_rows


def _s5_lane_spread():
    T, CH, G8 = S5_CHUNK, S5_CH, S5_TILE_GROUPS
    src = jnp.arange(T * CH)
    dst = jnp.arange(T * G8 * CH)
    same = (src[:, None] // CH == dst[None, :] // (G8 * CH)) & (src[:, None] % CH == dst[None, :] % CH)
    g8_of_dst = (dst // CH) % G8
    return (same[None] & (g8_of_dst[None, None, :] == jnp.arange(G8)[:, None, None])).astype(BF16)


def _s5_chunk_rows(u_ref, bsz, cblk):
    T = S5_CHUNK
    per_b = [jnp.concatenate([u_ref[b, pl.ds(s, cblk, stride=T), :] for s in range(T)], axis=1) for b in range(bsz)]
    return jnp.concatenate(per_b, axis=0).astype(BF16)


def _s5_summary_kernel(u_ref, pin_ref, g_ref, pin_scr, *, bsz, cblk):
    T, CH, G8 = S5_CHUNK, S5_CH, S5_TILE_GROUPS
    SWG = pin_ref.shape[-1]

    @pl.when(pl.program_id(1) == 0)
    def _():
        pin_scr[...] = jnp.zeros_like(pin_scr)
        for g8 in range(G8):
            for s in range(T):
                pin_scr[pl.ds(s * G8 * CH + g8 * CH, CH), pl.ds(g8 * SWG, SWG)] = pin_ref[g8, pl.ds(s * CH, CH), :]

    g = _dot(_s5_chunk_rows(u_ref, bsz, cblk), pin_scr[...])
    for k in range(g_ref.shape[0]):
        for b in range(bsz):
            g_ref[k, pl.ds(b, cblk, stride=bsz), :] = g[b * cblk:(b + 1) * cblk, k * 128:(k + 1) * 128]


def _s5_state_kernel(g_ref, lam_ref, h0_ref, hin_ref, hfin_ref, *, n_chunks, bsz):
    lre = lam_ref[0:1, :]
    lim = lam_ref[1:2, :]
    tiles = [slice(k * 128, (k + 1) * 128) for k in range(4)]

    def body(c, carry):
        hs, hx = carry
        rows = (pl.ds(pl.multiple_of(c * bsz, bsz), bsz), pl.ds(pl.multiple_of((n_chunks - 1 - c) * bsz, bsz), bsz))
        out, outx = [], []
        for k, sl in enumerate(tiles):
            r = rows[k % 2]
            hin_ref[k, r, :] = hs[k]
            g = g_ref[k, r, :]
            out.append(lre[:, sl] * hs[k] + lim[:, sl] * hx[k] + g)
            outx.append(lre[:, sl] * hx[k] - lim[:, sl] * hs[k] + pltpu.roll(g, 64, 1))
        return tuple(out), tuple(outx)

    h0 = tuple(h0_ref[:, sl] for sl in tiles)
    hs, _ = lax.fori_loop(0, n_chunks, body, (h0, tuple(pltpu.roll(h, 64, 1) for h in h0)), unroll=8)
    for k, sl in enumerate(tiles):
        hfin_ref[:, sl] = hs[k]


def _s5_output_kernel(u_ref, hin_ref, m_ref, pout_ref, spread_ref, y_ref, w_scr, pout_scr, *, bsz, cblk):
    T, CH, G8 = S5_CHUNK, S5_CH, S5_TILE_GROUPS
    SWG = pout_ref.shape[-2]

    @pl.when(pl.program_id(1) == 0)
    def _():
        for g8 in range(G8):
            spread = spread_ref[g8]
            wide = _dot(m_ref[g8], spread).astype(BF16)
            for s in range(T):
                w_scr[pl.ds(s * G8 * CH + g8 * CH, CH), :] = wide[s * CH:(s + 1) * CH, :]
            pout_scr[pl.ds(g8 * SWG, SWG), :] = _dot(pout_ref[g8], spread).astype(BF16)

    x = _s5_chunk_rows(u_ref, bsz, cblk)
    hin = jnp.concatenate(
        [jnp.concatenate([hin_ref[k, pl.ds(b, cblk, stride=bsz), :] for k in range(hin_ref.shape[0])], axis=1)
         for b in range(bsz)], axis=0)
    y = _dot(x, w_scr[...]) + _dot(hin.astype(BF16), pout_scr[...])
    for b in range(bsz):
        for s in range(T):
            y_ref[b, pl.ds(s, cblk, stride=T), :] = y[b * cblk:(b + 1) * cblk, s * 128:(s + 1) * 128]


def _s5_scan(u, h0, m_c, pin_c, pout_c, lam_rows, spread):
    bsz, seq, _ = u.shape
    G, P, T = S5_GROUPS, S5_STATE, S5_CHUNK
    GT, G8 = G // S5_TILE_GROUPS, S5_TILE_GROUPS
    SW = G8 * 4 * P
    nc = seq // T
    cblk = min(max(256 // bsz, 8), nc)
    nblk = nc // cblk
    h0g = h0.transpose(2, 0, 1, 4, 3).reshape(GT, G8, bsz, 4 * P).transpose(0, 2, 1, 3).reshape(GT, bsz, SW)
    u_spec = pl.BlockSpec((bsz, cblk * T, 128), lambda x, j: (0, j, x))
    n_tiles = SW // 128
    rows_spec = pl.BlockSpec((None, n_tiles, cblk * bsz, 128), lambda x, j: (x, 0, j, 0))
    per_tile = lambda a: pl.BlockSpec((None,) + a.shape[1:], lambda x, j: (x, 0, 0, 0))
    xw = T * 128
    g = pl.pallas_call(
        functools.partial(_s5_summary_kernel, bsz=bsz, cblk=cblk),
        grid=(GT, nblk),
        in_specs=[u_spec, per_tile(pin_c)],
        scratch_shapes=[pltpu.VMEM((xw, SW), BF16)],
        out_specs=rows_spec,
        out_shape=jax.ShapeDtypeStruct((GT, n_tiles, nc * bsz, 128), F32),
        compiler_params=_cp("parallel", "arbitrary"),
        name="s5_chunk_summary",
    )(u, pin_c)
    quarter = lambda r: pl.BlockSpec((None, r, 512), lambda x, q: (x, 0, q))
    quarter_rows = pl.BlockSpec((None, 4, nc * bsz, 128), lambda x, q: (x, q, 0, 0))
    hin, hfin = pl.pallas_call(
        functools.partial(_s5_state_kernel, n_chunks=nc, bsz=bsz),
        grid=(GT, n_tiles // 4),
        in_specs=[quarter_rows, quarter(2), quarter(bsz)],
        out_specs=[quarter_rows, quarter(bsz)],
        out_shape=[jax.ShapeDtypeStruct((GT, n_tiles, nc * bsz, 128), F32), jax.ShapeDtypeStruct((GT, bsz, SW), F32)],
        compiler_params=_cp("parallel", "parallel"),
        name="s5_state_scan",
    )(g, lam_rows, h0g)
    y = pl.pallas_call(
        functools.partial(_s5_output_kernel, bsz=bsz, cblk=cblk),
        grid=(GT, nblk),
        in_specs=[u_spec, rows_spec, per_tile(m_c), per_tile(pout_c),
                  pl.BlockSpec(spread.shape, lambda x, j: (0, 0, 0))],
        scratch_shapes=[pltpu.VMEM((xw, xw), BF16), pltpu.VMEM((SW, xw), BF16)],
        out_specs=u_spec,
        out_shape=jax.ShapeDtypeStruct((bsz, seq, S5_WIDTH), F32),
        compiler_params=_cp("parallel", "arbitrary"),
        name="s5_chunk_output",
    )(u, hin, m_c, pout_c, spread)
    hfin = hfin.reshape(GT, bsz, G8, 2, 2, P).transpose(1, 3, 0, 2, 5, 4).reshape(bsz, 2, G, P, 2)
    return y, hfin


def _s5_out_kernel(y_ref, u_ref, d_ref, w_ref, nw_ref, o_ref):
    y = y_ref[...] + u_ref[...] * d_ref[...]
    c = math.sqrt(2.0 / math.pi)
    y = y * (0.5 * (1.0 + jnp.tanh(c * (y + 0.044715 * (y * y * y)))))
    z = _dot(y.astype(BF16), w_ref[...])
    o = z[:, :S5_WIDTH] * jax.nn.sigmoid(z[:, S5_WIDTH:])
    o_ref[...] = _rms(o, nw_ref[...]).astype(BF16)


def _s5_out(y2, u2, d_skip, w_glu, nw, tm):
    n = y2.shape[0]
    full = lambda a: pl.BlockSpec(a.shape, lambda i: (0,) * a.ndim)
    row = pl.BlockSpec((tm, S5_WIDTH), lambda i: (i, 0))
    return pl.pallas_call(
        _s5_out_kernel,
        grid=(n // tm,),
        in_specs=[row, row, full(d_skip), full(w_glu), full(nw)],
        out_specs=row,
        out_shape=jax.ShapeDtypeStruct((n, S5_WIDTH), BF16),
        compiler_params=_cp("parallel"),
        name="s5_gelu_glu",
    )(y2, u2, d_skip, w_glu, nw)


def _rwkv_pre_kernel(p_ref, hp_ref, hn_ref, mu_ref, w0_ref, a0_ref, w2_ref, a2_ref, g2_ref,
                     kk_ref, ka_ref, rk_ref, bd_ref,
                     v_ref, nk_ref, dec_ref, kd_ref, bb_ref, qq_ref, vkr_ref, g_ref, bonus_ref, *, tiles_per_seq, tm):
    i = pl.program_id(0)
    j = i % tiles_per_seq
    p = p_ref[...]
    rows = lax.broadcasted_iota(jnp.int32, (tm, 1), 0)
    prev_edge = jnp.where(j == 0, 0.0, hp_ref[7:8, :])
    next_edge = jnp.where(j == tiles_per_seq - 1, 0.0, hn_ref[0:1, :])
    prev = jnp.where(rows == 0, prev_edge, pltpu.roll(p, 1, 0))
    nxt = jnp.where(rows == tm - 1, next_edge, pltpu.roll(p, tm - 1, 0))
    p = p + mu_ref[...] * (0.5 * (prev + nxt) - p)
    W = RWKV_WIDTH
    r = p[:, 0:W]
    k = p[:, W:2 * W]
    v = p[:, 2 * W:3 * W]
    wl = p[:, 3 * W:3 * W + LORA_PAD]
    al = p[:, 3 * W + LORA_PAD:3 * W + 2 * LORA_PAD]
    gl = p[:, 3 * W + 2 * LORA_PAD:3 * W + 3 * LORA_PAD]
    bd = bd_ref[...]
    kk = k * kk_ref[...]
    kk = kk * lax.rsqrt(_segsum(kk * kk, bd) + 1e-12)
    g_ref[...] = _dot(jax.nn.sigmoid(gl).astype(BF16), g2_ref[...])
    tw = jnp.tanh(wl).astype(BF16)
    alb = al.astype(BF16)
    ksum = None
    for d in range(2):
        z = -(w0_ref[d:d + 1, :] + _dot(tw, w2_ref[d]))
        w = -(jnp.maximum(z, 0.0) + jnp.log(1.0 + jnp.exp(-jnp.abs(z)))) - 0.5
        dec = jnp.exp(-jnp.exp(w))
        dec_ref[d] = dec
        a = jax.nn.sigmoid(a0_ref[d:d + 1, :] + _dot(alb, a2_ref[d]))
        kd = k * (1.0 + (a - 1.0) * ka_ref[...])
        kd_ref[d] = kd
        bb = kk * a
        bb_ref[d] = bb
        qq_ref[d] = dec * r - kk * _segsum(bb * r, bd)
        vkr_ref[d] = v * _segsum(kd * r, bd)
        ksum = kd if ksum is None else ksum + kd
    v_ref[...] = v
    nk_ref[...] = -kk
    bonus_ref[...] = _segsum(r * ksum * rk_ref[...], bd) * v


def _rwkv_pre(p2, bsz, seq, wts, bd):
    n = p2.shape[0]
    tm = min(256, seq)
    tps = seq // tm
    r8 = tm // 8
    nblk8 = n // 8
    full = lambda a: pl.BlockSpec(a.shape, lambda i: (0,) * a.ndim)
    o3 = pl.BlockSpec((None, tm, RWKV_WIDTH), lambda i: (i // tps, i % tps, 0))
    o4 = pl.BlockSpec((2, None, tm, RWKV_WIDTH), lambda i: (0, i // tps, i % tps, 0))
    s3 = jax.ShapeDtypeStruct((bsz, seq, RWKV_WIDTH), F32)
    s4 = jax.ShapeDtypeStruct((2, bsz, seq, RWKV_WIDTH), F32)
    names = ("mu", "w0", "a0", "w2", "a2", "g2", "k_k", "k_a", "r_k")
    return pl.pallas_call(
        functools.partial(_rwkv_pre_kernel, tiles_per_seq=tps, tm=tm),
        grid=(n // tm,),
        in_specs=[pl.BlockSpec((tm, RWKV_COLS), lambda i: (i, 0)),
                  pl.BlockSpec((8, RWKV_COLS), lambda i: (jnp.maximum(i * r8 - 1, 0), 0)),
                  pl.BlockSpec((8, RWKV_COLS), lambda i: (jnp.minimum((i + 1) * r8, nblk8 - 1), 0))]
                 + [full(wts[k]) for k in names] + [full(bd)],
        out_specs=[o3, o3, o4, o4, o4, o4, o4, o3, o3],
        out_shape=[s3, s3, s4, s4, s4, s4, s4, s3, s3],
        compiler_params=_cp("parallel"),
        name="rwkv_prepare",
    )(p2, p2, p2, *[wts[k] for k in names], bd)


def _rwkv_scan_kernel(nkf, nkb, vf, vb, decf, decb, kdf, kdb, bbf, bbb, qqf, qqb, vkf, vkb, s0_ref,
                      yf_ref, yb_ref, sfin_ref, s_scr, v8_scr, v8k_scr, y8_scr, *, nb, tlen, n_chunks):
    c = pl.program_id(1)

    @pl.when(c == 0)
    def _():
        s_scr[...] = s0_ref[...].reshape(s_scr.shape)

    N = RWKV_HEAD_DIM
    W = RWKV_WIDTH
    H = RWKV_HEADS
    nt = (((1,), (1,)), ((), ()))
    chains = [(d, n) for d in range(2) for n in range(nb)]
    pick = lambda d, f, b: f if d == 0 else b
    base = lambda m: m * tlen * H

    left = lambda rows: lax.broadcasted_iota(jnp.int32, (rows, 128), 1) < N
    for m, (d, n) in enumerate(chains):
        for src, dst in ((pick(d, vf, vb), v8_scr), (pick(d, vkf, vkb), v8k_scr)):
            for p in range(H // 2):
                tile = src[n, :, p * 128:(p + 1) * 128]
                dst[pl.ds(base(m) + 2 * p, tlen, stride=H), :] = tile
                dst[pl.ds(base(m) + 2 * p + 1, tlen, stride=H), :] = pltpu.roll(tile, N, 1)

    @pl.when(c == 0)
    def _():
        y8_scr[...] = jnp.zeros_like(y8_scr)

    hmask = (lax.broadcasted_iota(jnp.int32, (H, W), 1) // N == lax.broadcasted_iota(jnp.int32, (H, W), 0)).astype(F32)
    tn = (((0,), (0,)), ((), ()))

    def step(t, carry):
        tts = (t, tlen - 1 - t)
        row = lambda ref, n, tt: ref[n, pl.ds(tt, 1), :] * hmask
        tile8 = lambda m, tt: pl.ds(pl.multiple_of(base(m) + tt * H, H), H)
        os_ = []
        for m, (d, n) in enumerate(chains):
            tt = tts[d]
            rows = jnp.concatenate([row(pick(d, nkf, nkb), n, tt), row(pick(d, qqf, qqb), n, tt)], axis=0).astype(BF16)
            os_.append(lax.dot_general(rows, s_scr[m].astype(BF16), nt, preferred_element_type=F32))
        for m, (d, n) in enumerate(chains):
            tt = tts[d]
            sa_v = jnp.concatenate([os_[m][0:H], v8_scr[tile8(m, tt), :][:, 0:N]], axis=0).astype(BF16)
            w2 = jnp.concatenate([row(pick(d, bbf, bbb), n, tt), row(pick(d, kdf, kdb), n, tt)], axis=0).astype(BF16)
            s_scr[m] = s_scr[m] * pick(d, decf, decb)[n, pl.ds(tt, 1), :] + lax.dot_general(
                sa_v, w2, tn, preferred_element_type=F32)
        for m, (d, n) in enumerate(chains):
            y8_scr[tile8(m, tts[d]), 0:N] = os_[m][H:2 * H] + v8k_scr[tile8(m, tts[d]), :][:, 0:N]
        return carry

    lax.fori_loop(0, tlen, step, 0, unroll=8)

    for m, (d, n) in enumerate(chains):
        y_ref = pick(d, yf_ref, yb_ref)
        for p in range(H // 2):
            even = y8_scr[pl.ds(base(m) + 2 * p, tlen, stride=H), :]
            odd = pltpu.roll(y8_scr[pl.ds(base(m) + 2 * p + 1, tlen, stride=H), :], N, 1)
            y_ref[n, :, p * 128:(p + 1) * 128] = jnp.where(left(tlen), even, odd)

    @pl.when(c == n_chunks - 1)
    def _():
        sfin_ref[...] = s_scr[...].reshape(sfin_ref.shape)


def _rwkv_scan(nk, v, dec, kd, bb, qq, vkr, s0):
    bsz, seq, W = nk.shape
    N, H = RWKV_HEAD_DIM, RWKV_HEADS
    nb = 8 if bsz % 8 == 0 else 4
    tlen = min(32, seq)
    nc = seq // tlen
    fwd = pl.BlockSpec((nb, tlen, W), lambda b, c: (b, c, 0))
    bwd = pl.BlockSpec((nb, tlen, W), lambda b, c: (b, nc - 1 - c, 0))
    fwd_d = pl.BlockSpec((None, nb, tlen, W), lambda b, c: (0, b, c, 0))
    bwd_d = pl.BlockSpec((None, nb, tlen, W), lambda b, c: (1, b, nc - 1 - c, 0))
    st = pl.BlockSpec((2, nb, N, W), lambda b, c: (0, b, 0, 0))
    tiles = pltpu.VMEM((2 * nb * tlen * H, 128), F32)
    return pl.pallas_call(
        functools.partial(_rwkv_scan_kernel, nb=nb, tlen=tlen, n_chunks=nc),
        grid=(bsz // nb, nc),
        in_specs=[fwd, bwd, fwd, bwd] + [fwd_d, bwd_d] * 5 + [st],
        out_specs=[fwd, bwd, st],
        out_shape=[jax.ShapeDtypeStruct((bsz, seq, W), F32)] * 2 + [jax.ShapeDtypeStruct((2, bsz, N, W), F32)],
        scratch_shapes=[pltpu.VMEM((2 * nb, N, W), F32), tiles, tiles, tiles],
        compiler_params=_cp("parallel", "arbitrary"),
        name="rwkv_scan",
    )(nk, nk, v, v, dec, dec, kd, kd, bb, bb, qq, qq, vkr, vkr, s0)


def _store_k_heads(k_ref, kn, kpe):
    for h in range(MLA_HEADS):
        k_ref[:, h * MLA_QK_PAD:h * MLA_QK_PAD + 128] = kn[:, h * 128:(h + 1) * 128].astype(BF16)
        k_ref[:, h * MLA_QK_PAD + 128:(h + 1) * MLA_QK_PAD] = kpe


def _mla_prep_kernel(*refs, rope):
    if rope:
        (p_ref, qn_ref, kvn_ref, wq_ref, wqr_ref, wk_ref, wv_ref, cq_ref, sq_ref, ck_ref, sk_ref,
         q_ref, k_ref, v_ref) = refs
    else:
        (p_ref, qn_ref, kvn_ref, wq_ref, wk_ref, wv_ref, q_ref, k_ref, v_ref, ckv_ref, kr_ref) = refs
    p = p_ref[...]
    qn = _rms(p[:, 0:MLA_Q_RANK], qn_ref[...]).astype(BF16)
    q = _dot(qn, wq_ref[...])
    ckv = _rms(p[:, MLA_Q_RANK:MLA_Q_RANK + MLA_KV_RANK], kvn_ref[...])
    kr = p[:, MLA_Q_RANK + MLA_KV_RANK:MLA_Q_RANK + MLA_KV_RANK + 128]
    if rope:
        cq = jnp.concatenate([cq_ref[...]] * MLA_HEADS, axis=1)
        sq = jnp.concatenate([sq_ref[...]] * MLA_HEADS, axis=1)
        q = q * cq + _dot(qn, wqr_ref[...]) * sq
        krot = p[:, MLA_Q_RANK + MLA_KV_RANK + 128:MLA_Q_RANK + MLA_KV_RANK + 256]
        kpe = kr * ck_ref[...] + krot * sk_ref[...]
    else:
        kpe = kr
        ckv_ref[...] = ckv
        kr_ref[...] = kr[:, 0:MLA_ROPE_DIM]
    q_ref[...] = q.astype(BF16)
    cb = ckv.astype(BF16)
    _store_k_heads(k_ref, _dot(cb, wk_ref[...]), kpe.astype(BF16))
    v_ref[...] = _dot(cb, wv_ref[...]).astype(BF16)


def _mla_prep(p2, seq, wts, tables):
    n = p2.shape[0]
    rope = tables is not None
    tm = min(256, seq)
    tps = seq // tm
    full = lambda a: pl.BlockSpec(a.shape, lambda i: (0,) * a.ndim)
    row = lambda w: pl.BlockSpec((tm, w), lambda i: (i, 0))
    ins = [p2, wts["q_norm"], wts["kv_norm"], wts["wq"]]
    specs = [row(MLA_COLS), full(wts["q_norm"]), full(wts["kv_norm"]), full(wts["wq"])]
    if rope:
        ins.append(wts["wq_rot"])
        specs.append(full(wts["wq_rot"]))
    ins += [wts["wk"], wts["wv"]]
    specs += [full(wts["wk"]), full(wts["wv"])]
    outs = [row(MLA_HEADS * MLA_QK_PAD), row(MLA_HEADS * MLA_QK_PAD), row(MLA_HEADS * MLA_V_DIM)]
    shapes = [jax.ShapeDtypeStruct((n, MLA_HEADS * MLA_QK_PAD), BF16),
              jax.ShapeDtypeStruct((n, MLA_HEADS * MLA_QK_PAD), BF16),
              jax.ShapeDtypeStruct((n, MLA_HEADS * MLA_V_DIM), BF16)]
    if rope:
        ins += list(tables)
        specs += [pl.BlockSpec((tm, t.shape[1]), lambda i: (i % tps, 0)) for t in tables]
    else:
        outs += [row(MLA_KV_RANK), row(MLA_ROPE_DIM)]
        shapes += [jax.ShapeDtypeStruct((n, MLA_KV_RANK), F32), jax.ShapeDtypeStruct((n, MLA_ROPE_DIM), F32)]
    return pl.pallas_call(
        functools.partial(_mla_prep_kernel, rope=rope),
        grid=(n // tm,),
        in_specs=specs, out_specs=outs, out_shape=shapes,
        compiler_params=_cp("parallel"),
        name="mla_prepare_rope" if rope else "mla_prepare",
    )(*ins)


def _mla_cache_kernel(ckv_ref, kr_ref, wk_ref, wv_ref, k_ref, v_ref):
    cb = ckv_ref[...].astype(BF16)
    _store_k_heads(k_ref, _dot(cb, wk_ref[...]), kr_ref[...].astype(BF16))
    v_ref[...] = _dot(cb, wv_ref[...]).astype(BF16)


def _mla_cache(cache_ckv, cache_kr_pad, layer, wk, wv):
    bsz, _, past, _ = cache_ckv.shape
    full = lambda a: pl.BlockSpec(a.shape, lambda b: (0,) * a.ndim)
    return pl.pallas_call(
        _mla_cache_kernel,
        grid=(bsz,),
        in_specs=[pl.BlockSpec((None, None, past, MLA_KV_RANK), lambda b: (b, layer, 0, 0)),
                  pl.BlockSpec((None, None, past, 128), lambda b: (b, layer, 0, 0)),
                  full(wk), full(wv)],
        out_specs=[pl.BlockSpec((None, past, MLA_HEADS * MLA_QK_PAD), lambda b: (b, 0, 0)),
                   pl.BlockSpec((None, past, MLA_HEADS * MLA_V_DIM), lambda b: (b, 0, 0))],
        out_shape=[jax.ShapeDtypeStruct((bsz, past, MLA_HEADS * MLA_QK_PAD), BF16),
                   jax.ShapeDtypeStruct((bsz, past, MLA_HEADS * MLA_V_DIM), BF16)],
        compiler_params=_cp("parallel"),
        name="mla_cache_keys",
    )(cache_ckv, cache_kr_pad, wk, wv)


def _attn_kernel(*refs, cache, scale):
    if cache:
        q_ref, k_ref, v_ref, kc_ref, vc_ref, o_ref = refs
    else:
        q_ref, k_ref, v_ref, o_ref = refs
    nt = (((1,), (1,)), ((), ()))
    qk = lambda j: pl.ds(j * MLA_QK_PAD, MLA_QK_PAD)
    vv = lambda j: pl.ds(j * MLA_V_DIM, MLA_V_DIM)
    heads = range(ATTN_HEADS_PER_STEP)
    s = [lax.dot_general(q_ref[:, qk(j)], k_ref[:, qk(j)], nt, preferred_element_type=F32) * scale for j in heads]
    if cache:
        sc = [lax.dot_general(q_ref[:, qk(j)], kc_ref[:, qk(j)], nt, preferred_element_type=F32) * scale for j in heads]
    for j in heads:
        m = jnp.max(s[j], axis=-1, keepdims=True)
        if cache:
            m = jnp.maximum(m, jnp.max(sc[j], axis=-1, keepdims=True))
        e = jnp.exp(s[j] - m)
        den = jnp.sum(e, axis=-1, keepdims=True)
        o = _dot(e.astype(BF16), v_ref[:, vv(j)])
        if cache:
            ec = jnp.exp(sc[j] - m)
            den = den + jnp.sum(ec, axis=-1, keepdims=True)
            o = o + _dot(ec.astype(BF16), vc_ref[:, vv(j)])
        o_ref[:, vv(j)] = o / den


def _attention(q, k, v, kc=None, vc=None):
    bsz, seq, _ = q.shape
    tq = min(ATTN_Q_TILE, seq)
    cache = kc is not None
    scale = float(MLA_NOPE_DIM + MLA_ROPE_DIM) ** -0.5
    hq, hv = ATTN_HEADS_PER_STEP * MLA_QK_PAD, ATTN_HEADS_PER_STEP * MLA_V_DIM
    ins = [q, k, v]
    specs = [pl.BlockSpec((None, tq, hq), lambda b, h, i: (b, i, h)),
             pl.BlockSpec((None, seq, hq), lambda b, h, i: (b, 0, h)),
             pl.BlockSpec((None, seq, hv), lambda b, h, i: (b, 0, h))]
    if cache:
        past = kc.shape[1]
        ins += [kc, vc]
        specs += [pl.BlockSpec((None, past, hq), lambda b, h, i: (b, 0, h)),
                  pl.BlockSpec((None, past, hv), lambda b, h, i: (b, 0, h))]
    return pl.pallas_call(
        functools.partial(_attn_kernel, cache=cache, scale=scale),
        grid=(bsz, MLA_HEADS // ATTN_HEADS_PER_STEP, seq // tq),
        in_specs=specs,
        out_specs=pl.BlockSpec((None, tq, hv), lambda b, h, i: (b, i, h)),
        out_shape=jax.ShapeDtypeStruct((bsz, seq, MLA_HEADS * MLA_V_DIM), F32),
        compiler_params=_cp("parallel", "parallel", "arbitrary"),
        name="mla_attention_cached" if cache else "mla_attention",
    )(*ins)


def _outproj_kernel(ys_ref, yf_ref, yb_ref, bonus_ref, g_ref, ym_ref, x_ref, mod_ref, lw_ref, lb_ref, bd_ref, nm_ref,
                    w_ref, o_ref):
    bd = bd_ref[...]
    y = yf_ref[...] + yb_ref[...]
    inv_n = 1.0 / RWKV_HEAD_DIM
    yc = y - _segsum(y, bd) * inv_n
    var = _segsum(yc * yc, bd) * inv_n
    yn = yc * lax.rsqrt(var + GN_EPS) * lw_ref[...] + lb_ref[...]
    yr = ((yn + bonus_ref[...]) * g_ref[...]).astype(BF16)
    ym = _rms(ym_ref[...], nm_ref[...]).astype(BF16)
    acc = _dot(ys_ref[...], w_ref[0:S5_WIDTH, :])
    acc += _dot(yr, w_ref[S5_WIDTH:S5_WIDTH + RWKV_WIDTH, :])
    acc += _dot(ym, w_ref[S5_WIDTH + RWKV_WIDTH:, :])
    o_ref[...] = x_ref[...] + mod_ref[2:3, :] * acc


def _outproj(ys, yf, yb, bonus, g, ym, x2, mod, layer, row_of_tile, tm, ln_w, ln_b, bd, nm, w_out):
    n = x2.shape[0]
    full = lambda a: pl.BlockSpec(a.shape, lambda i: (0,) * a.ndim)
    row = lambda w: pl.BlockSpec((tm, w), lambda i: (i, 0))
    flat = lambda a: a.reshape(n, a.shape[-1])
    return pl.pallas_call(
        _outproj_kernel,
        grid=(n // tm,),
        in_specs=[row(S5_WIDTH)] + [row(RWKV_WIDTH)] * 4 + [row(MLA_WIDTH), row(D_MODEL), _mod_spec(layer, row_of_tile),
                                                             full(ln_w), full(ln_b), full(bd), full(nm), full(w_out)],
        out_specs=row(D_MODEL),
        out_shape=jax.ShapeDtypeStruct((n, D_MODEL), F32),
        compiler_params=_cp("parallel"),
        name="out_projection",
    )(ys, flat(yf), flat(yb), flat(bonus), flat(g), ym, x2, mod, ln_w, ln_b, bd, nm, w_out)


def _mlp_kernel(x_ref, nw_ref, mod_ref, w1_ref, w2_ref, nf_ref, o_ref, h_scr, *, final_norm):
    j = pl.program_id(1)

    @pl.when(j == 0)
    def _():
        h = _rms(x_ref[...], nw_ref[...]) * (1.0 + mod_ref[4:5, :]) + mod_ref[3:4, :]
        h_scr[...] = h.astype(BF16)
        o_ref[...] = jnp.zeros_like(o_ref)

    a = _dot(h_scr[...], w1_ref[...])
    a = jnp.square(jnp.maximum(a, 0.0)).astype(BF16)
    o_ref[...] += _dot(a, w2_ref[...])

    @pl.when(j == pl.num_programs(1) - 1)
    def _():
        y = x_ref[...] + mod_ref[5:6, :] * o_ref[...]
        if final_norm:
            y = _rms(y, nf_ref[...])
        o_ref[...] = y


def _mlp(x2, nw, mod, layer, row_of_tile, tm, w1, w2, nf, final_norm):
    n = x2.shape[0]
    tf = MLP_FF_TILE
    full = lambda a: pl.BlockSpec(a.shape, lambda i, j: (0,) * a.ndim)
    return pl.pallas_call(
        functools.partial(_mlp_kernel, final_norm=final_norm),
        grid=(n // tm, D_FF // tf),
        in_specs=[pl.BlockSpec((tm, D_MODEL), lambda i, j: (i, 0)), full(nw), _mod_spec(layer, row_of_tile),
                  pl.BlockSpec((D_MODEL, tf), lambda i, j: (0, j)),
                  pl.BlockSpec((tf, D_MODEL), lambda i, j: (j, 0)), full(nf)],
        out_specs=pl.BlockSpec((tm, D_MODEL), lambda i, j: (i, 0)),
        out_shape=jax.ShapeDtypeStruct((n, D_MODEL), F32),
        scratch_shapes=[pltpu.VMEM((tm, D_MODEL), BF16)],
        compiler_params=_cp("parallel", "arbitrary"),
        name="mlp_final" if final_norm else "mlp",
    )(x2, nw, mod, w1, w2, nf)


def _rope_tables(length):
    rows = length // GRID_W
    row_pos = jnp.repeat(jnp.arange(rows, dtype=F32), GRID_W)
    col_pos = jnp.tile(jnp.arange(GRID_W, dtype=F32), rows)
    axis_dim = MLA_ROPE_DIM // 2
    inv_freq = 1.0 / (ROPE_THETA ** (jnp.arange(0, axis_dim, 2, dtype=F32) / axis_dim))
    ang_r = row_pos[:, None] * inv_freq[None, :]
    ang_c = col_pos[:, None] * inv_freq[None, :]
    ang = jnp.concatenate([ang_r, ang_r, ang_c, ang_c], axis=-1)
    cos, sin = jnp.cos(ang), jnp.sin(ang)
    z64 = jnp.zeros((length, 64), F32)
    cos_q = jnp.concatenate([jnp.ones((length, MLA_NOPE_DIM), F32), cos, z64], axis=1)
    sin_q = jnp.concatenate([jnp.zeros((length, MLA_NOPE_DIM), F32), sin, z64], axis=1)
    cos_k = jnp.concatenate([cos, z64], axis=1)
    sin_k = jnp.concatenate([sin, z64], axis=1)
    return cos_q, sin_q, cos_k, sin_k


def _rot_cols(w):
    a, b, c, d = w[..., 0:16], w[..., 16:32], w[..., 32:48], w[..., 48:64]
    return jnp.concatenate([-b, a, -d, c], axis=-1)


def _layer_weights(l, p):
    d = D_MODEL
    w_in = p["w_in"][l]
    z64 = jnp.zeros((d, 64), F32)
    o = S5_WIDTH
    rk = w_in[:, o:o + 3 * RWKV_WIDTH]
    o += 3 * RWKV_WIDTH
    wl, al, gl = w_in[:, o:o + 64], w_in[:, o + 64:o + 128], w_in[:, o + 128:o + 256]
    o += 256
    cq, ckv, kr = w_in[:, o:o + 512], w_in[:, o + 512:o + 768], w_in[:, o + 768:o + 832]
    mu = p["rwkv_mu"][l]
    z1 = jnp.zeros((64,), F32)
    mu_pad = jnp.concatenate([mu[:1536], mu[1536:1600], z1, mu[1600:1664], z1, mu[1664:1792]])[None]
    pad_rows = lambda w: jnp.concatenate([w, jnp.zeros_like(w)], axis=-2)
    w_uq = p["mla_w_uq"][l].reshape(MLA_Q_RANK, MLA_HEADS, MLA_NOPE_DIM + MLA_ROPE_DIM)
    zq = jnp.zeros((MLA_Q_RANK, MLA_HEADS, 64), F32)
    wq = jnp.concatenate([w_uq, zq], axis=-1).reshape(MLA_Q_RANK, MLA_HEADS * MLA_QK_PAD)
    wq_rot = jnp.concatenate([jnp.zeros((MLA_Q_RANK, MLA_HEADS, MLA_NOPE_DIM), F32),
                              _rot_cols(w_uq[..., MLA_NOPE_DIM:]), zq], axis=-1).reshape(MLA_Q_RANK, MLA_HEADS * MLA_QK_PAD)
    w_ukv = p["mla_w_ukv"][l].reshape(MLA_KV_RANK, MLA_HEADS, MLA_NOPE_DIM + MLA_V_DIM)
    row = lambda a: a.reshape(1, -1)
    return {
        "norm_mix": row(p["norm_mix"][l]), "norm_mlp": row(p["norm_mlp"][l]),
        "w_s5": w_in[:, 0:S5_WIDTH].astype(BF16),
        "w_rwkv": jnp.concatenate([rk, wl, z64, al, z64, gl], axis=1).astype(BF16),
        "w_mla": jnp.concatenate([cq, ckv, kr, z64, _rot_cols(kr), z64], axis=1).astype(BF16),
        "w_out": p["w_out"][l].astype(BF16),
        "s5_d": row(p["s5_d"][l]), "s5_w_glu": p["s5_w_glu"][l].astype(BF16), "s5_out_norm": row(p["s5_out_norm"][l]),
        "rwkv": {
            "mu": mu_pad, "w0": p["rwkv_w0"][l], "a0": p["rwkv_a0"][l],
            "w2": pad_rows(p["rwkv_w2"][l]).astype(BF16), "a2": pad_rows(p["rwkv_a2"][l]).astype(BF16),
            "g2": p["rwkv_g2"][l].astype(BF16), "k_k": row(p["rwkv_k_k"][l]), "k_a": row(p["rwkv_k_a"][l]),
            "r_k": row(p["rwkv_r_k"][l]),
        },
        "rwkv_ln_w": row(p["rwkv_ln_w"][l]), "rwkv_ln_b": row(p["rwkv_ln_b"][l]),
        "mla": {
            "q_norm": row(p["mla_q_norm"][l]), "kv_norm": row(p["mla_kv_norm"][l]),
            "wq": wq.astype(BF16), "wq_rot": wq_rot.astype(BF16),
            "wk": w_ukv[..., :MLA_NOPE_DIM].reshape(MLA_KV_RANK, -1).astype(BF16),
            "wv": w_ukv[..., MLA_NOPE_DIM:].reshape(MLA_KV_RANK, -1).astype(BF16),
        },
        "mla_out_norm": row(p["mla_out_norm"][l]),
        "mlp_w1": p["mlp_w1"][l].astype(BF16), "mlp_w2": p["mlp_w2"][l].astype(BF16),
    }


def _trunk_layer(x, mod, layer, lw, s5w, bd, row_of_tile_fn, s5_h0, rwkv_s0, cache, tables, norm_final, final_norm):
    bsz, seq, d = x.shape
    n = bsz * seq
    x2 = x.reshape(n, d)
    tm = min(512, seq) if cache is not None else min(512, n)
    row_of_tile = row_of_tile_fn(tm)
    u2, pr2, pm2 = _inproj(x2, lw["norm_mix"], mod, layer, row_of_tile, tm, lw["w_s5"], lw["w_rwkv"], lw["w_mla"])

    ys, s5_fin = _s5_scan(u2.reshape(bsz, seq, S5_WIDTH), s5_h0, *s5w)
    ys2 = _s5_out(ys.reshape(n, S5_WIDTH), u2, lw["s5_d"], lw["s5_w_glu"], lw["s5_out_norm"], min(512, n))

    v, nk, dec, kd, bb, qq, vkr, g, bonus = _rwkv_pre(pr2, bsz, seq, lw["rwkv"], bd)
    s0 = rwkv_s0.transpose(1, 0, 3, 2, 4).reshape(2, bsz, RWKV_HEAD_DIM, RWKV_WIDTH)
    yf, yb, s_fin = _rwkv_scan(nk, v, dec, kd, bb, qq, vkr, s0)
    rwkv_fin = s_fin.reshape(2, bsz, RWKV_HEAD_DIM, RWKV_HEADS, RWKV_HEAD_DIM).transpose(1, 0, 3, 2, 4)

    shape3 = lambda a: a.reshape(bsz, seq, a.shape[-1])
    if cache is None:
        q, k, v_, ckv_n, k_rope = _mla_prep(pm2, seq, lw["mla"], None)
        ym = _attention(shape3(q), shape3(k), shape3(v_))
        extras = (shape3(ckv_n), shape3(k_rope), s5_fin, rwkv_fin)
    else:
        q, k, v_ = _mla_prep(pm2, seq, lw["mla"], tables)
        kc, vc = _mla_cache(cache[0], cache[1], layer, lw["mla"]["wk"], lw["mla"]["wv"])
        ym = _attention(shape3(q), shape3(k), shape3(v_), kc, vc)
        extras = None

    x2 = _outproj(ys2, yf, yb, bonus, g, ym.reshape(n, MLA_WIDTH), x2, mod, layer, row_of_tile, tm,
                  lw["rwkv_ln_w"], lw["rwkv_ln_b"], bd, lw["mla_out_norm"], lw["w_out"])
    tm_mlp = min(MLP_ROW_TILE, seq) if cache is not None else min(MLP_ROW_TILE, n)
    x2 = _mlp(x2, lw["norm_mlp"], mod, layer, row_of_tile_fn(tm_mlp), tm_mlp, lw["mlp_w1"], lw["mlp_w2"], norm_final,
              final_norm)
    return x2.reshape(bsz, seq, d), extras


def kernel(x_prompt, x_sample, cache_mla_ckv, cache_mla_krope, state_s5, state_rwkv, c, c_ctx, norm_mix, norm_mlp, norm_final, w_ada, b_ada, w_in, w_out, s5_a_re, s5_a_im, s5_log_dt, s5_b_re, s5_b_im, s5_c_re, s5_c_im, s5_d, s5_w_glu, s5_out_norm, rwkv_mu, rwkv_w0, rwkv_w2, rwkv_a0, rwkv_a2, rwkv_g2, rwkv_k_k, rwkv_k_a, rwkv_r_k, rwkv_ln_w, rwkv_ln_b, mla_q_norm, mla_w_uq, mla_kv_norm, mla_w_ukv, mla_out_norm, mlp_w1, mlp_w2):
    p = dict(norm_mix=norm_mix, norm_mlp=norm_mlp, w_in=w_in, w_out=w_out, s5_d=s5_d, s5_w_glu=s5_w_glu,
             s5_out_norm=s5_out_norm, rwkv_mu=rwkv_mu, rwkv_w0=rwkv_w0, rwkv_w2=rwkv_w2, rwkv_a0=rwkv_a0,
             rwkv_a2=rwkv_a2, rwkv_g2=rwkv_g2, rwkv_k_k=rwkv_k_k, rwkv_k_a=rwkv_k_a, rwkv_r_k=rwkv_r_k,
             rwkv_ln_w=rwkv_ln_w, rwkv_ln_b=rwkv_ln_b, mla_q_norm=mla_q_norm, mla_w_uq=mla_w_uq,
             mla_kv_norm=mla_kv_norm, mla_w_ukv=mla_w_ukv, mla_out_norm=mla_out_norm, mlp_w1=mlp_w1, mlp_w2=mlp_w2)
    depth = w_in.shape[0]
    b_ctx, l_ctx, d = x_prompt.shape
    b_dec, l_dec, _ = x_sample.shape

    rows = -(-(1 + b_dec) // 8) * 8
    cond = jnp.zeros((rows, d), F32).at[0].set(c_ctx).at[1:1 + b_dec].set(c)
    mod = _modulation(cond, w_ada, b_ada).reshape(depth, rows, N_MOD, d)

    bd = jnp.kron(jnp.eye(4, dtype=F32), jnp.ones((64, 64), F32)).astype(BF16)
    spread = _s5_lane_spread()
    tables = _rope_tables(l_dec)
    kr_pad = jnp.pad(cache_mla_krope, ((0, 0), (0, 0), (0, 0), (0, 128 - MLA_ROPE_DIM)))
    zero_s5 = jnp.zeros((b_ctx, 2, S5_GROUPS, S5_STATE, 2), F32)
    zero_rwkv = jnp.zeros((b_ctx, 2, RWKV_HEADS, RWKV_HEAD_DIM, RWKV_HEAD_DIM), F32)
    nf = norm_final.reshape(1, d)

    ctx_rows = lambda tm: (lambda i: 0)
    dec_rows = lambda tm: (lambda i: 1 + i // (l_dec // tm))

    xp, xs = x_prompt, x_sample
    new_ckv, new_krope, new_s5, new_rwkv = [], [], [], []
    for l in range(depth):
        lw = _layer_weights(l, p)
        s5w = _s5_prep(s5_a_re[l], s5_a_im[l], s5_log_dt[l], s5_b_re[l], s5_b_im[l], s5_c_re[l], s5_c_im[l]) + (spread,)
        last = l == depth - 1
        xp, (ckv_l, krope_l, s5_l, rwkv_l) = _trunk_layer(
            xp, mod, l, lw, s5w, bd, ctx_rows, zero_s5, zero_rwkv, None, None, nf, last)
        new_ckv.append(ckv_l)
        new_krope.append(krope_l)
        new_s5.append(s5_l)
        new_rwkv.append(rwkv_l)
        xs, _ = _trunk_layer(
            xs, mod, l, lw, s5w, bd, dec_rows, state_s5[:, l], state_rwkv[:, l], (cache_mla_ckv, kr_pad), tables, nf, last)
    return (xp, xs, jnp.stack(new_ckv, axis=1), jnp.stack(new_krope, axis=1),
            jnp.stack(new_s5, axis=1), jnp.stack(new_rwkv, axis=1))
```

```python
import functools
import math

import jax
import jax.numpy as jnp
from jax import lax
from jax.experimental import pallas as pl
from jax.experimental.pallas import tpu as pltpu

F32 = jnp.float32
BF16 = jnp.bfloat16

D_MODEL = 2048
N_MOD = 6
GRID_W = 64
S5_WIDTH = 512
S5_CH = 16
S5_GROUPS = 32
S5_STATE = 64
S5_CHUNK = 8
S5_TILE_GROUPS = 8
RWKV_WIDTH = 512
RWKV_HEAD_DIM = 64
RWKV_HEADS = 8
LORA_PAD = 128
RWKV_COLS = 3 * RWKV_WIDTH + 3 * LORA_PAD
MLA_HEADS = 8
MLA_V_DIM = 128
MLA_NOPE_DIM = 128
MLA_ROPE_DIM = 64
MLA_QK_PAD = 256
MLA_Q_RANK = 512
MLA_KV_RANK = 256
MLA_WIDTH = 1024
MLA_COLS = MLA_Q_RANK + MLA_KV_RANK + 2 * 128
D_FF = 8192
MLP_ROW_TILE = 512
MLP_FF_TILE = 1024
ATTN_Q_TILE = 256
ATTN_HEADS_PER_STEP = 8
ROPE_THETA = 10000.0
NORM_EPS = 1e-6
GN_EPS = 64e-5

VMEM_LIMIT_BYTES = 56 * 1024 * 1024


def _cp(*sem):
    return pltpu.CompilerParams(dimension_semantics=sem, vmem_limit_bytes=VMEM_LIMIT_BYTES)


def _dot(a, b):
    return jnp.dot(a, b, preferred_element_type=F32)


def _rms(x, g):
    ms = jnp.mean(x * x, axis=-1, keepdims=True)
    return x * lax.rsqrt(ms + NORM_EPS) * g


def _split_bf16(x):
    hi = x.astype(BF16)
    lo = (x - hi.astype(F32)).astype(BF16)
    return hi, lo


def _segsum(x, bd):
    hi, lo = _split_bf16(x)
    left = _dot(hi[:, :256], bd) + _dot(lo[:, :256], bd)
    right = _dot(hi[:, 256:], bd) + _dot(lo[:, 256:], bd)
    return jnp.concatenate([left, right], axis=1)


def _segsum_bf16(xb, bd):
    return jnp.concatenate([_dot(xb[:, :256], bd), _dot(xb[:, 256:], bd)], axis=1)


def _mod_kernel(c_ref, w_ref, b_ref, o_ref):
    c = c_ref[...]
    s = (c * jax.nn.sigmoid(c)).astype(BF16)
    o_ref[...] = _dot(s, w_ref[...].astype(BF16)) + b_ref[...]


def _modulation(cond, w_ada, b_ada):
    depth, d, n = w_ada.shape
    rows = cond.shape[0]
    tn = 1024
    return pl.pallas_call(
        _mod_kernel,
        grid=(depth, n // tn),
        in_specs=[
            pl.BlockSpec((rows, d), lambda l, j: (0, 0)),
            pl.BlockSpec((None, d, tn), lambda l, j: (l, 0, j)),
            pl.BlockSpec((None, 1, tn), lambda l, j: (l, 0, j)),
        ],
        out_specs=pl.BlockSpec((None, rows, tn), lambda l, j: (l, 0, j)),
        out_shape=jax.ShapeDtypeStruct((depth, rows, n), F32),
        compiler_params=_cp("parallel", "arbitrary"),
        name="adaln_modulation",
    )(cond, w_ada, b_ada.reshape(depth, 1, n))


def _mod_spec(layer, row_of_tile):
    return pl.BlockSpec((None, None, N_MOD, D_MODEL), lambda i, *_: (layer, row_of_tile(i), 0, 0))


def _inproj_kernel(x_ref, nw_ref, mod_ref, ws_ref, wr_ref, wm_ref, os_ref, or_ref, om_ref):
    h = _rms(x_ref[...], nw_ref[...]) * (1.0 + mod_ref[1:2, :]) + mod_ref[0:1, :]
    hb = h.astype(BF16)
    os_ref[...] = _dot(hb, ws_ref[...])
    or_ref[...] = _dot(hb, wr_ref[...])
    om_ref[...] = _dot(hb, wm_ref[...])


def _inproj(x2, nw, mod, layer, row_of_tile, tm, ws, wr, wm):
    n = x2.shape[0]
    full = lambda a: pl.BlockSpec(a.shape, lambda i: (0,) * a.ndim)
    return pl.pallas_call(
        _inproj_kernel,
        grid=(n // tm,),
        in_specs=[
            pl.BlockSpec((tm, D_MODEL), lambda i: (i, 0)),
            full(nw),
            _mod_spec(layer, row_of_tile),
            full(ws), full(wr), full(wm),
        ],
        out_specs=[
            pl.BlockSpec((tm, S5_WIDTH), lambda i: (i, 0)),
            pl.BlockSpec((tm, RWKV_COLS), lambda i: (i, 0)),
            pl.BlockSpec((tm, MLA_COLS), lambda i: (i, 0)),
        ],
        out_shape=[
            jax.ShapeDtypeStruct((n, S5_WIDTH), F32),
            jax.ShapeDtypeStruct((n, RWKV_COLS), F32),
            jax.ShapeDtypeStruct((n, MLA_COLS), F32),
        ],
        compiler_params=_cp("parallel"),
        name="in_projection",
    )(x2, nw, mod, ws, wr, wm)


def _s5_prep_kernel(are_ref, aim_ref, ldt_ref, bre_ref, bim_ref, cre_ref, cim_ref,
                    k_ref, pin_ref, poutt_ref, lam_ref):
    T = S5_CHUNK
    for d in range(2):
        are = are_ref[d:d + 1, :]
        aim = aim_ref[d:d + 1, :]
        dt = jnp.exp(ldt_ref[d:d + 1, :])
        lre = jnp.exp(are * dt) * jnp.cos(aim * dt)
        lim = jnp.exp(are * dt) * jnp.sin(aim * dt)
        den = are * are + aim * aim
        xr = lre - 1.0
        zre = (xr * are + lim * aim) / den
        zim = (lim * are - xr * aim) / den
        bre = bre_ref[d]
        bim = bim_ref[d]
        bbre = zre * bre - zim * bim
        bbim = zre * bim + zim * bre
        cre = cre_ref[d]
        cim = cim_ref[d]

        def powers(tau):
            mag = jnp.exp(tau * (are * dt))
            ang = tau * (aim * dt)
            return mag * jnp.cos(ang), mag * jnp.sin(ang)

        tau0 = lax.broadcasted_iota(jnp.int32, (T, 1), 0).astype(F32)
        ere, eim = powers(tau0)
        xre = (ere[:, None, :] * cre[None] - eim[:, None, :] * cim[None]).reshape(T * S5_CH, S5_STATE)
        xim = (ere[:, None, :] * cim[None] + eim[:, None, :] * cre[None]).reshape(T * S5_CH, S5_STATE)
        nt = (((1,), (1,)), ((), ()))
        k_ref[d] = (lax.dot_general(xre, bbre, nt, precision=lax.Precision.HIGHEST, preferred_element_type=F32)
                    - lax.dot_general(xim, bbim, nt, precision=lax.Precision.HIGHEST, preferred_element_type=F32))
        tau_out = tau0 + 1.0 if d == 0 else float(T) - tau0
        ore, oim = powers(tau_out)
        poutt_ref[2 * d] = (ore[:, None, :] * cre[None] - oim[:, None, :] * cim[None]).reshape(T * S5_CH, S5_STATE)
        poutt_ref[2 * d + 1] = -(ore[:, None, :] * cim[None] + oim[:, None, :] * cre[None]).reshape(T * S5_CH, S5_STATE)
        tau_in = float(T - 1) - tau0 if d == 0 else tau0
        ire, iim = powers(tau_in)
        pin_ref[2 * d] = (ire[:, None, :] * bbre[None] - iim[:, None, :] * bbim[None]).reshape(T * S5_CH, S5_STATE)
        pin_ref[2 * d + 1] = (ire[:, None, :] * bbim[None] + iim[:, None, :] * bbre[None]).reshape(T * S5_CH, S5_STATE)
        tre, tim = powers(jnp.full((1, 1), float(T), F32))
        lam_ref[2 * d:2 * d + 1, :] = tre
        lam_ref[2 * d + 1:2 * d + 2, :] = tim


def _s5_prep(a_re, a_im, log_dt, b_re, b_im, c_re, c_im):
    G, P, CH, T = S5_GROUPS, S5_STATE, S5_CH, S5_CHUNK
    g_first = lambda a: jnp.swapaxes(a, 0, 1)
    are = g_first(a_re)
    aim = g_first(a_im)
    ldt = g_first(log_dt)[..., None]
    bre = jnp.swapaxes(g_first(b_re), -1, -2)
    bim = jnp.swapaxes(g_first(b_im), -1, -2)
    cre = g_first(c_re)
    cim = g_first(c_im)
    spec3 = lambda s: pl.BlockSpec((None,) + s, lambda g: (g,) + (0,) * len(s))
    k, pin, poutt, lam = pl.pallas_call(
        _s5_prep_kernel,
        grid=(G,),
        in_specs=[spec3((2, P)), spec3((2, P)), spec3((2, 1)),
                  spec3((2, CH, P)), spec3((2, CH, P)), spec3((2, CH, P)), spec3((2, CH, P))],
        out_specs=[spec3((2, T * CH, CH)), spec3((4, T * CH, P)), spec3((4, T * CH, P)), spec3((4, P))],
        out_shape=[jax.ShapeDtypeStruct((G, 2, T * CH, CH), F32),
                   jax.ShapeDtypeStruct((G, 4, T * CH, P), F32),
                   jax.ShapeDtypeStruct((G, 4, T * CH, P), F32),
                   jax.ShapeDtypeStruct((G, 4, P), F32)],
        compiler_params=_cp("parallel"),
        name="s5_weight_prep",
    )(are, aim, ldt, bre, bim, cre, cim)
    k = k.reshape(G, 2, T, CH, CH)
    s_idx = jnp.arange(T)[:, None]
    t_idx = jnp.arange(T)[None, :]
    kf = jnp.where((t_idx >= s_idx)[None, :, :, None, None], k[:, 0][:, jnp.clip(t_idx - s_idx, 0, T - 1)], 0.0)
    kb = jnp.where((t_idx <= s_idx)[None, :, :, None, None], k[:, 1][:, jnp.clip(s_idx - t_idx, 0, T - 1)], 0.0)
    m = (kf + kb).transpose(0, 1, 4, 2, 3)
    GT, G8 = G // S5_TILE_GROUPS, S5_TILE_GROUPS
    m_c = m.reshape(GT, G8, T * CH, T * CH)
    pin_c = pin.transpose(0, 2, 1, 3).reshape(GT, G8, T * CH, 4 * P)
    pout_c = poutt.transpose(0, 1, 3, 2).reshape(GT, G8, 4 * P, T * CH)
    lre = jnp.concatenate([lam[:, 0], lam[:, 0], lam[:, 2], lam[:, 2]], axis=-1)
    lim = jnp.concatenate([-lam[:, 1], lam[:, 1], -lam[:, 3], lam[:, 3]], axis=-1)
    lam_rows = jnp.stack([lre.reshape(GT, G8 * 4 * P), lim.reshape(GT, G8 * 4 * P)], axis=1)
    return m_c.astype(BF16), pin_c.astype(BF16), pout_c.astype(BF16), lam_rows


def _s5_lane_spread():
    T, CH, G8 = S5_CHUNK, S5_CH, S5_TILE_GROUPS
    src = jnp.arange(T * CH)
    dst = jnp.arange(T * G8 * CH)
    same = (src[:, None] // CH == dst[None, :] // (G8 * CH)) & (src[:, None] % CH == dst[None, :] % CH)
    g8_of_dst = (dst // CH) % G8
    return (same[None] & (g8_of_dst[None, None, :] == jnp.arange(G8)[:, None, None])).astype(BF16)


def _s5_chunk_rows(u_ref, bsz, cblk):
    T = S5_CHUNK
    per_b = [jnp.concatenate([u_ref[b, pl.ds(s, cblk, stride=T), :] for s in range(T)], axis=1) for b in range(bsz)]
    return jnp.concatenate(per_b, axis=0).astype(BF16)


def _s5_summary_kernel(u_ref, pin_ref, g_ref, pin_scr, *, bsz, cblk):
    T, CH, G8 = S5_CHUNK, S5_CH, S5_TILE_GROUPS
    SWG = pin_ref.shape[-1]

    @pl.when(pl.program_id(1) == 0)
    def _():
        pin_scr[...] = jnp.zeros_like(pin_scr)
        for g8 in range(G8):
            for s in range(T):
                pin_scr[pl.ds(s * G8 * CH + g8 * CH, CH), pl.ds(g8 * SWG, SWG)] = pin_ref[g8, pl.ds(s * CH, CH), :]

    g = _dot(_s5_chunk_rows(u_ref, bsz, cblk), pin_scr[...])
    for k in range(g_ref.shape[0]):
        for b in range(bsz):
            g_ref[k, pl.ds(b, cblk, stride=bsz), :] = g[b * cblk:(b + 1) * cblk, k * 128:(k + 1) * 128]


def _s5_state_kernel(g_ref, lam_ref, h0_ref, hin_ref, hfin_ref, *, n_chunks, bsz):
    lre = lam_ref[0:1, :]
    lim = lam_ref[1:2, :]
    tiles = [slice(k * 128, (k + 1) * 128) for k in range(4)]

    def body(c, carry):
        hs, hx = carry
        rows = (pl.ds(pl.multiple_of(c * bsz, bsz), bsz), pl.ds(pl.multiple_of((n_chunks - 1 - c) * bsz, bsz), bsz))
        out, outx = [], []
        for k, sl in enumerate(tiles):
            r = rows[k % 2]
            hin_ref[k, r, :] = hs[k]
            g = g_ref[k, r, :]
            out.append(lre[:, sl] * hs[k] + lim[:, sl] * hx[k] + g)
            outx.append(lre[:, sl] * hx[k] - lim[:, sl] * hs[k] + pltpu.roll(g, 64, 1))
        return tuple(out), tuple(outx)

    h0 = tuple(h0_ref[:, sl] for sl in tiles)
    hs, _ = lax.fori_loop(0, n_chunks, body, (h0, tuple(pltpu.roll(h, 64, 1) for h in h0)), unroll=8)
    for k, sl in enumerate(tiles):
        hfin_ref[:, sl] = hs[k]


def _s5_output_kernel(u_ref, hin_ref, m_ref, pout_ref, spread_ref, y_ref, w_scr, pout_scr, *, bsz, cblk):
    T, CH, G8 = S5_CHUNK, S5_CH, S5_TILE_GROUPS
    SWG = pout_ref.shape[-2]

    @pl.when(pl.program_id(1) == 0)
    def _():
        for g8 in range(G8):
            spread = spread_ref[g8]
            wide = _dot(m_ref[g8], spread).astype(BF16)
            for s in range(T):
                w_scr[pl.ds(s * G8 * CH + g8 * CH, CH), :] = wide[s * CH:(s + 1) * CH, :]
            pout_scr[pl.ds(g8 * SWG, SWG), :] = _dot(pout_ref[g8], spread).astype(BF16)

    x = _s5_chunk_rows(u_ref, bsz, cblk)
    hin = jnp.concatenate(
        [jnp.concatenate([hin_ref[k, pl.ds(b, cblk, stride=bsz), :] for k in range(hin_ref.shape[0])], axis=1)
         for b in range(bsz)], axis=0)
    y = _dot(x, w_scr[...]) + _dot(hin.astype(BF16), pout_scr[...])
    for b in range(bsz):
        for s in range(T):
            y_ref[b, pl.ds(s, cblk, stride=T), :] = y[b * cblk:(b + 1) * cblk, s * 128:(s + 1) * 128]


def _s5_scan(u, h0, m_c, pin_c, pout_c, lam_rows, spread):
    bsz, seq, _ = u.shape
    G, P, T = S5_GROUPS, S5_STATE, S5_CHUNK
    GT, G8 = G // S5_TILE_GROUPS, S5_TILE_GROUPS
    SW = G8 * 4 * P
    nc = seq // T
    cblk = min(max(256 // bsz, 8), nc)
    nblk = nc // cblk
    h0g = h0.transpose(2, 0, 1, 4, 3).reshape(GT, G8, bsz, 4 * P).transpose(0, 2, 1, 3).reshape(GT, bsz, SW)
    u_spec = pl.BlockSpec((bsz, cblk * T, 128), lambda x, j: (0, j, x))
    n_tiles = SW // 128
    rows_spec = pl.BlockSpec((None, n_tiles, cblk * bsz, 128), lambda x, j: (x, 0, j, 0))
    per_tile = lambda a: pl.BlockSpec((None,) + a.shape[1:], lambda x, j: (x, 0, 0, 0))
    xw = T * 128
    g = pl.pallas_call(
        functools.partial(_s5_summary_kernel, bsz=bsz, cblk=cblk),
        grid=(GT, nblk),
        in_specs=[u_spec, per_tile(pin_c)],
        scratch_shapes=[pltpu.VMEM((xw, SW), BF16)],
        out_specs=rows_spec,
        out_shape=jax.ShapeDtypeStruct((GT, n_tiles, nc * bsz, 128), F32),
        compiler_params=_cp("parallel", "arbitrary"),
        name="s5_chunk_summary",
    )(u, pin_c)
    quarter = lambda r: pl.BlockSpec((None, r, 512), lambda x, q: (x, 0, q))
    quarter_rows = pl.BlockSpec((None, 4, nc * bsz, 128), lambda x, q: (x, q, 0, 0))
    hin, hfin = pl.pallas_call(
        functools.partial(_s5_state_kernel, n_chunks=nc, bsz=bsz),
        grid=(GT, n_tiles // 4),
        in_specs=[quarter_rows, quarter(2), quarter(bsz)],
        out_specs=[quarter_rows, quarter(bsz)],
        out_shape=[jax.ShapeDtypeStruct((GT, n_tiles, nc * bsz, 128), F32), jax.ShapeDtypeStruct((GT, bsz, SW), F32)],
        compiler_params=_cp("parallel", "parallel"),
        name="s5_state_scan",
    )(g, lam_rows, h0g)
    y = pl.pallas_call(
        functools.partial(_s5_output_kernel, bsz=bsz, cblk=cblk),
        grid=(GT, nblk),
        in_specs=[u_spec, rows_spec, per_tile(m_c), per_tile(pout_c),
                  pl.BlockSpec(spread.shape, lambda x, j: (0, 0, 0))],
        scratch_shapes=[pltpu.VMEM((xw, xw), BF16), pltpu.VMEM((SW, xw), BF16)],
        out_specs=u_spec,
        out_shape=jax.ShapeDtypeStruct((bsz, seq, S5_WIDTH), F32),
        compiler_params=_cp("parallel", "arbitrary"),
        name="s5_chunk_output",
    )(u, hin, m_c, pout_c, spread)
    hfin = hfin.reshape(GT, bsz, G8, 2, 2, P).transpose(1, 3, 0, 2, 5, 4).reshape(bsz, 2, G, P, 2)
    return y, hfin


def _s5_out_kernel(y_ref, u_ref, d_ref, w_ref, nw_ref, o_ref):
    y = y_ref[...] + u_ref[...] * d_ref[...]
    c = math.sqrt(2.0 / math.pi)
    y = y * (0.5 * (1.0 + jnp.tanh(c * (y + 0.044715 * (y * y * y)))))
    z = _dot(y.astype(BF16), w_ref[...])
    o = z[:, :S5_WIDTH] * jax.nn.sigmoid(z[:, S5_WIDTH:])
    o_ref[...] = _rms(o, nw_ref[...]).astype(BF16)


def _s5_out(y2, u2, d_skip, w_glu, nw, tm):
    n = y2.shape[0]
    full = lambda a: pl.BlockSpec(a.shape, lambda i: (0,) * a.ndim)
    row = pl.BlockSpec((tm, S5_WIDTH), lambda i: (i, 0))
    return pl.pallas_call(
        _s5_out_kernel,
        grid=(n // tm,),
        in_specs=[row, row, full(d_skip), full(w_glu), full(nw)],
        out_specs=row,
        out_shape=jax.ShapeDtypeStruct((n, S5_WIDTH), BF16),
        compiler_params=_cp("parallel"),
        name="s5_gelu_glu",
    )(y2, u2, d_skip, w_glu, nw)


def _rwkv_pre_kernel(p_ref, hp_ref, hn_ref, mu_ref, w0_ref, a0_ref, w2_ref, a2_ref, g2_ref,
                     kk_ref, ka_ref, rk_ref, bd_ref,
                     v_ref, nk_ref, dec_ref, kd_ref, bb_ref, qq_ref, vkr_ref, g_ref, bonus_ref, *, tiles_per_seq, tm):
    i = pl.program_id(0)
    j = i % tiles_per_seq
    p = p_ref[...]
    rows = lax.broadcasted_iota(jnp.int32, (tm, 1), 0)
    prev_edge = jnp.where(j == 0, 0.0, hp_ref[7:8, :])
    next_edge = jnp.where(j == tiles_per_seq - 1, 0.0, hn_ref[0:1, :])
    prev = jnp.where(rows == 0, prev_edge, pltpu.roll(p, 1, 0))
    nxt = jnp.where(rows == tm - 1, next_edge, pltpu.roll(p, tm - 1, 0))
    p = p + mu_ref[...] * (0.5 * (prev + nxt) - p)
    W = RWKV_WIDTH
    r = p[:, 0:W]
    k = p[:, W:2 * W]
    v = p[:, 2 * W:3 * W]
    wl = p[:, 3 * W:3 * W + LORA_PAD]
    al = p[:, 3 * W + LORA_PAD:3 * W + 2 * LORA_PAD]
    gl = p[:, 3 * W + 2 * LORA_PAD:3 * W + 3 * LORA_PAD]
    bd = bd_ref[...]
    kk = k * kk_ref[...]
    kk = kk * lax.rsqrt(_segsum(kk * kk, bd) + 1e-12)
    g_ref[...] = _dot(jax.nn.sigmoid(gl).astype(BF16), g2_ref[...])
    tw = jnp.tanh(wl).astype(BF16)
    alb = al.astype(BF16)
    ksum = None
    for d in range(2):
        z = -(w0_ref[d:d + 1, :] + _dot(tw, w2_ref[d]))
        w = -(jnp.maximum(z, 0.0) + jnp.log(1.0 + jnp.exp(-jnp.abs(z)))) - 0.5
        dec = jnp.exp(-jnp.exp(w))
        dec_ref[d] = dec
        a = jax.nn.sigmoid(a0_ref[d:d + 1, :] + _dot(alb, a2_ref[d]))
        kd = k * (1.0 + (a - 1.0) * ka_ref[...])
        kd_ref[d] = kd
        bb = kk * a
        bb_ref[d] = bb
        qq_ref[d] = dec * r - kk * _segsum(bb * r, bd)
        vkr_ref[d] = v * _segsum(kd * r, bd)
        ksum = kd if ksum is None else ksum + kd
    v_ref[...] = v
    nk_ref[...] = -kk
    bonus_ref[...] = _segsum(r * ksum * rk_ref[...], bd) * v


def _rwkv_pre(p2, bsz, seq, wts, bd):
    n = p2.shape[0]
    tm = min(256, seq)
    tps = seq // tm
    r8 = tm // 8
    nblk8 = n // 8
    full = lambda a: pl.BlockSpec(a.shape, lambda i: (0,) * a.ndim)
    o3 = pl.BlockSpec((None, tm, RWKV_WIDTH), lambda i: (i // tps, i % tps, 0))
    o4 = pl.BlockSpec((2, None, tm, RWKV_WIDTH), lambda i: (0, i // tps, i % tps, 0))
    s3 = jax.ShapeDtypeStruct((bsz, seq, RWKV_WIDTH), F32)
    s4 = jax.ShapeDtypeStruct((2, bsz, seq, RWKV_WIDTH), F32)
    names = ("mu", "w0", "a0", "w2", "a2", "g2", "k_k", "k_a", "r_k")
    return pl.pallas_call(
        functools.partial(_rwkv_pre_kernel, tiles_per_seq=tps, tm=tm),
        grid=(n // tm,),
        in_specs=[pl.BlockSpec((tm, RWKV_COLS), lambda i: (i, 0)),
                  pl.BlockSpec((8, RWKV_COLS), lambda i: (jnp.maximum(i * r8 - 1, 0), 0)),
                  pl.BlockSpec((8, RWKV_COLS), lambda i: (jnp.minimum((i + 1) * r8, nblk8 - 1), 0))]
                 + [full(wts[k]) for k in names] + [full(bd)],
        out_specs=[o3, o3, o4, o4, o4, o4, o4, o3, o3],
        out_shape=[s3, s3, s4, s4, s4, s4, s4, s3, s3],
        compiler_params=_cp("parallel"),
        name="rwkv_prepare",
    )(p2, p2, p2, *[wts[k] for k in names], bd)


def _rwkv_scan_kernel(nkf, nkb, vf, vb, decf, decb, kdf, kdb, bbf, bbb, qqf, qqb, vkf, vkb, s0_ref,
                      yf_ref, yb_ref, sfin_ref, s_scr, v8_scr, v8k_scr, y8_scr, *, nb, tlen, n_chunks):
    c = pl.program_id(1)

    @pl.when(c == 0)
    def _():
        s_scr[...] = s0_ref[...].reshape(s_scr.shape)

    N = RWKV_HEAD_DIM
    W = RWKV_WIDTH
    H = RWKV_HEADS
    nt = (((1,), (1,)), ((), ()))
    chains = [(d, n) for d in range(2) for n in range(nb)]
    pick = lambda d, f, b: f if d == 0 else b
    base = lambda m: m * tlen * H

    left = lambda rows: lax.broadcasted_iota(jnp.int32, (rows, 128), 1) < N
    for m, (d, n) in enumerate(chains):
        for src, dst in ((pick(d, vf, vb), v8_scr), (pick(d, vkf, vkb), v8k_scr)):
            for p in range(H // 2):
                tile = src[n, :, p * 128:(p + 1) * 128]
                dst[pl.ds(base(m) + 2 * p, tlen, stride=H), :] = tile
                dst[pl.ds(base(m) + 2 * p + 1, tlen, stride=H), :] = pltpu.roll(tile, N, 1)

    @pl.when(c == 0)
    def _():
        y8_scr[...] = jnp.zeros_like(y8_scr)

    hmask = (lax.broadcasted_iota(jnp.int32, (H, W), 1) // N == lax.broadcasted_iota(jnp.int32, (H, W), 0)).astype(F32)
    tn = (((0,), (0,)), ((), ()))

    def step(t, carry):
        tts = (t, tlen - 1 - t)
        row = lambda ref, n, tt: ref[n, pl.ds(tt, 1), :] * hmask
        tile8 = lambda m, tt: pl.ds(pl.multiple_of(base(m) + tt * H, H), H)
        os_ = []
        for m, (d, n) in enumerate(chains):
            tt = tts[d]
            rows = jnp.concatenate([row(pick(d, nkf, nkb), n, tt), row(pick(d, qqf, qqb), n, tt)], axis=0).astype(BF16)
            os_.append(lax.dot_general(rows, s_scr[m].astype(BF16), nt, preferred_element_type=F32))
        for m, (d, n) in enumerate(chains):
            tt = tts[d]
            sa_v = jnp.concatenate([os_[m][0:H], v8_scr[tile8(m, tt), :][:, 0:N]], axis=0).astype(BF16)
            w2 = jnp.concatenate([row(pick(d, bbf, bbb), n, tt), row(pick(d, kdf, kdb), n, tt)], axis=0).astype(BF16)
            s_scr[m] = s_scr[m] * pick(d, decf, decb)[n, pl.ds(tt, 1), :] + lax.dot_general(
                sa_v, w2, tn, preferred_element_type=F32)
        for m, (d, n) in enumerate(chains):
            y8_scr[tile8(m, tts[d]), 0:N] = os_[m][H:2 * H] + v8k_scr[tile8(m, tts[d]), :][:, 0:N]
        return carry

    lax.fori_loop(0, tlen, step, 0, unroll=8)

    for m, (d, n) in enumerate(chains):
        y_ref = pick(d, yf_ref, yb_ref)
        for p in range(H // 2):
            even = y8_scr[pl.ds(base(m) + 2 * p, tlen, stride=H), :]
            odd = pltpu.roll(y8_scr[pl.ds(base(m) + 2 * p + 1, tlen, stride=H), :], N, 1)
            y_ref[n, :, p * 128:(p + 1) * 128] = jnp.where(left(tlen), even, odd)

    @pl.when(c == n_chunks - 1)
    def _():
        sfin_ref[...] = s_scr[...].reshape(sfin_ref.shape)


def _rwkv_scan(nk, v, dec, kd, bb, qq, vkr, s0):
    bsz, seq, W = nk.shape
    N, H = RWKV_HEAD_DIM, RWKV_HEADS
    nb = 8 if bsz % 8 == 0 else 4
    tlen = min(32, seq)
    nc = seq // tlen
    fwd = pl.BlockSpec((nb, tlen, W), lambda b, c: (b, c, 0))
    bwd = pl.BlockSpec((nb, tlen, W), lambda b, c: (b, nc - 1 - c, 0))
    fwd_d = pl.BlockSpec((None, nb, tlen, W), lambda b, c: (0, b, c, 0))
    bwd_d = pl.BlockSpec((None, nb, tlen, W), lambda b, c: (1, b, nc - 1 - c, 0))
    st = pl.BlockSpec((2, nb, N, W), lambda b, c: (0, b, 0, 0))
    tiles = pltpu.VMEM((2 * nb * tlen * H, 128), F32)
    return pl.pallas_call(
        functools.partial(_rwkv_scan_kernel, nb=nb, tlen=tlen, n_chunks=nc),
        grid=(bsz // nb, nc),
        in_specs=[fwd, bwd, fwd, bwd] + [fwd_d, bwd_d] * 5 + [st],
        out_specs=[fwd, bwd, st],
        out_shape=[jax.ShapeDtypeStruct((bsz, seq, W), F32)] * 2 + [jax.ShapeDtypeStruct((2, bsz, N, W), F32)],
        scratch_shapes=[pltpu.VMEM((2 * nb, N, W), F32), tiles, tiles, tiles],
        compiler_params=_cp("parallel", "arbitrary"),
        name="rwkv_scan",
    )(nk, nk, v, v, dec, dec, kd, kd, bb, bb, qq, qq, vkr, vkr, s0)


def _store_k_heads(k_ref, kn, kpe):
    for h in range(MLA_HEADS):
        k_ref[:, h * MLA_QK_PAD:h * MLA_QK_PAD + 128] = kn[:, h * 128:(h + 1) * 128].astype(BF16)
        k_ref[:, h * MLA_QK_PAD + 128:(h + 1) * MLA_QK_PAD] = kpe


def _mla_prep_kernel(*refs, rope):
    if rope:
        (p_ref, qn_ref, kvn_ref, wq_ref, wqr_ref, wk_ref, wv_ref, cq_ref, sq_ref, ck_ref, sk_ref,
         q_ref, k_ref, v_ref) = refs
    else:
        (p_ref, qn_ref, kvn_ref, wq_ref, wk_ref, wv_ref, q_ref, k_ref, v_ref, ckv_ref, kr_ref) = refs
    p = p_ref[...]
    qn = _rms(p[:, 0:MLA_Q_RANK], qn_ref[...]).astype(BF16)
    q = _dot(qn, wq_ref[...])
    ckv = _rms(p[:, MLA_Q_RANK:MLA_Q_RANK + MLA_KV_RANK], kvn_ref[...])
    kr = p[:, MLA_Q_RANK + MLA_KV_RANK:MLA_Q_RANK + MLA_KV_RANK + 128]
    if rope:
        cq = jnp.concatenate([cq_ref[...]] * MLA_HEADS, axis=1)
        sq = jnp.concatenate([sq_ref[...]] * MLA_HEADS, axis=1)
        q = q * cq + _dot(qn, wqr_ref[...]) * sq
        krot = p[:, MLA_Q_RANK + MLA_KV_RANK + 128:MLA_Q_RANK + MLA_KV_RANK + 256]
        kpe = kr * ck_ref[...] + krot * sk_ref[...]
    else:
        kpe = kr
        ckv_ref[...] = ckv
        kr_ref[...] = kr[:, 0:MLA_ROPE_DIM]
    q_ref[...] = q.astype(BF16)
    cb = ckv.astype(BF16)
    _store_k_heads(k_ref, _dot(cb, wk_ref[...]), kpe.astype(BF16))
    v_ref[...] = _dot(cb, wv_ref[...]).astype(BF16)


def _mla_prep(p2, seq, wts, tables):
    n = p2.shape[0]
    rope = tables is not None
    tm = min(256, seq)
    tps = seq // tm
    full = lambda a: pl.BlockSpec(a.shape, lambda i: (0,) * a.ndim)
    row = lambda w: pl.BlockSpec((tm, w), lambda i: (i, 0))
    ins = [p2, wts["q_norm"], wts["kv_norm"], wts["wq"]]
    specs = [row(MLA_COLS), full(wts["q_norm"]), full(wts["kv_norm"]), full(wts["wq"])]
    if rope:
        ins.append(wts["wq_rot"])
        specs.append(full(wts["wq_rot"]))
    ins += [wts["wk"], wts["wv"]]
    specs += [full(wts["wk"]), full(wts["wv"])]
    outs = [row(MLA_HEADS * MLA_QK_PAD), row(MLA_HEADS * MLA_QK_PAD), row(MLA_HEADS * MLA_V_DIM)]
    shapes = [jax.ShapeDtypeStruct((n, MLA_HEADS * MLA_QK_PAD), BF16),
              jax.ShapeDtypeStruct((n, MLA_HEADS * MLA_QK_PAD), BF16),
              jax.ShapeDtypeStruct((n, MLA_HEADS * MLA_V_DIM), BF16)]
    if rope:
        ins += list(tables)
        specs += [pl.BlockSpec((tm, t.shape[1]), lambda i: (i % tps, 0)) for t in tables]
    else:
        outs += [row(MLA_KV_RANK), row(MLA_ROPE_DIM)]
        shapes += [jax.ShapeDtypeStruct((n, MLA_KV_RANK), F32), jax.ShapeDtypeStruct((n, MLA_ROPE_DIM), F32)]
    return pl.pallas_call(
        functools.partial(_mla_prep_kernel, rope=rope),
        grid=(n // tm,),
        in_specs=specs, out_specs=outs, out_shape=shapes,
        compiler_params=_cp("parallel"),
        name="mla_prepare_rope" if rope else "mla_prepare",
    )(*ins)


def _mla_cache_kernel(ckv_ref, kr_ref, wk_ref, wv_ref, k_ref, v_ref):
    cb = ckv_ref[...].astype(BF16)
    _store_k_heads(k_ref, _dot(cb, wk_ref[...]), kr_ref[...].astype(BF16))
    v_ref[...] = _dot(cb, wv_ref[...]).astype(BF16)


def _mla_cache(cache_ckv, cache_kr_pad, layer, wk, wv):
    bsz, _, past, _ = cache_ckv.shape
    full = lambda a: pl.BlockSpec(a.shape, lambda b: (0,) * a.ndim)
    return pl.pallas_call(
        _mla_cache_kernel,
        grid=(bsz,),
        in_specs=[pl.BlockSpec((None, None, past, MLA_KV_RANK), lambda b: (b, layer, 0, 0)),
                  pl.BlockSpec((None, None, past, 128), lambda b: (b, layer, 0, 0)),
                  full(wk), full(wv)],
        out_specs=[pl.BlockSpec((None, past, MLA_HEADS * MLA_QK_PAD), lambda b: (b, 0, 0)),
                   pl.BlockSpec((None, past, MLA_HEADS * MLA_V_DIM), lambda b: (b, 0, 0))],
        out_shape=[jax.ShapeDtypeStruct((bsz, past, MLA_HEADS * MLA_QK_PAD), BF16),
                   jax.ShapeDtypeStruct((bsz, past, MLA_HEADS * MLA_V_DIM), BF16)],
        compiler_params=_cp("parallel"),
        name="mla_cache_keys",
    )(cache_ckv, cache_kr_pad, wk, wv)


def _attn_kernel(*refs, cache, scale):
    if cache:
        q_ref, k_ref, v_ref, kc_ref, vc_ref, o_ref = refs
    else:
        q_ref, k_ref, v_ref, o_ref = refs
    nt = (((1,), (1,)), ((), ()))
    qk = lambda j: pl.ds(j * MLA_QK_PAD, MLA_QK_PAD)
    vv = lambda j: pl.ds(j * MLA_V_DIM, MLA_V_DIM)
    heads = range(ATTN_HEADS_PER_STEP)
    s = [lax.dot_general(q_ref[:, qk(j)], k_ref[:, qk(j)], nt, preferred_element_type=F32) * scale for j in heads]
    if cache:
        sc = [lax.dot_general(q_ref[:, qk(j)], kc_ref[:, qk(j)], nt, preferred_element_type=F32) * scale for j in heads]
    for j in heads:
        m = jnp.max(s[j], axis=-1, keepdims=True)
        if cache:
            m = jnp.maximum(m, jnp.max(sc[j], axis=-1, keepdims=True))
        e = jnp.exp(s[j] - m)
        den = jnp.sum(e, axis=-1, keepdims=True)
        o = _dot(e.astype(BF16), v_ref[:, vv(j)])
        if cache:
            ec = jnp.exp(sc[j] - m)
            den = den + jnp.sum(ec, axis=-1, keepdims=True)
            o = o + _dot(ec.astype(BF16), vc_ref[:, vv(j)])
        o_ref[:, vv(j)] = o / den


def _attention(q, k, v, kc=None, vc=None):
    bsz, seq, _ = q.shape
    tq = min(ATTN_Q_TILE, seq)
    cache = kc is not None
    scale = float(MLA_NOPE_DIM + MLA_ROPE_DIM) ** -0.5
    hq, hv = ATTN_HEADS_PER_STEP * MLA_QK_PAD, ATTN_HEADS_PER_STEP * MLA_V_DIM
    ins = [q, k, v]
    specs = [pl.BlockSpec((None, tq, hq), lambda b, h, i: (b, i, h)),
             pl.BlockSpec((None, seq, hq), lambda b, h, i: (b, 0, h), pipeline_mode=pl.Buffered(1)),
             pl.BlockSpec((None, seq, hv), lambda b, h, i: (b, 0, h), pipeline_mode=pl.Buffered(1))]
    if cache:
        past = kc.shape[1]
        ins += [kc, vc]
        specs += [pl.BlockSpec((None, past, hq), lambda b, h, i: (b, 0, h)),
                  pl.BlockSpec((None, past, hv), lambda b, h, i: (b, 0, h))]
    return pl.pallas_call(
        functools.partial(_attn_kernel, cache=cache, scale=scale),
        grid=(bsz, MLA_HEADS // ATTN_HEADS_PER_STEP, seq // tq),
        in_specs=specs,
        out_specs=pl.BlockSpec((None, tq, hv), lambda b, h, i: (b, i, h)),
        out_shape=jax.ShapeDtypeStruct((bsz, seq, MLA_HEADS * MLA_V_DIM), F32),
        compiler_params=_cp("parallel", "parallel", "arbitrary"),
        name="mla_attention_cached" if cache else "mla_attention",
    )(*ins)


def _outproj_kernel(ys_ref, yf_ref, yb_ref, bonus_ref, g_ref, ym_ref, x_ref, mod_ref, lw_ref, lb_ref, bd_ref, nm_ref,
                    w_ref, o_ref):
    bd = bd_ref[...]
    y = yf_ref[...] + yb_ref[...]
    inv_n = 1.0 / RWKV_HEAD_DIM
    yc = y - _segsum(y, bd) * inv_n
    var = _segsum(yc * yc, bd) * inv_n
    yn = yc * lax.rsqrt(var + GN_EPS) * lw_ref[...] + lb_ref[...]
    yr = ((yn + bonus_ref[...]) * g_ref[...]).astype(BF16)
    ym = _rms(ym_ref[...], nm_ref[...]).astype(BF16)
    acc = _dot(ys_ref[...], w_ref[0:S5_WIDTH, :])
    acc += _dot(yr, w_ref[S5_WIDTH:S5_WIDTH + RWKV_WIDTH, :])
    acc += _dot(ym, w_ref[S5_WIDTH + RWKV_WIDTH:, :])
    o_ref[...] = x_ref[...] + mod_ref[2:3, :] * acc


def _outproj(ys, yf, yb, bonus, g, ym, x2, mod, layer, row_of_tile, tm, ln_w, ln_b, bd, nm, w_out):
    n = x2.shape[0]
    full = lambda a: pl.BlockSpec(a.shape, lambda i: (0,) * a.ndim)
    row = lambda w: pl.BlockSpec((tm, w), lambda i: (i, 0))
    flat = lambda a: a.reshape(n, a.shape[-1])
    return pl.pallas_call(
        _outproj_kernel,
        grid=(n // tm,),
        in_specs=[row(S5_WIDTH)] + [row(RWKV_WIDTH)] * 4 + [row(MLA_WIDTH), row(D_MODEL), _mod_spec(layer, row_of_tile),
                                                             full(ln_w), full(ln_b), full(bd), full(nm), full(w_out)],
        out_specs=row(D_MODEL),
        out_shape=jax.ShapeDtypeStruct((n, D_MODEL), F32),
        compiler_params=_cp("parallel"),
        name="out_projection",
    )(ys, flat(yf), flat(yb), flat(bonus), flat(g), ym, x2, mod, ln_w, ln_b, bd, nm, w_out)


def _mlp_kernel(x_ref, nw_ref, mod_ref, w1_ref, w2_ref, nf_ref, o_ref, h_scr, *, final_norm):
    j = pl.program_id(1)

    @pl.when(j == 0)
    def _():
        h = _rms(x_ref[...], nw_ref[...]) * (1.0 + mod_ref[4:5, :]) + mod_ref[3:4, :]
        h_scr[...] = h.astype(BF16)
        o_ref[...] = jnp.zeros_like(o_ref)

    a = _dot(h_scr[...], w1_ref[...])
    a = jnp.square(jnp.maximum(a, 0.0)).astype(BF16)
    o_ref[...] += _dot(a, w2_ref[...])

    @pl.when(j == pl.num_programs(1) - 1)
    def _():
        y = x_ref[...] + mod_ref[5:6, :] * o_ref[...]
        if final_norm:
            y = _rms(y, nf_ref[...])
        o_ref[...] = y


def _mlp(x2, nw, mod, layer, row_of_tile, tm, w1, w2, nf, final_norm):
    n = x2.shape[0]
    tf = MLP_FF_TILE
    full = lambda a: pl.BlockSpec(a.shape, lambda i, j: (0,) * a.ndim)
    return pl.pallas_call(
        functools.partial(_mlp_kernel, final_norm=final_norm),
        grid=(n // tm, D_FF // tf),
        in_specs=[pl.BlockSpec((tm, D_MODEL), lambda i, j: (i, 0)), full(nw), _mod_spec(layer, row_of_tile),
                  pl.BlockSpec((D_MODEL, tf), lambda i, j: (0, j)),
                  pl.BlockSpec((tf, D_MODEL), lambda i, j: (j, 0)), full(nf)],
        out_specs=pl.BlockSpec((tm, D_MODEL), lambda i, j: (i, 0)),
        out_shape=jax.ShapeDtypeStruct((n, D_MODEL), F32),
        scratch_shapes=[pltpu.VMEM((tm, D_MODEL), BF16)],
        compiler_params=_cp("parallel", "arbitrary"),
        name="mlp_final" if final_norm else "mlp",
    )(x2, nw, mod, w1, w2, nf)


def _rope_tables(length):
    rows = length // GRID_W
    row_pos = jnp.repeat(jnp.arange(rows, dtype=F32), GRID_W)
    col_pos = jnp.tile(jnp.arange(GRID_W, dtype=F32), rows)
    axis_dim = MLA_ROPE_DIM // 2
    inv_freq = 1.0 / (ROPE_THETA ** (jnp.arange(0, axis_dim, 2, dtype=F32) / axis_dim))
    ang_r = row_pos[:, None] * inv_freq[None, :]
    ang_c = col_pos[:, None] * inv_freq[None, :]
    ang = jnp.concatenate([ang_r, ang_r, ang_c, ang_c], axis=-1)
    cos, sin = jnp.cos(ang), jnp.sin(ang)
    z64 = jnp.zeros((length, 64), F32)
    cos_q = jnp.concatenate([jnp.ones((length, MLA_NOPE_DIM), F32), cos, z64], axis=1)
    sin_q = jnp.concatenate([jnp.zeros((length, MLA_NOPE_DIM), F32), sin, z64], axis=1)
    cos_k = jnp.concatenate([cos, z64], axis=1)
    sin_k = jnp.concatenate([sin, z64], axis=1)
    return cos_q, sin_q, cos_k, sin_k


def _rot_cols(w):
    a, b, c, d = w[..., 0:16], w[..., 16:32], w[..., 32:48], w[..., 48:64]
    return jnp.concatenate([-b, a, -d, c], axis=-1)


def _layer_weights(l, p):
    d = D_MODEL
    w_in = p["w_in"][l]
    z64 = jnp.zeros((d, 64), F32)
    o = S5_WIDTH
    rk = w_in[:, o:o + 3 * RWKV_WIDTH]
    o += 3 * RWKV_WIDTH
    wl, al, gl = w_in[:, o:o + 64], w_in[:, o + 64:o + 128], w_in[:, o + 128:o + 256]
    o += 256
    cq, ckv, kr = w_in[:, o:o + 512], w_in[:, o + 512:o + 768], w_in[:, o + 768:o + 832]
    mu = p["rwkv_mu"][l]
    z1 = jnp.zeros((64,), F32)
    mu_pad = jnp.concatenate([mu[:1536], mu[1536:1600], z1, mu[1600:1664], z1, mu[1664:1792]])[None]
    pad_rows = lambda w: jnp.concatenate([w, jnp.zeros_like(w)], axis=-2)
    w_uq = p["mla_w_uq"][l].reshape(MLA_Q_RANK, MLA_HEADS, MLA_NOPE_DIM + MLA_ROPE_DIM)
    zq = jnp.zeros((MLA_Q_RANK, MLA_HEADS, 64), F32)
    wq = jnp.concatenate([w_uq, zq], axis=-1).reshape(MLA_Q_RANK, MLA_HEADS * MLA_QK_PAD)
    wq_rot = jnp.concatenate([jnp.zeros((MLA_Q_RANK, MLA_HEADS, MLA_NOPE_DIM), F32),
                              _rot_cols(w_uq[..., MLA_NOPE_DIM:]), zq], axis=-1).reshape(MLA_Q_RANK, MLA_HEADS * MLA_QK_PAD)
    w_ukv = p["mla_w_ukv"][l].reshape(MLA_KV_RANK, MLA_HEADS, MLA_NOPE_DIM + MLA_V_DIM)
    row = lambda a: a.reshape(1, -1)
    return {
        "norm_mix": row(p["norm_mix"][l]), "norm_mlp": row(p["norm_mlp"][l]),
        "w_s5": w_in[:, 0:S5_WIDTH].astype(BF16),
        "w_rwkv": jnp.concatenate([rk, wl, z64, al, z64, gl], axis=1).astype(BF16),
        "w_mla": jnp.concatenate([cq, ckv, kr, z64, _rot_cols(kr), z64], axis=1).astype(BF16),
        "w_out": p["w_out"][l].astype(BF16),
        "s5_d": row(p["s5_d"][l]), "s5_w_glu": p["s5_w_glu"][l].astype(BF16), "s5_out_norm": row(p["s5_out_norm"][l]),
        "rwkv": {
            "mu": mu_pad, "w0": p["rwkv_w0"][l], "a0": p["rwkv_a0"][l],
            "w2": pad_rows(p["rwkv_w2"][l]).astype(BF16), "a2": pad_rows(p["rwkv_a2"][l]).astype(BF16),
            "g2": p["rwkv_g2"][l].astype(BF16), "k_k": row(p["rwkv_k_k"][l]), "k_a": row(p["rwkv_k_a"][l]),
            "r_k": row(p["rwkv_r_k"][l]),
        },
        "rwkv_ln_w": row(p["rwkv_ln_w"][l]), "rwkv_ln_b": row(p["rwkv_ln_b"][l]),
        "mla": {
            "q_norm": row(p["mla_q_norm"][l]), "kv_norm": row(p["mla_kv_norm"][l]),
            "wq": wq.astype(BF16), "wq_rot": wq_rot.astype(BF16),
            "wk": w_ukv[..., :MLA_NOPE_DIM].reshape(MLA_KV_RANK, -1).astype(BF16),
            "wv": w_ukv[..., MLA_NOPE_DIM:].reshape(MLA_KV_RANK, -1).astype(BF16),
        },
        "mla_out_norm": row(p["mla_out_norm"][l]),
        "mlp_w1": p["mlp_w1"][l].astype(BF16), "mlp_w2": p["mlp_w2"][l].astype(BF16),
    }


def _trunk_layer(x, mod, layer, lw, s5w, bd, row_of_tile_fn, s5_h0, rwkv_s0, cache, tables, norm_final, final_norm):
    bsz, seq, d = x.shape
    n = bsz * seq
    x2 = x.reshape(n, d)
    tm = min(512, seq) if cache is not None else min(512, n)
    row_of_tile = row_of_tile_fn(tm)
    u2, pr2, pm2 = _inproj(x2, lw["norm_mix"], mod, layer, row_of_tile, tm, lw["w_s5"], lw["w_rwkv"], lw["w_mla"])

    ys, s5_fin = _s5_scan(u2.reshape(bsz, seq, S5_WIDTH), s5_h0, *s5w)
    ys2 = _s5_out(ys.reshape(n, S5_WIDTH), u2, lw["s5_d"], lw["s5_w_glu"], lw["s5_out_norm"], min(512, n))

    v, nk, dec, kd, bb, qq, vkr, g, bonus = _rwkv_pre(pr2, bsz, seq, lw["rwkv"], bd)
    s0 = rwkv_s0.transpose(1, 0, 3, 2, 4).reshape(2, bsz, RWKV_HEAD_DIM, RWKV_WIDTH)
    yf, yb, s_fin = _rwkv_scan(nk, v, dec, kd, bb, qq, vkr, s0)
    rwkv_fin = s_fin.reshape(2, bsz, RWKV_HEAD_DIM, RWKV_HEADS, RWKV_HEAD_DIM).transpose(1, 0, 3, 2, 4)

    shape3 = lambda a: a.reshape(bsz, seq, a.shape[-1])
    if cache is None:
        q, k, v_, ckv_n, k_rope = _mla_prep(pm2, seq, lw["mla"], None)
        ym = _attention(shape3(q), shape3(k), shape3(v_))
        extras = (shape3(ckv_n), shape3(k_rope), s5_fin, rwkv_fin)
    else:
        q, k, v_ = _mla_prep(pm2, seq, lw["mla"], tables)
        kc, vc = _mla_cache(cache[0], cache[1], layer, lw["mla"]["wk"], lw["mla"]["wv"])
        ym = _attention(shape3(q), shape3(k), shape3(v_), kc, vc)
        extras = None

    x2 = _outproj(ys2, yf, yb, bonus, g, ym.reshape(n, MLA_WIDTH), x2, mod, layer, row_of_tile, tm,
                  lw["rwkv_ln_w"], lw["rwkv_ln_b"], bd, lw["mla_out_norm"], lw["w_out"])
    tm_mlp = min(MLP_ROW_TILE, seq) if cache is not None else min(MLP_ROW_TILE, n)
    x2 = _mlp(x2, lw["norm_mlp"], mod, layer, row_of_tile_fn(tm_mlp), tm_mlp, lw["mlp_w1"], lw["mlp_w2"], norm_final,
              final_norm)
    return x2.reshape(bsz, seq, d), extras


def kernel(x_prompt, x_sample, cache_mla_ckv, cache_mla_krope, state_s5, state_rwkv, c, c_ctx, norm_mix, norm_mlp, norm_final, w_ada, b_ada, w_in, w_out, s5_a_re, s5_a_im, s5_log_dt, s5_b_re, s5_b_im, s5_c_re, s5_c_im, s5_d, s5_w_glu, s5_out_norm, rwkv_mu, rwkv_w0, rwkv_w2, rwkv_a0, rwkv_a2, rwkv_g2, rwkv_k_k, rwkv_k_a, rwkv_r_k, rwkv_ln_w, rwkv_ln_b, mla_q_norm, mla_w_uq, mla_kv_norm, mla_w_ukv, mla_out_norm, mlp_w1, mlp_w2):
    p = dict(norm_mix=norm_mix, norm_mlp=norm_mlp, w_in=w_in, w_out=w_out, s5_d=s5_d, s5_w_glu=s5_w_glu,
             s5_out_norm=s5_out_norm, rwkv_mu=rwkv_mu, rwkv_w0=rwkv_w0, rwkv_w2=rwkv_w2, rwkv_a0=rwkv_a0,
             rwkv_a2=rwkv_a2, rwkv_g2=rwkv_g2, rwkv_k_k=rwkv_k_k, rwkv_k_a=rwkv_k_a, rwkv_r_k=rwkv_r_k,
             rwkv_ln_w=rwkv_ln_w, rwkv_ln_b=rwkv_ln_b, mla_q_norm=mla_q_norm, mla_w_uq=mla_w_uq,
             mla_kv_norm=mla_kv_norm, mla_w_ukv=mla_w_ukv, mla_out_norm=mla_out_norm, mlp_w1=mlp_w1, mlp_w2=mlp_w2)
    depth = w_in.shape[0]
    b_ctx, l_ctx, d = x_prompt.shape
    b_dec, l_dec, _ = x_sample.shape

    rows = -(-(1 + b_dec) // 8) * 8
    cond = jnp.zeros((rows, d), F32).at[0].set(c_ctx).at[1:1 + b_dec].set(c)
    mod = _modulation(cond, w_ada, b_ada).reshape(depth, rows, N_MOD, d)

    bd = jnp.kron(jnp.eye(4, dtype=F32), jnp.ones((64, 64), F32)).astype(BF16)
    spread = _s5_lane_spread()
    tables = _rope_tables(l_dec)
    kr_pad = jnp.pad(cache_mla_krope, ((0, 0), (0, 0), (0, 0), (0, 128 - MLA_ROPE_DIM)))
    zero_s5 = jnp.zeros((b_ctx, 2, S5_GROUPS, S5_STATE, 2), F32)
    zero_rwkv = jnp.zeros((b_ctx, 2, RWKV_HEADS, RWKV_HEAD_DIM, RWKV_HEAD_DIM), F32)
    nf = norm_final.reshape(1, d)

    ctx_rows = lambda tm: (lambda i: 0)
    dec_rows = lambda tm: (lambda i: 1 + i // (l_dec // tm))

    xp, xs = x_prompt, x_sample
    new_ckv, new_krope, new_s5, new_rwkv = [], [], [], []
    for l in range(depth):
        lw = _layer_weights(l, p)
        s5w = _s5_prep(s5_a_re[l], s5_a_im[l], s5_log_dt[l], s5_b_re[l], s5_b_im[l], s5_c_re[l], s5_c_im[l]) + (spread,)
        last = l == depth - 1
        xp, (ckv_l, krope_l, s5_l, rwkv_l) = _trunk_layer(
            xp, mod, l, lw, s5w, bd, ctx_rows, zero_s5, zero_rwkv, None, None, nf, last)
        new_ckv.append(ckv_l)
        new_krope.append(krope_l)
        new_s5.append(s5_l)
        new_rwkv.append(rwkv_l)
        xs, _ = _trunk_layer(
            xs, mod, l, lw, s5w, bd, dec_rows, state_s5[:, l], state_rwkv[:, l], (cache_mla_ckv, kr_pad), tables, nf, last)
    return (xp, xs, jnp.stack(new_ckv, axis=1), jnp.stack(new_krope, axis=1),
            jnp.stack(new_s5, axis=1), jnp.stack(new_rwkv, axis=1))
```
